```python
import math
import jax, jax.numpy as jnp
from jax import lax
import numpy as np

D_MODEL = 2048
BATCH = 4
SEQ = 2048
DEPTH = 4
DEC_BATCH = 128
DEC_SEQ = 1
PAST_LEN = 16384
PAGE_SIZE = 128

N_MIXERS = 4
CHUNK = 64
DN_ALPHA = (2 * DEPTH) ** 0.25
DN_BETA = (8 * DEPTH) ** -0.25
LN_EPS = 1e-5
RMS_EPS = 1e-6

A_HEADS = 16
A_DK = D_MODEL // A_HEADS
A_DV = D_MODEL // A_HEADS
B_QK_HEADS = 16
B_V_HEADS = 32
B_DK = 128
B_DV = 128
B_CONV = 4
B_QK_DIM = B_QK_HEADS * B_DK
B_V_DIM = B_V_HEADS * B_DV
B_CONV_DIM = 2 * B_QK_DIM + B_V_DIM
C_HEADS = 8
C_DK = D_MODEL // (2 * C_HEADS)
C_DV = D_MODEL // C_HEADS
C_QK_DIM = C_HEADS * C_DK
C_V_DIM = C_HEADS * C_DV
C_GATE_CAP = 15.0
D_GROUP = 16
D_GROUPS = D_MODEL // D_GROUP
D_STATE = 64
N_EXPERTS = 64
TOP_K = 8
N_ROUTE_GROUPS = 8
TOPK_ROUTE_GROUPS = 4
D_EXPERT = 512
D_SHARED = 512
ROUTED_SCALE = 2.5
EXPERT_BLOCK = 64

kernel_name = 'hybrid_hgrn2_gdn_mlstm_s5_moe_step'


def _layernorm(x, g, b):
    xf = x.astype(jnp.float32)
    mu = jnp.mean(xf, axis=-1, keepdims=True)
    var = jnp.mean(jnp.square(xf - mu), axis=-1, keepdims=True)
    y = (xf - mu) * lax.rsqrt(var + LN_EPS) * g.astype(jnp.float32) + b.astype(jnp.float32)
    return y.astype(x.dtype)


def _rmsnorm(x, g):
    xf = x.astype(jnp.float32)
    return xf * lax.rsqrt(jnp.mean(xf * xf, axis=-1, keepdims=True) + RMS_EPS) * g.astype(jnp.float32)


def _l2norm(x):
    return x * lax.rsqrt(jnp.sum(x * x, axis=-1, keepdims=True) + RMS_EPS)


def _chunk_len(L):
    return CHUNK if L % CHUNK == 0 else L


def _to_chunks(t, c):
    b, l, h = t.shape[:3]
    t = t.reshape((b, l // c, c, h) + t.shape[3:])
    return jnp.moveaxis(jnp.moveaxis(t, 1, 0), 2, 3)


def _from_chunks(t):
    t = jnp.moveaxis(jnp.moveaxis(t, 3, 2), 0, 1)
    return t.reshape((t.shape[0], t.shape[1] * t.shape[2]) + t.shape[3:])


def _cmul(ar, ai, br, bi):
    return ar * br - ai * bi, ar * bi + ai * br


def hgrn2_mixer(x, S0, w_in, lb_logits, g_norm, w_out, layer):
    bsz, L, _ = x.shape
    c = _chunk_len(L)
    q, f, v, g = jnp.split(x @ w_in, 4, axis=-1)
    lb = jnp.cumsum(jax.nn.softmax(lb_logits.astype(jnp.float32), axis=0), axis=0)[layer]
    f = lb + (1.0 - lb) * jax.nn.sigmoid(f.astype(jnp.float32))
    heads = lambda t, d: t.astype(jnp.float32).reshape(bsz, L, A_HEADS, d)
    q = heads(jax.nn.silu(q), A_DK)
    k = heads(1.0 - f, A_DK)
    log_f = heads(jnp.log(f), A_DK)
    v = heads(v, A_DV)
    tri = jnp.tril(jnp.ones((c, c), bool))

    def chunk_step(S, inp):
        qc, kc, lfc, vc = inp
        b = jnp.cumsum(lfc, axis=2)
        dec = jnp.exp(jnp.where(tri[:, :, None], b[:, :, :, None, :] - b[:, :, None, :, :], -jnp.inf))
        att = jnp.einsum('bhtd,bhsd,bhtsd->bhts', qc, kc, dec)
        o = jnp.einsum('bhts,bhsv->bhtv', att, vc) + jnp.einsum('bhtd,bhdv->bhtv', qc * jnp.exp(b), S)
        b_end = b[:, :, -1:, :]
        S = jnp.exp(b_end[:, :, 0, :, None]) * S + jnp.einsum('bhsd,bhsv->bhdv', kc * jnp.exp(b_end - b), vc)
        return S, o

    S, o = lax.scan(chunk_step, S0.astype(jnp.float32),
                    (_to_chunks(q, c), _to_chunks(k, c), _to_chunks(log_f, c), _to_chunks(v, c)))
    o = _rmsnorm(_from_chunks(o), g_norm) * jax.nn.silu(g.astype(jnp.float32)).reshape(bsz, L, A_HEADS, A_DV)
    return o.reshape(bsz, L, D_MODEL).astype(x.dtype) @ w_out, S.astype(S0.dtype)


def _causal_conv(x, buf, w):
    L = x.shape[1]
    xp = jnp.concatenate([buf.astype(x.dtype), x], axis=1)
    y = xp[:, 0:L] * w[0]
    for j in range(1, B_CONV):
        y = y + xp[:, j:j + L] * w[j]
    return y, xp[:, L:]


def gdn_mixer(x, S0, conv0, w_in, conv_w, a_log, dt_bias, g_norm, w_out):
    bsz, L, _ = x.shape
    c = _chunk_len(L)
    qkv, z, b_pre, a_pre = jnp.split(
        x @ w_in, [B_CONV_DIM, B_CONV_DIM + B_V_DIM, B_CONV_DIM + B_V_DIM + B_V_HEADS], axis=-1)
    qkv, conv_new = _causal_conv(qkv, conv0, conv_w)
    qkv = jax.nn.silu(qkv.astype(jnp.float32))
    q, k, v = jnp.split(qkv, [B_QK_DIM, 2 * B_QK_DIM], axis=-1)
    rep = B_V_HEADS // B_QK_HEADS
    q = jnp.repeat(_l2norm(q.reshape(bsz, L, B_QK_HEADS, B_DK)) * B_DK ** -0.5, rep, axis=2)
    k = jnp.repeat(_l2norm(k.reshape(bsz, L, B_QK_HEADS, B_DK)), rep, axis=2)
    v = v.reshape(bsz, L, B_V_HEADS, B_DV)
    beta = jax.nn.sigmoid(b_pre.astype(jnp.float32))
    g = -jnp.exp(a_log.astype(jnp.float32)) * jax.nn.softplus(
        a_pre.astype(jnp.float32) + dt_bias.astype(jnp.float32))
    tri = jnp.tril(jnp.ones((c, c), bool))
    strict = jnp.tril(jnp.ones((c, c), bool), -1)
    eye = jnp.eye(c, dtype=jnp.float32)

    def chunk_step(S, inp):
        qc, kc, vc, bc, gc = inp
        G = jnp.cumsum(gc, axis=-1)
        dec = jnp.exp(jnp.where(tri, G[..., :, None] - G[..., None, :], -jnp.inf))
        A = jnp.where(strict, bc[..., :, None] * dec * jnp.einsum('bhtd,bhsd->bhts', kc, kc), 0.0)
        rhs = jnp.concatenate([bc[..., None] * vc, (bc * jnp.exp(G))[..., None] * kc], axis=-1)
        sol = lax.linalg.triangular_solve(eye + A, rhs, left_side=True, lower=True, unit_diagonal=True)
        W = sol[..., :B_DV] - jnp.einsum('bhtd,bhdv->bhtv', sol[..., B_DV:], S)
        o = (jnp.einsum('bhts,bhsv->bhtv', dec * jnp.einsum('bhtd,bhsd->bhts', qc, kc), W)
             + jnp.einsum('bhtd,bhdv->bhtv', qc * jnp.exp(G)[..., None], S))
        G_end = G[..., -1:]
        S = jnp.exp(G_end)[..., None] * S + jnp.einsum('bhsd,bhsv->bhdv', kc * jnp.exp(G_end - G)[..., None], W)
        return S, o

    S, o = lax.scan(chunk_step, S0.astype(jnp.float32),
                    (_to_chunks(q, c), _to_chunks(k, c), _to_chunks(v, c), _to_chunks(beta, c), _to_chunks(g, c)))
    o = _rmsnorm(_from_chunks(o), g_norm) * jax.nn.silu(z.astype(jnp.float32)).reshape(bsz, L, B_V_HEADS, B_DV)
    return (o.reshape(bsz, L, B_V_DIM).astype(x.dtype) @ w_out, S.astype(S0.dtype), conv_new.astype(conv0.dtype))


def mlstm_mixer(x, C0, n0, m0, w_in, b_gate, g_norm, w_out):
    bsz, L, _ = x.shape
    c = _chunk_len(L)
    q, k, v, o_pre, gates = jnp.split(
        x @ w_in, [C_QK_DIM, 2 * C_QK_DIM, 2 * C_QK_DIM + C_V_DIM, 2 * C_QK_DIM + 2 * C_V_DIM], axis=-1)
    gates = C_GATE_CAP * jnp.tanh((gates.astype(jnp.float32) + b_gate.astype(jnp.float32)) / C_GATE_CAP)
    i_pre = gates[..., :C_HEADS]
    log_f = jax.nn.log_sigmoid(gates[..., C_HEADS:])
    q = q.astype(jnp.float32).reshape(bsz, L, C_HEADS, C_DK) * C_DK ** -0.5
    k = k.astype(jnp.float32).reshape(bsz, L, C_HEADS, C_DK)
    v = v.astype(jnp.float32).reshape(bsz, L, C_HEADS, C_DV)
    tri = jnp.tril(jnp.ones((c, c), bool))

    def chunk_step(carry, inp):
        Cs, ns, ms = carry
        qc, kc, vc, ic, fc = inp
        F = jnp.cumsum(fc, axis=-1)
        log_w = jnp.where(tri, F[..., :, None] - F[..., None, :] + ic[..., None, :], -jnp.inf)
        log_s = F + ms[..., None]
        m_t = jnp.maximum(jnp.max(log_w, axis=-1), log_s)
        qk = jnp.einsum('bhtd,bhsd->bhts', qc, kc) * jnp.exp(log_w - m_t[..., None])
        w_s = jnp.exp(log_s - m_t)
        num = jnp.einsum('bhts,bhsv->bhtv', qk, vc) + w_s[..., None] * jnp.einsum('bhtd,bhdv->bhtv', qc, Cs)
        den = jnp.sum(qk, axis=-1) + w_s * jnp.einsum('bhtd,bhd->bht', qc, ns)
        h = num / jnp.maximum(jnp.abs(den), jnp.exp(-m_t))[..., None]
        m_end = m_t[..., -1]
        w_end = jnp.exp(F[..., -1:] - F + ic - m_end[..., None])
        s_end = jnp.exp(F[..., -1] + ms - m_end)
        Cs = s_end[..., None, None] * Cs + jnp.einsum('bhsd,bhsv->bhdv', kc * w_end[..., None], vc)
        ns = s_end[..., None] * ns + jnp.einsum('bhsd,bhs->bhd', kc, w_end)
        return (Cs, ns, m_end), h

    (C, n, m), h = lax.scan(
        chunk_step, (C0.astype(jnp.float32), n0.astype(jnp.float32), m0.astype(jnp.float32)),
        (_to_chunks(q, c), _to_chunks(k, c), _to_chunks(v, c), _to_chunks(i_pre, c), _to_chunks(log_f, c)))
    h = _rmsnorm(_from_chunks(h), g_norm) * jax.nn.sigmoid(o_pre.astype(jnp.float32)).reshape(bsz, L, C_HEADS, C_DV)
    return (h.reshape(bsz, L, C_V_DIM).astype(x.dtype) @ w_out, C.astype(C0.dtype), n.astype(n0.dtype), m.astype(m0.dtype))


def s5_mixer(x, s0_re, s0_im, a_re, a_im, b_re, b_im, c_re, c_im, d_skip, log_dt, w_glu):
    bsz, L, _ = x.shape
    f32 = jnp.float32
    u = x.astype(f32).reshape(bsz, L, D_GROUPS, D_GROUP)
    lam_re, lam_im = a_re.astype(f32), a_im.astype(f32)
    dt = jnp.exp(log_dt.astype(f32))[:, None]
    mag = jnp.exp(dt * lam_re)
    ab_re, ab_im = mag * jnp.cos(dt * lam_im), mag * jnp.sin(dt * lam_im)
    inv = 1.0 / (lam_re * lam_re + lam_im * lam_im)
    e_re, e_im = _cmul(ab_re - 1.0, ab_im, lam_re * inv, -lam_im * inv)
    bb_re, bb_im = _cmul(e_re[..., None], e_im[..., None], b_re.astype(f32), b_im.astype(f32))
    bu_re = jnp.einsum('gpc,blgc->blgp', bb_re, u)
    bu_im = jnp.einsum('gpc,blgc->blgp', bb_im, u)
    init_re, init_im = _cmul(ab_re, ab_im, s0_re.astype(f32), s0_im.astype(f32))
    bu_re = bu_re.at[:, 0].add(init_re)
    bu_im = bu_im.at[:, 0].add(init_im)
    a_seq_re = jnp.broadcast_to(ab_re, (1, L) + ab_re.shape)
    a_seq_im = jnp.broadcast_to(ab_im, (1, L) + ab_im.shape)

    def combine(e1, e2):
        a1r, a1i, b1r, b1i = e1
        a2r, a2i, b2r, b2i = e2
        ar, ai = _cmul(a2r, a2i, a1r, a1i)
        br, bi = _cmul(a2r, a2i, b1r, b1i)
        return (ar, ai, br + b2r, bi + b2i)

    _, _, s_re, s_im = lax.associative_scan(combine, (a_seq_re, a_seq_im, bu_re, bu_im), axis=1)
    y = (jnp.einsum('gcp,blgp->blgc', c_re.astype(f32), s_re)
         - jnp.einsum('gcp,blgp->blgc', c_im.astype(f32), s_im)
         + d_skip.astype(f32).reshape(D_GROUPS, D_GROUP) * u)
    y = jax.nn.gelu(y.reshape(bsz, L, D_MODEL)).astype(x.dtype) @ w_glu
    out = y[..., :D_MODEL] * jax.nn.sigmoid(y[..., D_MODEL:])
    return out, s_re[:, -1].astype(s0_re.dtype), s_im[:, -1].astype(s0_im.dtype)


def _swiglu(x, w_gu, w_down):
    g, u = jnp.split(x @ w_gu, 2, axis=-1)
    return (jax.nn.silu(g) * u) @ w_down


def moe_ffn(x, w_router, b_router, w_gu, w_down, w_gu_s, w_down_s):
    bsz, L, D = x.shape
    xt = x.reshape(bsz * L, D)
    T = xt.shape[0]
    scores = jax.nn.sigmoid((xt @ w_router).astype(jnp.float32))
    biased = scores + b_router.astype(jnp.float32)
    group_score = jnp.sum(lax.top_k(biased.reshape(T, N_ROUTE_GROUPS, -1), 2)[0], axis=-1)
    _, top_groups = lax.top_k(group_score, TOPK_ROUTE_GROUPS)
    group_ok = jnp.any(top_groups[:, :, None] == jnp.arange(N_ROUTE_GROUPS), axis=1)
    expert_ok = jnp.repeat(group_ok, N_EXPERTS // N_ROUTE_GROUPS, axis=1)
    _, top_e = lax.top_k(jnp.where(expert_ok, biased, -jnp.inf), TOP_K)
    gate = jnp.take_along_axis(scores, top_e, axis=1)
    gate = ROUTED_SCALE * gate / jnp.sum(gate, axis=-1, keepdims=True)
    n_blocks = (T * TOP_K + N_EXPERTS * (EXPERT_BLOCK - 1) + EXPERT_BLOCK - 1) // EXPERT_BLOCK
    n_rows = n_blocks * EXPERT_BLOCK
    flat_e = top_e.reshape(-1)
    order = jnp.argsort(flat_e)
    e_sorted = flat_e[order]
    counts = jnp.bincount(flat_e, length=N_EXPERTS)
    padded = (counts + EXPERT_BLOCK - 1) // EXPERT_BLOCK * EXPERT_BLOCK
    ends = jnp.cumsum(padded)
    dest = (ends - padded)[e_sorted] + jnp.arange(T * TOP_K) - (jnp.cumsum(counts) - counts)[e_sorted]
    row_tok = jnp.full((n_rows,), T, jnp.int32).at[dest].set((order // TOP_K).astype(jnp.int32))
    row_gate = jnp.zeros((n_rows,), jnp.float32).at[dest].set(gate.reshape(-1)[order])
    block_e = jnp.minimum(jnp.searchsorted(ends, jnp.arange(n_blocks) * EXPERT_BLOCK, side='right'), N_EXPERTS - 1)
    x_pad = jnp.concatenate([xt, jnp.zeros((1, D), xt.dtype)], axis=0)

    def expert_block(args):
        toks, e = args
        return _swiglu(x_pad[toks], w_gu[e], w_down[e])

    rows = lax.map(expert_block, (row_tok.reshape(n_blocks, EXPERT_BLOCK), block_e))
    routed = jnp.zeros((T + 1, D), jnp.float32).at[row_tok].add(
        rows.reshape(n_rows, D).astype(jnp.float32) * row_gate[:, None])
    y = routed[:T] + _swiglu(xt, w_gu_s, w_down_s).astype(jnp.float32)
    return y.astype(x.dtype).reshape(bsz, L, D)


def setup_inputs(seed: int = 0) -> dict:
    key = jax.random.key(seed)
    ks = iter(jax.random.split(key, 64))
    f32 = jnp.float32

    def nrm(shape, scale):
        return scale * jax.random.normal(next(ks), shape, f32)

    def gain(shape):
        return 1.0 + nrm(shape, 0.02)

    def log_uniform(shape, lo, hi):
        return jax.random.uniform(next(ks), shape, f32, math.log(lo), math.log(hi))

    s_in = D_MODEL ** -0.5
    x_prompt = nrm((BATCH, SEQ, D_MODEL), 1.0)
    x_sample = nrm((DEC_BATCH, DEC_SEQ, D_MODEL), 1.0)
    state_a_S = nrm((DEC_BATCH, A_HEADS, A_DK, A_DV), 0.5)
    state_b_S = nrm((DEC_BATCH, B_V_HEADS, B_DK, B_DV), 0.1)
    state_b_conv = nrm((DEC_BATCH, B_CONV - 1, B_CONV_DIM), 1.0)
    state_c_C = nrm((DEC_BATCH, C_HEADS, C_DK, C_DV), 0.5)
    state_c_n = nrm((DEC_BATCH, C_HEADS, C_DK), 0.5)
    state_c_m = nrm((DEC_BATCH, C_HEADS), 1.0)
    state_d_re = nrm((DEC_BATCH, D_GROUPS, D_STATE), 0.5)
    state_d_im = nrm((DEC_BATCH, D_GROUPS, D_STATE), 0.5)

    a_w_in = nrm((D_MODEL, 4 * D_MODEL), s_in).at[:, 2 * D_MODEL:3 * D_MODEL].multiply(DN_BETA)
    a_lb_logits = nrm((DEPTH + 1, A_HEADS * A_DK), 0.5)
    a_g_norm = gain((A_DV,))
    a_w_out = nrm((D_MODEL, D_MODEL), s_in * DN_BETA)

    b_w_in = nrm((D_MODEL, B_CONV_DIM + B_V_DIM + 2 * B_V_HEADS), s_in).at[:, 2 * B_QK_DIM:B_CONV_DIM].multiply(DN_BETA)
    b_conv_w = nrm((B_CONV, B_CONV_DIM), B_CONV ** -0.5)
    b_a_log = jnp.log(jax.random.uniform(next(ks), (B_V_HEADS,), f32, 1.0, 16.0))
    dt = jnp.exp(log_uniform((B_V_HEADS,), 1e-3, 1e-1))
    b_dt_bias = dt + jnp.log(-jnp.expm1(-dt))
    b_g_norm = gain((B_DV,))
    b_w_out = nrm((B_V_DIM, D_MODEL), B_V_DIM ** -0.5 * DN_BETA)

    c_w_in = nrm((D_MODEL, 2 * C_QK_DIM + 2 * C_V_DIM + 2 * C_HEADS), s_in).at[:, 2 * C_QK_DIM:2 * C_QK_DIM + C_V_DIM].multiply(DN_BETA)
    c_b_gate = jnp.concatenate([nrm((C_HEADS,), 0.1), 3.0 + nrm((C_HEADS,), 0.1)])
    c_g_norm = gain((C_HEADS, C_DV))
    c_w_out = nrm((C_V_DIM, D_MODEL), C_V_DIM ** -0.5 * DN_BETA)

    d_a_re = -0.5 + nrm((D_GROUPS, D_STATE), 0.01)
    d_a_im = math.pi * jnp.arange(D_STATE, dtype=f32) + nrm((D_GROUPS, D_STATE), 0.01)
    d_b_re = nrm((D_GROUPS, D_STATE, D_GROUP), (2 * D_GROUP) ** -0.5)
    d_b_im = nrm((D_GROUPS, D_STATE, D_GROUP), (2 * D_GROUP) ** -0.5)
    d_c_re = nrm((D_GROUPS, D_GROUP, D_STATE), D_STATE ** -0.5)
    d_c_im = nrm((D_GROUPS, D_GROUP, D_STATE), D_STATE ** -0.5)
    d_skip = nrm((D_MODEL,), 1.0)
    d_log_dt = log_uniform((D_GROUPS,), 1e-3, 1e-1)
    d_w_glu = nrm((D_MODEL, 2 * D_MODEL), s_in).at[:, :D_MODEL].multiply(DN_BETA)

    moe_w_router = nrm((DEPTH, D_MODEL, N_EXPERTS), s_in)
    moe_b_router = nrm((DEPTH, N_EXPERTS), 0.01)
    moe_w_gu = nrm((DEPTH, N_EXPERTS, D_MODEL, 2 * D_EXPERT), s_in)
    moe_w_down = nrm((DEPTH, N_EXPERTS, D_EXPERT, D_MODEL), D_EXPERT ** -0.5 * DN_BETA)
    moe_w_gu_s = nrm((DEPTH, D_MODEL, 2 * D_SHARED), s_in)
    moe_w_down_s = nrm((DEPTH, D_SHARED, D_MODEL), D_SHARED ** -0.5 * DN_BETA)
    ln_mix_g = gain((DEPTH, D_MODEL))
    ln_mix_b = nrm((DEPTH, D_MODEL), 0.02)
    ln_ffn_g = gain((DEPTH, D_MODEL))
    ln_ffn_b = nrm((DEPTH, D_MODEL), 0.02)

    return {'x_prompt': x_prompt, 'x_sample': x_sample,
            'state_a_S': state_a_S, 'state_b_S': state_b_S, 'state_b_conv': state_b_conv,
            'state_c_C': state_c_C, 'state_c_n': state_c_n, 'state_c_m': state_c_m,
            'state_d_re': state_d_re, 'state_d_im': state_d_im,
            'a_w_in': a_w_in, 'a_lb_logits': a_lb_logits, 'a_g_norm': a_g_norm, 'a_w_out': a_w_out,
            'b_w_in': b_w_in, 'b_conv_w': b_conv_w, 'b_a_log': b_a_log, 'b_dt_bias': b_dt_bias,
            'b_g_norm': b_g_norm, 'b_w_out': b_w_out,
            'c_w_in': c_w_in, 'c_b_gate': c_b_gate, 'c_g_norm': c_g_norm, 'c_w_out': c_w_out,
            'd_a_re': d_a_re, 'd_a_im': d_a_im, 'd_b_re': d_b_re, 'd_b_im': d_b_im,
            'd_c_re': d_c_re, 'd_c_im': d_c_im, 'd_skip': d_skip, 'd_log_dt': d_log_dt, 'd_w_glu': d_w_glu,
            'moe_w_router': moe_w_router, 'moe_b_router': moe_b_router, 'moe_w_gu': moe_w_gu,
            'moe_w_down': moe_w_down, 'moe_w_gu_s': moe_w_gu_s, 'moe_w_down_s': moe_w_down_s,
            'ln_mix_g': ln_mix_g, 'ln_mix_b': ln_mix_b, 'ln_ffn_g': ln_ffn_g, 'ln_ffn_b': ln_ffn_b}


def reference(x_prompt, x_sample,
              state_a_S, state_b_S, state_b_conv, state_c_C, state_c_n, state_c_m, state_d_re, state_d_im,
              a_w_in, a_lb_logits, a_g_norm, a_w_out,
              b_w_in, b_conv_w, b_a_log, b_dt_bias, b_g_norm, b_w_out,
              c_w_in, c_b_gate, c_g_norm, c_w_out,
              d_a_re, d_a_im, d_b_re, d_b_im, d_c_re, d_c_im, d_skip, d_log_dt, d_w_glu,
              moe_w_router, moe_b_router, moe_w_gu, moe_w_down, moe_w_gu_s, moe_w_down_s,
              ln_mix_g, ln_mix_b, ln_ffn_g, ln_ffn_b):

    def trunk(x, a_S, b_S, b_conv, c_C, c_n, c_m, d_re, d_im):
        for layer in range(DEPTH):
            kind = layer % N_MIXERS
            if kind == 0:
                h, a_S = hgrn2_mixer(x, a_S, a_w_in, a_lb_logits, a_g_norm, a_w_out, layer)
            elif kind == 1:
                h, b_S, b_conv = gdn_mixer(x, b_S, b_conv, b_w_in, b_conv_w, b_a_log, b_dt_bias, b_g_norm, b_w_out)
            elif kind == 2:
                h, c_C, c_n, c_m = mlstm_mixer(x, c_C, c_n, c_m, c_w_in, c_b_gate, c_g_norm, c_w_out)
            else:
                h, d_re, d_im = s5_mixer(x, d_re, d_im, d_a_re, d_a_im, d_b_re, d_b_im, d_c_re, d_c_im,
                                         d_skip, d_log_dt, d_w_glu)
            x = _layernorm(DN_ALPHA * x + h, ln_mix_g[layer], ln_mix_b[layer])
            f = moe_ffn(x, moe_w_router[layer], moe_b_router[layer], moe_w_gu[layer], moe_w_down[layer],
                        moe_w_gu_s[layer], moe_w_down_s[layer])
            x = _layernorm(DN_ALPHA * x + f, ln_ffn_g[layer], ln_ffn_b[layer])
        return x, a_S, b_S, b_conv, c_C, c_n, c_m, d_re, d_im

    bp = x_prompt.shape[0]
    zeros = lambda s: jnp.zeros((bp,) + s.shape[1:], s.dtype)
    (y_prompt, a_S_p, b_S_p, b_conv_p, c_C_p, c_n_p, c_m_p, d_re_p, d_im_p) = trunk(
        x_prompt, zeros(state_a_S), zeros(state_b_S), zeros(state_b_conv), zeros(state_c_C),
        zeros(state_c_n), zeros(state_c_m), zeros(state_d_re), zeros(state_d_im))
    (y_sample, a_S_s, b_S_s, b_conv_s, c_C_s, c_n_s, c_m_s, d_re_s, d_im_s) = trunk(
        x_sample, state_a_S, state_b_S, state_b_conv, state_c_C, state_c_n, state_c_m, state_d_re, state_d_im)
    return (y_prompt, y_sample, a_S_p, a_S_s, b_S_p, b_S_s, b_conv_p, b_conv_s,
            c_C_p, c_C_s, c_n_p, c_n_s, c_m_p, c_m_s, d_re_p, d_re_s, d_im_p, d_im_s)
```

```python
import functools
import math

import jax
import jax.numpy as jnp
from jax import lax
from jax.experimental import pallas as pl
from jax.experimental.pallas import tpu as pltpu

F32 = jnp.float32
BF16 = jnp.bfloat16
HI = lax.Precision.HIGHEST

D_MODEL = 2048
DEPTH = 4
CHUNK = 64
DN_ALPHA = (2 * DEPTH) ** 0.25
LN_EPS = 1e-5
RMS_EPS = 1e-6

VMEM_LIMIT_BYTES = 56 * 1024 * 1024


def _params(*sem):
    return pltpu.CompilerParams(dimension_semantics=sem, vmem_limit_bytes=VMEM_LIMIT_BYTES)


def _pick_tile(n, target, mult):
    best = None
    for t in range(mult, min(n, target) + 1, mult):
        if n % t == 0:
            best = t
    return best if best is not None else n


def _mm_kernel(x_ref, w_ref, o_ref):
    o_ref[...] = jnp.dot(x_ref[...], w_ref[...].astype(BF16), preferred_element_type=F32).astype(o_ref.dtype)


def _matmul(x, w, n_cols=None, out_dtype=F32, tm_target=1664, tn_target=512):
    m, k = x.shape
    n = w.shape[1] if n_cols is None else n_cols
    tm = _pick_tile(m, tm_target, 16)
    tn = _pick_tile(n, tn_target, 128)
    return pl.pallas_call(
        _mm_kernel,
        grid=(m // tm, n // tn),
        in_specs=[pl.BlockSpec((tm, k), lambda i, j: (i, 0)), pl.BlockSpec((k, tn), lambda i, j: (0, j))],
        out_specs=pl.BlockSpec((tm, tn), lambda i, j: (i, j)),
        out_shape=jax.ShapeDtypeStruct((m, n), out_dtype),
        compiler_params=_params("parallel", "parallel"),
        name="matmul",
    )(x, w)


def _mm_hi_kernel(x_ref, w_ref, o_ref):
    o_ref[...] = jnp.dot(x_ref[...], w_ref[...], preferred_element_type=F32, precision=HI)


def _matmul_hi(x, w, tm_target=640):
    m, k = x.shape
    n = w.shape[1]
    tm = _pick_tile(m, tm_target, 8)
    return pl.pallas_call(
        _mm_hi_kernel,
        grid=(m // tm,),
        in_specs=[pl.BlockSpec((tm, k), lambda i: (i, 0)), pl.BlockSpec((k, n), lambda i: (0, 0))],
        out_specs=pl.BlockSpec((tm, n), lambda i: (i, 0)),
        out_shape=jax.ShapeDtypeStruct((m, n), F32),
        compiler_params=_params("parallel"),
        name="matmul_hi",
    )(x, w)


def _layernorm_rows(z, g, b):
    mu = jnp.mean(z, axis=-1, keepdims=True)
    zc = z - mu
    var = jnp.mean(zc * zc, axis=-1, keepdims=True)
    return zc * lax.rsqrt(var + LN_EPS) * g + b


def _add_ln_kernel(x_ref, h_ref, g_ref, b_ref, o_ref, ob_ref):
    y = _layernorm_rows(DN_ALPHA * x_ref[...] + h_ref[...], g_ref[...], b_ref[...])
    o_ref[...] = y
    ob_ref[...] = y.astype(BF16)


def _add_ln(x, h, g, b, tm_target=640):
    m, d = x.shape
    tm = _pick_tile(m, tm_target, 16)
    row = pl.BlockSpec((tm, d), lambda i: (i, 0))
    vec = pl.BlockSpec((1, d), lambda i: (0, 0))
    return pl.pallas_call(
        _add_ln_kernel,
        grid=(m // tm,),
        in_specs=[row, row, vec, vec],
        out_specs=[row, row],
        out_shape=[jax.ShapeDtypeStruct((m, d), F32), jax.ShapeDtypeStruct((m, d), BF16)],
        compiler_params=_params("parallel"),
        name="add_ln",
    )(x, h, g.reshape(1, d), b.reshape(1, d))


def _silu(x):
    return x * jax.nn.sigmoid(x)


def _ffn_kernel(x_ref, wgu_ref, wd_ref, o_ref, *, dh):
    gu = jnp.dot(x_ref[...], wgu_ref[0].astype(BF16), preferred_element_type=F32)
    a = _silu(gu[:, :dh]) * gu[:, dh:]
    o_ref[...] = jnp.dot(a.astype(BF16), wd_ref[0].astype(BF16), preferred_element_type=F32)


def _dense_ffn(xb, w_gu, w_down, layer, tm_target=640):
    m, d = xb.shape
    dh = w_down.shape[1]
    tm = _pick_tile(m, tm_target, 16)
    return pl.pallas_call(
        functools.partial(_ffn_kernel, dh=dh),
        grid=(m // tm,),
        in_specs=[pl.BlockSpec((tm, d), lambda i: (i, 0)),
                  pl.BlockSpec((1, d, 2 * dh), lambda i: (layer, 0, 0)),
                  pl.BlockSpec((1, dh, d), lambda i: (layer, 0, 0))],
        out_specs=pl.BlockSpec((tm, d), lambda i: (i, 0)),
        out_shape=jax.ShapeDtypeStruct((m, d), F32),
        compiler_params=_params("parallel"),
        name="shared_ffn",
    )(xb, w_gu, w_down)


N_EXPERTS = 64
TOP_K = 8
N_ROUTE_GROUPS = 8
TOPK_ROUTE_GROUPS = 4
GROUP_SIZE = N_EXPERTS // N_ROUTE_GROUPS
ROUTED_SCALE = 2.5
EXPERT_TILE = 256


def _router_kernel(x_ref, wt_ref, b_ref, e_ref, g_ref):
    logits = lax.dot_general(wt_ref[...], x_ref[...], (((1,), (1,)), ((), ())),
                             preferred_element_type=F32, precision=HI)
    scores = jax.nn.sigmoid(logits)
    biased = scores + b_ref[...]
    tm = scores.shape[1]
    neg = jnp.float32(-jnp.inf)
    iota_g = lax.broadcasted_iota(jnp.int32, (GROUP_SIZE, tm), 0)
    gscore = jnp.zeros((N_ROUTE_GROUPS, tm), F32)
    iota_r = lax.broadcasted_iota(jnp.int32, (N_ROUTE_GROUPS, tm), 0)
    for g in range(N_ROUTE_GROUPS):
        v = biased[g * GROUP_SIZE:(g + 1) * GROUP_SIZE, :]
        m1 = jnp.max(v, axis=0, keepdims=True)
        i1 = jnp.min(jnp.where(v == m1, iota_g, GROUP_SIZE), axis=0, keepdims=True)
        m2 = jnp.max(jnp.where(iota_g == i1, neg, v), axis=0, keepdims=True)
        gscore = jnp.where(iota_r == g, m1 + m2, gscore)
    ok = jnp.zeros((N_ROUTE_GROUPS, tm), jnp.int32)
    for _ in range(TOPK_ROUTE_GROUPS):
        m = jnp.max(gscore, axis=0, keepdims=True)
        gi = jnp.min(jnp.where(gscore == m, iota_r, N_ROUTE_GROUPS), axis=0, keepdims=True)
        hit = iota_r == gi
        ok = jnp.where(hit, 1, ok)
        gscore = jnp.where(hit, neg, gscore)
    masked = jnp.concatenate(
        [jnp.where(ok[g:g + 1, :] > 0, biased[g * GROUP_SIZE:(g + 1) * GROUP_SIZE, :], neg)
         for g in range(N_ROUTE_GROUPS)], axis=0)
    iota_e = lax.broadcasted_iota(jnp.int32, (N_EXPERTS, tm), 0)
    iota_k = lax.broadcasted_iota(jnp.int32, (TOP_K, tm), 0)
    top_e = jnp.zeros((TOP_K, tm), jnp.int32)
    gate = jnp.zeros((TOP_K, tm), F32)
    for k in range(TOP_K):
        m = jnp.max(masked, axis=0, keepdims=True)
        ei = jnp.min(jnp.where(masked == m, iota_e, N_EXPERTS), axis=0, keepdims=True)
        hit = iota_e == ei
        gk = jnp.sum(jnp.where(hit, scores, 0.0), axis=0, keepdims=True)
        top_e = jnp.where(iota_k == k, ei, top_e)
        gate = jnp.where(iota_k == k, gk, gate)
        masked = jnp.where(hit, neg, masked)
    e_ref[...] = top_e
    g_ref[...] = ROUTED_SCALE * gate / jnp.sum(gate, axis=0, keepdims=True)


def _router(x, w_router_t, b_router, tm_target=640):
    t, d = x.shape
    tm = _pick_tile(t, tm_target, 128)
    return pl.pallas_call(
        _router_kernel,
        grid=(t // tm,),
        in_specs=[pl.BlockSpec((tm, d), lambda i: (i, 0)),
                  pl.BlockSpec((N_EXPERTS, d), lambda i: (0, 0)),
                  pl.BlockSpec((N_EXPERTS, 1), lambda i: (0, 0))],
        out_specs=[pl.BlockSpec((TOP_K, tm), lambda i: (0, i)), pl.BlockSpec((TOP_K, tm), lambda i: (0, i))],
        out_shape=[jax.ShapeDtypeStruct((TOP_K, t), jnp.int32), jax.ShapeDtypeStruct((TOP_K, t), F32)],
        compiler_params=_params("parallel"),
        name="router",
    )(x, w_router_t, b_router.reshape(N_EXPERTS, 1))


def _dispatch_tables(top_e, n_tiles):
    t = top_e.shape[0]
    flat_e = top_e.reshape(-1)
    onehot = (flat_e[:, None] == jnp.arange(N_EXPERTS, dtype=jnp.int32)[None, :]).astype(jnp.int32)
    csum = jnp.cumsum(onehot, axis=0)
    counts = csum[-1]
    rank = jnp.take_along_axis(csum, flat_e[:, None], axis=1)[:, 0] - 1
    padded = (counts + EXPERT_TILE - 1) // EXPERT_TILE * EXPERT_TILE
    pends = jnp.cumsum(padded)
    pos = (pends - padded)[flat_e] + rank
    n_rows = n_tiles * EXPERT_TILE
    row_tok = jnp.zeros((n_rows,), jnp.int32).at[pos].set(jnp.arange(t * TOP_K, dtype=jnp.int32) // TOP_K)
    tile_e = jnp.minimum(
        jnp.searchsorted(pends, jnp.arange(n_tiles, dtype=jnp.int32) * EXPERT_TILE, side='right'),
        N_EXPERTS - 1).astype(jnp.int32)
    n_used = (pends[-1] // EXPERT_TILE).astype(jnp.int32).reshape(1)
    return row_tok, tile_e, n_used, pos.astype(jnp.int32)


def _expert_kernel(row_tok_ref, tile_e_ref, n_used_ref, x_hbm, wgu_ref, wd_ref, o_ref,
                   xbuf, sem, wgu_b, wd_b, *, dh):
    i = pl.program_id(0)
    n_used = n_used_ref[0]
    slot = i % 2

    def issue(tile, s):
        base = tile * EXPERT_TILE

        def body(r, c):
            tok = row_tok_ref[base + r]
            pltpu.make_async_copy(x_hbm.at[pl.ds(tok, 1)], xbuf.at[s, pl.ds(r, 1)], sem.at[s]).start()
            return c

        lax.fori_loop(0, EXPERT_TILE, body, 0, unroll=8)

    @pl.when(jnp.logical_and(i == 0, n_used > 0))
    def _():
        issue(0, 0)

    @pl.when(i + 1 < n_used)
    def _():
        issue(i + 1, 1 - slot)

    e = tile_e_ref[i]
    e_prev = tile_e_ref[jnp.maximum(i - 1, 0)]

    @pl.when(jnp.logical_and(i < n_used, jnp.logical_or(i == 0, e != e_prev)))
    def _():
        wgu_b[...] = wgu_ref[0, 0].astype(BF16)
        wd_b[...] = wd_ref[0, 0].astype(BF16)

    @pl.when(i < n_used)
    def _():
        pltpu.make_async_copy(x_hbm.at[pl.ds(0, EXPERT_TILE)], xbuf.at[slot], sem.at[slot]).wait()
        gu = jnp.dot(xbuf[slot].astype(BF16), wgu_b[...], preferred_element_type=F32)
        a = _silu(gu[:, :dh]) * gu[:, dh:]
        o_ref[...] = jnp.dot(a.astype(BF16), wd_b[...], preferred_element_type=F32)

    @pl.when(i >= n_used)
    def _():
        o_ref[...] = jnp.zeros_like(o_ref)


def _routed_experts(x, row_tok, tile_e, n_used, w_gu, w_down, layer):
    t, d = x.shape
    dh = w_down.shape[2]
    n_tiles = tile_e.shape[0]
    grid_spec = pltpu.PrefetchScalarGridSpec(
        num_scalar_prefetch=3,
        grid=(n_tiles,),
        in_specs=[pl.BlockSpec(memory_space=pl.ANY),
                  pl.BlockSpec((1, 1, d, 2 * dh), lambda i, rt, te, nu: (layer, te[i], 0, 0)),
                  pl.BlockSpec((1, 1, dh, d), lambda i, rt, te, nu: (layer, te[i], 0, 0))],
        out_specs=pl.BlockSpec((EXPERT_TILE, d), lambda i, rt, te, nu: (i, 0)),
        scratch_shapes=[pltpu.VMEM((2, EXPERT_TILE, d), F32),
                        pltpu.SemaphoreType.DMA((2,)),
                        pltpu.VMEM((d, 2 * dh), BF16),
                        pltpu.VMEM((dh, d), BF16)],
    )
    return pl.pallas_call(
        functools.partial(_expert_kernel, dh=dh),
        grid_spec=grid_spec,
        out_shape=jax.ShapeDtypeStruct((n_tiles * EXPERT_TILE, d), F32),
        compiler_params=_params("arbitrary"),
        name="routed_experts",
    )(row_tok, tile_e, n_used, x, w_gu, w_down)


COMBINE_TILE = 128


def _combine_kernel(pos_ref, rows_hbm, gate_ref, x_ref, sh_ref, g_ref, b_ref, o_ref, ob_ref, buf, sem):
    i = pl.program_id(0)
    n = pl.num_programs(0)
    slot = i % 2

    def issue(tile, s):
        base = tile * (COMBINE_TILE * TOP_K)

        def body(r, c):
            for k in range(TOP_K):
                p = pos_ref[base + r * TOP_K + k]
                pltpu.make_async_copy(rows_hbm.at[pl.ds(p, 1)], buf.at[s, k, pl.ds(r, 1)], sem.at[s]).start()
            return c

        lax.fori_loop(0, COMBINE_TILE, body, 0)

    @pl.when(i == 0)
    def _():
        issue(0, 0)

    @pl.when(i + 1 < n)
    def _():
        issue(i + 1, 1 - slot)

    for k in range(TOP_K):
        pltpu.make_async_copy(rows_hbm.at[pl.ds(0, COMBINE_TILE)], buf.at[slot, k], sem.at[slot]).wait()
    gate = gate_ref[...]
    z = DN_ALPHA * x_ref[...] + sh_ref[...]
    for k in range(TOP_K):
        z = z + gate[:, k:k + 1] * buf[slot, k]
    y = _layernorm_rows(z, g_ref[...], b_ref[...])
    o_ref[...] = y
    ob_ref[...] = y.astype(BF16)


def _combine_ln(rows, pos, gate, x, shared, g, b):
    t, d = x.shape
    row = lambda w: pl.BlockSpec((COMBINE_TILE, w), lambda i, p: (i, 0))
    vec = pl.BlockSpec((1, d), lambda i, p: (0, 0))
    grid_spec = pltpu.PrefetchScalarGridSpec(
        num_scalar_prefetch=1,
        grid=(t // COMBINE_TILE,),
        in_specs=[pl.BlockSpec(memory_space=pl.ANY), row(TOP_K), row(d), row(d), vec, vec],
        out_specs=[row(d), row(d)],
        scratch_shapes=[pltpu.VMEM((2, TOP_K, COMBINE_TILE, d), F32), pltpu.SemaphoreType.DMA((2,))],
    )
    return pl.pallas_call(
        _combine_kernel,
        grid_spec=grid_spec,
        out_shape=[jax.ShapeDtypeStruct((t, d), F32), jax.ShapeDtypeStruct((t, d), BF16)],
        compiler_params=_params("arbitrary"),
        name="combine_ln",
    )(pos, rows, gate, x, shared, g.reshape(1, d), b.reshape(1, d))


def _moe_ln(x, xb, w_router, b_router, w_gu, w_down, w_gu_s, w_down_s, ln_g, ln_b, layer):
    t = x.shape[0]
    top_e_t, gate_t = _router(x, w_router[layer].T, b_router[layer])
    top_e, gate = top_e_t.T, gate_t.T
    n_tiles = (t * TOP_K + N_EXPERTS * (EXPERT_TILE - 1) + EXPERT_TILE - 1) // EXPERT_TILE
    row_tok, tile_e, n_used, pos = _dispatch_tables(top_e, n_tiles)
    rows = _routed_experts(x, row_tok, tile_e, n_used, w_gu, w_down, layer)
    shared = _dense_ffn(xb, w_gu_s, w_down_s, layer)
    return _combine_ln(rows, pos, gate, x, shared, ln_g[layer], ln_b[layer])


SUB = 16
SEQ_BLOCK = 512


def _tri(n, strict=False):
    r = lax.broadcasted_iota(jnp.int32, (n, n), 0)
    c = lax.broadcasted_iota(jnp.int32, (n, n), 1)
    return (r > c) if strict else (r >= c)


def _cumsum_rows(x):
    n = x.shape[0]
    return jnp.dot(_tri(n).astype(F32), x, preferred_element_type=F32, precision=HI)


def _bdot(a, b):
    return jnp.dot(a.astype(BF16), b.astype(BF16), preferred_element_type=F32)


def _bdot_nt(a, b):
    return lax.dot_general(a.astype(BF16), b.astype(BF16), (((1,), (1,)), ((), ())), preferred_element_type=F32)


def _bdot_tn(a, b):
    return lax.dot_general(a.astype(BF16), b.astype(BF16), (((0,), (0,)), ((), ())), preferred_element_type=F32)


def _rms(x):
    return x * lax.rsqrt(jnp.mean(x * x, axis=-1, keepdims=True) + RMS_EPS)


def _row_of(col_b):
    c = col_b.shape[0]
    return col_b.T[0:1, :c]


A_HEADS = 16
A_DK = 128
A_DV = 128


def _hgrn2_chunk_kernel(q_ref, f_ref, v_ref, g_ref, lb_ref, gn_ref, o_ref, s_out_ref, s_ref, *, n_chunks):
    l = pl.program_id(2)

    @pl.when(l == 0)
    def _():
        s_ref[...] = jnp.zeros_like(s_ref)

    lb = lb_ref[...]
    gn = gn_ref[...]

    def chunk(j, carry):
        rows = pl.ds(pl.multiple_of(j * CHUNK, CHUNK), CHUNK)
        q = _silu(q_ref[rows, :])
        f = lb + (1.0 - lb) * jax.nn.sigmoid(f_ref[rows, :])
        k = 1.0 - f
        v = v_ref[rows, :]
        bc = _cumsum_rows(jnp.log(f))
        s = s_ref[...]
        o_inter = _bdot(q * jnp.exp(bc), s)
        outs = []
        for i in range(CHUNK // SUB):
            lo, hi = i * SUB, (i + 1) * SUB
            bref = bc[lo - 1:lo, :] if i > 0 else jnp.zeros((1, A_DK), F32)
            qi = q[lo:hi] * jnp.exp(bc[lo:hi] - bref)
            ki = k[:hi] * jnp.exp(bref - bc[:hi])
            att = _bdot_nt(qi, ki)
            r = lax.broadcasted_iota(jnp.int32, (SUB, hi), 0) + lo
            c = lax.broadcasted_iota(jnp.int32, (SUB, hi), 1)
            att = jnp.where(r >= c, att, 0.0)
            outs.append(_bdot(att, v[:hi]))
        o = jnp.concatenate(outs, axis=0) + o_inter
        b_end = bc[CHUNK - 1:CHUNK, :]
        d_col = jnp.exp(bc.T[:, CHUNK - 1:CHUNK])
        s_ref[...] = d_col * s + _bdot_tn(k * jnp.exp(b_end - bc), v)
        o_ref[rows, :] = (_rms(o) * gn * _silu(g_ref[rows, :])).astype(o_ref.dtype)
        return carry

    lax.fori_loop(0, n_chunks, chunk, 0)

    @pl.when(l == pl.num_programs(2) - 1)
    def _():
        s_out_ref[0, 0] = s_ref[...]


def _hgrn2_prompt(proj, b, l, lb, g_norm):
    lblk = _pick_tile(l, SEQ_BLOCK, CHUNK)
    nl = l // lblk
    part = lambda p: pl.BlockSpec((lblk, A_DK), lambda bi, h, li, p=p: (bi * nl + li, p * A_HEADS + h))
    return pl.pallas_call(
        functools.partial(_hgrn2_chunk_kernel, n_chunks=lblk // CHUNK),
        grid=(b, A_HEADS, nl),
        in_specs=[part(0), part(1), part(2), part(3),
                  pl.BlockSpec((1, A_DK), lambda bi, h, li: (0, h)),
                  pl.BlockSpec((1, A_DV), lambda bi, h, li: (0, 0))],
        out_specs=[pl.BlockSpec((lblk, A_DV), lambda bi, h, li: (bi * nl + li, h)),
                   pl.BlockSpec((1, 1, A_DK, A_DV), lambda bi, h, li: (bi, h, 0, 0))],
        out_shape=[jax.ShapeDtypeStruct((b * l, A_HEADS * A_DV), BF16),
                   jax.ShapeDtypeStruct((b, A_HEADS, A_DK, A_DV), F32)],
        scratch_shapes=[pltpu.VMEM((A_DK, A_DV), F32)],
        compiler_params=_params("parallel", "parallel", "arbitrary"),
        name="hgrn2_chunks",
    )(proj, proj, proj, proj, lb.reshape(1, -1), g_norm.reshape(1, -1))


HGRN2_STEP_BATCH = 4


def _hgrn2_step_kernel(p_ref, s_ref, lb_ref, gn_ref, o_ref, s_out_ref):
    lb = lb_ref[...]
    iota_h = lax.broadcasted_iota(jnp.int32, (A_HEADS, A_DV), 0)
    for bi in range(HGRN2_STEP_BATCH):
        p = p_ref[bi]
        q = _silu(p[0:A_HEADS])
        f = lb + (1.0 - lb) * jax.nn.sigmoid(p[A_HEADS:2 * A_HEADS])
        v = p[2 * A_HEADS:3 * A_HEADS]
        g = p[3 * A_HEADS:4 * A_HEADS]
        q_t, f_t = q.T, f.T
        o = jnp.zeros((A_HEADS, A_DV), F32)
        for h in range(A_HEADS):
            fc = f_t[:, h:h + 1]
            s_new = fc * s_ref[bi, h] + (1.0 - fc) * v[h:h + 1, :]
            s_out_ref[bi, h] = s_new
            oh = jnp.sum(q_t[:, h:h + 1] * s_new, axis=0, keepdims=True)
            o = jnp.where(iota_h == h, oh, o)
        o_ref[bi] = (_rms(o) * gn_ref[...] * _silu(g)).astype(o_ref.dtype)


def _hgrn2_step(proj, s0, lb, g_norm):
    b = proj.shape[0]
    bb = HGRN2_STEP_BATCH
    p3 = proj.reshape(b, 4 * A_HEADS, A_DK)
    o, s = pl.pallas_call(
        _hgrn2_step_kernel,
        grid=(b // bb,),
        in_specs=[pl.BlockSpec((bb, 4 * A_HEADS, A_DK), lambda i: (i, 0, 0)),
                  pl.BlockSpec((bb, A_HEADS, A_DK, A_DV), lambda i: (i, 0, 0, 0)),
                  pl.BlockSpec((A_HEADS, A_DK), lambda i: (0, 0)),
                  pl.BlockSpec((1, A_DV), lambda i: (0, 0))],
        out_specs=[pl.BlockSpec((bb, A_HEADS, A_DV), lambda i: (i, 0, 0)),
                   pl.BlockSpec((bb, A_HEADS, A_DK, A_DV), lambda i: (i, 0, 0, 0))],
        out_shape=[jax.ShapeDtypeStruct((b, A_HEADS, A_DV), BF16),
                   jax.ShapeDtypeStruct(s0.shape, F32)],
        compiler_params=_params("parallel"),
        name="hgrn2_step",
    )(p3, s0, lb.reshape(A_HEADS, A_DK), g_norm.reshape(1, -1))
    return o.reshape(b, A_HEADS * A_DV), s


def _hgrn2_lower_bound(lb_logits, layer):
    return jnp.cumsum(jax.nn.softmax(lb_logits.astype(F32), axis=0), axis=0)[layer]


C_HEADS = 8
C_DK = 128
C_DV = 256
C_QK_DIM = C_HEADS * C_DK
C_V_DIM = C_HEADS * C_DV
C_GATE_CAP = 15.0


def _log_sigmoid(x):
    return jnp.minimum(x, 0.0) - jnp.log(1.0 + jnp.exp(-jnp.abs(x)))


def _cap(x):
    return C_GATE_CAP * jnp.tanh(x / C_GATE_CAP)


def _mlstm_chunk_kernel(q_ref, k_ref, v_ref, op_ref, gt_ref, bg_ref, gn_ref,
                        o_ref, c_out_ref, n_out_ref, m_out_ref, c_ref, n_ref, m_ref, *, n_chunks):
    h = pl.program_id(1)
    l = pl.program_id(2)

    @pl.when(l == 0)
    def _():
        c_ref[...] = jnp.zeros_like(c_ref)
        n_ref[...] = jnp.zeros_like(n_ref)
        m_ref[...] = jnp.zeros_like(m_ref)

    lane = lax.broadcasted_iota(jnp.int32, (CHUNK, 2 * C_HEADS), 1)
    tri = _tri(CHUNK)

    def chunk(j, carry):
        rows = pl.ds(pl.multiple_of(j * CHUNK, CHUNK), CHUNK)
        gates = _cap(gt_ref[rows, :] + bg_ref[...])
        i_col = jnp.sum(jnp.where(lane == h, gates, 0.0), axis=-1, keepdims=True)
        f_col = jnp.sum(jnp.where(lane == h + C_HEADS, gates, 0.0), axis=-1, keepdims=True)
        i_b = jnp.broadcast_to(i_col, (CHUNK, 128))
        f_b = _cumsum_rows(jnp.broadcast_to(_log_sigmoid(f_col), (CHUNK, 128)))
        a_b = i_b - f_b
        f_c = f_b[:, 0:1]
        log_w = jnp.where(tri, f_c + _row_of(a_b), -jnp.inf)
        m_prev = m_ref[0:1, 0:1]
        log_s = f_c + m_prev
        m_t = jnp.maximum(jnp.max(log_w, axis=-1, keepdims=True), log_s)
        qs = q_ref[rows, :] * (C_DK ** -0.5)
        k = k_ref[rows, :]
        v = v_ref[rows, :]
        qk = _bdot_nt(qs, k) * jnp.exp(log_w - m_t)
        w_s = jnp.exp(log_s - m_t)
        num = _bdot(qk, v) + w_s * _bdot(qs, c_ref[...])
        den = jnp.sum(qk, axis=-1, keepdims=True) + w_s * jnp.sum(qs * n_ref[...], axis=-1, keepdims=True)
        hid = num / jnp.maximum(jnp.abs(den), jnp.exp(-m_t))
        m_end = m_t[CHUNK - 1:CHUNK, :]
        f_end = f_c[CHUNK - 1:CHUNK, :]
        w_end = jnp.exp(f_end + a_b[:, 0:1] - m_end)
        s_end = jnp.exp(f_end + m_prev - m_end)
        kw = k * w_end
        c_ref[...] = s_end * c_ref[...] + _bdot_tn(kw, v)
        n_ref[...] = s_end * n_ref[...] + jnp.sum(kw, axis=0, keepdims=True)
        m_ref[...] = jnp.broadcast_to(m_end, m_ref.shape)
        o_ref[rows, :] = (_rms(hid) * gn_ref[0] * jax.nn.sigmoid(op_ref[rows, :])).astype(o_ref.dtype)
        return carry

    lax.fori_loop(0, n_chunks, chunk, 0)

    @pl.when(l == pl.num_programs(2) - 1)
    def _():
        c_out_ref[0, 0] = c_ref[...]
        n_out_ref[0, 0] = n_ref[...]
        m_out_ref[0, 0] = m_ref[...]


def _mlstm_prompt(proj, gates, b, l, b_gate, g_norm):
    lblk = _pick_tile(l, SEQ_BLOCK, CHUNK)
    nl = l // lblk
    qk_blocks = C_QK_DIM // C_DK
    v_off = 2 * C_QK_DIM // C_DV
    o, c, n, m = pl.pallas_call(
        functools.partial(_mlstm_chunk_kernel, n_chunks=lblk // CHUNK),
        grid=(b, C_HEADS, nl),
        in_specs=[pl.BlockSpec((lblk, C_DK), lambda bi, h, li: (bi * nl + li, h)),
                  pl.BlockSpec((lblk, C_DK), lambda bi, h, li: (bi * nl + li, qk_blocks + h)),
                  pl.BlockSpec((lblk, C_DV), lambda bi, h, li: (bi * nl + li, v_off + h)),
                  pl.BlockSpec((lblk, C_DV), lambda bi, h, li: (bi * nl + li, v_off + C_HEADS + h)),
                  pl.BlockSpec((lblk, 2 * C_HEADS), lambda bi, h, li: (bi * nl + li, 0)),
                  pl.BlockSpec((1, 2 * C_HEADS), lambda bi, h, li: (0, 0)),
                  pl.BlockSpec((1, 1, C_DV), lambda bi, h, li: (h, 0, 0))],
        out_specs=[pl.BlockSpec((lblk, C_DV), lambda bi, h, li: (bi * nl + li, h)),
                   pl.BlockSpec((1, 1, C_DK, C_DV), lambda bi, h, li: (bi, h, 0, 0)),
                   pl.BlockSpec((1, 1, 1, C_DK), lambda bi, h, li: (bi, h, 0, 0)),
                   pl.BlockSpec((1, 1, 1, 128), lambda bi, h, li: (bi, h, 0, 0))],
        out_shape=[jax.ShapeDtypeStruct((b * l, C_V_DIM), BF16),
                   jax.ShapeDtypeStruct((b, C_HEADS, C_DK, C_DV), F32),
                   jax.ShapeDtypeStruct((b, C_HEADS, 1, C_DK), F32),
                   jax.ShapeDtypeStruct((b, C_HEADS, 1, 128), F32)],
        scratch_shapes=[pltpu.VMEM((C_DK, C_DV), F32), pltpu.VMEM((1, C_DK), F32), pltpu.VMEM((1, 128), F32)],
        compiler_params=_params("parallel", "parallel", "arbitrary"),
        name="mlstm_chunks",
    )(proj, proj, proj, proj, gates, b_gate.reshape(1, -1), g_norm.reshape(C_HEADS, 1, C_DV))
    return o, c, n[:, :, 0, :], m[:, :, 0, 0]


MLSTM_STEP_BATCH = 8


def _mlstm_step_kernel(qk_ref, vo_ref, gt_ref, bg_ref, gn_ref, c_ref, n_ref, m_ref,
                       o_ref, c_out_ref, n_out_ref, m_out_ref):
    iota_h = lax.broadcasted_iota(jnp.int32, (C_HEADS, C_DV), 0)
    iota_hk = lax.broadcasted_iota(jnp.int32, (C_HEADS, C_DK), 0)
    iota_m = lax.broadcasted_iota(jnp.int32, (MLSTM_STEP_BATCH, C_HEADS), 1)
    iota_mb = lax.broadcasted_iota(jnp.int32, (MLSTM_STEP_BATCH, C_HEADS), 0)
    gates = _cap(gt_ref[...] + bg_ref[...])
    m_all = m_ref[...]
    m_new_all = jnp.zeros((MLSTM_STEP_BATCH, C_HEADS), F32)
    for bi in range(MLSTM_STEP_BATCH):
        qs = qk_ref[bi, 0:C_HEADS, :] * (C_DK ** -0.5)
        k = qk_ref[bi, C_HEADS:2 * C_HEADS, :]
        v = vo_ref[bi, C_HEADS:2 * C_HEADS, :]
        op = vo_ref[bi, 2 * C_HEADS:3 * C_HEADS, :]
        q_t, k_t = qs.T, k.T
        n_all = n_ref[bi]
        hid = jnp.zeros((C_HEADS, C_DV), F32)
        n_new = jnp.zeros((C_HEADS, C_DK), F32)
        for h in range(C_HEADS):
            i_g = gates[bi:bi + 1, h:h + 1]
            lf = _log_sigmoid(gates[bi:bi + 1, C_HEADS + h:C_HEADS + h + 1])
            log_s = lf + m_all[bi:bi + 1, h:h + 1]
            m_t = jnp.maximum(i_g, log_s)
            w_i = jnp.exp(i_g - m_t)
            w_s = jnp.exp(log_s - m_t)
            qk = jnp.sum(qs[h:h + 1, :] * k[h:h + 1, :], axis=-1, keepdims=True) * w_i
            c_old = c_ref[bi, h]
            num = qk * v[h:h + 1, :] + w_s * jnp.sum(q_t[:, h:h + 1] * c_old, axis=0, keepdims=True)
            den = qk + w_s * jnp.sum(qs[h:h + 1, :] * n_all[h:h + 1, :], axis=-1, keepdims=True)
            hid = jnp.where(iota_h == h, num / jnp.maximum(jnp.abs(den), jnp.exp(-m_t)), hid)
            c_out_ref[bi, h] = w_s * c_old + (w_i * k_t[:, h:h + 1]) * v[h:h + 1, :]
            n_new = jnp.where(iota_hk == h, w_s * n_all[h:h + 1, :] + w_i * k[h:h + 1, :], n_new)
            m_new_all = jnp.where(jnp.logical_and(iota_m == h, iota_mb == bi), m_t, m_new_all)
        n_out_ref[bi] = n_new
        o_ref[bi] = (_rms(hid) * gn_ref[...] * jax.nn.sigmoid(op)).astype(o_ref.dtype)
    m_out_ref[...] = m_new_all


def _mlstm_step(proj, gates, b_gate, g_norm, c0, n0, m0):
    b = proj.shape[0]
    bb = MLSTM_STEP_BATCH
    qk_view = proj.reshape(b, -1, C_DK)
    vo_view = proj.reshape(b, -1, C_DV)
    o, c, n, m = pl.pallas_call(
        _mlstm_step_kernel,
        grid=(b // bb,),
        in_specs=[pl.BlockSpec((bb, 2 * C_HEADS, C_DK), lambda i: (i, 0, 0)),
                  pl.BlockSpec((bb, 3 * C_HEADS, C_DV), lambda i: (i, 0, 0)),
                  pl.BlockSpec((bb, 2 * C_HEADS), lambda i: (i, 0)),
                  pl.BlockSpec((1, 2 * C_HEADS), lambda i: (0, 0)),
                  pl.BlockSpec((C_HEADS, C_DV), lambda i: (0, 0)),
                  pl.BlockSpec((bb, C_HEADS, C_DK, C_DV), lambda i: (i, 0, 0, 0)),
                  pl.BlockSpec((bb, C_HEADS, C_DK), lambda i: (i, 0, 0)),
                  pl.BlockSpec((bb, C_HEADS), lambda i: (i, 0))],
        out_specs=[pl.BlockSpec((bb, C_HEADS, C_DV), lambda i: (i, 0, 0)),
                   pl.BlockSpec((bb, C_HEADS, C_DK, C_DV), lambda i: (i, 0, 0, 0)),
                   pl.BlockSpec((bb, C_HEADS, C_DK), lambda i: (i, 0, 0)),
                   pl.BlockSpec((bb, C_HEADS), lambda i: (i, 0))],
        out_shape=[jax.ShapeDtypeStruct((b, C_HEADS, C_DV), BF16),
                   jax.ShapeDtypeStruct(c0.shape, F32),
                   jax.ShapeDtypeStruct(n0.shape, F32),
                   jax.ShapeDtypeStruct(m0.shape, F32)],
        compiler_params=_params("parallel"),
        name="mlstm_step",
    )(qk_view, vo_view, gates, b_gate.reshape(1, -1), g_norm, c0, n0, m0)
    return o.reshape(b, C_V_DIM), c, n, m


B_QK_HEADS = 16
B_V_HEADS = 32
B_DK = 128
B_DV = 128
B_CONV = 4
B_QK_DIM = B_QK_HEADS * B_DK
B_V_DIM = B_V_HEADS * B_DV
B_CONV_DIM = 2 * B_QK_DIM + B_V_DIM
B_REP = B_V_HEADS // B_QK_HEADS
CONV_PAD = 8


def _softplus(x):
    return jnp.maximum(x, 0.0) + jnp.log(1.0 + jnp.exp(-jnp.abs(x)))


def _l2norm(x):
    return x * lax.rsqrt(jnp.sum(x * x, axis=-1, keepdims=True) + RMS_EPS)


def _unit_lower_inverse(a):
    c = a.shape[0]
    hdot = lambda x, y: jnp.dot(x, y, preferred_element_type=F32, precision=HI)
    eye = (lax.broadcasted_iota(jnp.int32, (c, c), 0) == lax.broadcasted_iota(jnp.int32, (c, c), 1)).astype(F32)
    p = -a
    t = eye + p
    span = 1
    while 2 * span < c:
        p = hdot(p, p)
        t = t + hdot(t, p)
        span *= 2
    return t


def _gdn_chunk_kernel(q_ref, k_ref, v_ref, z_ref, ba_ref, wq_ref, wk_ref, wv_ref, alog_ref, dtb_ref, gn_ref,
                      o_ref, s_out_ref, s_ref, qbuf, kbuf, vbuf, *, lblk):
    h = pl.program_id(1)
    l = pl.program_id(2)
    n_chunks = lblk // CHUNK

    @pl.when(l == 0)
    def _():
        s_ref[...] = jnp.zeros_like(s_ref)
        qbuf[0:CONV_PAD, :] = jnp.zeros((CONV_PAD, B_DK), F32)
        kbuf[0:CONV_PAD, :] = jnp.zeros((CONV_PAD, B_DK), F32)
        vbuf[0:CONV_PAD, :] = jnp.zeros((CONV_PAD, B_REP * B_DV), F32)

    def conv(x_ref, w_ref, buf):
        buf[CONV_PAD:CONV_PAD + lblk, :] = x_ref[...]
        y = w_ref[0:1, :] * buf[CONV_PAD - 3:CONV_PAD - 3 + lblk, :]
        for j in range(1, B_CONV):
            y = y + w_ref[j:j + 1, :] * buf[CONV_PAD - 3 + j:CONV_PAD - 3 + j + lblk, :]
        tail = buf[lblk:lblk + CONV_PAD, :]
        buf[CONV_PAD:CONV_PAD + lblk, :] = _silu(y)
        return tail

    q_tail = conv(q_ref, wq_ref, qbuf)
    k_tail = conv(k_ref, wk_ref, kbuf)
    v_tail = conv(v_ref, wv_ref, vbuf)

    lane = lax.broadcasted_iota(jnp.int32, (CHUNK, 2 * B_V_HEADS), 1)
    lane_h = lax.broadcasted_iota(jnp.int32, (1, B_V_HEADS), 1)
    tri = _tri(CHUNK)
    strict = _tri(CHUNK, strict=True)

    def chunk(j, carry):
        rows = pl.ds(pl.multiple_of(j * CHUNK, CHUNK), CHUNK)
        brow = pl.ds(pl.multiple_of(j * CHUNK + CONV_PAD, 8), CHUNK)
        q = _l2norm(qbuf[brow, :]) * (B_DK ** -0.5)
        k = _l2norm(kbuf[brow, :])
        v2 = vbuf[brow, :]
        kk = _bdot_nt(k, k)
        qk = _bdot_nt(q, k)
        ba = ba_ref[rows, :]
        for e in range(B_REP):
            hv = h * B_REP + e
            beta = jax.nn.sigmoid(jnp.sum(jnp.where(lane == hv, ba, 0.0), axis=-1, keepdims=True))
            a_pre = jnp.sum(jnp.where(lane == hv + B_V_HEADS, ba, 0.0), axis=-1, keepdims=True)
            a_scale = -jnp.exp(jnp.sum(jnp.where(lane_h == hv, alog_ref[...], 0.0), axis=-1, keepdims=True))
            dt_b = jnp.sum(jnp.where(lane_h == hv, dtb_ref[...], 0.0), axis=-1, keepdims=True)
            g = a_scale * _softplus(a_pre + dt_b)
            g_b = _cumsum_rows(jnp.broadcast_to(g, (CHUNK, 128)))
            g_c = g_b[:, 0:1]
            dec = jnp.exp(jnp.where(tri, g_c - _row_of(g_b), -jnp.inf))
            a = jnp.where(strict, beta * dec * kk, 0.0)
            t_inv = _unit_lower_inverse(a)
            v = v2[:, e * B_DV:(e + 1) * B_DV]
            rhs = jnp.concatenate([beta * v, (beta * jnp.exp(g_c)) * k], axis=-1)
            sol = jnp.dot(t_inv, rhs, preferred_element_type=F32, precision=HI)
            s = s_ref[e]
            w = sol[:, :B_DV] - _bdot(sol[:, B_DV:], s)
            o = _bdot(dec * qk, w) + _bdot(q * jnp.exp(g_c), s)
            g_end = g_c[CHUNK - 1:CHUNK, :]
            s_ref[e] = jnp.exp(g_end) * s + _bdot_tn(k * jnp.exp(g_end - g_c), w)
            zc = z_ref[rows, e * B_DV:(e + 1) * B_DV]
            o_ref[rows, e * B_DV:(e + 1) * B_DV] = (_rms(o) * gn_ref[...] * _silu(zc)).astype(o_ref.dtype)
        return carry

    lax.fori_loop(0, n_chunks, chunk, 0)
    qbuf[0:CONV_PAD, :] = q_tail
    kbuf[0:CONV_PAD, :] = k_tail
    vbuf[0:CONV_PAD, :] = v_tail

    @pl.when(l == pl.num_programs(2) - 1)
    def _():
        s_out_ref[0] = s_ref[...]


def _gdn_prompt(proj, ba, b, l, conv_w, a_log, dt_bias, g_norm):
    lblk = _pick_tile(l, SEQ_BLOCK, CHUNK)
    nl = l // lblk
    vw = B_REP * B_DV
    k_off = B_QK_DIM // B_DK
    v_off = 2 * B_QK_DIM // vw
    z_off = B_CONV_DIM // vw
    return pl.pallas_call(
        functools.partial(_gdn_chunk_kernel, lblk=lblk),
        grid=(b, B_QK_HEADS, nl),
        in_specs=[pl.BlockSpec((lblk, B_DK), lambda bi, h, li: (bi * nl + li, h)),
                  pl.BlockSpec((lblk, B_DK), lambda bi, h, li: (bi * nl + li, k_off + h)),
                  pl.BlockSpec((lblk, vw), lambda bi, h, li: (bi * nl + li, v_off + h)),
                  pl.BlockSpec((lblk, vw), lambda bi, h, li: (bi * nl + li, z_off + h)),
                  pl.BlockSpec((lblk, 2 * B_V_HEADS), lambda bi, h, li: (bi * nl + li, 0)),
                  pl.BlockSpec((B_CONV, B_DK), lambda bi, h, li: (0, h)),
                  pl.BlockSpec((B_CONV, B_DK), lambda bi, h, li: (0, k_off + h)),
                  pl.BlockSpec((B_CONV, vw), lambda bi, h, li: (0, v_off + h)),
                  pl.BlockSpec((1, B_V_HEADS), lambda bi, h, li: (0, 0)),
                  pl.BlockSpec((1, B_V_HEADS), lambda bi, h, li: (0, 0)),
                  pl.BlockSpec((1, B_DV), lambda bi, h, li: (0, 0))],
        out_specs=[pl.BlockSpec((lblk, vw), lambda bi, h, li: (bi * nl + li, h)),
                   pl.BlockSpec((1, B_REP, B_DK, B_DV), lambda bi, h, li: (bi, h, 0, 0))],
        out_shape=[jax.ShapeDtypeStruct((b * l, B_V_DIM), BF16),
                   jax.ShapeDtypeStruct((b, B_V_HEADS, B_DK, B_DV), F32)],
        scratch_shapes=[pltpu.VMEM((B_REP, B_DK, B_DV), F32),
                        pltpu.VMEM((CONV_PAD + lblk, B_DK), F32),
                        pltpu.VMEM((CONV_PAD + lblk, B_DK), F32),
                        pltpu.VMEM((CONV_PAD + lblk, vw), F32)],
        compiler_params=_params("parallel", "parallel", "arbitrary"),
        name="gdn_chunks",
    )(proj, proj, proj, proj, ba, conv_w, conv_w, conv_w,
      a_log.reshape(1, -1), dt_bias.reshape(1, -1), g_norm.reshape(1, -1))


GDN_STEP_BATCH = 2
B_ROWS = B_CONV_DIM // 128


def _gdn_step_kernel(p_ref, cv_ref, cw_ref, ba_ref, alog_ref, dtb_ref, gn_ref, s_ref, o_ref, s_out_ref):
    ba = ba_ref[0]
    beta_all = jax.nn.sigmoid(ba[:, :B_V_HEADS])
    g_all = -jnp.exp(alog_ref[...]) * _softplus(ba[:, B_V_HEADS:] + dtb_ref[...])
    decay_all = jnp.exp(g_all)
    iota_h = lax.broadcasted_iota(jnp.int32, (B_V_HEADS, B_DV), 0)
    for bi in range(GDN_STEP_BATCH):
        y = cw_ref[B_CONV - 1] * p_ref[bi, 0:B_ROWS, :]
        for j in range(B_CONV - 1):
            y = y + cw_ref[j] * cv_ref[bi, j]
        y = _silu(y)
        q = _l2norm(y[0:B_QK_HEADS]) * (B_DK ** -0.5)
        k = _l2norm(y[B_QK_HEADS:2 * B_QK_HEADS])
        v = y[2 * B_QK_HEADS:]
        z = p_ref[bi, B_ROWS:B_ROWS + B_V_HEADS, :]
        q_t, k_t = q.T, k.T
        qk = jnp.sum(q * k, axis=-1, keepdims=True)
        o = jnp.zeros((B_V_HEADS, B_DV), F32)
        for hv in range(B_V_HEADS):
            hq = hv // B_REP
            beta = beta_all[bi:bi + 1, hv:hv + 1]
            decay = decay_all[bi:bi + 1, hv:hv + 1]
            s = s_ref[bi, hv]
            k_col = k_t[:, hq:hq + 1]
            w = beta * v[hv:hv + 1, :] - (beta * decay) * jnp.sum(k_col * s, axis=0, keepdims=True)
            oh = qk[hq:hq + 1, :] * w + decay * jnp.sum(q_t[:, hq:hq + 1] * s, axis=0, keepdims=True)
            s_out_ref[bi, hv] = decay * s + k_col * w
            o = jnp.where(iota_h == hv, oh, o)
        o_ref[bi] = (_rms(o) * gn_ref[...] * _silu(z)).astype(o_ref.dtype)


def _gdn_step(proj, ba, conv0, conv_w, a_log, dt_bias, g_norm, s0):
    b = proj.shape[0]
    bb = GDN_STEP_BATCH
    p3 = proj.reshape(b, -1, 128)
    o, s = pl.pallas_call(
        _gdn_step_kernel,
        grid=(b // bb,),
        in_specs=[pl.BlockSpec((bb, p3.shape[1], 128), lambda i: (i, 0, 0)),
                  pl.BlockSpec((bb, B_CONV - 1, B_ROWS, 128), lambda i: (i, 0, 0, 0)),
                  pl.BlockSpec((B_CONV, B_ROWS, 128), lambda i: (0, 0, 0)),
                  pl.BlockSpec((1, bb, 2 * B_V_HEADS), lambda i: (i, 0, 0)),
                  pl.BlockSpec((1, B_V_HEADS), lambda i: (0, 0)),
                  pl.BlockSpec((1, B_V_HEADS), lambda i: (0, 0)),
                  pl.BlockSpec((1, B_DV), lambda i: (0, 0)),
                  pl.BlockSpec((bb, B_V_HEADS, B_DK, B_DV), lambda i: (i, 0, 0, 0))],
        out_specs=[pl.BlockSpec((bb, B_V_HEADS, B_DV), lambda i: (i, 0, 0)),
                   pl.BlockSpec((bb, B_V_HEADS, B_DK, B_DV), lambda i: (i, 0, 0, 0))],
        out_shape=[jax.ShapeDtypeStruct((b, B_V_HEADS, B_DV), BF16), jax.ShapeDtypeStruct(s0.shape, F32)],
        compiler_params=_params("parallel"),
        name="gdn_step",
    )(p3, conv0.reshape(b, B_CONV - 1, B_ROWS, 128), conv_w.reshape(B_CONV, B_ROWS, 128),
      ba.reshape(b // bb, bb, 2 * B_V_HEADS), a_log.reshape(1, -1), dt_bias.reshape(1, -1), g_norm.reshape(1, -1), s0)
    return o.reshape(b, B_V_DIM), s


D_GROUP = 16
D_GROUPS = D_MODEL // D_GROUP
D_STATE = 64
S5_LANES = 128
S5_GROUPS = S5_LANES // D_GROUP
S5_HALF = S5_GROUPS * D_STATE
S5_BLOCKS = D_MODEL // S5_LANES
S5_SEGMENTS = 8


def _cmul(ar, ai, br, bi):
    return ar * br - ai * bi, ar * bi + ai * br


def _gelu_tanh(x):
    return 0.5 * x * (1.0 + jnp.tanh(math.sqrt(2.0 / math.pi) * (x + 0.044715 * (x * x * x))))


def _s5_tables(a_re, a_im, b_re, b_im, c_re, c_im, log_dt, seg_len):
    lam_re, lam_im = a_re.astype(F32), a_im.astype(F32)
    dt = jnp.exp(log_dt.astype(F32))[:, None]
    mag = jnp.exp(dt * lam_re)
    ab_re, ab_im = mag * jnp.cos(dt * lam_im), mag * jnp.sin(dt * lam_im)
    inv = 1.0 / (lam_re * lam_re + lam_im * lam_im)
    e_re, e_im = _cmul(ab_re - 1.0, ab_im, lam_re * inv, -lam_im * inv)
    bb_re, bb_im = _cmul(e_re[..., None], e_im[..., None], b_re.astype(F32), b_im.astype(F32))
    j = S5_BLOCKS
    eye = jnp.eye(S5_GROUPS, dtype=F32)

    def pack_rows(t):
        return t.reshape(j, 1, S5_HALF)

    def bdiag_in(t):
        t = t.reshape(j, S5_GROUPS, D_STATE, D_GROUP)
        return jnp.einsum('jgpc,gh->jgchp', t, eye).reshape(j, S5_LANES, S5_HALF)

    def bdiag_out(t):
        t = t.reshape(j, S5_GROUPS, D_GROUP, D_STATE)
        return jnp.einsum('jgcp,gh->jgphc', t, eye).reshape(j, S5_HALF, S5_LANES)

    ab = jnp.concatenate([pack_rows(ab_re), pack_rows(ab_im)], axis=-1)
    w_bu = jnp.concatenate([bdiag_in(bb_re), bdiag_in(bb_im)], axis=-1)
    w_c = jnp.concatenate([bdiag_out(c_re.astype(F32)), -bdiag_out(c_im.astype(F32))], axis=1)
    pr, pi = pack_rows(ab_re), pack_rows(ab_im)
    n = 1
    while n < seg_len:
        tr, ti = pr[:, n - 1:n], pi[:, n - 1:n]
        nr, ni = _cmul(pr, pi, tr, ti)
        pr, pi = jnp.concatenate([pr, nr], axis=1), jnp.concatenate([pi, ni], axis=1)
        n *= 2
    pw = jnp.concatenate([pr[:, :seg_len], pi[:, :seg_len]], axis=-1)
    return ab, w_bu, w_c, pw


def _s5_scan_kernel(x_ref, ab_ref, wbu_ref, wc_ref, pw_ref, dsk_ref, y_ref, s_out_ref, buf, *, seq, seg):
    nc = S5_HALF // 128
    x = x_ref[...]
    bu = _bdot(x, wbu_ref[0])
    for c in range(2 * nc):
        buf[c] = bu[:, c * 128:(c + 1) * 128]

    for c in range(nc):
        a_re = jnp.broadcast_to(ab_ref[0, :, c * 128:(c + 1) * 128], (S5_SEGMENTS, 128))
        a_im = jnp.broadcast_to(ab_ref[0, :, S5_HALF + c * 128:S5_HALF + (c + 1) * 128], (S5_SEGMENTS, 128))

        def step(t, carry, c=c, a_re=a_re, a_im=a_im):
            s_re, s_im = carry
            rows = pl.ds(t, S5_SEGMENTS, stride=seg)
            n_re = a_re * s_re - a_im * s_im + buf[c, rows, :]
            n_im = a_re * s_im + a_im * s_re + buf[nc + c, rows, :]
            buf[c, rows, :] = n_re
            buf[nc + c, rows, :] = n_im
            return n_re, n_im

        zeros = jnp.zeros((S5_SEGMENTS, 128), F32)
        lax.fori_loop(0, seg, step, (zeros, zeros), unroll=4)

        p_re = pw_ref[0, :, c * 128:(c + 1) * 128]
        p_im = pw_ref[0, :, S5_HALF + c * 128:S5_HALF + (c + 1) * 128]
        pe_re, pe_im = p_re[seg - 1:seg], p_im[seg - 1:seg]
        c_re, c_im = buf[c, seg - 1:seg, :], buf[nc + c, seg - 1:seg, :]
        for r in range(1, S5_SEGMENTS):
            lo = r * seg
            l_re, l_im = buf[c, lo:lo + seg, :], buf[nc + c, lo:lo + seg, :]
            f_re, f_im = _cmul(p_re, p_im, c_re, c_im)
            buf[c, lo:lo + seg, :] = l_re + f_re
            buf[nc + c, lo:lo + seg, :] = l_im + f_im
            e_re, e_im = _cmul(pe_re, pe_im, c_re, c_im)
            c_re, c_im = l_re[seg - 1:seg] + e_re, l_im[seg - 1:seg] + e_im

    y = dsk_ref[...] * x
    for c in range(2 * nc):
        s_out_ref[0, 0, :, c * 128:(c + 1) * 128] = buf[c, seq - 1:seq, :]
        y = y + _bdot(buf[c], wc_ref[0, c * 128:(c + 1) * 128, :])
    y_ref[...] = _gelu_tanh(y).astype(y_ref.dtype)


def _unpack_state(s):
    b = s.shape[0]
    return (s[..., :S5_HALF].reshape(b, D_GROUPS, D_STATE), s[..., S5_HALF:].reshape(b, D_GROUPS, D_STATE))


def _s5_prompt(x, b, l, tables, d_skip):
    d = x.shape[1]
    ab, w_bu, w_c, pw = tables
    seg = l // S5_SEGMENTS
    y, s = pl.pallas_call(
        functools.partial(_s5_scan_kernel, seq=l, seg=seg),
        grid=(b, S5_BLOCKS),
        in_specs=[pl.BlockSpec((l, S5_LANES), lambda bi, j: (bi, j)),
                  pl.BlockSpec((1, 1, 2 * S5_HALF), lambda bi, j: (j, 0, 0)),
                  pl.BlockSpec((1, S5_LANES, 2 * S5_HALF), lambda bi, j: (j, 0, 0)),
                  pl.BlockSpec((1, 2 * S5_HALF, S5_LANES), lambda bi, j: (j, 0, 0)),
                  pl.BlockSpec((1, seg, 2 * S5_HALF), lambda bi, j: (j, 0, 0)),
                  pl.BlockSpec((1, S5_LANES), lambda bi, j: (0, j))],
        out_specs=[pl.BlockSpec((l, S5_LANES), lambda bi, j: (bi, j)),
                   pl.BlockSpec((1, 1, 1, 2 * S5_HALF), lambda bi, j: (bi, j, 0, 0))],
        out_shape=[jax.ShapeDtypeStruct((b * l, d), BF16),
                   jax.ShapeDtypeStruct((b, S5_BLOCKS, 1, 2 * S5_HALF), F32)],
        scratch_shapes=[pltpu.VMEM((2 * S5_HALF // 128, l, 128), F32)],
        compiler_params=_params("parallel", "parallel"),
        name="s5_scan",
    )(x, ab, w_bu, w_c, pw, d_skip.reshape(1, d))
    return y, _unpack_state(s[:, :, 0, :])


def _s5_step_kernel(x_ref, s_ref, ab_ref, wbu_ref, wc_ref, dsk_ref, y_ref, s_out_ref):
    hf = S5_HALF
    x = x_ref[...]
    bu = _bdot(x, wbu_ref[0])
    s0 = s_ref[0]
    i_re, i_im = _cmul(ab_ref[0, :, :hf], ab_ref[0, :, hf:], s0[:, :hf], s0[:, hf:])
    s_new = jnp.concatenate([i_re, i_im], axis=-1) + bu
    s_out_ref[0] = s_new
    y = _bdot(s_new, wc_ref[0]) + dsk_ref[...] * x
    y_ref[...] = _gelu_tanh(y).astype(y_ref.dtype)


def _s5_step(x, tables, d_skip, s0_re, s0_im):
    b, d = x.shape
    ab, w_bu, w_c, _ = tables
    s0 = jnp.concatenate([s0_re.reshape(b, S5_BLOCKS, S5_HALF), s0_im.reshape(b, S5_BLOCKS, S5_HALF)], axis=-1)
    s0 = jnp.swapaxes(s0, 0, 1)
    y, s = pl.pallas_call(
        _s5_step_kernel,
        grid=(S5_BLOCKS,),
        in_specs=[pl.BlockSpec((b, S5_LANES), lambda j: (0, j)),
                  pl.BlockSpec((1, b, 2 * S5_HALF), lambda j: (j, 0, 0)),
                  pl.BlockSpec((1, 1, 2 * S5_HALF), lambda j: (j, 0, 0)),
                  pl.BlockSpec((1, S5_LANES, 2 * S5_HALF), lambda j: (j, 0, 0)),
                  pl.BlockSpec((1, 2 * S5_HALF, S5_LANES), lambda j: (j, 0, 0)),
                  pl.BlockSpec((1, S5_LANES), lambda j: (0, j))],
        out_specs=[pl.BlockSpec((b, S5_LANES), lambda j: (0, j)),
                   pl.BlockSpec((1, b, 2 * S5_HALF), lambda j: (j, 0, 0))],
        out_shape=[jax.ShapeDtypeStruct((b, d), BF16), jax.ShapeDtypeStruct((S5_BLOCKS, b, 2 * S5_HALF), F32)],
        compiler_params=_params("parallel"),
        name="s5_step",
    )(x, s0, ab, w_bu, w_c, d_skip.reshape(1, d))
    return y, _unpack_state(jnp.swapaxes(s, 0, 1))


def _glu_kernel(x_ref, w1_ref, w2_ref, o_ref):
    x = x_ref[...]
    y1 = jnp.dot(x, w1_ref[...].astype(BF16), preferred_element_type=F32)
    y2 = jnp.dot(x, w2_ref[...].astype(BF16), preferred_element_type=F32)
    o_ref[...] = y1 * jax.nn.sigmoid(y2)


def _glu_matmul(x, w, tm_target=1664, tn_target=256):
    m, k = x.shape
    n = w.shape[1] // 2
    tm = _pick_tile(m, tm_target, 16)
    tn = _pick_tile(n, tn_target, 128)
    nb = n // tn
    return pl.pallas_call(
        _glu_kernel,
        grid=(m // tm, nb),
        in_specs=[pl.BlockSpec((tm, k), lambda i, j: (i, 0)),
                  pl.BlockSpec((k, tn), lambda i, j: (0, j)),
                  pl.BlockSpec((k, tn), lambda i, j: (0, j + nb))],
        out_specs=pl.BlockSpec((tm, tn), lambda i, j: (i, j)),
        out_shape=jax.ShapeDtypeStruct((m, n), F32),
        compiler_params=_params("parallel", "parallel"),
        name="glu_matmul",
    )(x, w, w)


def kernel(x_prompt, x_sample, state_a_S, state_b_S, state_b_conv, state_c_C, state_c_n, state_c_m, state_d_re, state_d_im, a_w_in, a_lb_logits, a_g_norm, a_w_out, b_w_in, b_conv_w, b_a_log, b_dt_bias, b_g_norm, b_w_out, c_w_in, c_b_gate, c_g_norm, c_w_out, d_a_re, d_a_im, d_b_re, d_b_im, d_c_re, d_c_im, d_skip, d_log_dt, d_w_glu, moe_w_router, moe_b_router, moe_w_gu, moe_w_down, moe_w_gu_s, moe_w_down_s, ln_mix_g, ln_mix_b, ln_ffn_g, ln_ffn_b):
    bp, l, d = x_prompt.shape
    bs = x_sample.shape[0]
    tp = bp * l
    x = jnp.concatenate([x_prompt.reshape(tp, d), x_sample.reshape(bs, d)], axis=0)
    xb = x.astype(BF16)

    def finish_layer(x, h, layer):
        x, xb = _add_ln(x, h, ln_mix_g[layer], ln_mix_b[layer])
        return _moe_ln(x, xb, moe_w_router, moe_b_router, moe_w_gu, moe_w_down, moe_w_gu_s, moe_w_down_s,
                       ln_ffn_g, ln_ffn_b, layer)

    proj = _matmul(xb, a_w_in)
    lb = _hgrn2_lower_bound(a_lb_logits, 0)
    o_p, a_s_p = _hgrn2_prompt(proj, bp, l, lb, a_g_norm)
    o_s, a_s_s = _hgrn2_step(proj[tp:], state_a_S, lb, a_g_norm)
    h = _matmul(jnp.concatenate([o_p, o_s], axis=0), a_w_out)
    x, xb = finish_layer(x, h, 0)

    n_main = B_CONV_DIM + B_V_DIM
    proj = _matmul(xb, b_w_in, n_cols=n_main)
    ba = _matmul_hi(x, b_w_in[:, n_main:])
    o_p, b_s_p = _gdn_prompt(proj, ba, bp, l, b_conv_w, b_a_log, b_dt_bias, b_g_norm)
    o_s, b_s_s = _gdn_step(proj[tp:], ba[tp:], state_b_conv, b_conv_w, b_a_log, b_dt_bias, b_g_norm, state_b_S)
    qkv_s = proj[tp:, :B_CONV_DIM]
    b_conv_p = proj[:tp, :B_CONV_DIM].reshape(bp, l, B_CONV_DIM)[:, l - (B_CONV - 1):, :]
    b_conv_s = jnp.concatenate([state_b_conv[:, 1:, :], qkv_s[:, None, :]], axis=1)
    h = _matmul(jnp.concatenate([o_p, o_s], axis=0), b_w_out)
    x, xb = finish_layer(x, h, 1)

    n_main = 2 * C_QK_DIM + 2 * C_V_DIM
    proj = _matmul(xb, c_w_in, n_cols=n_main)
    gates = _matmul_hi(x, c_w_in[:, n_main:])
    o_p, c_c_p, c_n_p, c_m_p = _mlstm_prompt(proj, gates, bp, l, c_b_gate, c_g_norm)
    o_s, c_c_s, c_n_s, c_m_s = _mlstm_step(proj[tp:], gates[tp:], c_b_gate, c_g_norm, state_c_C, state_c_n, state_c_m)
    h = _matmul(jnp.concatenate([o_p, o_s], axis=0), c_w_out)
    x, xb = finish_layer(x, h, 2)

    tables = _s5_tables(d_a_re, d_a_im, d_b_re, d_b_im, d_c_re, d_c_im, d_log_dt, l // S5_SEGMENTS)
    y_p, (d_re_p, d_im_p) = _s5_prompt(x, bp, l, tables, d_skip)
    y_s, (d_re_s, d_im_s) = _s5_step(x[tp:], tables, d_skip, state_d_re, state_d_im)
    h = _glu_matmul(jnp.concatenate([y_p, y_s], axis=0), d_w_glu)
    x, xb = finish_layer(x, h, 3)

    y_prompt = x[:tp].reshape(bp, l, d)
    y_sample = x[tp:].reshape(bs, 1, d)
    return (y_prompt, y_sample, a_s_p, a_s_s, b_s_p, b_s_s, b_conv_p, b_conv_s,
            c_c_p, c_c_s, c_n_p, c_n_s, c_m_p, c_m_s, d_re_p, d_re_s, d_im_p, d_im_s)
```

```python
import functools
import math

import jax
import jax.numpy as jnp
from jax import lax
from jax.experimental import pallas as pl
from jax.experimental.pallas import tpu as pltpu

F32 = jnp.float32
BF16 = jnp.bfloat16
HI = lax.Precision.HIGHEST

D_MODEL = 2048
DEPTH = 4
CHUNK = 64
DN_ALPHA = (2 * DEPTH) ** 0.25
LN_EPS = 1e-5
RMS_EPS = 1e-6

VMEM_LIMIT_BYTES = 56 * 1024 * 1024


def _params(*sem):
    return pltpu.CompilerParams(dimension_semantics=sem, vmem_limit_bytes=VMEM_LIMIT_BYTES)


def _pick_tile(n, target, mult):
    best = None
    for t in range(mult, min(n, target) + 1, mult):
        if n % t == 0:
            best = t
    return best if best is not None else n


def _mm_kernel(x_ref, w_ref, o_ref):
    o_ref[...] = jnp.dot(x_ref[...], w_ref[...].astype(BF16), preferred_element_type=F32).astype(o_ref.dtype)


def _matmul(x, w, n_cols=None, out_dtype=F32, tm_target=1664, tn_target=512):
    m, k = x.shape
    n = w.shape[1] if n_cols is None else n_cols
    tm = _pick_tile(m, tm_target, 16)
    tn = _pick_tile(n, tn_target, 128)
    return pl.pallas_call(
        _mm_kernel,
        grid=(m // tm, n // tn),
        in_specs=[pl.BlockSpec((tm, k), lambda i, j: (i, 0)), pl.BlockSpec((k, tn), lambda i, j: (0, j))],
        out_specs=pl.BlockSpec((tm, tn), lambda i, j: (i, j)),
        out_shape=jax.ShapeDtypeStruct((m, n), out_dtype),
        compiler_params=_params("parallel", "parallel"),
        name="matmul",
    )(x, w)


def _layernorm_rows(z, g, b):
    mu = jnp.mean(z, axis=-1, keepdims=True)
    zc = z - mu
    var = jnp.mean(zc * zc, axis=-1, keepdims=True)
    return zc * lax.rsqrt(var + LN_EPS) * g + b


def _add_ln_kernel(x_ref, h_ref, g_ref, b_ref, o_ref, ob_ref):
    y = _layernorm_rows(DN_ALPHA * x_ref[...] + h_ref[...], g_ref[...], b_ref[...])
    o_ref[...] = y
    ob_ref[...] = y.astype(BF16)


def _add_ln(x, h, g, b, tm_target=640):
    m, d = x.shape
    tm = _pick_tile(m, tm_target, 16)
    row = pl.BlockSpec((tm, d), lambda i: (i, 0))
    vec = pl.BlockSpec((1, d), lambda i: (0, 0))
    return pl.pallas_call(
        _add_ln_kernel,
        grid=(m // tm,),
        in_specs=[row, row, vec, vec],
        out_specs=[row, row],
        out_shape=[jax.ShapeDtypeStruct((m, d), F32), jax.ShapeDtypeStruct((m, d), BF16)],
        compiler_params=_params("parallel"),
        name="add_ln",
    )(x, h, g.reshape(1, d), b.reshape(1, d))


def _silu(x):
    return x * jax.nn.sigmoid(x)


def _ffn_kernel(x_ref, wgu_ref, wd_ref, o_ref, *, dh):
    gu = jnp.dot(x_ref[...], wgu_ref[0].astype(BF16), preferred_element_type=F32)
    a = _silu(gu[:, :dh]) * gu[:, dh:]
    o_ref[...] = jnp.dot(a.astype(BF16), wd_ref[0].astype(BF16), preferred_element_type=F32)


def _dense_ffn(xb, w_gu, w_down, layer, tm_target=640):
    m, d = xb.shape
    dh = w_down.shape[1]
    tm = _pick_tile(m, tm_target, 16)
    return pl.pallas_call(
        functools.partial(_ffn_kernel, dh=dh),
        grid=(m // tm,),
        in_specs=[pl.BlockSpec((tm, d), lambda i: (i, 0)),
                  pl.BlockSpec((1, d, 2 * dh), lambda i: (layer, 0, 0)),
                  pl.BlockSpec((1, dh, d), lambda i: (layer, 0, 0))],
        out_specs=pl.BlockSpec((tm, d), lambda i: (i, 0)),
        out_shape=jax.ShapeDtypeStruct((m, d), F32),
        compiler_params=_params("parallel"),
        name="shared_ffn",
    )(xb, w_gu, w_down)


N_EXPERTS = 64
TOP_K = 8
N_ROUTE_GROUPS = 8
TOPK_ROUTE_GROUPS = 4
GROUP_SIZE = N_EXPERTS // N_ROUTE_GROUPS
ROUTED_SCALE = 2.5
EXPERT_TILE = 256


def _router_kernel(x_ref, wt_ref, b_ref, e_ref, g_ref):
    logits = lax.dot_general(wt_ref[...].astype(BF16), x_ref[...], (((1,), (1,)), ((), ())),
                             preferred_element_type=F32)
    scores = jax.nn.sigmoid(logits)
    biased = scores + b_ref[...]
    tm = scores.shape[1]
    neg = jnp.float32(-jnp.inf)
    iota_g = lax.broadcasted_iota(jnp.int32, (GROUP_SIZE, tm), 0)
    gscore = jnp.zeros((N_ROUTE_GROUPS, tm), F32)
    iota_r = lax.broadcasted_iota(jnp.int32, (N_ROUTE_GROUPS, tm), 0)
    for g in range(N_ROUTE_GROUPS):
        v = biased[g * GROUP_SIZE:(g + 1) * GROUP_SIZE, :]
        m1 = jnp.max(v, axis=0, keepdims=True)
        i1 = jnp.min(jnp.where(v == m1, iota_g, GROUP_SIZE), axis=0, keepdims=True)
        m2 = jnp.max(jnp.where(iota_g == i1, neg, v), axis=0, keepdims=True)
        gscore = jnp.where(iota_r == g, m1 + m2, gscore)
    ok = jnp.zeros((N_ROUTE_GROUPS, tm), jnp.int32)
    for _ in range(TOPK_ROUTE_GROUPS):
        m = jnp.max(gscore, axis=0, keepdims=True)
        gi = jnp.min(jnp.where(gscore == m, iota_r, N_ROUTE_GROUPS), axis=0, keepdims=True)
        hit = iota_r == gi
        ok = jnp.where(hit, 1, ok)
        gscore = jnp.where(hit, neg, gscore)
    masked = jnp.concatenate(
        [jnp.where(ok[g:g + 1, :] > 0, biased[g * GROUP_SIZE:(g + 1) * GROUP_SIZE, :], neg)
         for g in range(N_ROUTE_GROUPS)], axis=0)
    iota_e = lax.broadcasted_iota(jnp.int32, (N_EXPERTS, tm), 0)
    iota_k = lax.broadcasted_iota(jnp.int32, (TOP_K, tm), 0)
    top_e = jnp.zeros((TOP_K, tm), jnp.int32)
    gate = jnp.zeros((TOP_K, tm), F32)
    for k in range(TOP_K):
        m = jnp.max(masked, axis=0, keepdims=True)
        ei = jnp.min(jnp.where(masked == m, iota_e, N_EXPERTS), axis=0, keepdims=True)
        hit = iota_e == ei
        gk = jnp.sum(jnp.where(hit, scores, 0.0), axis=0, keepdims=True)
        top_e = jnp.where(iota_k == k, ei, top_e)
        gate = jnp.where(iota_k == k, gk, gate)
        masked = jnp.where(hit, neg, masked)
    e_ref[...] = top_e
    g_ref[...] = ROUTED_SCALE * gate / jnp.sum(gate, axis=0, keepdims=True)


def _router(x, w_router_t, b_router, tm_target=640):
    t, d = x.shape
    tm = _pick_tile(t, tm_target, 128)
    return pl.pallas_call(
        _router_kernel,
        grid=(t // tm,),
        in_specs=[pl.BlockSpec((tm, d), lambda i: (i, 0)),
                  pl.BlockSpec((N_EXPERTS, d), lambda i: (0, 0)),
                  pl.BlockSpec((N_EXPERTS, 1), lambda i: (0, 0))],
        out_specs=[pl.BlockSpec((TOP_K, tm), lambda i: (0, i)), pl.BlockSpec((TOP_K, tm), lambda i: (0, i))],
        out_shape=[jax.ShapeDtypeStruct((TOP_K, t), jnp.int32), jax.ShapeDtypeStruct((TOP_K, t), F32)],
        compiler_params=_params("parallel"),
        name="router",
    )(x, w_router_t, b_router.reshape(N_EXPERTS, 1))


LANES = 128


def _dispatch_kernel(e_ref, pos_ref, cnt_ref, rank_ref):
    n_blocks = e_ref.shape[1] // LANES
    iota_e = lax.broadcasted_iota(jnp.int32, (N_EXPERTS, LANES), 0)
    incl = (lax.broadcasted_iota(jnp.int32, (LANES, LANES), 0)
            <= lax.broadcasted_iota(jnp.int32, (LANES, LANES), 1)).astype(BF16)
    iota_k = lax.broadcasted_iota(jnp.int32, (TOP_K, LANES), 0)

    def pick(e_blk, table):
        out = jnp.zeros((TOP_K, LANES), F32)
        for k in range(TOP_K):
            v = jnp.sum(jnp.where(iota_e == e_blk[k:k + 1, :], table, 0.0), axis=0, keepdims=True)
            out = jnp.where(iota_k == k, v, out)
        return out

    def count_block(b, carry):
        cols = pl.ds(pl.multiple_of(b * LANES, LANES), LANES)
        e_blk = e_ref[:, cols]
        member = jnp.zeros((N_EXPERTS, LANES), F32)
        for k in range(TOP_K):
            member = member + (iota_e == e_blk[k:k + 1, :]).astype(F32)
        seen = jnp.dot(member.astype(BF16), incl, preferred_element_type=F32) + carry
        rank_ref[:, cols] = pick(e_blk, seen - member)
        return jnp.broadcast_to(seen[:, LANES - 1:LANES], (N_EXPERTS, LANES))

    counts = lax.fori_loop(0, n_blocks, count_block, jnp.zeros((N_EXPERTS, LANES), F32))
    cnt_ref[...] = counts.astype(jnp.int32)
    padded = jnp.ceil(counts * (1.0 / EXPERT_TILE)) * EXPERT_TILE
    start = jnp.dot(_tri(N_EXPERTS, strict=True).astype(F32), padded, preferred_element_type=F32, precision=HI)

    def place_block(b, carry):
        cols = pl.ds(pl.multiple_of(b * LANES, LANES), LANES)
        pos_ref[:, cols] = (pick(e_ref[:, cols], start) + rank_ref[:, cols]).astype(jnp.int32)
        return carry

    lax.fori_loop(0, n_blocks, place_block, 0)


def _dispatch_tables(top_e_t, n_tiles):
    k, t = top_e_t.shape
    pos_t, cnt = pl.pallas_call(
        _dispatch_kernel,
        out_shape=[jax.ShapeDtypeStruct((k, t), jnp.int32), jax.ShapeDtypeStruct((N_EXPERTS, LANES), jnp.int32)],
        scratch_shapes=[pltpu.VMEM((k, t), F32)],
        compiler_params=pltpu.CompilerParams(vmem_limit_bytes=VMEM_LIMIT_BYTES),
        name="dispatch",
    )(top_e_t)
    counts = cnt[:, 0]
    pends = jnp.cumsum((counts + EXPERT_TILE - 1) // EXPERT_TILE * EXPERT_TILE)
    pos = pos_t.T.reshape(-1)
    tok = jnp.arange(t * k, dtype=jnp.int32) // k
    row_tok = jnp.zeros((n_tiles * EXPERT_TILE,), jnp.int32).at[pos].set(tok, unique_indices=True)
    tile_e = jnp.minimum(
        jnp.searchsorted(pends, jnp.arange(n_tiles, dtype=jnp.int32) * EXPERT_TILE, side='right'),
        N_EXPERTS - 1).astype(jnp.int32)
    n_used = (pends[-1] // EXPERT_TILE).astype(jnp.int32).reshape(1)
    return row_tok, tile_e, n_used, pos


def _expert_kernel(row_tok_ref, tile_e_ref, n_used_ref, x_hbm, wgu_ref, wd_ref, o_ref,
                   xbuf, sem, wgu_b, wd_b, *, dh):
    i = pl.program_id(0)
    n_used = n_used_ref[0]
    slot = i % 2

    def issue(tile, s):
        base = tile * EXPERT_TILE

        def body(r, c):
            tok = row_tok_ref[base + r]
            pltpu.make_async_copy(x_hbm.at[pl.ds(tok, 1)], xbuf.at[s, pl.ds(r, 1)], sem.at[s]).start()
            return c

        lax.fori_loop(0, EXPERT_TILE, body, 0, unroll=8)

    @pl.when(jnp.logical_and(i == 0, n_used > 0))
    def _():
        issue(0, 0)

    @pl.when(i + 1 < n_used)
    def _():
        issue(i + 1, 1 - slot)

    e = tile_e_ref[i]
    e_prev = tile_e_ref[jnp.maximum(i - 1, 0)]

    @pl.when(jnp.logical_and(i < n_used, jnp.logical_or(i == 0, e != e_prev)))
    def _():
        wgu_b[...] = wgu_ref[0, 0].astype(BF16)
        wd_b[...] = wd_ref[0, 0].astype(BF16)

    @pl.when(i < n_used)
    def _():
        pltpu.make_async_copy(x_hbm.at[pl.ds(0, EXPERT_TILE)], xbuf.at[slot], sem.at[slot]).wait()
        gu = jnp.dot(xbuf[slot].astype(BF16), wgu_b[...], preferred_element_type=F32)
        a = _silu(gu[:, :dh]) * gu[:, dh:]
        o_ref[...] = jnp.dot(a.astype(BF16), wd_b[...], preferred_element_type=F32)

    @pl.when(i >= n_used)
    def _():
        o_ref[...] = jnp.zeros_like(o_ref)


def _routed_experts(x, row_tok, tile_e, n_used, w_gu, w_down, layer):
    t, d = x.shape
    dh = w_down.shape[2]
    n_tiles = tile_e.shape[0]
    grid_spec = pltpu.PrefetchScalarGridSpec(
        num_scalar_prefetch=3,
        grid=(n_tiles,),
        in_specs=[pl.BlockSpec(memory_space=pl.ANY),
                  pl.BlockSpec((1, 1, d, 2 * dh), lambda i, rt, te, nu: (layer, te[i], 0, 0)),
                  pl.BlockSpec((1, 1, dh, d), lambda i, rt, te, nu: (layer, te[i], 0, 0))],
        out_specs=pl.BlockSpec((EXPERT_TILE, d), lambda i, rt, te, nu: (i, 0)),
        scratch_shapes=[pltpu.VMEM((2, EXPERT_TILE, d), F32),
                        pltpu.SemaphoreType.DMA((2,)),
                        pltpu.VMEM((d, 2 * dh), BF16),
                        pltpu.VMEM((dh, d), BF16)],
    )
    return pl.pallas_call(
        functools.partial(_expert_kernel, dh=dh),
        grid_spec=grid_spec,
        out_shape=jax.ShapeDtypeStruct((n_tiles * EXPERT_TILE, d), F32),
        compiler_params=_params("arbitrary"),
        name="routed_experts",
    )(row_tok, tile_e, n_used, x, w_gu, w_down)


COMBINE_TILE = 128


def _combine_kernel(pos_ref, rows_hbm, gate_ref, x_ref, sh_ref, g_ref, b_ref, o_ref, ob_ref, buf, sem):
    i = pl.program_id(0)
    n = pl.num_programs(0)
    slot = i % 2

    def issue(tile, s):
        base = tile * (COMBINE_TILE * TOP_K)

        def body(r, c):
            for k in range(TOP_K):
                p = pos_ref[base + r * TOP_K + k]
                pltpu.make_async_copy(rows_hbm.at[pl.ds(p, 1)], buf.at[s, k, pl.ds(r, 1)], sem.at[s]).start()
            return c

        lax.fori_loop(0, COMBINE_TILE, body, 0)

    @pl.when(i == 0)
    def _():
        issue(0, 0)

    @pl.when(i + 1 < n)
    def _():
        issue(i + 1, 1 - slot)

    for k in range(TOP_K):
        pltpu.make_async_copy(rows_hbm.at[pl.ds(0, COMBINE_TILE)], buf.at[slot, k], sem.at[slot]).wait()
    gate = gate_ref[...]
    z = DN_ALPHA * x_ref[...] + sh_ref[...]
    for k in range(TOP_K):
        z = z + gate[:, k:k + 1] * buf[slot, k]
    y = _layernorm_rows(z, g_ref[...], b_ref[...])
    o_ref[...] = y
    ob_ref[...] = y.astype(BF16)


def _combine_ln(rows, pos, gate, x, shared, g, b):
    t, d = x.shape
    row = lambda w: pl.BlockSpec((COMBINE_TILE, w), lambda i, p: (i, 0))
    vec = pl.BlockSpec((1, d), lambda i, p: (0, 0))
    grid_spec = pltpu.PrefetchScalarGridSpec(
        num_scalar_prefetch=1,
        grid=(t // COMBINE_TILE,),
        in_specs=[pl.BlockSpec(memory_space=pl.ANY), row(TOP_K), row(d), row(d), vec, vec],
        out_specs=[row(d), row(d)],
        scratch_shapes=[pltpu.VMEM((2, TOP_K, COMBINE_TILE, d), F32), pltpu.SemaphoreType.DMA((2,))],
    )
    return pl.pallas_call(
        _combine_kernel,
        grid_spec=grid_spec,
        out_shape=[jax.ShapeDtypeStruct((t, d), F32), jax.ShapeDtypeStruct((t, d), BF16)],
        compiler_params=_params("arbitrary"),
        name="combine_ln",
    )(pos, rows, gate, x, shared, g.reshape(1, d), b.reshape(1, d))


def _moe_ln(x, xb, w_router, b_router, w_gu, w_down, w_gu_s, w_down_s, ln_g, ln_b, layer):
    t = x.shape[0]
    top_e_t, gate_t = _router(xb, w_router[layer].T, b_router[layer])
    n_tiles = (t * TOP_K + N_EXPERTS * (EXPERT_TILE - 1) + EXPERT_TILE - 1) // EXPERT_TILE
    row_tok, tile_e, n_used, pos = _dispatch_tables(top_e_t, n_tiles)
    rows = _routed_experts(x, row_tok, tile_e, n_used, w_gu, w_down, layer)
    shared = _dense_ffn(xb, w_gu_s, w_down_s, layer)
    return _combine_ln(rows, pos, gate_t.T, x, shared, ln_g[layer], ln_b[layer])


SUB = 16
SEQ_BLOCK = 512


def _tri(n, strict=False):
    r = lax.broadcasted_iota(jnp.int32, (n, n), 0)
    c = lax.broadcasted_iota(jnp.int32, (n, n), 1)
    return (r > c) if strict else (r >= c)


def _cumsum_rows(x):
    n = x.shape[0]
    return jnp.dot(_tri(n).astype(F32), x, preferred_element_type=F32, precision=HI)


def _bdot(a, b):
    return jnp.dot(a.astype(BF16), b.astype(BF16), preferred_element_type=F32)


def _bdot_nt(a, b):
    return lax.dot_general(a.astype(BF16), b.astype(BF16), (((1,), (1,)), ((), ())), preferred_element_type=F32)


def _bdot_tn(a, b):
    return lax.dot_general(a.astype(BF16), b.astype(BF16), (((0,), (0,)), ((), ())), preferred_element_type=F32)


def _rms(x):
    return x * lax.rsqrt(jnp.mean(x * x, axis=-1, keepdims=True) + RMS_EPS)


def _row_of(col_b):
    c = col_b.shape[0]
    return col_b.T[0:1, :c]


A_HEADS = 16
A_DK = 128
A_DV = 128


def _hgrn2_chunk_kernel(q_ref, f_ref, v_ref, g_ref, lb_ref, gn_ref, o_ref, s_out_ref, s_ref, *, n_chunks):
    l = pl.program_id(2)

    @pl.when(l == 0)
    def _():
        s_ref[...] = jnp.zeros_like(s_ref)

    lb = lb_ref[...]
    gn = gn_ref[...]

    def chunk(j, carry):
        rows = pl.ds(pl.multiple_of(j * CHUNK, CHUNK), CHUNK)
        q = _silu(q_ref[rows, :])
        f = lb + (1.0 - lb) * jax.nn.sigmoid(f_ref[rows, :])
        k = 1.0 - f
        v = v_ref[rows, :]
        bc = _cumsum_rows(jnp.log(f))
        s = s_ref[...]
        o_inter = _bdot(q * jnp.exp(bc), s)
        outs = []
        for i in range(CHUNK // SUB):
            lo, hi = i * SUB, (i + 1) * SUB
            bref = bc[lo - 1:lo, :] if i > 0 else jnp.zeros((1, A_DK), F32)
            qi = q[lo:hi] * jnp.exp(bc[lo:hi] - bref)
            ki = k[:hi] * jnp.exp(bref - bc[:hi])
            att = _bdot_nt(qi, ki)
            r = lax.broadcasted_iota(jnp.int32, (SUB, hi), 0) + lo
            c = lax.broadcasted_iota(jnp.int32, (SUB, hi), 1)
            att = jnp.where(r >= c, att, 0.0)
            outs.append(_bdot(att, v[:hi]))
        o = jnp.concatenate(outs, axis=0) + o_inter
        b_end = bc[CHUNK - 1:CHUNK, :]
        d_col = jnp.exp(bc.T[:, CHUNK - 1:CHUNK])
        s_ref[...] = d_col * s + _bdot_tn(k * jnp.exp(b_end - bc), v)
        o_ref[rows, :] = (_rms(o) * gn * _silu(g_ref[rows, :])).astype(o_ref.dtype)
        return carry

    lax.fori_loop(0, n_chunks, chunk, 0)

    @pl.when(l == pl.num_programs(2) - 1)
    def _():
        s_out_ref[0, 0] = s_ref[...]


def _hgrn2_prompt(proj, b, l, lb, g_norm):
    lblk = _pick_tile(l, SEQ_BLOCK, CHUNK)
    nl = l // lblk
    part = lambda p: pl.BlockSpec((lblk, A_DK), lambda bi, h, li, p=p: (bi * nl + li, p * A_HEADS + h))
    return pl.pallas_call(
        functools.partial(_hgrn2_chunk_kernel, n_chunks=lblk // CHUNK),
        grid=(b, A_HEADS, nl),
        in_specs=[part(0), part(1), part(2), part(3),
                  pl.BlockSpec((1, A_DK), lambda bi, h, li: (0, h)),
                  pl.BlockSpec((1, A_DV), lambda bi, h, li: (0, 0))],
        out_specs=[pl.BlockSpec((lblk, A_DV), lambda bi, h, li: (bi * nl + li, h)),
                   pl.BlockSpec((1, 1, A_DK, A_DV), lambda bi, h, li: (bi, h, 0, 0))],
        out_shape=[jax.ShapeDtypeStruct((b * l, A_HEADS * A_DV), BF16),
                   jax.ShapeDtypeStruct((b, A_HEADS, A_DK, A_DV), F32)],
        scratch_shapes=[pltpu.VMEM((A_DK, A_DV), F32)],
        compiler_params=_params("parallel", "parallel", "arbitrary"),
        name="hgrn2_chunks",
    )(proj, proj, proj, proj, lb.reshape(1, -1), g_norm.reshape(1, -1))


HGRN2_STEP_BATCH = 4


def _hgrn2_step_kernel(p_ref, s_ref, lb_ref, gn_ref, o_ref, s_out_ref):
    lb = lb_ref[...]
    iota_h = lax.broadcasted_iota(jnp.int32, (A_HEADS, A_DV), 0)
    for bi in range(HGRN2_STEP_BATCH):
        p = p_ref[bi]
        q = _silu(p[0:A_HEADS])
        f = lb + (1.0 - lb) * jax.nn.sigmoid(p[A_HEADS:2 * A_HEADS])
        v = p[2 * A_HEADS:3 * A_HEADS]
        g = p[3 * A_HEADS:4 * A_HEADS]
        q_t, f_t = q.T, f.T
        o = jnp.zeros((A_HEADS, A_DV), F32)
        for h in range(A_HEADS):
            fc = f_t[:, h:h + 1]
            s_new = fc * s_ref[bi, h] + (1.0 - fc) * v[h:h + 1, :]
            s_out_ref[bi, h] = s_new
            oh = jnp.sum(q_t[:, h:h + 1] * s_new, axis=0, keepdims=True)
            o = jnp.where(iota_h == h, oh, o)
        o_ref[bi] = (_rms(o) * gn_ref[...] * _silu(g)).astype(o_ref.dtype)


def _hgrn2_step(proj, s0, lb, g_norm):
    b = proj.shape[0]
    bb = HGRN2_STEP_BATCH
    p3 = proj.reshape(b, 4 * A_HEADS, A_DK)
    o, s = pl.pallas_call(
        _hgrn2_step_kernel,
        grid=(b // bb,),
        in_specs=[pl.BlockSpec((bb, 4 * A_HEADS, A_DK), lambda i: (i, 0, 0)),
                  pl.BlockSpec((bb, A_HEADS, A_DK, A_DV), lambda i: (i, 0, 0, 0)),
                  pl.BlockSpec((A_HEADS, A_DK), lambda i: (0, 0)),
                  pl.BlockSpec((1, A_DV), lambda i: (0, 0))],
        out_specs=[pl.BlockSpec((bb, A_HEADS, A_DV), lambda i: (i, 0, 0)),
                   pl.BlockSpec((bb, A_HEADS, A_DK, A_DV), lambda i: (i, 0, 0, 0))],
        out_shape=[jax.ShapeDtypeStruct((b, A_HEADS, A_DV), BF16),
                   jax.ShapeDtypeStruct(s0.shape, F32)],
        compiler_params=_params("parallel"),
        name="hgrn2_step",
    )(p3, s0, lb.reshape(A_HEADS, A_DK), g_norm.reshape(1, -1))
    return o.reshape(b, A_HEADS * A_DV), s


def _hgrn2_lower_bound(lb_logits, layer):
    return jnp.cumsum(jax.nn.softmax(lb_logits.astype(F32), axis=0), axis=0)[layer]


C_HEADS = 8
C_DK = 128
C_DV = 256
C_QK_DIM = C_HEADS * C_DK
C_V_DIM = C_HEADS * C_DV
C_GATE_CAP = 15.0


def _log_sigmoid(x):
    return jnp.minimum(x, 0.0) - jnp.log(1.0 + jnp.exp(-jnp.abs(x)))


def _cap(x):
    return C_GATE_CAP * jnp.tanh(x / C_GATE_CAP)


def _mlstm_chunk_kernel(q_ref, k_ref, v_ref, op_ref, gt_ref, bg_ref, gn_ref,
                        o_ref, c_out_ref, n_out_ref, m_out_ref, c_ref, n_ref, m_ref, *, n_chunks):
    h = pl.program_id(1)
    l = pl.program_id(2)

    @pl.when(l == 0)
    def _():
        c_ref[...] = jnp.zeros_like(c_ref)
        n_ref[...] = jnp.zeros_like(n_ref)
        m_ref[...] = jnp.zeros_like(m_ref)

    lane = lax.broadcasted_iota(jnp.int32, (CHUNK, 2 * C_HEADS), 1)
    tri = _tri(CHUNK)

    def chunk(j, carry):
        rows = pl.ds(pl.multiple_of(j * CHUNK, CHUNK), CHUNK)
        gates = _cap(gt_ref[rows, :] + bg_ref[...])
        i_col = jnp.sum(jnp.where(lane == h, gates, 0.0), axis=-1, keepdims=True)
        f_col = jnp.sum(jnp.where(lane == h + C_HEADS, gates, 0.0), axis=-1, keepdims=True)
        i_b = jnp.broadcast_to(i_col, (CHUNK, 128))
        f_b = _cumsum_rows(jnp.broadcast_to(_log_sigmoid(f_col), (CHUNK, 128)))
        a_b = i_b - f_b
        f_c = f_b[:, 0:1]
        log_w = jnp.where(tri, f_c + _row_of(a_b), -jnp.inf)
        m_prev = m_ref[0:1, 0:1]
        log_s = f_c + m_prev
        m_t = jnp.maximum(jnp.max(log_w, axis=-1, keepdims=True), log_s)
        qs = q_ref[rows, :] * (C_DK ** -0.5)
        k = k_ref[rows, :]
        v = v_ref[rows, :]
        qk = _bdot_nt(qs, k) * jnp.exp(log_w - m_t)
        w_s = jnp.exp(log_s - m_t)
        num = _bdot(qk, v) + w_s * _bdot(qs, c_ref[...])
        den = jnp.sum(qk, axis=-1, keepdims=True) + w_s * jnp.sum(qs * n_ref[...], axis=-1, keepdims=True)
        hid = num / jnp.maximum(jnp.abs(den), jnp.exp(-m_t))
        m_end = m_t[CHUNK - 1:CHUNK, :]
        f_end = f_c[CHUNK - 1:CHUNK, :]
        w_end = jnp.exp(f_end + a_b[:, 0:1] - m_end)
        s_end = jnp.exp(f_end + m_prev - m_end)
        kw = k * w_end
        c_ref[...] = s_end * c_ref[...] + _bdot_tn(kw, v)
        n_ref[...] = s_end * n_ref[...] + jnp.sum(kw, axis=0, keepdims=True)
        m_ref[...] = jnp.broadcast_to(m_end, m_ref.shape)
        o_ref[rows, :] = (_rms(hid) * gn_ref[0] * jax.nn.sigmoid(op_ref[rows, :])).astype(o_ref.dtype)
        return carry

    lax.fori_loop(0, n_chunks, chunk, 0)

    @pl.when(l == pl.num_programs(2) - 1)
    def _():
        c_out_ref[0, 0] = c_ref[...]
        n_out_ref[0, 0] = n_ref[...]
        m_out_ref[0, 0] = m_ref[...]


def _mlstm_prompt(proj, gates, b, l, b_gate, g_norm):
    lblk = _pick_tile(l, SEQ_BLOCK, CHUNK)
    nl = l // lblk
    qk_blocks = C_QK_DIM // C_DK
    v_off = 2 * C_QK_DIM // C_DV
    o, c, n, m = pl.pallas_call(
        functools.partial(_mlstm_chunk_kernel, n_chunks=lblk // CHUNK),
        grid=(b, C_HEADS, nl),
        in_specs=[pl.BlockSpec((lblk, C_DK), lambda bi, h, li: (bi * nl + li, h)),
                  pl.BlockSpec((lblk, C_DK), lambda bi, h, li: (bi * nl + li, qk_blocks + h)),
                  pl.BlockSpec((lblk, C_DV), lambda bi, h, li: (bi * nl + li, v_off + h)),
                  pl.BlockSpec((lblk, C_DV), lambda bi, h, li: (bi * nl + li, v_off + C_HEADS + h)),
                  pl.BlockSpec((lblk, 2 * C_HEADS), lambda bi, h, li: (bi * nl + li, 0)),
                  pl.BlockSpec((1, 2 * C_HEADS), lambda bi, h, li: (0, 0)),
                  pl.BlockSpec((1, 1, C_DV), lambda bi, h, li: (h, 0, 0))],
        out_specs=[pl.BlockSpec((lblk, C_DV), lambda bi, h, li: (bi * nl + li, h)),
                   pl.BlockSpec((1, 1, C_DK, C_DV), lambda bi, h, li: (bi, h, 0, 0)),
                   pl.BlockSpec((1, 1, 1, C_DK), lambda bi, h, li: (bi, h, 0, 0)),
                   pl.BlockSpec((1, 1, 1, 128), lambda bi, h, li: (bi, h, 0, 0))],
        out_shape=[jax.ShapeDtypeStruct((b * l, C_V_DIM), BF16),
                   jax.ShapeDtypeStruct((b, C_HEADS, C_DK, C_DV), F32),
                   jax.ShapeDtypeStruct((b, C_HEADS, 1, C_DK), F32),
                   jax.ShapeDtypeStruct((b, C_HEADS, 1, 128), F32)],
        scratch_shapes=[pltpu.VMEM((C_DK, C_DV), F32), pltpu.VMEM((1, C_DK), F32), pltpu.VMEM((1, 128), F32)],
        compiler_params=_params("parallel", "parallel", "arbitrary"),
        name="mlstm_chunks",
    )(proj, proj, proj, proj, gates, b_gate.reshape(1, -1), g_norm.reshape(C_HEADS, 1, C_DV))
    return o, c, n[:, :, 0, :], m[:, :, 0, 0]


MLSTM_STEP_BATCH = 8


def _mlstm_step_kernel(qk_ref, vo_ref, gt_ref, bg_ref, gn_ref, c_ref, n_ref, m_ref,
                       o_ref, c_out_ref, n_out_ref, m_out_ref):
    iota_h = lax.broadcasted_iota(jnp.int32, (C_HEADS, C_DV), 0)
    iota_hk = lax.broadcasted_iota(jnp.int32, (C_HEADS, C_DK), 0)
    iota_m = lax.broadcasted_iota(jnp.int32, (MLSTM_STEP_BATCH, C_HEADS), 1)
    iota_mb = lax.broadcasted_iota(jnp.int32, (MLSTM_STEP_BATCH, C_HEADS), 0)
    gates = _cap(gt_ref[...] + bg_ref[...])
    m_all = m_ref[...]
    m_new_all = jnp.zeros((MLSTM_STEP_BATCH, C_HEADS), F32)
    for bi in range(MLSTM_STEP_BATCH):
        qs = qk_ref[bi, 0:C_HEADS, :] * (C_DK ** -0.5)
        k = qk_ref[bi, C_HEADS:2 * C_HEADS, :]
        v = vo_ref[bi, C_HEADS:2 * C_HEADS, :]
        op = vo_ref[bi, 2 * C_HEADS:3 * C_HEADS, :]
        q_t, k_t = qs.T, k.T
        n_all = n_ref[bi]
        hid = jnp.zeros((C_HEADS, C_DV), F32)
        n_new = jnp.zeros((C_HEADS, C_DK), F32)
        for h in range(C_HEADS):
            i_g = gates[bi:bi + 1, h:h + 1]
            lf = _log_sigmoid(gates[bi:bi + 1, C_HEADS + h:C_HEADS + h + 1])
            log_s = lf + m_all[bi:bi + 1, h:h + 1]
            m_t = jnp.maximum(i_g, log_s)
            w_i = jnp.exp(i_g - m_t)
            w_s = jnp.exp(log_s - m_t)
            qk = jnp.sum(qs[h:h + 1, :] * k[h:h + 1, :], axis=-1, keepdims=True) * w_i
            c_old = c_ref[bi, h]
            num = qk * v[h:h + 1, :] + w_s * jnp.sum(q_t[:, h:h + 1] * c_old, axis=0, keepdims=True)
            den = qk + w_s * jnp.sum(qs[h:h + 1, :] * n_all[h:h + 1, :], axis=-1, keepdims=True)
            hid = jnp.where(iota_h == h, num / jnp.maximum(jnp.abs(den), jnp.exp(-m_t)), hid)
            c_out_ref[bi, h] = w_s * c_old + (w_i * k_t[:, h:h + 1]) * v[h:h + 1, :]
            n_new = jnp.where(iota_hk == h, w_s * n_all[h:h + 1, :] + w_i * k[h:h + 1, :], n_new)
            m_new_all = jnp.where(jnp.logical_and(iota_m == h, iota_mb == bi), m_t, m_new_all)
        n_out_ref[bi] = n_new
        o_ref[bi] = (_rms(hid) * gn_ref[...] * jax.nn.sigmoid(op)).astype(o_ref.dtype)
    m_out_ref[...] = m_new_all


def _mlstm_step(proj, gates, b_gate, g_norm, c0, n0, m0):
    b = proj.shape[0]
    bb = MLSTM_STEP_BATCH
    qk_view = proj.reshape(b, -1, C_DK)
    vo_view = proj.reshape(b, -1, C_DV)
    o, c, n, m = pl.pallas_call(
        _mlstm_step_kernel,
        grid=(b // bb,),
        in_specs=[pl.BlockSpec((bb, 2 * C_HEADS, C_DK), lambda i: (i, 0, 0)),
                  pl.BlockSpec((bb, 3 * C_HEADS, C_DV), lambda i: (i, 0, 0)),
                  pl.BlockSpec((bb, 2 * C_HEADS), lambda i: (i, 0)),
                  pl.BlockSpec((1, 2 * C_HEADS), lambda i: (0, 0)),
                  pl.BlockSpec((C_HEADS, C_DV), lambda i: (0, 0)),
                  pl.BlockSpec((bb, C_HEADS, C_DK, C_DV), lambda i: (i, 0, 0, 0)),
                  pl.BlockSpec((bb, C_HEADS, C_DK), lambda i: (i, 0, 0)),
                  pl.BlockSpec((bb, C_HEADS), lambda i: (i, 0))],
        out_specs=[pl.BlockSpec((bb, C_HEADS, C_DV), lambda i: (i, 0, 0)),
                   pl.BlockSpec((bb, C_HEADS, C_DK, C_DV), lambda i: (i, 0, 0, 0)),
                   pl.BlockSpec((bb, C_HEADS, C_DK), lambda i: (i, 0, 0)),
                   pl.BlockSpec((bb, C_HEADS), lambda i: (i, 0))],
        out_shape=[jax.ShapeDtypeStruct((b, C_HEADS, C_DV), BF16),
                   jax.ShapeDtypeStruct(c0.shape, F32),
                   jax.ShapeDtypeStruct(n0.shape, F32),
                   jax.ShapeDtypeStruct(m0.shape, F32)],
        compiler_params=_params("parallel"),
        name="mlstm_step",
    )(qk_view, vo_view, gates, b_gate.reshape(1, -1), g_norm, c0, n0, m0)
    return o.reshape(b, C_V_DIM), c, n, m


B_QK_HEADS = 16
B_V_HEADS = 32
B_DK = 128
B_DV = 128
B_CONV = 4
B_QK_DIM = B_QK_HEADS * B_DK
B_V_DIM = B_V_HEADS * B_DV
B_CONV_DIM = 2 * B_QK_DIM + B_V_DIM
B_REP = B_V_HEADS // B_QK_HEADS
CONV_PAD = 8


def _softplus(x):
    return jnp.maximum(x, 0.0) + jnp.log(1.0 + jnp.exp(-jnp.abs(x)))


def _l2norm(x):
    return x * lax.rsqrt(jnp.sum(x * x, axis=-1, keepdims=True) + RMS_EPS)


def _split2(x):
    hi = x.astype(BF16)
    return hi, (x - hi.astype(F32)).astype(BF16)


def _dot3_parts(ah, al, bh, bl):
    n = bh.shape[1]
    r = jnp.dot(ah, jnp.concatenate([bh, bl], axis=1), preferred_element_type=F32)
    return r[:, :n] + r[:, n:] + jnp.dot(al, bh, preferred_element_type=F32)


def _unit_lower_inverses(mats, order):
    n = mats[0].shape[0]
    eye = (lax.broadcasted_iota(jnp.int32, (n, n), 0) == lax.broadcasted_iota(jnp.int32, (n, n), 1)).astype(F32)
    ps = [-a for a in mats]
    ts = [eye + p for p in ps]
    parts = [_split2(p) for p in ps]
    span = 1
    while 2 * span < order:
        ps = [_dot3_parts(ph, pl_, ph, pl_) for ph, pl_ in parts]
        parts = [_split2(p) for p in ps]
        t_parts = [_split2(t) for t in ts]
        ts = [t + _dot3_parts(th, tl, ph, pl_) for t, (th, tl), (ph, pl_) in zip(ts, t_parts, parts)]
        span *= 2
    return ts


GDN_QK_PER_STEP = 4
GDN_V_PER_STEP = GDN_QK_PER_STEP * B_REP


def _gdn_chunk_kernel(q_ref, k_ref, v_ref, z_ref, ba_ref, wq_ref, wk_ref, wv_ref, alog_ref, dtb_ref, gn_ref,
                      o_ref, s_out_ref, s_ref, qbuf, kbuf, vbuf, gcol, bcol, grow, *, lblk):
    h = pl.program_id(1)
    l = pl.program_id(2)
    n_chunks = lblk // CHUNK

    @pl.when(l == 0)
    def _():
        s_ref[...] = jnp.zeros_like(s_ref)
        qbuf[0:CONV_PAD, :] = jnp.zeros((CONV_PAD, qbuf.shape[1]), F32)
        kbuf[0:CONV_PAD, :] = jnp.zeros((CONV_PAD, kbuf.shape[1]), F32)
        vbuf[0:CONV_PAD, :] = jnp.zeros((CONV_PAD, vbuf.shape[1]), F32)

    def conv(x_ref, w_ref, buf, post):
        buf[CONV_PAD:CONV_PAD + lblk, :] = x_ref[...]
        tail = buf[lblk:lblk + CONV_PAD, :]
        taps = B_CONV - 1
        for c in reversed(range(n_chunks)):
            lo = CONV_PAD + c * CHUNK
            y = w_ref[0:1, :] * buf[lo - taps:lo - taps + CHUNK, :]
            for j in range(1, B_CONV):
                y = y + w_ref[j:j + 1, :] * buf[lo - taps + j:lo - taps + j + CHUNK, :]
            buf[lo:lo + CHUNK, :] = post(_silu(y))
        return tail

    def per_head_l2norm(scale):
        def post(y):
            return jnp.concatenate([_l2norm(y[:, i * B_DK:(i + 1) * B_DK]) * scale
                                    for i in range(GDN_QK_PER_STEP)], axis=1)
        return post

    q_tail = conv(q_ref, wq_ref, qbuf, per_head_l2norm(B_DK ** -0.5))
    k_tail = conv(k_ref, wk_ref, kbuf, per_head_l2norm(1.0))
    v_tail = conv(v_ref, wv_ref, vbuf, lambda y: y)

    lane = lax.broadcasted_iota(jnp.int32, (lblk, 2 * B_V_HEADS), 1)
    lane_h = lax.broadcasted_iota(jnp.int32, (1, B_V_HEADS), 1)
    lane_o = lax.broadcasted_iota(jnp.int32, (lblk, LANES), 1)
    ba = ba_ref[...]
    beta_all = jnp.zeros((lblk, LANES), F32)
    g_all = jnp.zeros((lblk, LANES), F32)
    for e in range(GDN_V_PER_STEP):
        hv = h * GDN_V_PER_STEP + e
        beta = jax.nn.sigmoid(jnp.sum(jnp.where(lane == hv, ba, 0.0), axis=-1, keepdims=True))
        a_pre = jnp.sum(jnp.where(lane == hv + B_V_HEADS, ba, 0.0), axis=-1, keepdims=True)
        a_scale = -jnp.exp(jnp.sum(jnp.where(lane_h == hv, alog_ref[...], 0.0), axis=-1, keepdims=True))
        dt_b = jnp.sum(jnp.where(lane_h == hv, dtb_ref[...], 0.0), axis=-1, keepdims=True)
        g = a_scale * _softplus(a_pre + dt_b)
        beta_all = jnp.where(lane_o == e, beta, beta_all)
        g_all = jnp.where(lane_o == e, g, g_all)
    bcol[...] = beta_all
    for j in range(n_chunks):
        g_b = _cumsum_rows(g_all[j * CHUNK:(j + 1) * CHUNK])
        gcol[j * CHUNK:(j + 1) * CHUNK, :] = g_b
        grow[j] = g_b.T[0:8, 0:CHUNK]

    tri = _tri(CHUNK)
    strict = _tri(CHUNK, strict=True)
    zero_blk = jnp.zeros((CHUNK, CHUNK), F32)

    def chunk(j, carry):
        rows = pl.ds(pl.multiple_of(j * CHUNK, CHUNK), CHUNK)
        brow = pl.ds(pl.multiple_of(j * CHUNK + CONV_PAD, 8), CHUNK)
        g_rows = grow[j]
        heads_q = range(GDN_QK_PER_STEP)
        heads_v = range(GDN_V_PER_STEP)
        qs = [qbuf[brow, i * B_DK:(i + 1) * B_DK] for i in heads_q]
        ks = [kbuf[brow, i * B_DK:(i + 1) * B_DK] for i in heads_q]
        kks = [_bdot_nt(k, k) for k in ks]
        qks = [_bdot_nt(q, k) for q, k in zip(qs, ks)]
        g_cs = [gcol[rows, e:e + 1] for e in heads_v]
        betas = [bcol[rows, e:e + 1] for e in heads_v]
        decs = [jnp.exp(jnp.where(tri, g_cs[e] - g_rows[e:e + 1, :], -jnp.inf)) for e in heads_v]
        a_blks = [jnp.where(strict, betas[e] * decs[e] * kks[e // B_REP], 0.0) for e in heads_v]
        rhs = [jnp.concatenate([betas[e] * vbuf[brow, e * B_DV:(e + 1) * B_DV],
                                (betas[e] * jnp.exp(g_cs[e])) * ks[e // B_REP]], axis=-1) for e in heads_v]
        a_bds = [jnp.concatenate(
            [jnp.concatenate([a_blks[i * B_REP + r] if r == c else zero_blk for c in range(B_REP)], axis=1)
             for r in range(B_REP)], axis=0) for i in heads_q]
        t_invs = _unit_lower_inverses(a_bds, CHUNK)
        t_parts = [_split2(t) for t in t_invs]
        r_parts = [_split2(jnp.concatenate(rhs[i * B_REP:(i + 1) * B_REP], axis=0)) for i in heads_q]
        sols = [_dot3_parts(th, tl, rh, rl) for (th, tl), (rh, rl) in zip(t_parts, r_parts)]
        sol_e = [sols[e // B_REP][(e % B_REP) * CHUNK:(e % B_REP + 1) * CHUNK] for e in heads_v]
        s_old = [s_ref[e] for e in heads_v]
        ws = [sol_e[e][:, :B_DV] - _bdot(sol_e[e][:, B_DV:], s_old[e]) for e in heads_v]
        o_inter = [_bdot(qs[e // B_REP] * jnp.exp(g_cs[e]), s_old[e]) for e in heads_v]
        o_intra = [_bdot(decs[e] * qks[e // B_REP], ws[e]) for e in heads_v]
        for e in heads_v:
            g_end = g_cs[e][CHUNK - 1:CHUNK, :]
            s_ref[e] = jnp.exp(g_end) * s_old[e] + _bdot_tn(ks[e // B_REP] * jnp.exp(g_end - g_cs[e]), ws[e])
        for e in heads_v:
            zc = z_ref[rows, e * B_DV:(e + 1) * B_DV]
            o = o_intra[e] + o_inter[e]
            o_ref[rows, e * B_DV:(e + 1) * B_DV] = (_rms(o) * gn_ref[...] * _silu(zc)).astype(o_ref.dtype)
        return carry

    lax.fori_loop(0, n_chunks, chunk, 0)
    qbuf[0:CONV_PAD, :] = q_tail
    kbuf[0:CONV_PAD, :] = k_tail
    vbuf[0:CONV_PAD, :] = v_tail

    @pl.when(l == pl.num_programs(2) - 1)
    def _():
        s_out_ref[0] = s_ref[...]


def _gdn_prompt(proj, ba, b, l, conv_w, a_log, dt_bias, g_norm):
    lblk = _pick_tile(l, SEQ_BLOCK, CHUNK)
    nl = l // lblk
    qw = GDN_QK_PER_STEP * B_DK
    vw = GDN_V_PER_STEP * B_DV
    k_off = B_QK_DIM // qw
    v_off = 2 * B_QK_DIM // vw
    z_off = B_CONV_DIM // vw
    return pl.pallas_call(
        functools.partial(_gdn_chunk_kernel, lblk=lblk),
        grid=(b, B_QK_HEADS // GDN_QK_PER_STEP, nl),
        in_specs=[pl.BlockSpec((lblk, qw), lambda bi, h, li: (bi * nl + li, h)),
                  pl.BlockSpec((lblk, qw), lambda bi, h, li: (bi * nl + li, k_off + h)),
                  pl.BlockSpec((lblk, vw), lambda bi, h, li: (bi * nl + li, v_off + h)),
                  pl.BlockSpec((lblk, vw), lambda bi, h, li: (bi * nl + li, z_off + h)),
                  pl.BlockSpec((lblk, 2 * B_V_HEADS), lambda bi, h, li: (bi * nl + li, 0)),
                  pl.BlockSpec((B_CONV, qw), lambda bi, h, li: (0, h)),
                  pl.BlockSpec((B_CONV, qw), lambda bi, h, li: (0, k_off + h)),
                  pl.BlockSpec((B_CONV, vw), lambda bi, h, li: (0, v_off + h)),
                  pl.BlockSpec((1, B_V_HEADS), lambda bi, h, li: (0, 0)),
                  pl.BlockSpec((1, B_V_HEADS), lambda bi, h, li: (0, 0)),
                  pl.BlockSpec((1, B_DV), lambda bi, h, li: (0, 0))],
        out_specs=[pl.BlockSpec((lblk, vw), lambda bi, h, li: (bi * nl + li, h)),
                   pl.BlockSpec((1, GDN_V_PER_STEP, B_DK, B_DV), lambda bi, h, li: (bi, h, 0, 0))],
        out_shape=[jax.ShapeDtypeStruct((b * l, B_V_DIM), BF16),
                   jax.ShapeDtypeStruct((b, B_V_HEADS, B_DK, B_DV), F32)],
        scratch_shapes=[pltpu.VMEM((GDN_V_PER_STEP, B_DK, B_DV), F32),
                        pltpu.VMEM((CONV_PAD + lblk, qw), F32),
                        pltpu.VMEM((CONV_PAD + lblk, qw), F32),
                        pltpu.VMEM((CONV_PAD + lblk, vw), F32),
                        pltpu.VMEM((lblk, LANES), F32),
                        pltpu.VMEM((lblk, LANES), F32),
                        pltpu.VMEM((lblk // CHUNK, 8, CHUNK), F32)],
        compiler_params=_params("parallel", "parallel", "arbitrary"),
        name="gdn_chunks",
    )(proj, proj, proj, proj, ba, conv_w, conv_w, conv_w,
      a_log.reshape(1, -1), dt_bias.reshape(1, -1), g_norm.reshape(1, -1))


GDN_STEP_BATCH = 2
B_ROWS = B_CONV_DIM // 128


def _gdn_step_kernel(p_ref, cv_ref, cw_ref, ba_ref, alog_ref, dtb_ref, gn_ref, s_ref, o_ref, s_out_ref):
    ba = ba_ref[0]
    beta_all = jax.nn.sigmoid(ba[:, :B_V_HEADS])
    g_all = -jnp.exp(alog_ref[...]) * _softplus(ba[:, B_V_HEADS:] + dtb_ref[...])
    decay_all = jnp.exp(g_all)
    iota_h = lax.broadcasted_iota(jnp.int32, (B_V_HEADS, B_DV), 0)
    for bi in range(GDN_STEP_BATCH):
        y = cw_ref[B_CONV - 1] * p_ref[bi, 0:B_ROWS, :]
        for j in range(B_CONV - 1):
            y = y + cw_ref[j] * cv_ref[bi, j]
        y = _silu(y)
        q = _l2norm(y[0:B_QK_HEADS]) * (B_DK ** -0.5)
        k = _l2norm(y[B_QK_HEADS:2 * B_QK_HEADS])
        v = y[2 * B_QK_HEADS:]
        z = p_ref[bi, B_ROWS:B_ROWS + B_V_HEADS, :]
        q_t, k_t = q.T, k.T
        qk = jnp.sum(q * k, axis=-1, keepdims=True)
        o = jnp.zeros((B_V_HEADS, B_DV), F32)
        for hv in range(B_V_HEADS):
            hq = hv // B_REP
            beta = beta_all[bi:bi + 1, hv:hv + 1]
            decay = decay_all[bi:bi + 1, hv:hv + 1]
            s = s_ref[bi, hv]
            k_col = k_t[:, hq:hq + 1]
            w = beta * v[hv:hv + 1, :] - (beta * decay) * jnp.sum(k_col * s, axis=0, keepdims=True)
            oh = qk[hq:hq + 1, :] * w + decay * jnp.sum(q_t[:, hq:hq + 1] * s, axis=0, keepdims=True)
            s_out_ref[bi, hv] = decay * s + k_col * w
            o = jnp.where(iota_h == hv, oh, o)
        o_ref[bi] = (_rms(o) * gn_ref[...] * _silu(z)).astype(o_ref.dtype)


def _gdn_step(proj, ba, conv0, conv_w, a_log, dt_bias, g_norm, s0):
    b = proj.shape[0]
    bb = GDN_STEP_BATCH
    p3 = proj.reshape(b, -1, 128)
    o, s = pl.pallas_call(
        _gdn_step_kernel,
        grid=(b // bb,),
        in_specs=[pl.BlockSpec((bb, p3.shape[1], 128), lambda i: (i, 0, 0)),
                  pl.BlockSpec((bb, B_CONV - 1, B_ROWS, 128), lambda i: (i, 0, 0, 0)),
                  pl.BlockSpec((B_CONV, B_ROWS, 128), lambda i: (0, 0, 0)),
                  pl.BlockSpec((1, bb, 2 * B_V_HEADS), lambda i: (i, 0, 0)),
                  pl.BlockSpec((1, B_V_HEADS), lambda i: (0, 0)),
                  pl.BlockSpec((1, B_V_HEADS), lambda i: (0, 0)),
                  pl.BlockSpec((1, B_DV), lambda i: (0, 0)),
                  pl.BlockSpec((bb, B_V_HEADS, B_DK, B_DV), lambda i: (i, 0, 0, 0))],
        out_specs=[pl.BlockSpec((bb, B_V_HEADS, B_DV), lambda i: (i, 0, 0)),
                   pl.BlockSpec((bb, B_V_HEADS, B_DK, B_DV), lambda i: (i, 0, 0, 0))],
        out_shape=[jax.ShapeDtypeStruct((b, B_V_HEADS, B_DV), BF16), jax.ShapeDtypeStruct(s0.shape, F32)],
        compiler_params=_params("parallel"),
        name="gdn_step",
    )(p3, conv0.reshape(b, B_CONV - 1, B_ROWS, 128), conv_w.reshape(B_CONV, B_ROWS, 128),
      ba.reshape(b // bb, bb, 2 * B_V_HEADS), a_log.reshape(1, -1), dt_bias.reshape(1, -1), g_norm.reshape(1, -1), s0)
    return o.reshape(b, B_V_DIM), s


D_GROUP = 16
D_GROUPS = D_MODEL // D_GROUP
D_STATE = 64
S5_LANES = 128
S5_GROUPS = S5_LANES // D_GROUP
S5_HALF = S5_GROUPS * D_STATE
S5_BLOCKS = D_MODEL // S5_LANES
S5_SEGMENTS = 8


def _cmul(ar, ai, br, bi):
    return ar * br - ai * bi, ar * bi + ai * br


def _gelu_tanh(x):
    return 0.5 * x * (1.0 + jnp.tanh(math.sqrt(2.0 / math.pi) * (x + 0.044715 * (x * x * x))))


def _s5_tables(a_re, a_im, b_re, b_im, c_re, c_im, log_dt, seg_len):
    lam_re, lam_im = a_re.astype(F32), a_im.astype(F32)
    dt = jnp.exp(log_dt.astype(F32))[:, None]
    mag = jnp.exp(dt * lam_re)
    ab_re, ab_im = mag * jnp.cos(dt * lam_im), mag * jnp.sin(dt * lam_im)
    inv = 1.0 / (lam_re * lam_re + lam_im * lam_im)
    e_re, e_im = _cmul(ab_re - 1.0, ab_im, lam_re * inv, -lam_im * inv)
    bb_re, bb_im = _cmul(e_re[..., None], e_im[..., None], b_re.astype(F32), b_im.astype(F32))
    j = S5_BLOCKS
    eye = jnp.eye(S5_GROUPS, dtype=F32)

    def pack_rows(t):
        return t.reshape(j, 1, S5_HALF)

    def bdiag_in(t):
        t = t.reshape(j, S5_GROUPS, D_STATE, D_GROUP)
        return jnp.einsum('jgpc,gh->jgchp', t, eye).reshape(j, S5_LANES, S5_HALF)

    def bdiag_out(t):
        t = t.reshape(j, S5_GROUPS, D_GROUP, D_STATE)
        return jnp.einsum('jgcp,gh->jgphc', t, eye).reshape(j, S5_HALF, S5_LANES)

    ab = jnp.concatenate([pack_rows(ab_re), pack_rows(ab_im)], axis=-1)
    w_bu = jnp.concatenate([bdiag_in(bb_re), bdiag_in(bb_im)], axis=-1)
    w_c = jnp.concatenate([bdiag_out(c_re.astype(F32)), -bdiag_out(c_im.astype(F32))], axis=1)
    pr, pi = pack_rows(ab_re), pack_rows(ab_im)
    n = 1
    while n < seg_len:
        tr, ti = pr[:, n - 1:n], pi[:, n - 1:n]
        nr, ni = _cmul(pr, pi, tr, ti)
        pr, pi = jnp.concatenate([pr, nr], axis=1), jnp.concatenate([pi, ni], axis=1)
        n *= 2
    pw = jnp.concatenate([pr[:, :seg_len], pi[:, :seg_len]], axis=-1)
    return ab, w_bu, w_c, pw


def _s5_scan_kernel(x_ref, ab_ref, wbu_ref, wc_ref, pw_ref, dsk_ref, y_ref, s_out_ref, buf, *, seq, seg):
    nc = S5_HALF // 128
    x = x_ref[...]
    bu = _bdot(x, wbu_ref[0])
    for c in range(2 * nc):
        buf[c] = bu[:, c * 128:(c + 1) * 128]

    for c in range(nc):
        a_re = jnp.broadcast_to(ab_ref[0, :, c * 128:(c + 1) * 128], (S5_SEGMENTS, 128))
        a_im = jnp.broadcast_to(ab_ref[0, :, S5_HALF + c * 128:S5_HALF + (c + 1) * 128], (S5_SEGMENTS, 128))

        def step(t, carry, c=c, a_re=a_re, a_im=a_im):
            s_re, s_im = carry
            rows = pl.ds(t, S5_SEGMENTS, stride=seg)
            n_re = a_re * s_re - a_im * s_im + buf[c, rows, :]
            n_im = a_re * s_im + a_im * s_re + buf[nc + c, rows, :]
            buf[c, rows, :] = n_re
            buf[nc + c, rows, :] = n_im
            return n_re, n_im

        zeros = jnp.zeros((S5_SEGMENTS, 128), F32)
        lax.fori_loop(0, seg, step, (zeros, zeros), unroll=4)

        p_re = pw_ref[0, :, c * 128:(c + 1) * 128]
        p_im = pw_ref[0, :, S5_HALF + c * 128:S5_HALF + (c + 1) * 128]
        pe_re, pe_im = p_re[seg - 1:seg], p_im[seg - 1:seg]
        c_re, c_im = buf[c, seg - 1:seg, :], buf[nc + c, seg - 1:seg, :]
        for r in range(1, S5_SEGMENTS):
            lo = r * seg
            l_re, l_im = buf[c, lo:lo + seg, :], buf[nc + c, lo:lo + seg, :]
            f_re, f_im = _cmul(p_re, p_im, c_re, c_im)
            buf[c, lo:lo + seg, :] = l_re + f_re
            buf[nc + c, lo:lo + seg, :] = l_im + f_im
            e_re, e_im = _cmul(pe_re, pe_im, c_re, c_im)
            c_re, c_im = l_re[seg - 1:seg] + e_re, l_im[seg - 1:seg] + e_im

    y = dsk_ref[...] * x
    for c in range(2 * nc):
        s_out_ref[0, 0, :, c * 128:(c + 1) * 128] = buf[c, seq - 1:seq, :]
        y = y + _bdot(buf[c], wc_ref[0, c * 128:(c + 1) * 128, :])
    y_ref[...] = _gelu_tanh(y).astype(y_ref.dtype)


def _unpack_state(s):
    b = s.shape[0]
    return (s[..., :S5_HALF].reshape(b, D_GROUPS, D_STATE), s[..., S5_HALF:].reshape(b, D_GROUPS, D_STATE))


def _s5_prompt(x, b, l, tables, d_skip):
    d = x.shape[1]
    ab, w_bu, w_c, pw = tables
    seg = l // S5_SEGMENTS
    y, s = pl.pallas_call(
        functools.partial(_s5_scan_kernel, seq=l, seg=seg),
        grid=(b, S5_BLOCKS),
        in_specs=[pl.BlockSpec((l, S5_LANES), lambda bi, j: (bi, j)),
                  pl.BlockSpec((1, 1, 2 * S5_HALF), lambda bi, j: (j, 0, 0)),
                  pl.BlockSpec((1, S5_LANES, 2 * S5_HALF), lambda bi, j: (j, 0, 0)),
                  pl.BlockSpec((1, 2 * S5_HALF, S5_LANES), lambda bi, j: (j, 0, 0)),
                  pl.BlockSpec((1, seg, 2 * S5_HALF), lambda bi, j: (j, 0, 0)),
                  pl.BlockSpec((1, S5_LANES), lambda bi, j: (0, j))],
        out_specs=[pl.BlockSpec((l, S5_LANES), lambda bi, j: (bi, j)),
                   pl.BlockSpec((1, 1, 1, 2 * S5_HALF), lambda bi, j: (bi, j, 0, 0))],
        out_shape=[jax.ShapeDtypeStruct((b * l, d), BF16),
                   jax.ShapeDtypeStruct((b, S5_BLOCKS, 1, 2 * S5_HALF), F32)],
        scratch_shapes=[pltpu.VMEM((2 * S5_HALF // 128, l, 128), F32)],
        compiler_params=_params("parallel", "parallel"),
        name="s5_scan",
    )(x, ab, w_bu, w_c, pw, d_skip.reshape(1, d))
    return y, _unpack_state(s[:, :, 0, :])


def _s5_step_kernel(x_ref, s_ref, ab_ref, wbu_ref, wc_ref, dsk_ref, y_ref, s_out_ref):
    hf = S5_HALF
    x = x_ref[...]
    bu = _bdot(x, wbu_ref[0])
    s0 = s_ref[0]
    i_re, i_im = _cmul(ab_ref[0, :, :hf], ab_ref[0, :, hf:], s0[:, :hf], s0[:, hf:])
    s_new = jnp.concatenate([i_re, i_im], axis=-1) + bu
    s_out_ref[0] = s_new
    y = _bdot(s_new, wc_ref[0]) + dsk_ref[...] * x
    y_ref[...] = _gelu_tanh(y).astype(y_ref.dtype)


def _s5_step(x, tables, d_skip, s0_re, s0_im):
    b, d = x.shape
    ab, w_bu, w_c, _ = tables
    s0 = jnp.concatenate([s0_re.reshape(b, S5_BLOCKS, S5_HALF), s0_im.reshape(b, S5_BLOCKS, S5_HALF)], axis=-1)
    s0 = jnp.swapaxes(s0, 0, 1)
    y, s = pl.pallas_call(
        _s5_step_kernel,
        grid=(S5_BLOCKS,),
        in_specs=[pl.BlockSpec((b, S5_LANES), lambda j: (0, j)),
                  pl.BlockSpec((1, b, 2 * S5_HALF), lambda j: (j, 0, 0)),
                  pl.BlockSpec((1, 1, 2 * S5_HALF), lambda j: (j, 0, 0)),
                  pl.BlockSpec((1, S5_LANES, 2 * S5_HALF), lambda j: (j, 0, 0)),
                  pl.BlockSpec((1, 2 * S5_HALF, S5_LANES), lambda j: (j, 0, 0)),
                  pl.BlockSpec((1, S5_LANES), lambda j: (0, j))],
        out_specs=[pl.BlockSpec((b, S5_LANES), lambda j: (0, j)),
                   pl.BlockSpec((1, b, 2 * S5_HALF), lambda j: (j, 0, 0))],
        out_shape=[jax.ShapeDtypeStruct((b, d), BF16), jax.ShapeDtypeStruct((S5_BLOCKS, b, 2 * S5_HALF), F32)],
        compiler_params=_params("parallel"),
        name="s5_step",
    )(x, s0, ab, w_bu, w_c, d_skip.reshape(1, d))
    return y, _unpack_state(jnp.swapaxes(s, 0, 1))


def _glu_kernel(x_ref, w1_ref, w2_ref, o_ref):
    x = x_ref[...]
    y1 = jnp.dot(x, w1_ref[...].astype(BF16), preferred_element_type=F32)
    y2 = jnp.dot(x, w2_ref[...].astype(BF16), preferred_element_type=F32)
    o_ref[...] = y1 * jax.nn.sigmoid(y2)


def _glu_matmul(x, w, tm_target=1664, tn_target=256):
    m, k = x.shape
    n = w.shape[1] // 2
    tm = _pick_tile(m, tm_target, 16)
    tn = _pick_tile(n, tn_target, 128)
    nb = n // tn
    return pl.pallas_call(
        _glu_kernel,
        grid=(m // tm, nb),
        in_specs=[pl.BlockSpec((tm, k), lambda i, j: (i, 0)),
                  pl.BlockSpec((k, tn), lambda i, j: (0, j)),
                  pl.BlockSpec((k, tn), lambda i, j: (0, j + nb))],
        out_specs=pl.BlockSpec((tm, tn), lambda i, j: (i, j)),
        out_shape=jax.ShapeDtypeStruct((m, n), F32),
        compiler_params=_params("parallel", "parallel"),
        name="glu_matmul",
    )(x, w, w)


def kernel(x_prompt, x_sample, state_a_S, state_b_S, state_b_conv, state_c_C, state_c_n, state_c_m, state_d_re, state_d_im, a_w_in, a_lb_logits, a_g_norm, a_w_out, b_w_in, b_conv_w, b_a_log, b_dt_bias, b_g_norm, b_w_out, c_w_in, c_b_gate, c_g_norm, c_w_out, d_a_re, d_a_im, d_b_re, d_b_im, d_c_re, d_c_im, d_skip, d_log_dt, d_w_glu, moe_w_router, moe_b_router, moe_w_gu, moe_w_down, moe_w_gu_s, moe_w_down_s, ln_mix_g, ln_mix_b, ln_ffn_g, ln_ffn_b):
    bp, l, d = x_prompt.shape
    bs = x_sample.shape[0]
    tp = bp * l
    x = jnp.concatenate([x_prompt.reshape(tp, d), x_sample.reshape(bs, d)], axis=0)
    xb = x.astype(BF16)

    def finish_layer(x, h, layer):
        x, xb = _add_ln(x, h, ln_mix_g[layer], ln_mix_b[layer])
        return _moe_ln(x, xb, moe_w_router, moe_b_router, moe_w_gu, moe_w_down, moe_w_gu_s, moe_w_down_s,
                       ln_ffn_g, ln_ffn_b, layer)

    proj = _matmul(xb, a_w_in)
    lb = _hgrn2_lower_bound(a_lb_logits, 0)
    o_p, a_s_p = _hgrn2_prompt(proj, bp, l, lb, a_g_norm)
    o_s, a_s_s = _hgrn2_step(proj[tp:], state_a_S, lb, a_g_norm)
    h = _matmul(jnp.concatenate([o_p, o_s], axis=0), a_w_out)
    x, xb = finish_layer(x, h, 0)

    n_main = B_CONV_DIM + B_V_DIM
    proj = _matmul(xb, b_w_in, n_cols=n_main)
    ba = _matmul(xb, b_w_in[:, n_main:])
    o_p, b_s_p = _gdn_prompt(proj, ba, bp, l, b_conv_w, b_a_log, b_dt_bias, b_g_norm)
    o_s, b_s_s = _gdn_step(proj[tp:], ba[tp:], state_b_conv, b_conv_w, b_a_log, b_dt_bias, b_g_norm, state_b_S)
    qkv_s = proj[tp:, :B_CONV_DIM]
    b_conv_p = jnp.stack([proj[(i + 1) * l - (B_CONV - 1):(i + 1) * l, :B_CONV_DIM] for i in range(bp)], axis=0)
    b_conv_s = jnp.concatenate([state_b_conv[:, 1:, :], qkv_s[:, None, :]], axis=1)
    h = _matmul(jnp.concatenate([o_p, o_s], axis=0), b_w_out)
    x, xb = finish_layer(x, h, 1)

    n_main = 2 * C_QK_DIM + 2 * C_V_DIM
    proj = _matmul(xb, c_w_in, n_cols=n_main)
    gates = _matmul(xb, c_w_in[:, n_main:])
    o_p, c_c_p, c_n_p, c_m_p = _mlstm_prompt(proj, gates, bp, l, c_b_gate, c_g_norm)
    o_s, c_c_s, c_n_s, c_m_s = _mlstm_step(proj[tp:], gates[tp:], c_b_gate, c_g_norm, state_c_C, state_c_n, state_c_m)
    h = _matmul(jnp.concatenate([o_p, o_s], axis=0), c_w_out)
    x, xb = finish_layer(x, h, 2)

    tables = _s5_tables(d_a_re, d_a_im, d_b_re, d_b_im, d_c_re, d_c_im, d_log_dt, l // S5_SEGMENTS)
    y_p, (d_re_p, d_im_p) = _s5_prompt(x, bp, l, tables, d_skip)
    y_s, (d_re_s, d_im_s) = _s5_step(x[tp:], tables, d_skip, state_d_re, state_d_im)
    h = _glu_matmul(jnp.concatenate([y_p, y_s], axis=0), d_w_glu)
    x, xb = finish_layer(x, h, 3)

    y_prompt = x[:tp].reshape(bp, l, d)
    y_sample = x[tp:].reshape(bs, 1, d)
    return (y_prompt, y_sample, a_s_p, a_s_s, b_s_p, b_s_s, b_conv_p, b_conv_s,
            c_c_p, c_c_s, c_n_p, c_n_s, c_m_p, c_m_s, d_re_p, d_re_s, d_im_p, d_im_s)
```

```python
import functools
import math

import jax
import jax.numpy as jnp
from jax import lax
from jax.experimental import pallas as pl
from jax.experimental.pallas import tpu as pltpu

F32 = jnp.float32
BF16 = jnp.bfloat16
HI = lax.Precision.HIGHEST

D_MODEL = 2048
DEPTH = 4
CHUNK = 64
DN_ALPHA = (2 * DEPTH) ** 0.25
LN_EPS = 1e-5
RMS_EPS = 1e-6

VMEM_LIMIT_BYTES = 56 * 1024 * 1024


def _params(*sem):
    return pltpu.CompilerParams(dimension_semantics=sem, vmem_limit_bytes=VMEM_LIMIT_BYTES)


def _pick_tile(n, target, mult):
    best = None
    for t in range(mult, min(n, target) + 1, mult):
        if n % t == 0:
            best = t
    return best if best is not None else n


def _mm_kernel(x_ref, w_ref, o_ref):
    o_ref[...] = jnp.dot(x_ref[...], w_ref[...].astype(BF16), preferred_element_type=F32).astype(o_ref.dtype)


def _matmul(x, w, n_cols=None, out_dtype=F32, tm_target=1664, tn_target=512):
    m, k = x.shape
    n = w.shape[1] if n_cols is None else n_cols
    tm = _pick_tile(m, tm_target, 16)
    tn = _pick_tile(n, tn_target, 128)
    return pl.pallas_call(
        _mm_kernel,
        grid=(m // tm, n // tn),
        in_specs=[pl.BlockSpec((tm, k), lambda i, j: (i, 0)), pl.BlockSpec((k, tn), lambda i, j: (0, j))],
        out_specs=pl.BlockSpec((tm, tn), lambda i, j: (i, j)),
        out_shape=jax.ShapeDtypeStruct((m, n), out_dtype),
        compiler_params=_params("parallel", "parallel"),
        name="matmul",
    )(x, w)


def _layernorm_rows(z, g, b):
    mu = jnp.mean(z, axis=-1, keepdims=True)
    zc = z - mu
    var = jnp.mean(zc * zc, axis=-1, keepdims=True)
    return zc * lax.rsqrt(var + LN_EPS) * g + b


def _add_ln_kernel(x_ref, h_ref, g_ref, b_ref, o_ref, ob_ref):
    y = _layernorm_rows(DN_ALPHA * x_ref[...] + h_ref[...], g_ref[...], b_ref[...])
    o_ref[...] = y
    ob_ref[...] = y.astype(BF16)


def _add_ln(x, h, g, b, tm_target=640):
    m, d = x.shape
    tm = _pick_tile(m, tm_target, 16)
    row = pl.BlockSpec((tm, d), lambda i: (i, 0))
    vec = pl.BlockSpec((1, d), lambda i: (0, 0))
    return pl.pallas_call(
        _add_ln_kernel,
        grid=(m // tm,),
        in_specs=[row, row, vec, vec],
        out_specs=[row, row],
        out_shape=[jax.ShapeDtypeStruct((m, d), F32), jax.ShapeDtypeStruct((m, d), BF16)],
        compiler_params=_params("parallel"),
        name="add_ln",
    )(x, h, g.reshape(1, d), b.reshape(1, d))


def _silu(x):
    return x * jax.nn.sigmoid(x)


def _ffn_kernel(x_ref, wgu_ref, wd_ref, o_ref, *, dh):
    gu = jnp.dot(x_ref[...], wgu_ref[0].astype(BF16), preferred_element_type=F32)
    a = _silu(gu[:, :dh]) * gu[:, dh:]
    o_ref[...] = jnp.dot(a.astype(BF16), wd_ref[0].astype(BF16), preferred_element_type=F32)


def _dense_ffn(xb, w_gu, w_down, layer, tm_target=640):
    m, d = xb.shape
    dh = w_down.shape[1]
    tm = _pick_tile(m, tm_target, 16)
    return pl.pallas_call(
        functools.partial(_ffn_kernel, dh=dh),
        grid=(m // tm,),
        in_specs=[pl.BlockSpec((tm, d), lambda i: (i, 0)),
                  pl.BlockSpec((1, d, 2 * dh), lambda i: (layer, 0, 0)),
                  pl.BlockSpec((1, dh, d), lambda i: (layer, 0, 0))],
        out_specs=pl.BlockSpec((tm, d), lambda i: (i, 0)),
        out_shape=jax.ShapeDtypeStruct((m, d), F32),
        compiler_params=_params("parallel"),
        name="shared_ffn",
    )(xb, w_gu, w_down)


N_EXPERTS = 64
TOP_K = 8
N_ROUTE_GROUPS = 8
TOPK_ROUTE_GROUPS = 4
GROUP_SIZE = N_EXPERTS // N_ROUTE_GROUPS
ROUTED_SCALE = 2.5
EXPERT_TILE = 256


def _router_kernel(x_ref, wt_ref, b_ref, e_ref, g_ref):
    logits = lax.dot_general(wt_ref[...].astype(BF16), x_ref[...], (((1,), (1,)), ((), ())),
                             preferred_element_type=F32)
    scores = jax.nn.sigmoid(logits)
    biased = scores + b_ref[...]
    tm = scores.shape[1]
    neg = jnp.float32(-jnp.inf)
    iota_g = lax.broadcasted_iota(jnp.int32, (GROUP_SIZE, tm), 0)
    gscore = jnp.zeros((N_ROUTE_GROUPS, tm), F32)
    iota_r = lax.broadcasted_iota(jnp.int32, (N_ROUTE_GROUPS, tm), 0)
    for g in range(N_ROUTE_GROUPS):
        v = biased[g * GROUP_SIZE:(g + 1) * GROUP_SIZE, :]
        m1 = jnp.max(v, axis=0, keepdims=True)
        i1 = jnp.min(jnp.where(v == m1, iota_g, GROUP_SIZE), axis=0, keepdims=True)
        m2 = jnp.max(jnp.where(iota_g == i1, neg, v), axis=0, keepdims=True)
        gscore = jnp.where(iota_r == g, m1 + m2, gscore)
    ok = jnp.zeros((N_ROUTE_GROUPS, tm), jnp.int32)
    for _ in range(TOPK_ROUTE_GROUPS):
        m = jnp.max(gscore, axis=0, keepdims=True)
        gi = jnp.min(jnp.where(gscore == m, iota_r, N_ROUTE_GROUPS), axis=0, keepdims=True)
        hit = iota_r == gi
        ok = jnp.where(hit, 1, ok)
        gscore = jnp.where(hit, neg, gscore)
    masked = jnp.concatenate(
        [jnp.where(ok[g:g + 1, :] > 0, biased[g * GROUP_SIZE:(g + 1) * GROUP_SIZE, :], neg)
         for g in range(N_ROUTE_GROUPS)], axis=0)
    iota_e = lax.broadcasted_iota(jnp.int32, (N_EXPERTS, tm), 0)
    iota_k = lax.broadcasted_iota(jnp.int32, (TOP_K, tm), 0)
    top_e = jnp.zeros((TOP_K, tm), jnp.int32)
    gate = jnp.zeros((TOP_K, tm), F32)
    for k in range(TOP_K):
        m = jnp.max(masked, axis=0, keepdims=True)
        ei = jnp.min(jnp.where(masked == m, iota_e, N_EXPERTS), axis=0, keepdims=True)
        hit = iota_e == ei
        gk = jnp.sum(jnp.where(hit, scores, 0.0), axis=0, keepdims=True)
        top_e = jnp.where(iota_k == k, ei, top_e)
        gate = jnp.where(iota_k == k, gk, gate)
        masked = jnp.where(hit, neg, masked)
    e_ref[...] = top_e
    g_ref[...] = ROUTED_SCALE * gate / jnp.sum(gate, axis=0, keepdims=True)


def _router(x, w_router_t, b_router, tm_target=640):
    t, d = x.shape
    tm = _pick_tile(t, tm_target, 128)
    return pl.pallas_call(
        _router_kernel,
        grid=(t // tm,),
        in_specs=[pl.BlockSpec((tm, d), lambda i: (i, 0)),
                  pl.BlockSpec((N_EXPERTS, d), lambda i: (0, 0)),
                  pl.BlockSpec((N_EXPERTS, 1), lambda i: (0, 0))],
        out_specs=[pl.BlockSpec((TOP_K, tm), lambda i: (0, i)), pl.BlockSpec((TOP_K, tm), lambda i: (0, i))],
        out_shape=[jax.ShapeDtypeStruct((TOP_K, t), jnp.int32), jax.ShapeDtypeStruct((TOP_K, t), F32)],
        compiler_params=_params("parallel"),
        name="router",
    )(x, w_router_t, b_router.reshape(N_EXPERTS, 1))


LANES = 128


def _dispatch_kernel(e_ref, pos_ref, cnt_ref, rank_ref):
    n_blocks = e_ref.shape[1] // LANES
    iota_e = lax.broadcasted_iota(jnp.int32, (N_EXPERTS, LANES), 0)
    incl = (lax.broadcasted_iota(jnp.int32, (LANES, LANES), 0)
            <= lax.broadcasted_iota(jnp.int32, (LANES, LANES), 1)).astype(BF16)
    iota_k = lax.broadcasted_iota(jnp.int32, (TOP_K, LANES), 0)

    def pick(e_blk, table):
        out = jnp.zeros((TOP_K, LANES), F32)
        for k in range(TOP_K):
            v = jnp.sum(jnp.where(iota_e == e_blk[k:k + 1, :], table, 0.0), axis=0, keepdims=True)
            out = jnp.where(iota_k == k, v, out)
        return out

    def count_block(b, carry):
        cols = pl.ds(pl.multiple_of(b * LANES, LANES), LANES)
        e_blk = e_ref[:, cols]
        member = jnp.zeros((N_EXPERTS, LANES), F32)
        for k in range(TOP_K):
            member = member + (iota_e == e_blk[k:k + 1, :]).astype(F32)
        seen = jnp.dot(member.astype(BF16), incl, preferred_element_type=F32) + carry
        rank_ref[:, cols] = pick(e_blk, seen - member)
        return jnp.broadcast_to(seen[:, LANES - 1:LANES], (N_EXPERTS, LANES))

    counts = lax.fori_loop(0, n_blocks, count_block, jnp.zeros((N_EXPERTS, LANES), F32))
    cnt_ref[...] = counts.astype(jnp.int32)
    padded = jnp.ceil(counts * (1.0 / EXPERT_TILE)) * EXPERT_TILE
    start = jnp.dot(_tri(N_EXPERTS, strict=True).astype(F32), padded, preferred_element_type=F32, precision=HI)

    def place_block(b, carry):
        cols = pl.ds(pl.multiple_of(b * LANES, LANES), LANES)
        pos_ref[:, cols] = (pick(e_ref[:, cols], start) + rank_ref[:, cols]).astype(jnp.int32)
        return carry

    lax.fori_loop(0, n_blocks, place_block, 0)


def _dispatch_tables(top_e_t, n_tiles):
    k, t = top_e_t.shape
    pos_t, cnt = pl.pallas_call(
        _dispatch_kernel,
        out_shape=[jax.ShapeDtypeStruct((k, t), jnp.int32), jax.ShapeDtypeStruct((N_EXPERTS, LANES), jnp.int32)],
        scratch_shapes=[pltpu.VMEM((k, t), F32)],
        compiler_params=pltpu.CompilerParams(vmem_limit_bytes=VMEM_LIMIT_BYTES),
        name="dispatch",
    )(top_e_t)
    counts = cnt[:, 0]
    pends = jnp.cumsum((counts + EXPERT_TILE - 1) // EXPERT_TILE * EXPERT_TILE)
    pos = pos_t.T.reshape(-1)
    tok = jnp.arange(t * k, dtype=jnp.int32) // k
    row_tok = jnp.zeros((n_tiles * EXPERT_TILE,), jnp.int32).at[pos].set(tok, unique_indices=True)
    tile_start = jnp.arange(n_tiles, dtype=jnp.int32) * EXPERT_TILE
    tile_e = jnp.minimum(jnp.sum((pends[None, :] <= tile_start[:, None]).astype(jnp.int32), axis=1), N_EXPERTS - 1)
    n_used = (pends[-1] // EXPERT_TILE).astype(jnp.int32).reshape(1)
    return row_tok, tile_e, n_used, pos


def _expert_kernel(row_tok_ref, tile_e_ref, n_used_ref, x_hbm, wgu_ref, wd_ref, o_ref,
                   xbuf, sem, wgu_b, wd_b, *, dh):
    i = pl.program_id(0)
    n_used = n_used_ref[0]
    slot = i % 2

    def issue(tile, s):
        base = tile * EXPERT_TILE

        def body(r, c):
            tok = row_tok_ref[base + r]
            pltpu.make_async_copy(x_hbm.at[pl.ds(tok, 1)], xbuf.at[s, pl.ds(r, 1)], sem.at[s]).start()
            return c

        lax.fori_loop(0, EXPERT_TILE, body, 0, unroll=8)

    @pl.when(jnp.logical_and(i == 0, n_used > 0))
    def _():
        issue(0, 0)

    @pl.when(i + 1 < n_used)
    def _():
        issue(i + 1, 1 - slot)

    e = tile_e_ref[i]
    e_prev = tile_e_ref[jnp.maximum(i - 1, 0)]

    @pl.when(jnp.logical_and(i < n_used, jnp.logical_or(i == 0, e != e_prev)))
    def _():
        wgu_b[...] = wgu_ref[0, 0].astype(BF16)
        wd_b[...] = wd_ref[0, 0].astype(BF16)

    @pl.when(i < n_used)
    def _():
        pltpu.make_async_copy(x_hbm.at[pl.ds(0, EXPERT_TILE)], xbuf.at[slot], sem.at[slot]).wait()
        gu = jnp.dot(xbuf[slot].astype(BF16), wgu_b[...], preferred_element_type=F32)
        a = _silu(gu[:, :dh]) * gu[:, dh:]
        o_ref[...] = jnp.dot(a.astype(BF16), wd_b[...], preferred_element_type=F32)

    @pl.when(i >= n_used)
    def _():
        o_ref[...] = jnp.zeros_like(o_ref)


def _routed_experts(x, row_tok, tile_e, n_used, w_gu, w_down, layer):
    t, d = x.shape
    dh = w_down.shape[2]
    n_tiles = tile_e.shape[0]
    grid_spec = pltpu.PrefetchScalarGridSpec(
        num_scalar_prefetch=3,
        grid=(n_tiles,),
        in_specs=[pl.BlockSpec(memory_space=pl.ANY),
                  pl.BlockSpec((1, 1, d, 2 * dh), lambda i, rt, te, nu: (layer, te[i], 0, 0)),
                  pl.BlockSpec((1, 1, dh, d), lambda i, rt, te, nu: (layer, te[i], 0, 0))],
        out_specs=pl.BlockSpec((EXPERT_TILE, d), lambda i, rt, te, nu: (i, 0)),
        scratch_shapes=[pltpu.VMEM((2, EXPERT_TILE, d), F32),
                        pltpu.SemaphoreType.DMA((2,)),
                        pltpu.VMEM((d, 2 * dh), BF16),
                        pltpu.VMEM((dh, d), BF16)],
    )
    return pl.pallas_call(
        functools.partial(_expert_kernel, dh=dh),
        grid_spec=grid_spec,
        out_shape=jax.ShapeDtypeStruct((n_tiles * EXPERT_TILE, d), F32),
        compiler_params=_params("arbitrary"),
        name="routed_experts",
    )(row_tok, tile_e, n_used, x, w_gu, w_down)


COMBINE_TILE = 128


def _combine_kernel(pos_ref, rows_hbm, gate_ref, x_ref, sh_ref, g_ref, b_ref, o_ref, ob_ref, buf, sem):
    i = pl.program_id(0)
    n = pl.num_programs(0)
    slot = i % 2

    def issue(tile, s):
        base = tile * (COMBINE_TILE * TOP_K)

        def body(r, c):
            for k in range(TOP_K):
                p = pos_ref[base + r * TOP_K + k]
                pltpu.make_async_copy(rows_hbm.at[pl.ds(p, 1)], buf.at[s, k, pl.ds(r, 1)], sem.at[s]).start()
            return c

        lax.fori_loop(0, COMBINE_TILE, body, 0)

    @pl.when(i == 0)
    def _():
        issue(0, 0)

    @pl.when(i + 1 < n)
    def _():
        issue(i + 1, 1 - slot)

    for k in range(TOP_K):
        pltpu.make_async_copy(rows_hbm.at[pl.ds(0, COMBINE_TILE)], buf.at[slot, k], sem.at[slot]).wait()
    gate = gate_ref[...]
    z = DN_ALPHA * x_ref[...] + sh_ref[...]
    for k in range(TOP_K):
        z = z + gate[:, k:k + 1] * buf[slot, k]
    y = _layernorm_rows(z, g_ref[...], b_ref[...])
    o_ref[...] = y
    ob_ref[...] = y.astype(BF16)


def _combine_ln(rows, pos, gate, x, shared, g, b):
    t, d = x.shape
    row = lambda w: pl.BlockSpec((COMBINE_TILE, w), lambda i, p: (i, 0))
    vec = pl.BlockSpec((1, d), lambda i, p: (0, 0))
    grid_spec = pltpu.PrefetchScalarGridSpec(
        num_scalar_prefetch=1,
        grid=(t // COMBINE_TILE,),
        in_specs=[pl.BlockSpec(memory_space=pl.ANY), row(TOP_K), row(d), row(d), vec, vec],
        out_specs=[row(d), row(d)],
        scratch_shapes=[pltpu.VMEM((2, TOP_K, COMBINE_TILE, d), F32), pltpu.SemaphoreType.DMA((2,))],
    )
    return pl.pallas_call(
        _combine_kernel,
        grid_spec=grid_spec,
        out_shape=[jax.ShapeDtypeStruct((t, d), F32), jax.ShapeDtypeStruct((t, d), BF16)],
        compiler_params=_params("arbitrary"),
        name="combine_ln",
    )(pos, rows, gate, x, shared, g.reshape(1, d), b.reshape(1, d))


def _moe_ln(x, xb, w_router, b_router, w_gu, w_down, w_gu_s, w_down_s, ln_g, ln_b, layer):
    t = x.shape[0]
    top_e_t, gate_t = _router(xb, w_router[layer].T, b_router[layer])
    n_tiles = (t * TOP_K + N_EXPERTS * (EXPERT_TILE - 1) + EXPERT_TILE - 1) // EXPERT_TILE
    row_tok, tile_e, n_used, pos = _dispatch_tables(top_e_t, n_tiles)
    rows = _routed_experts(x, row_tok, tile_e, n_used, w_gu, w_down, layer)
    shared = _dense_ffn(xb, w_gu_s, w_down_s, layer)
    return _combine_ln(rows, pos, gate_t.T, x, shared, ln_g[layer], ln_b[layer])


SUB = 16
SEQ_BLOCK = 512


def _tri(n, strict=False):
    r = lax.broadcasted_iota(jnp.int32, (n, n), 0)
    c = lax.broadcasted_iota(jnp.int32, (n, n), 1)
    return (r > c) if strict else (r >= c)


def _cumsum_rows(x):
    n = x.shape[0]
    return jnp.dot(_tri(n).astype(F32), x, preferred_element_type=F32, precision=HI)


def _bdot(a, b):
    return jnp.dot(a.astype(BF16), b.astype(BF16), preferred_element_type=F32)


def _bdot_nt(a, b):
    return lax.dot_general(a.astype(BF16), b.astype(BF16), (((1,), (1,)), ((), ())), preferred_element_type=F32)


def _bdot_tn(a, b):
    return lax.dot_general(a.astype(BF16), b.astype(BF16), (((0,), (0,)), ((), ())), preferred_element_type=F32)


def _rms(x):
    return x * lax.rsqrt(jnp.mean(x * x, axis=-1, keepdims=True) + RMS_EPS)


def _row_of(col_b):
    c = col_b.shape[0]
    return col_b.T[0:1, :c]


A_HEADS = 16
A_DK = 128
A_DV = 128
HGRN2_HEADS_PER_STEP = 4


def _hgrn2_chunk_kernel(q_ref, f_ref, v_ref, g_ref, lb_ref, gn_ref, o_ref, s_out_ref, s_ref, *, n_chunks):
    l = pl.program_id(2)

    @pl.when(l == 0)
    def _():
        s_ref[...] = jnp.zeros_like(s_ref)

    lb = lb_ref[...]
    gn = gn_ref[...]
    heads = range(HGRN2_HEADS_PER_STEP)

    def head(x, i):
        return x[:, i * A_DK:(i + 1) * A_DK]

    def chunk(j, carry):
        rows = pl.ds(pl.multiple_of(j * CHUNK, CHUNK), CHUNK)
        q = _silu(q_ref[rows, :])
        f = lb + (1.0 - lb) * jax.nn.sigmoid(f_ref[rows, :])
        k = 1.0 - f
        v = v_ref[rows, :]
        bc = _cumsum_rows(jnp.log(f))
        qe = q * jnp.exp(bc)
        s_old = [s_ref[i] for i in heads]
        o_inter = [_bdot(head(qe, i), s_old[i]) for i in heads]
        outs = [[] for _ in heads]
        for sc in range(CHUNK // SUB):
            lo, hi = sc * SUB, (sc + 1) * SUB
            bref = bc[lo - 1:lo, :] if sc > 0 else jnp.zeros((1, bc.shape[1]), F32)
            qi = q[lo:hi] * jnp.exp(bc[lo:hi] - bref)
            ki = k[:hi] * jnp.exp(bref - bc[:hi])
            causal = (lax.broadcasted_iota(jnp.int32, (SUB, hi), 0) + lo
                      >= lax.broadcasted_iota(jnp.int32, (SUB, hi), 1))
            atts = [jnp.where(causal, _bdot_nt(head(qi, i), head(ki, i)), 0.0) for i in heads]
            for i in heads:
                outs[i].append(_bdot(atts[i], head(v, i)[:hi]))
        b_end = bc[CHUNK - 1:CHUNK, :]
        kd = k * jnp.exp(b_end - bc)
        d_cols = jnp.exp(bc.T[:, CHUNK - 1:CHUNK])
        for i in heads:
            s_ref[i] = d_cols[i * A_DK:(i + 1) * A_DK] * s_old[i] + _bdot_tn(head(kd, i), head(v, i))
        o_n = jnp.concatenate([_rms(jnp.concatenate(outs[i], axis=0) + o_inter[i]) * gn for i in heads], axis=1)
        o_ref[rows, :] = (o_n * _silu(g_ref[rows, :])).astype(o_ref.dtype)
        return carry

    lax.fori_loop(0, n_chunks, chunk, 0)

    @pl.when(l == pl.num_programs(2) - 1)
    def _():
        s_out_ref[0] = s_ref[...]


def _hgrn2_prompt(proj, b, l, lb, g_norm):
    lblk = _pick_tile(l, SEQ_BLOCK, CHUNK)
    nl = l // lblk
    hg = HGRN2_HEADS_PER_STEP
    groups = A_HEADS // hg
    part = lambda p: pl.BlockSpec((lblk, hg * A_DK), lambda bi, h, li, p=p: (bi * nl + li, p * groups + h))
    return pl.pallas_call(
        functools.partial(_hgrn2_chunk_kernel, n_chunks=lblk // CHUNK),
        grid=(b, groups, nl),
        in_specs=[part(0), part(1), part(2), part(3),
                  pl.BlockSpec((1, hg * A_DK), lambda bi, h, li: (0, h)),
                  pl.BlockSpec((1, A_DV), lambda bi, h, li: (0, 0))],
        out_specs=[pl.BlockSpec((lblk, hg * A_DV), lambda bi, h, li: (bi * nl + li, h)),
                   pl.BlockSpec((1, hg, A_DK, A_DV), lambda bi, h, li: (bi, h, 0, 0))],
        out_shape=[jax.ShapeDtypeStruct((b * l, A_HEADS * A_DV), BF16),
                   jax.ShapeDtypeStruct((b, A_HEADS, A_DK, A_DV), F32)],
        scratch_shapes=[pltpu.VMEM((hg, A_DK, A_DV), F32)],
        compiler_params=_params("parallel", "parallel", "arbitrary"),
        name="hgrn2_chunks",
    )(proj, proj, proj, proj, lb.reshape(1, -1), g_norm.reshape(1, -1))


HGRN2_STEP_BATCH = 4


def _hgrn2_step_kernel(p_ref, s_ref, lb_ref, gn_ref, o_ref, s_out_ref):
    lb = lb_ref[...]
    iota_h = lax.broadcasted_iota(jnp.int32, (A_HEADS, A_DV), 0)
    for bi in range(HGRN2_STEP_BATCH):
        p = p_ref[bi]
        q = _silu(p[0:A_HEADS])
        f = lb + (1.0 - lb) * jax.nn.sigmoid(p[A_HEADS:2 * A_HEADS])
        v = p[2 * A_HEADS:3 * A_HEADS]
        g = p[3 * A_HEADS:4 * A_HEADS]
        q_t, f_t = q.T, f.T
        o = jnp.zeros((A_HEADS, A_DV), F32)
        for h in range(A_HEADS):
            fc = f_t[:, h:h + 1]
            s_new = fc * s_ref[bi, h] + (1.0 - fc) * v[h:h + 1, :]
            s_out_ref[bi, h] = s_new
            oh = jnp.sum(q_t[:, h:h + 1] * s_new, axis=0, keepdims=True)
            o = jnp.where(iota_h == h, oh, o)
        o_ref[bi] = (_rms(o) * gn_ref[...] * _silu(g)).astype(o_ref.dtype)


def _hgrn2_step(proj, s0, lb, g_norm):
    b = proj.shape[0]
    bb = HGRN2_STEP_BATCH
    p3 = proj.reshape(b, 4 * A_HEADS, A_DK)
    o, s = pl.pallas_call(
        _hgrn2_step_kernel,
        grid=(b // bb,),
        in_specs=[pl.BlockSpec((bb, 4 * A_HEADS, A_DK), lambda i: (i, 0, 0)),
                  pl.BlockSpec((bb, A_HEADS, A_DK, A_DV), lambda i: (i, 0, 0, 0)),
                  pl.BlockSpec((A_HEADS, A_DK), lambda i: (0, 0)),
                  pl.BlockSpec((1, A_DV), lambda i: (0, 0))],
        out_specs=[pl.BlockSpec((bb, A_HEADS, A_DV), lambda i: (i, 0, 0)),
                   pl.BlockSpec((bb, A_HEADS, A_DK, A_DV), lambda i: (i, 0, 0, 0))],
        out_shape=[jax.ShapeDtypeStruct((b, A_HEADS, A_DV), BF16),
                   jax.ShapeDtypeStruct(s0.shape, F32)],
        compiler_params=_params("parallel"),
        name="hgrn2_step",
    )(p3, s0, lb.reshape(A_HEADS, A_DK), g_norm.reshape(1, -1))
    return o.reshape(b, A_HEADS * A_DV), s


def _hgrn2_lower_bound(lb_logits, layer):
    return jnp.cumsum(jax.nn.softmax(lb_logits.astype(F32), axis=0), axis=0)[layer]


C_HEADS = 8
C_DK = 128
C_DV = 256
C_QK_DIM = C_HEADS * C_DK
C_V_DIM = C_HEADS * C_DV
C_GATE_CAP = 15.0
MLSTM_HEADS_PER_STEP = 4


def _log_sigmoid(x):
    return jnp.minimum(x, 0.0) - jnp.log(1.0 + jnp.exp(-jnp.abs(x)))


def _cap(x):
    return C_GATE_CAP * jnp.tanh(x / C_GATE_CAP)


def _mlstm_chunk_kernel(q_ref, k_ref, v_ref, op_ref, gt_ref, bg_ref, gn_ref,
                        o_ref, c_out_ref, n_out_ref, m_out_ref, c_ref, n_ref, m_ref, *, n_chunks):
    h = pl.program_id(1)
    l = pl.program_id(2)

    @pl.when(l == 0)
    def _():
        c_ref[...] = jnp.zeros_like(c_ref)
        n_ref[...] = jnp.zeros_like(n_ref)
        m_ref[...] = jnp.zeros_like(m_ref)

    hg = MLSTM_HEADS_PER_STEP
    heads = range(hg)
    lane = lax.broadcasted_iota(jnp.int32, (CHUNK, 2 * C_HEADS), 1)
    lane_o = lax.broadcasted_iota(jnp.int32, (CHUNK, LANES), 1)
    tri = _tri(CHUNK)

    def chunk(j, carry):
        rows = pl.ds(pl.multiple_of(j * CHUNK, CHUNK), CHUNK)
        gates = _cap(gt_ref[rows, :] + bg_ref[...])
        lsig = _log_sigmoid(gates)
        i_all = jnp.zeros((CHUNK, LANES), F32)
        lf_all = jnp.zeros((CHUNK, LANES), F32)
        for e in heads:
            he = h * hg + e
            i_all = jnp.where(lane_o == e, jnp.sum(jnp.where(lane == he, gates, 0.0), axis=-1, keepdims=True), i_all)
            lf_all = jnp.where(lane_o == e,
                               jnp.sum(jnp.where(lane == he + C_HEADS, lsig, 0.0), axis=-1, keepdims=True), lf_all)
        f_b = _cumsum_rows(lf_all)
        a_b = i_all - f_b
        a_t = a_b.T
        f_cs = [f_b[:, e:e + 1] for e in heads]
        log_ws = [jnp.where(tri, f_cs[e] + a_t[e:e + 1, :CHUNK], -jnp.inf) for e in heads]
        m_prevs = [m_ref[e, 0:1, 0:1] for e in heads]
        log_ss = [f_cs[e] + m_prevs[e] for e in heads]
        m_ts = [jnp.maximum(jnp.max(log_ws[e], axis=-1, keepdims=True), log_ss[e]) for e in heads]
        qss = [q_ref[rows, e * C_DK:(e + 1) * C_DK] * (C_DK ** -0.5) for e in heads]
        ks = [k_ref[rows, e * C_DK:(e + 1) * C_DK] for e in heads]
        vs = [v_ref[rows, e * C_DV:(e + 1) * C_DV] for e in heads]
        qks = [_bdot_nt(qss[e], ks[e]) * jnp.exp(log_ws[e] - m_ts[e]) for e in heads]
        w_ss = [jnp.exp(log_ss[e] - m_ts[e]) for e in heads]
        c_old = [c_ref[e] for e in heads]
        n_old = [n_ref[e] for e in heads]
        inter = [_bdot(qss[e], c_old[e]) for e in heads]
        intra = [_bdot(qks[e], vs[e]) for e in heads]
        hids = []
        for e in heads:
            den = (jnp.sum(qks[e], axis=-1, keepdims=True)
                   + w_ss[e] * jnp.sum(qss[e] * n_old[e], axis=-1, keepdims=True))
            num = intra[e] + w_ss[e] * inter[e]
            hids.append(_rms(num / jnp.maximum(jnp.abs(den), jnp.exp(-m_ts[e]))) * gn_ref[e])
        for e in heads:
            m_end = m_ts[e][CHUNK - 1:CHUNK, :]
            f_end = f_cs[e][CHUNK - 1:CHUNK, :]
            w_end = jnp.exp(f_end + a_b[:, e:e + 1] - m_end)
            s_end = jnp.exp(f_end + m_prevs[e] - m_end)
            kw = ks[e] * w_end
            c_ref[e] = s_end * c_old[e] + _bdot_tn(kw, vs[e])
            n_ref[e] = s_end * n_old[e] + jnp.sum(kw, axis=0, keepdims=True)
            m_ref[e] = jnp.broadcast_to(m_end, (1, LANES))
        o_ref[rows, :] = (jnp.concatenate(hids, axis=1) * jax.nn.sigmoid(op_ref[rows, :])).astype(o_ref.dtype)
        return carry

    lax.fori_loop(0, n_chunks, chunk, 0)

    @pl.when(l == pl.num_programs(2) - 1)
    def _():
        c_out_ref[0] = c_ref[...]
        n_out_ref[0] = n_ref[...]
        m_out_ref[0] = m_ref[...]


def _mlstm_prompt(proj, gates, b, l, b_gate, g_norm):
    lblk = _pick_tile(l, SEQ_BLOCK, CHUNK)
    nl = l // lblk
    hg = MLSTM_HEADS_PER_STEP
    groups = C_HEADS // hg
    qw, vw = hg * C_DK, hg * C_DV
    v_off = 2 * C_QK_DIM // vw
    o, c, n, m = pl.pallas_call(
        functools.partial(_mlstm_chunk_kernel, n_chunks=lblk // CHUNK),
        grid=(b, groups, nl),
        in_specs=[pl.BlockSpec((lblk, qw), lambda bi, h, li: (bi * nl + li, h)),
                  pl.BlockSpec((lblk, qw), lambda bi, h, li: (bi * nl + li, groups + h)),
                  pl.BlockSpec((lblk, vw), lambda bi, h, li: (bi * nl + li, v_off + h)),
                  pl.BlockSpec((lblk, vw), lambda bi, h, li: (bi * nl + li, v_off + groups + h)),
                  pl.BlockSpec((lblk, 2 * C_HEADS), lambda bi, h, li: (bi * nl + li, 0)),
                  pl.BlockSpec((1, 2 * C_HEADS), lambda bi, h, li: (0, 0)),
                  pl.BlockSpec((hg, 1, C_DV), lambda bi, h, li: (h, 0, 0))],
        out_specs=[pl.BlockSpec((lblk, vw), lambda bi, h, li: (bi * nl + li, h)),
                   pl.BlockSpec((1, hg, C_DK, C_DV), lambda bi, h, li: (bi, h, 0, 0)),
                   pl.BlockSpec((1, hg, 1, C_DK), lambda bi, h, li: (bi, h, 0, 0)),
                   pl.BlockSpec((1, hg, 1, LANES), lambda bi, h, li: (bi, h, 0, 0))],
        out_shape=[jax.ShapeDtypeStruct((b * l, C_V_DIM), BF16),
                   jax.ShapeDtypeStruct((b, C_HEADS, C_DK, C_DV), F32),
                   jax.ShapeDtypeStruct((b, C_HEADS, 1, C_DK), F32),
                   jax.ShapeDtypeStruct((b, C_HEADS, 1, LANES), F32)],
        scratch_shapes=[pltpu.VMEM((hg, C_DK, C_DV), F32), pltpu.VMEM((hg, 1, C_DK), F32),
                        pltpu.VMEM((hg, 1, LANES), F32)],
        compiler_params=_params("parallel", "parallel", "arbitrary"),
        name="mlstm_chunks",
    )(proj, proj, proj, proj, gates, b_gate.reshape(1, -1), g_norm.reshape(C_HEADS, 1, C_DV))
    return o, c, n[:, :, 0, :], m[:, :, 0, 0]


MLSTM_STEP_BATCH = 8


def _mlstm_step_kernel(qk_ref, vo_ref, gt_ref, bg_ref, gn_ref, c_ref, n_ref, m_ref,
                       o_ref, c_out_ref, n_out_ref, m_out_ref):
    iota_h = lax.broadcasted_iota(jnp.int32, (C_HEADS, C_DV), 0)
    iota_hk = lax.broadcasted_iota(jnp.int32, (C_HEADS, C_DK), 0)
    iota_m = lax.broadcasted_iota(jnp.int32, (MLSTM_STEP_BATCH, C_HEADS), 1)
    iota_mb = lax.broadcasted_iota(jnp.int32, (MLSTM_STEP_BATCH, C_HEADS), 0)
    gates = _cap(gt_ref[...] + bg_ref[...])
    m_all = m_ref[...]
    m_new_all = jnp.zeros((MLSTM_STEP_BATCH, C_HEADS), F32)
    for bi in range(MLSTM_STEP_BATCH):
        qs = qk_ref[bi, 0:C_HEADS, :] * (C_DK ** -0.5)
        k = qk_ref[bi, C_HEADS:2 * C_HEADS, :]
        v = vo_ref[bi, C_HEADS:2 * C_HEADS, :]
        op = vo_ref[bi, 2 * C_HEADS:3 * C_HEADS, :]
        q_t, k_t = qs.T, k.T
        n_all = n_ref[bi]
        hid = jnp.zeros((C_HEADS, C_DV), F32)
        n_new = jnp.zeros((C_HEADS, C_DK), F32)
        for h in range(C_HEADS):
            i_g = gates[bi:bi + 1, h:h + 1]
            lf = _log_sigmoid(gates[bi:bi + 1, C_HEADS + h:C_HEADS + h + 1])
            log_s = lf + m_all[bi:bi + 1, h:h + 1]
            m_t = jnp.maximum(i_g, log_s)
            w_i = jnp.exp(i_g - m_t)
            w_s = jnp.exp(log_s - m_t)
            qk = jnp.sum(qs[h:h + 1, :] * k[h:h + 1, :], axis=-1, keepdims=True) * w_i
            c_old = c_ref[bi, h]
            num = qk * v[h:h + 1, :] + w_s * jnp.sum(q_t[:, h:h + 1] * c_old, axis=0, keepdims=True)
            den = qk + w_s * jnp.sum(qs[h:h + 1, :] * n_all[h:h + 1, :], axis=-1, keepdims=True)
            hid = jnp.where(iota_h == h, num / jnp.maximum(jnp.abs(den), jnp.exp(-m_t)), hid)
            c_out_ref[bi, h] = w_s * c_old + (w_i * k_t[:, h:h + 1]) * v[h:h + 1, :]
            n_new = jnp.where(iota_hk == h, w_s * n_all[h:h + 1, :] + w_i * k[h:h + 1, :], n_new)
            m_new_all = jnp.where(jnp.logical_and(iota_m == h, iota_mb == bi), m_t, m_new_all)
        n_out_ref[bi] = n_new
        o_ref[bi] = (_rms(hid) * gn_ref[...] * jax.nn.sigmoid(op)).astype(o_ref.dtype)
    m_out_ref[...] = m_new_all


def _mlstm_step(proj, gates, b_gate, g_norm, c0, n0, m0):
    b = proj.shape[0]
    bb = MLSTM_STEP_BATCH
    qk_view = proj.reshape(b, -1, C_DK)
    vo_view = proj.reshape(b, -1, C_DV)
    o, c, n, m = pl.pallas_call(
        _mlstm_step_kernel,
        grid=(b // bb,),
        in_specs=[pl.BlockSpec((bb, 2 * C_HEADS, C_DK), lambda i: (i, 0, 0)),
                  pl.BlockSpec((bb, 3 * C_HEADS, C_DV), lambda i: (i, 0, 0)),
                  pl.BlockSpec((bb, 2 * C_HEADS), lambda i: (i, 0)),
                  pl.BlockSpec((1, 2 * C_HEADS), lambda i: (0, 0)),
                  pl.BlockSpec((C_HEADS, C_DV), lambda i: (0, 0)),
                  pl.BlockSpec((bb, C_HEADS, C_DK, C_DV), lambda i: (i, 0, 0, 0)),
                  pl.BlockSpec((bb, C_HEADS, C_DK), lambda i: (i, 0, 0)),
                  pl.BlockSpec((bb, C_HEADS), lambda i: (i, 0))],
        out_specs=[pl.BlockSpec((bb, C_HEADS, C_DV), lambda i: (i, 0, 0)),
                   pl.BlockSpec((bb, C_HEADS, C_DK, C_DV), lambda i: (i, 0, 0, 0)),
                   pl.BlockSpec((bb, C_HEADS, C_DK), lambda i: (i, 0, 0)),
                   pl.BlockSpec((bb, C_HEADS), lambda i: (i, 0))],
        out_shape=[jax.ShapeDtypeStruct((b, C_HEADS, C_DV), BF16),
                   jax.ShapeDtypeStruct(c0.shape, F32),
                   jax.ShapeDtypeStruct(n0.shape, F32),
                   jax.ShapeDtypeStruct(m0.shape, F32)],
        compiler_params=_params("parallel"),
        name="mlstm_step",
    )(qk_view, vo_view, gates, b_gate.reshape(1, -1), g_norm, c0, n0, m0)
    return o.reshape(b, C_V_DIM), c, n, m


B_QK_HEADS = 16
B_V_HEADS = 32
B_DK = 128
B_DV = 128
B_CONV = 4
B_QK_DIM = B_QK_HEADS * B_DK
B_V_DIM = B_V_HEADS * B_DV
B_CONV_DIM = 2 * B_QK_DIM + B_V_DIM
B_REP = B_V_HEADS // B_QK_HEADS
CONV_PAD = 8


def _softplus(x):
    return jnp.maximum(x, 0.0) + jnp.log(1.0 + jnp.exp(-jnp.abs(x)))


def _l2norm(x):
    return x * lax.rsqrt(jnp.sum(x * x, axis=-1, keepdims=True) + RMS_EPS)


def _split2(x):
    hi = x.astype(BF16)
    return hi, (x - hi.astype(F32)).astype(BF16)


def _dot3_parts(ah, al, bh, bl):
    n = bh.shape[1]
    r = jnp.dot(ah, jnp.concatenate([bh, bl], axis=1), preferred_element_type=F32)
    return r[:, :n] + r[:, n:] + jnp.dot(al, bh, preferred_element_type=F32)


def _unit_lower_inverses(mats, order):
    n = mats[0].shape[0]
    eye = (lax.broadcasted_iota(jnp.int32, (n, n), 0) == lax.broadcasted_iota(jnp.int32, (n, n), 1)).astype(F32)
    ps = [-a for a in mats]
    ts = [eye + p for p in ps]
    parts = [_split2(p) for p in ps]
    span = 1
    while 2 * span < order:
        ps = [_dot3_parts(ph, pl_, ph, pl_) for ph, pl_ in parts]
        parts = [_split2(p) for p in ps]
        t_parts = [_split2(t) for t in ts]
        ts = [t + _dot3_parts(th, tl, ph, pl_) for t, (th, tl), (ph, pl_) in zip(ts, t_parts, parts)]
        span *= 2
    return ts


GDN_QK_PER_STEP = 4
GDN_V_PER_STEP = GDN_QK_PER_STEP * B_REP


def _gdn_chunk_kernel(q_ref, k_ref, v_ref, z_ref, ba_ref, wq_ref, wk_ref, wv_ref, alog_ref, dtb_ref, gn_ref,
                      o_ref, s_out_ref, s_ref, qbuf, kbuf, vbuf, gcol, bcol, grow, *, lblk):
    h = pl.program_id(1)
    l = pl.program_id(2)
    n_chunks = lblk // CHUNK

    @pl.when(l == 0)
    def _():
        s_ref[...] = jnp.zeros_like(s_ref)
        qbuf[0:CONV_PAD, :] = jnp.zeros((CONV_PAD, qbuf.shape[1]), F32)
        kbuf[0:CONV_PAD, :] = jnp.zeros((CONV_PAD, kbuf.shape[1]), F32)
        vbuf[0:CONV_PAD, :] = jnp.zeros((CONV_PAD, vbuf.shape[1]), F32)

    def conv(x_ref, w_ref, buf, post):
        buf[CONV_PAD:CONV_PAD + lblk, :] = x_ref[...]
        tail = buf[lblk:lblk + CONV_PAD, :]
        taps = B_CONV - 1
        for c in reversed(range(n_chunks)):
            lo = CONV_PAD + c * CHUNK
            y = w_ref[0:1, :] * buf[lo - taps:lo - taps + CHUNK, :]
            for j in range(1, B_CONV):
                y = y + w_ref[j:j + 1, :] * buf[lo - taps + j:lo - taps + j + CHUNK, :]
            buf[lo:lo + CHUNK, :] = post(_silu(y))
        return tail

    def per_head_l2norm(scale):
        def post(y):
            return jnp.concatenate([_l2norm(y[:, i * B_DK:(i + 1) * B_DK]) * scale
                                    for i in range(GDN_QK_PER_STEP)], axis=1)
        return post

    q_tail = conv(q_ref, wq_ref, qbuf, per_head_l2norm(B_DK ** -0.5))
    k_tail = conv(k_ref, wk_ref, kbuf, per_head_l2norm(1.0))
    v_tail = conv(v_ref, wv_ref, vbuf, lambda y: y)

    lane = lax.broadcasted_iota(jnp.int32, (lblk, 2 * B_V_HEADS), 1)
    lane_h = lax.broadcasted_iota(jnp.int32, (1, B_V_HEADS), 1)
    lane_o = lax.broadcasted_iota(jnp.int32, (lblk, LANES), 1)
    ba = ba_ref[...]
    beta_all = jnp.zeros((lblk, LANES), F32)
    g_all = jnp.zeros((lblk, LANES), F32)
    for e in range(GDN_V_PER_STEP):
        hv = h * GDN_V_PER_STEP + e
        beta = jax.nn.sigmoid(jnp.sum(jnp.where(lane == hv, ba, 0.0), axis=-1, keepdims=True))
        a_pre = jnp.sum(jnp.where(lane == hv + B_V_HEADS, ba, 0.0), axis=-1, keepdims=True)
        a_scale = -jnp.exp(jnp.sum(jnp.where(lane_h == hv, alog_ref[...], 0.0), axis=-1, keepdims=True))
        dt_b = jnp.sum(jnp.where(lane_h == hv, dtb_ref[...], 0.0), axis=-1, keepdims=True)
        g = a_scale * _softplus(a_pre + dt_b)
        beta_all = jnp.where(lane_o == e, beta, beta_all)
        g_all = jnp.where(lane_o == e, g, g_all)
    bcol[...] = beta_all
    for j in range(n_chunks):
        g_b = _cumsum_rows(g_all[j * CHUNK:(j + 1) * CHUNK])
        gcol[j * CHUNK:(j + 1) * CHUNK, :] = g_b
        grow[j] = g_b.T[0:8, 0:CHUNK]

    tri = _tri(CHUNK)
    strict = _tri(CHUNK, strict=True)
    zero_blk = jnp.zeros((CHUNK, CHUNK), F32)

    def chunk(j, carry):
        rows = pl.ds(pl.multiple_of(j * CHUNK, CHUNK), CHUNK)
        brow = pl.ds(pl.multiple_of(j * CHUNK + CONV_PAD, 8), CHUNK)
        g_rows = grow[j]
        heads_q = range(GDN_QK_PER_STEP)
        heads_v = range(GDN_V_PER_STEP)
        qs = [qbuf[brow, i * B_DK:(i + 1) * B_DK] for i in heads_q]
        ks = [kbuf[brow, i * B_DK:(i + 1) * B_DK] for i in heads_q]
        kks = [_bdot_nt(k, k) for k in ks]
        qks = [_bdot_nt(q, k) for q, k in zip(qs, ks)]
        g_cs = [gcol[rows, e:e + 1] for e in heads_v]
        betas = [bcol[rows, e:e + 1] for e in heads_v]
        decs = [jnp.exp(jnp.where(tri, g_cs[e] - g_rows[e:e + 1, :], -jnp.inf)) for e in heads_v]
        a_blks = [jnp.where(strict, betas[e] * decs[e] * kks[e // B_REP], 0.0) for e in heads_v]
        rhs = [jnp.concatenate([betas[e] * vbuf[brow, e * B_DV:(e + 1) * B_DV],
                                (betas[e] * jnp.exp(g_cs[e])) * ks[e // B_REP]], axis=-1) for e in heads_v]
        a_bds = [jnp.concatenate(
            [jnp.concatenate([a_blks[i * B_REP + r] if r == c else zero_blk for c in range(B_REP)], axis=1)
             for r in range(B_REP)], axis=0) for i in heads_q]
        t_invs = _unit_lower_inverses(a_bds, CHUNK)
        t_parts = [_split2(t) for t in t_invs]
        r_parts = [_split2(jnp.concatenate(rhs[i * B_REP:(i + 1) * B_REP], axis=0)) for i in heads_q]
        sols = [_dot3_parts(th, tl, rh, rl) for (th, tl), (rh, rl) in zip(t_parts, r_parts)]
        sol_e = [sols[e // B_REP][(e % B_REP) * CHUNK:(e % B_REP + 1) * CHUNK] for e in heads_v]
        s_old = [s_ref[e] for e in heads_v]
        ws = [sol_e[e][:, :B_DV] - _bdot(sol_e[e][:, B_DV:], s_old[e]) for e in heads_v]
        o_inter = [_bdot(qs[e // B_REP] * jnp.exp(g_cs[e]), s_old[e]) for e in heads_v]
        o_intra = [_bdot(decs[e] * qks[e // B_REP], ws[e]) for e in heads_v]
        for e in heads_v:
            g_end = g_cs[e][CHUNK - 1:CHUNK, :]
            s_ref[e] = jnp.exp(g_end) * s_old[e] + _bdot_tn(ks[e // B_REP] * jnp.exp(g_end - g_cs[e]), ws[e])
        for e in heads_v:
            zc = z_ref[rows, e * B_DV:(e + 1) * B_DV]
            o = o_intra[e] + o_inter[e]
            o_ref[rows, e * B_DV:(e + 1) * B_DV] = (_rms(o) * gn_ref[...] * _silu(zc)).astype(o_ref.dtype)
        return carry

    lax.fori_loop(0, n_chunks, chunk, 0)
    qbuf[0:CONV_PAD, :] = q_tail
    kbuf[0:CONV_PAD, :] = k_tail
    vbuf[0:CONV_PAD, :] = v_tail

    @pl.when(l == pl.num_programs(2) - 1)
    def _():
        s_out_ref[0] = s_ref[...]


def _gdn_prompt(proj, ba, b, l, conv_w, a_log, dt_bias, g_norm):
    lblk = _pick_tile(l, SEQ_BLOCK, CHUNK)
    nl = l // lblk
    qw = GDN_QK_PER_STEP * B_DK
    vw = GDN_V_PER_STEP * B_DV
    k_off = B_QK_DIM // qw
    v_off = 2 * B_QK_DIM // vw
    z_off = B_CONV_DIM // vw
    return pl.pallas_call(
        functools.partial(_gdn_chunk_kernel, lblk=lblk),
        grid=(b, B_QK_HEADS // GDN_QK_PER_STEP, nl),
        in_specs=[pl.BlockSpec((lblk, qw), lambda bi, h, li: (bi * nl + li, h)),
                  pl.BlockSpec((lblk, qw), lambda bi, h, li: (bi * nl + li, k_off + h)),
                  pl.BlockSpec((lblk, vw), lambda bi, h, li: (bi * nl + li, v_off + h)),
                  pl.BlockSpec((lblk, vw), lambda bi, h, li: (bi * nl + li, z_off + h)),
                  pl.BlockSpec((lblk, 2 * B_V_HEADS), lambda bi, h, li: (bi * nl + li, 0)),
                  pl.BlockSpec((B_CONV, qw), lambda bi, h, li: (0, h)),
                  pl.BlockSpec((B_CONV, qw), lambda bi, h, li: (0, k_off + h)),
                  pl.BlockSpec((B_CONV, vw), lambda bi, h, li: (0, v_off + h)),
                  pl.BlockSpec((1, B_V_HEADS), lambda bi, h, li: (0, 0)),
                  pl.BlockSpec((1, B_V_HEADS), lambda bi, h, li: (0, 0)),
                  pl.BlockSpec((1, B_DV), lambda bi, h, li: (0, 0))],
        out_specs=[pl.BlockSpec((lblk, vw), lambda bi, h, li: (bi * nl + li, h)),
                   pl.BlockSpec((1, GDN_V_PER_STEP, B_DK, B_DV), lambda bi, h, li: (bi, h, 0, 0))],
        out_shape=[jax.ShapeDtypeStruct((b * l, B_V_DIM), BF16),
                   jax.ShapeDtypeStruct((b, B_V_HEADS, B_DK, B_DV), F32)],
        scratch_shapes=[pltpu.VMEM((GDN_V_PER_STEP, B_DK, B_DV), F32),
                        pltpu.VMEM((CONV_PAD + lblk, qw), F32),
                        pltpu.VMEM((CONV_PAD + lblk, qw), F32),
                        pltpu.VMEM((CONV_PAD + lblk, vw), F32),
                        pltpu.VMEM((lblk, LANES), F32),
                        pltpu.VMEM((lblk, LANES), F32),
                        pltpu.VMEM((lblk // CHUNK, 8, CHUNK), F32)],
        compiler_params=_params("parallel", "parallel", "arbitrary"),
        name="gdn_chunks",
    )(proj, proj, proj, proj, ba, conv_w, conv_w, conv_w,
      a_log.reshape(1, -1), dt_bias.reshape(1, -1), g_norm.reshape(1, -1))


GDN_STEP_BATCH = 2
B_ROWS = B_CONV_DIM // 128


def _gdn_step_kernel(p_ref, cv_ref, cw_ref, ba_ref, alog_ref, dtb_ref, gn_ref, s_ref, o_ref, s_out_ref):
    ba = ba_ref[0]
    beta_all = jax.nn.sigmoid(ba[:, :B_V_HEADS])
    g_all = -jnp.exp(alog_ref[...]) * _softplus(ba[:, B_V_HEADS:] + dtb_ref[...])
    decay_all = jnp.exp(g_all)
    iota_h = lax.broadcasted_iota(jnp.int32, (B_V_HEADS, B_DV), 0)
    for bi in range(GDN_STEP_BATCH):
        y = cw_ref[B_CONV - 1] * p_ref[bi, 0:B_ROWS, :]
        for j in range(B_CONV - 1):
            y = y + cw_ref[j] * cv_ref[bi, j]
        y = _silu(y)
        q = _l2norm(y[0:B_QK_HEADS]) * (B_DK ** -0.5)
        k = _l2norm(y[B_QK_HEADS:2 * B_QK_HEADS])
        v = y[2 * B_QK_HEADS:]
        z = p_ref[bi, B_ROWS:B_ROWS + B_V_HEADS, :]
        q_t, k_t = q.T, k.T
        qk = jnp.sum(q * k, axis=-1, keepdims=True)
        o = jnp.zeros((B_V_HEADS, B_DV), F32)
        for hv in range(B_V_HEADS):
            hq = hv // B_REP
            beta = beta_all[bi:bi + 1, hv:hv + 1]
            decay = decay_all[bi:bi + 1, hv:hv + 1]
            s = s_ref[bi, hv]
            k_col = k_t[:, hq:hq + 1]
            w = beta * v[hv:hv + 1, :] - (beta * decay) * jnp.sum(k_col * s, axis=0, keepdims=True)
            oh = qk[hq:hq + 1, :] * w + decay * jnp.sum(q_t[:, hq:hq + 1] * s, axis=0, keepdims=True)
            s_out_ref[bi, hv] = decay * s + k_col * w
            o = jnp.where(iota_h == hv, oh, o)
        o_ref[bi] = (_rms(o) * gn_ref[...] * _silu(z)).astype(o_ref.dtype)


def _gdn_step(proj, ba, conv0, conv_w, a_log, dt_bias, g_norm, s0):
    b = proj.shape[0]
    bb = GDN_STEP_BATCH
    p3 = proj.reshape(b, -1, 128)
    o, s = pl.pallas_call(
        _gdn_step_kernel,
        grid=(b // bb,),
        in_specs=[pl.BlockSpec((bb, p3.shape[1], 128), lambda i: (i, 0, 0)),
                  pl.BlockSpec((bb, B_CONV - 1, B_ROWS, 128), lambda i: (i, 0, 0, 0)),
                  pl.BlockSpec((B_CONV, B_ROWS, 128), lambda i: (0, 0, 0)),
                  pl.BlockSpec((1, bb, 2 * B_V_HEADS), lambda i: (i, 0, 0)),
                  pl.BlockSpec((1, B_V_HEADS), lambda i: (0, 0)),
                  pl.BlockSpec((1, B_V_HEADS), lambda i: (0, 0)),
                  pl.BlockSpec((1, B_DV), lambda i: (0, 0)),
                  pl.BlockSpec((bb, B_V_HEADS, B_DK, B_DV), lambda i: (i, 0, 0, 0))],
        out_specs=[pl.BlockSpec((bb, B_V_HEADS, B_DV), lambda i: (i, 0, 0)),
                   pl.BlockSpec((bb, B_V_HEADS, B_DK, B_DV), lambda i: (i, 0, 0, 0))],
        out_shape=[jax.ShapeDtypeStruct((b, B_V_HEADS, B_DV), BF16), jax.ShapeDtypeStruct(s0.shape, F32)],
        compiler_params=_params("parallel"),
        name="gdn_step",
    )(p3, conv0.reshape(b, B_CONV - 1, B_ROWS, 128), conv_w.reshape(B_CONV, B_ROWS, 128),
      ba.reshape(b // bb, bb, 2 * B_V_HEADS), a_log.reshape(1, -1), dt_bias.reshape(1, -1), g_norm.reshape(1, -1), s0)
    return o.reshape(b, B_V_DIM), s


D_GROUP = 16
D_GROUPS = D_MODEL // D_GROUP
D_STATE = 64
S5_LANES = 128
S5_GROUPS = S5_LANES // D_GROUP
S5_HALF = S5_GROUPS * D_STATE
S5_BLOCKS = D_MODEL // S5_LANES
S5_SEGMENTS = 8


def _cmul(ar, ai, br, bi):
    return ar * br - ai * bi, ar * bi + ai * br


def _gelu_tanh(x):
    return 0.5 * x * (1.0 + jnp.tanh(math.sqrt(2.0 / math.pi) * (x + 0.044715 * (x * x * x))))


def _s5_tables(a_re, a_im, b_re, b_im, c_re, c_im, log_dt, seg_len):
    lam_re, lam_im = a_re.astype(F32), a_im.astype(F32)
    dt = jnp.exp(log_dt.astype(F32))[:, None]
    mag = jnp.exp(dt * lam_re)
    ab_re, ab_im = mag * jnp.cos(dt * lam_im), mag * jnp.sin(dt * lam_im)
    inv = 1.0 / (lam_re * lam_re + lam_im * lam_im)
    e_re, e_im = _cmul(ab_re - 1.0, ab_im, lam_re * inv, -lam_im * inv)
    bb_re, bb_im = _cmul(e_re[..., None], e_im[..., None], b_re.astype(F32), b_im.astype(F32))
    j = S5_BLOCKS
    eye = jnp.eye(S5_GROUPS, dtype=F32)

    def pack_rows(t):
        return t.reshape(j, 1, S5_HALF)

    def bdiag_in(t):
        t = t.reshape(j, S5_GROUPS, D_STATE, D_GROUP)
        return jnp.einsum('jgpc,gh->jgchp', t, eye).reshape(j, S5_LANES, S5_HALF)

    def bdiag_out(t):
        t = t.reshape(j, S5_GROUPS, D_GROUP, D_STATE)
        return jnp.einsum('jgcp,gh->jgphc', t, eye).reshape(j, S5_HALF, S5_LANES)

    ab = jnp.concatenate([pack_rows(ab_re), pack_rows(ab_im)], axis=-1)
    w_bu = jnp.concatenate([bdiag_in(bb_re), bdiag_in(bb_im)], axis=-1)
    w_c = jnp.concatenate([bdiag_out(c_re.astype(F32)), -bdiag_out(c_im.astype(F32))], axis=1)
    pr, pi = pack_rows(ab_re), pack_rows(ab_im)
    n = 1
    while n < seg_len:
        tr, ti = pr[:, n - 1:n], pi[:, n - 1:n]
        nr, ni = _cmul(pr, pi, tr, ti)
        pr, pi = jnp.concatenate([pr, nr], axis=1), jnp.concatenate([pi, ni], axis=1)
        n *= 2
    pw = jnp.concatenate([pr[:, :seg_len], pi[:, :seg_len]], axis=-1)
    return ab, w_bu, w_c, pw


def _s5_scan_kernel(x_ref, ab_ref, wbu_ref, wc_ref, pw_ref, dsk_ref, y_ref, s_out_ref, buf, *, seq, seg):
    nc = S5_HALF // 128
    x = x_ref[...]
    bu = _bdot(x, wbu_ref[0])
    for c in range(2 * nc):
        buf[c] = bu[:, c * 128:(c + 1) * 128]

    a_res = [jnp.broadcast_to(ab_ref[0, :, c * 128:(c + 1) * 128], (S5_SEGMENTS, 128)) for c in range(nc)]
    a_ims = [jnp.broadcast_to(ab_ref[0, :, S5_HALF + c * 128:S5_HALF + (c + 1) * 128], (S5_SEGMENTS, 128))
             for c in range(nc)]

    def step(t, carry):
        rows = pl.ds(t, S5_SEGMENTS, stride=seg)
        out = []
        for c in range(nc):
            s_re, s_im = carry[c]
            n_re = a_res[c] * s_re - a_ims[c] * s_im + buf[c, rows, :]
            n_im = a_res[c] * s_im + a_ims[c] * s_re + buf[nc + c, rows, :]
            buf[c, rows, :] = n_re
            buf[nc + c, rows, :] = n_im
            out.append((n_re, n_im))
        return tuple(out)

    zeros = jnp.zeros((S5_SEGMENTS, 128), F32)
    lax.fori_loop(0, seg, step, tuple((zeros, zeros) for _ in range(nc)), unroll=2)

    for c in range(nc):
        p_re = pw_ref[0, :, c * 128:(c + 1) * 128]
        p_im = pw_ref[0, :, S5_HALF + c * 128:S5_HALF + (c + 1) * 128]
        pe_re, pe_im = p_re[seg - 1:seg], p_im[seg - 1:seg]
        c_re, c_im = buf[c, seg - 1:seg, :], buf[nc + c, seg - 1:seg, :]
        for r in range(1, S5_SEGMENTS):
            lo = r * seg
            l_re, l_im = buf[c, lo:lo + seg, :], buf[nc + c, lo:lo + seg, :]
            f_re, f_im = _cmul(p_re, p_im, c_re, c_im)
            buf[c, lo:lo + seg, :] = l_re + f_re
            buf[nc + c, lo:lo + seg, :] = l_im + f_im
            e_re, e_im = _cmul(pe_re, pe_im, c_re, c_im)
            c_re, c_im = l_re[seg - 1:seg] + e_re, l_im[seg - 1:seg] + e_im

    y = dsk_ref[...] * x
    for c in range(2 * nc):
        s_out_ref[0, 0, :, c * 128:(c + 1) * 128] = buf[c, seq - 1:seq, :]
        y = y + _bdot(buf[c], wc_ref[0, c * 128:(c + 1) * 128, :])
    y_ref[...] = _gelu_tanh(y).astype(y_ref.dtype)


def _unpack_state(s):
    b = s.shape[0]
    return (s[..., :S5_HALF].reshape(b, D_GROUPS, D_STATE), s[..., S5_HALF:].reshape(b, D_GROUPS, D_STATE))


def _s5_prompt(x, b, l, tables, d_skip):
    d = x.shape[1]
    ab, w_bu, w_c, pw = tables
    seg = l // S5_SEGMENTS
    y, s = pl.pallas_call(
        functools.partial(_s5_scan_kernel, seq=l, seg=seg),
        grid=(b, S5_BLOCKS),
        in_specs=[pl.BlockSpec((l, S5_LANES), lambda bi, j: (bi, j)),
                  pl.BlockSpec((1, 1, 2 * S5_HALF), lambda bi, j: (j, 0, 0)),
                  pl.BlockSpec((1, S5_LANES, 2 * S5_HALF), lambda bi, j: (j, 0, 0)),
                  pl.BlockSpec((1, 2 * S5_HALF, S5_LANES), lambda bi, j: (j, 0, 0)),
                  pl.BlockSpec((1, seg, 2 * S5_HALF), lambda bi, j: (j, 0, 0)),
                  pl.BlockSpec((1, S5_LANES), lambda bi, j: (0, j))],
        out_specs=[pl.BlockSpec((l, S5_LANES), lambda bi, j: (bi, j)),
                   pl.BlockSpec((1, 1, 1, 2 * S5_HALF), lambda bi, j: (bi, j, 0, 0))],
        out_shape=[jax.ShapeDtypeStruct((b * l, d), BF16),
                   jax.ShapeDtypeStruct((b, S5_BLOCKS, 1, 2 * S5_HALF), F32)],
        scratch_shapes=[pltpu.VMEM((2 * S5_HALF // 128, l, 128), F32)],
        compiler_params=_params("parallel", "parallel"),
        name="s5_scan",
    )(x, ab, w_bu, w_c, pw, d_skip.reshape(1, d))
    return y, _unpack_state(s[:, :, 0, :])


def _s5_step_kernel(x_ref, s_ref, ab_ref, wbu_ref, wc_ref, dsk_ref, y_ref, s_out_ref):
    hf = S5_HALF
    x = x_ref[...]
    bu = _bdot(x, wbu_ref[0])
    s0 = s_ref[0]
    i_re, i_im = _cmul(ab_ref[0, :, :hf], ab_ref[0, :, hf:], s0[:, :hf], s0[:, hf:])
    s_new = jnp.concatenate([i_re, i_im], axis=-1) + bu
    s_out_ref[0] = s_new
    y = _bdot(s_new, wc_ref[0]) + dsk_ref[...] * x
    y_ref[...] = _gelu_tanh(y).astype(y_ref.dtype)


def _s5_step(x, tables, d_skip, s0_re, s0_im):
    b, d = x.shape
    ab, w_bu, w_c, _ = tables
    s0 = jnp.concatenate([s0_re.reshape(b, S5_BLOCKS, S5_HALF), s0_im.reshape(b, S5_BLOCKS, S5_HALF)], axis=-1)
    s0 = jnp.swapaxes(s0, 0, 1)
    y, s = pl.pallas_call(
        _s5_step_kernel,
        grid=(S5_BLOCKS,),
        in_specs=[pl.BlockSpec((b, S5_LANES), lambda j: (0, j)),
                  pl.BlockSpec((1, b, 2 * S5_HALF), lambda j: (j, 0, 0)),
                  pl.BlockSpec((1, 1, 2 * S5_HALF), lambda j: (j, 0, 0)),
                  pl.BlockSpec((1, S5_LANES, 2 * S5_HALF), lambda j: (j, 0, 0)),
                  pl.BlockSpec((1, 2 * S5_HALF, S5_LANES), lambda j: (j, 0, 0)),
                  pl.BlockSpec((1, S5_LANES), lambda j: (0, j))],
        out_specs=[pl.BlockSpec((b, S5_LANES), lambda j: (0, j)),
                   pl.BlockSpec((1, b, 2 * S5_HALF), lambda j: (j, 0, 0))],
        out_shape=[jax.ShapeDtypeStruct((b, d), BF16), jax.ShapeDtypeStruct((S5_BLOCKS, b, 2 * S5_HALF), F32)],
        compiler_params=_params("parallel"),
        name="s5_step",
    )(x, s0, ab, w_bu, w_c, d_skip.reshape(1, d))
    return y, _unpack_state(jnp.swapaxes(s, 0, 1))


def _glu_kernel(x_ref, w1_ref, w2_ref, o_ref):
    x = x_ref[...]
    y1 = jnp.dot(x, w1_ref[...].astype(BF16), preferred_element_type=F32)
    y2 = jnp.dot(x, w2_ref[...].astype(BF16), preferred_element_type=F32)
    o_ref[...] = y1 * jax.nn.sigmoid(y2)


def _glu_matmul(x, w, tm_target=1664, tn_target=256):
    m, k = x.shape
    n = w.shape[1] // 2
    tm = _pick_tile(m, tm_target, 16)
    tn = _pick_tile(n, tn_target, 128)
    nb = n // tn
    return pl.pallas_call(
        _glu_kernel,
        grid=(m // tm, nb),
        in_specs=[pl.BlockSpec((tm, k), lambda i, j: (i, 0)),
                  pl.BlockSpec((k, tn), lambda i, j: (0, j)),
                  pl.BlockSpec((k, tn), lambda i, j: (0, j + nb))],
        out_specs=pl.BlockSpec((tm, tn), lambda i, j: (i, j)),
        out_shape=jax.ShapeDtypeStruct((m, n), F32),
        compiler_params=_params("parallel", "parallel"),
        name="glu_matmul",
    )(x, w, w)


def kernel(x_prompt, x_sample, state_a_S, state_b_S, state_b_conv, state_c_C, state_c_n, state_c_m, state_d_re, state_d_im, a_w_in, a_lb_logits, a_g_norm, a_w_out, b_w_in, b_conv_w, b_a_log, b_dt_bias, b_g_norm, b_w_out, c_w_in, c_b_gate, c_g_norm, c_w_out, d_a_re, d_a_im, d_b_re, d_b_im, d_c_re, d_c_im, d_skip, d_log_dt, d_w_glu, moe_w_router, moe_b_router, moe_w_gu, moe_w_down, moe_w_gu_s, moe_w_down_s, ln_mix_g, ln_mix_b, ln_ffn_g, ln_ffn_b):
    bp, l, d = x_prompt.shape
    bs = x_sample.shape[0]
    tp = bp * l
    x = jnp.concatenate([x_prompt.reshape(tp, d), x_sample.reshape(bs, d)], axis=0)
    xb = x.astype(BF16)

    def finish_layer(x, h, layer):
        x, xb = _add_ln(x, h, ln_mix_g[layer], ln_mix_b[layer])
        return _moe_ln(x, xb, moe_w_router, moe_b_router, moe_w_gu, moe_w_down, moe_w_gu_s, moe_w_down_s,
                       ln_ffn_g, ln_ffn_b, layer)

    proj = _matmul(xb, a_w_in)
    lb = _hgrn2_lower_bound(a_lb_logits, 0)
    o_p, a_s_p = _hgrn2_prompt(proj, bp, l, lb, a_g_norm)
    o_s, a_s_s = _hgrn2_step(proj[tp:], state_a_S, lb, a_g_norm)
    h = _matmul(jnp.concatenate([o_p, o_s], axis=0), a_w_out)
    x, xb = finish_layer(x, h, 0)

    n_main = B_CONV_DIM + B_V_DIM
    proj = _matmul(xb, b_w_in, n_cols=n_main)
    ba = _matmul(xb, b_w_in[:, n_main:])
    o_p, b_s_p = _gdn_prompt(proj, ba, bp, l, b_conv_w, b_a_log, b_dt_bias, b_g_norm)
    o_s, b_s_s = _gdn_step(proj[tp:], ba[tp:], state_b_conv, b_conv_w, b_a_log, b_dt_bias, b_g_norm, state_b_S)
    qkv_s = proj[tp:, :B_CONV_DIM]
    b_conv_p = jnp.stack([proj[(i + 1) * l - (B_CONV - 1):(i + 1) * l, :B_CONV_DIM] for i in range(bp)], axis=0)
    b_conv_s = jnp.concatenate([state_b_conv[:, 1:, :], qkv_s[:, None, :]], axis=1)
    h = _matmul(jnp.concatenate([o_p, o_s], axis=0), b_w_out)
    x, xb = finish_layer(x, h, 1)

    n_main = 2 * C_QK_DIM + 2 * C_V_DIM
    proj = _matmul(xb, c_w_in, n_cols=n_main)
    gates = _matmul(xb, c_w_in[:, n_main:])
    o_p, c_c_p, c_n_p, c_m_p = _mlstm_prompt(proj, gates, bp, l, c_b_gate, c_g_norm)
    o_s, c_c_s, c_n_s, c_m_s = _mlstm_step(proj[tp:], gates[tp:], c_b_gate, c_g_norm, state_c_C, state_c_n, state_c_m)
    h = _matmul(jnp.concatenate([o_p, o_s], axis=0), c_w_out)
    x, xb = finish_layer(x, h, 2)

    tables = _s5_tables(d_a_re, d_a_im, d_b_re, d_b_im, d_c_re, d_c_im, d_log_dt, l // S5_SEGMENTS)
    y_p, (d_re_p, d_im_p) = _s5_prompt(x, bp, l, tables, d_skip)
    y_s, (d_re_s, d_im_s) = _s5_step(x[tp:], tables, d_skip, state_d_re, state_d_im)
    h = _glu_matmul(jnp.concatenate([y_p, y_s], axis=0), d_w_glu)
    x, xb = finish_layer(x, h, 3)

    y_prompt = x[:tp].reshape(bp, l, d)
    y_sample = x[tp:].reshape(bs, 1, d)
    return (y_prompt, y_sample, a_s_p, a_s_s, b_s_p, b_s_s, b_conv_p, b_conv_s,
            c_c_p, c_c_s, c_n_p, c_n_s, c_m_p, c_m_s, d_re_p, d_re_s, d_im_p, d_im_s)
```

```python
import functools
import math

import jax
import jax.numpy as jnp
from jax import lax
from jax.experimental import pallas as pl
from jax.experimental.pallas import tpu as pltpu

F32 = jnp.float32
BF16 = jnp.bfloat16
HI = lax.Precision.HIGHEST

D_MODEL = 2048
DEPTH = 4
CHUNK = 64
DN_ALPHA = (2 * DEPTH) ** 0.25
LN_EPS = 1e-5
RMS_EPS = 1e-6

VMEM_LIMIT_BYTES = 56 * 1024 * 1024


def _params(*sem):
    return pltpu.CompilerParams(dimension_semantics=sem, vmem_limit_bytes=VMEM_LIMIT_BYTES)


def _pick_tile(n, target, mult):
    best = None
    for t in range(mult, min(n, target) + 1, mult):
        if n % t == 0:
            best = t
    return best if best is not None else n


def _mm_kernel(x_ref, w_ref, o_ref):
    o_ref[...] = jnp.dot(x_ref[...], w_ref[...].astype(BF16), preferred_element_type=F32).astype(o_ref.dtype)


def _matmul(x, w, n_cols=None, out_dtype=F32, tm_target=1664, tn_target=512):
    m, k = x.shape
    n = w.shape[1] if n_cols is None else n_cols
    tm = _pick_tile(m, tm_target, 16)
    tn = _pick_tile(n, tn_target, 128)
    return pl.pallas_call(
        _mm_kernel,
        grid=(m // tm, n // tn),
        in_specs=[pl.BlockSpec((tm, k), lambda i, j: (i, 0)), pl.BlockSpec((k, tn), lambda i, j: (0, j))],
        out_specs=pl.BlockSpec((tm, tn), lambda i, j: (i, j)),
        out_shape=jax.ShapeDtypeStruct((m, n), out_dtype),
        compiler_params=_params("parallel", "parallel"),
        name="matmul",
    )(x, w)


def _layernorm_rows(z, g, b):
    mu = jnp.mean(z, axis=-1, keepdims=True)
    zc = z - mu
    var = jnp.mean(zc * zc, axis=-1, keepdims=True)
    return zc * lax.rsqrt(var + LN_EPS) * g + b


def _add_ln_kernel(x_ref, h_ref, g_ref, b_ref, o_ref, ob_ref):
    y = _layernorm_rows(DN_ALPHA * x_ref[...] + h_ref[...], g_ref[...], b_ref[...])
    o_ref[...] = y
    ob_ref[...] = y.astype(BF16)


def _add_ln(x, h, g, b, tm_target=640):
    m, d = x.shape
    tm = _pick_tile(m, tm_target, 16)
    row = pl.BlockSpec((tm, d), lambda i: (i, 0))
    vec = pl.BlockSpec((1, d), lambda i: (0, 0))
    return pl.pallas_call(
        _add_ln_kernel,
        grid=(m // tm,),
        in_specs=[row, row, vec, vec],
        out_specs=[row, row],
        out_shape=[jax.ShapeDtypeStruct((m, d), F32), jax.ShapeDtypeStruct((m, d), BF16)],
        compiler_params=_params("parallel"),
        name="add_ln",
    )(x, h, g.reshape(1, d), b.reshape(1, d))


def _silu(x):
    return x * jax.nn.sigmoid(x)


def _ffn_kernel(x_ref, wgu_ref, wd_ref, o_ref, *, dh):
    gu = jnp.dot(x_ref[...], wgu_ref[0].astype(BF16), preferred_element_type=F32)
    a = _silu(gu[:, :dh]) * gu[:, dh:]
    o_ref[...] = jnp.dot(a.astype(BF16), wd_ref[0].astype(BF16), preferred_element_type=F32)


def _dense_ffn(xb, w_gu, w_down, layer, tm_target=640):
    m, d = xb.shape
    dh = w_down.shape[1]
    tm = _pick_tile(m, tm_target, 16)
    return pl.pallas_call(
        functools.partial(_ffn_kernel, dh=dh),
        grid=(m // tm,),
        in_specs=[pl.BlockSpec((tm, d), lambda i: (i, 0)),
                  pl.BlockSpec((1, d, 2 * dh), lambda i: (layer, 0, 0)),
                  pl.BlockSpec((1, dh, d), lambda i: (layer, 0, 0))],
        out_specs=pl.BlockSpec((tm, d), lambda i: (i, 0)),
        out_shape=jax.ShapeDtypeStruct((m, d), F32),
        compiler_params=_params("parallel"),
        name="shared_ffn",
    )(xb, w_gu, w_down)


N_EXPERTS = 64
TOP_K = 8
N_ROUTE_GROUPS = 8
TOPK_ROUTE_GROUPS = 4
GROUP_SIZE = N_EXPERTS // N_ROUTE_GROUPS
ROUTED_SCALE = 2.5
EXPERT_TILE = 256


def _router_kernel(x_ref, wt_ref, b_ref, e_ref, g_ref):
    logits = lax.dot_general(wt_ref[...].astype(BF16), x_ref[...], (((1,), (1,)), ((), ())),
                             preferred_element_type=F32)
    scores = jax.nn.sigmoid(logits)
    biased = scores + b_ref[...]
    tm = scores.shape[1]
    neg = jnp.float32(-jnp.inf)
    iota_g = lax.broadcasted_iota(jnp.int32, (GROUP_SIZE, tm), 0)
    gscore = jnp.zeros((N_ROUTE_GROUPS, tm), F32)
    iota_r = lax.broadcasted_iota(jnp.int32, (N_ROUTE_GROUPS, tm), 0)
    for g in range(N_ROUTE_GROUPS):
        v = biased[g * GROUP_SIZE:(g + 1) * GROUP_SIZE, :]
        m1 = jnp.max(v, axis=0, keepdims=True)
        i1 = jnp.min(jnp.where(v == m1, iota_g, GROUP_SIZE), axis=0, keepdims=True)
        m2 = jnp.max(jnp.where(iota_g == i1, neg, v), axis=0, keepdims=True)
        gscore = jnp.where(iota_r == g, m1 + m2, gscore)
    ok = jnp.zeros((N_ROUTE_GROUPS, tm), jnp.int32)
    for _ in range(TOPK_ROUTE_GROUPS):
        m = jnp.max(gscore, axis=0, keepdims=True)
        gi = jnp.min(jnp.where(gscore == m, iota_r, N_ROUTE_GROUPS), axis=0, keepdims=True)
        hit = iota_r == gi
        ok = jnp.where(hit, 1, ok)
        gscore = jnp.where(hit, neg, gscore)
    masked = jnp.concatenate(
        [jnp.where(ok[g:g + 1, :] > 0, biased[g * GROUP_SIZE:(g + 1) * GROUP_SIZE, :], neg)
         for g in range(N_ROUTE_GROUPS)], axis=0)
    iota_e = lax.broadcasted_iota(jnp.int32, (N_EXPERTS, tm), 0)
    iota_k = lax.broadcasted_iota(jnp.int32, (TOP_K, tm), 0)
    top_e = jnp.zeros((TOP_K, tm), jnp.int32)
    gate = jnp.zeros((TOP_K, tm), F32)
    for k in range(TOP_K):
        m = jnp.max(masked, axis=0, keepdims=True)
        ei = jnp.min(jnp.where(masked == m, iota_e, N_EXPERTS), axis=0, keepdims=True)
        hit = iota_e == ei
        gk = jnp.sum(jnp.where(hit, scores, 0.0), axis=0, keepdims=True)
        top_e = jnp.where(iota_k == k, ei, top_e)
        gate = jnp.where(iota_k == k, gk, gate)
        masked = jnp.where(hit, neg, masked)
    e_ref[...] = top_e
    g_ref[...] = ROUTED_SCALE * gate / jnp.sum(gate, axis=0, keepdims=True)


def _router(x, w_router_t, b_router, tm_target=640):
    t, d = x.shape
    tm = _pick_tile(t, tm_target, 128)
    return pl.pallas_call(
        _router_kernel,
        grid=(t // tm,),
        in_specs=[pl.BlockSpec((tm, d), lambda i: (i, 0)),
                  pl.BlockSpec((N_EXPERTS, d), lambda i: (0, 0)),
                  pl.BlockSpec((N_EXPERTS, 1), lambda i: (0, 0))],
        out_specs=[pl.BlockSpec((TOP_K, tm), lambda i: (0, i)), pl.BlockSpec((TOP_K, tm), lambda i: (0, i))],
        out_shape=[jax.ShapeDtypeStruct((TOP_K, t), jnp.int32), jax.ShapeDtypeStruct((TOP_K, t), F32)],
        compiler_params=_params("parallel"),
        name="router",
    )(x, w_router_t, b_router.reshape(N_EXPERTS, 1))


LANES = 128


def _dispatch_kernel(e_ref, pos_ref, cnt_ref, rank_ref):
    n_blocks = e_ref.shape[1] // LANES
    iota_e = lax.broadcasted_iota(jnp.int32, (N_EXPERTS, LANES), 0)
    incl = (lax.broadcasted_iota(jnp.int32, (LANES, LANES), 0)
            <= lax.broadcasted_iota(jnp.int32, (LANES, LANES), 1)).astype(BF16)
    iota_k = lax.broadcasted_iota(jnp.int32, (TOP_K, LANES), 0)

    def pick(e_blk, table):
        out = jnp.zeros((TOP_K, LANES), F32)
        for k in range(TOP_K):
            v = jnp.sum(jnp.where(iota_e == e_blk[k:k + 1, :], table, 0.0), axis=0, keepdims=True)
            out = jnp.where(iota_k == k, v, out)
        return out

    def count_block(b, carry):
        cols = pl.ds(pl.multiple_of(b * LANES, LANES), LANES)
        e_blk = e_ref[:, cols]
        member = jnp.zeros((N_EXPERTS, LANES), F32)
        for k in range(TOP_K):
            member = member + (iota_e == e_blk[k:k + 1, :]).astype(F32)
        seen = jnp.dot(member.astype(BF16), incl, preferred_element_type=F32) + carry
        rank_ref[:, cols] = pick(e_blk, seen - member)
        return jnp.broadcast_to(seen[:, LANES - 1:LANES], (N_EXPERTS, LANES))

    counts = lax.fori_loop(0, n_blocks, count_block, jnp.zeros((N_EXPERTS, LANES), F32))
    cnt_ref[...] = counts.astype(jnp.int32)
    padded = jnp.ceil(counts * (1.0 / EXPERT_TILE)) * EXPERT_TILE
    start = jnp.dot(_tri(N_EXPERTS, strict=True).astype(F32), padded, preferred_element_type=F32, precision=HI)

    def place_block(b, carry):
        cols = pl.ds(pl.multiple_of(b * LANES, LANES), LANES)
        pos_ref[:, cols] = (pick(e_ref[:, cols], start) + rank_ref[:, cols]).astype(jnp.int32)
        return carry

    lax.fori_loop(0, n_blocks, place_block, 0)


def _dispatch_tables(top_e_t, n_tiles):
    k, t = top_e_t.shape
    pos_t, cnt = pl.pallas_call(
        _dispatch_kernel,
        out_shape=[jax.ShapeDtypeStruct((k, t), jnp.int32), jax.ShapeDtypeStruct((N_EXPERTS, LANES), jnp.int32)],
        scratch_shapes=[pltpu.VMEM((k, t), F32)],
        compiler_params=pltpu.CompilerParams(vmem_limit_bytes=VMEM_LIMIT_BYTES),
        name="dispatch",
    )(top_e_t)
    counts = cnt[:, 0]
    pends = jnp.cumsum((counts + EXPERT_TILE - 1) // EXPERT_TILE * EXPERT_TILE)
    n_used = pends[-1] // EXPERT_TILE
    tile_idx = jnp.arange(n_tiles, dtype=jnp.int32)
    tile_e = jnp.minimum(jnp.sum((pends[None, :] <= (tile_idx * EXPERT_TILE)[:, None]).astype(jnp.int32), axis=1),
                         N_EXPERTS - 1)
    tile_e = jnp.where(tile_idx < n_used, tile_e, tile_e[jnp.maximum(n_used - 1, 0)])
    last_tile = jnp.where(counts > 0, pends // EXPERT_TILE - 1, -1).astype(jnp.int32)
    return pos_t.reshape(-1), tile_e.astype(jnp.int32), n_used.astype(jnp.int32).reshape(1), last_tile


SCATTER_TILE = 640


def _scatter_rows_kernel(pos_ref, last_tile_ref, n_used_ref, x_ref, xs_ref, zbuf, zsem, sem, *, n_tok, max_idle):
    tt = x_ref.shape[0]
    n_tiles = xs_ref.shape[0] // EXPERT_TILE
    n_used = n_used_ref[0]

    @pl.when(pl.program_id(0) == 0)
    def _():
        zbuf[...] = jnp.zeros_like(zbuf)

        def fill(tile):
            return pltpu.make_async_copy(zbuf, xs_ref.at[pl.ds(tile * EXPERT_TILE, EXPERT_TILE)], zsem)

        def for_each_padded_tile(action):
            for e in range(N_EXPERTS):
                tile = last_tile_ref[e]

                @pl.when(tile >= 0)
                def _(tile=tile):
                    action(fill(tile))
            for j in range(max_idle):
                @pl.when(n_used + j < n_tiles)
                def _(j=j):
                    action(fill(n_used + j))

        for_each_padded_tile(lambda copy: copy.start())
        for_each_padded_tile(lambda copy: copy.wait())

    base = pl.program_id(0) * tt

    def body(r, c):
        for k in range(TOP_K):
            p = pos_ref[k * n_tok + base + r]
            pltpu.make_async_copy(x_ref.at[pl.ds(r, 1)], xs_ref.at[pl.ds(p, 1)], sem).start()
        return c

    lax.fori_loop(0, tt, body, 0)
    for k in range(TOP_K):
        pltpu.make_async_copy(x_ref, xs_ref.at[pl.ds(0, tt)], sem).wait()


def _sort_rows(x, pos, last_tile, n_used, n_tiles):
    t, d = x.shape
    tt = _pick_tile(t, SCATTER_TILE, 8)
    max_idle = n_tiles - (t * TOP_K) // EXPERT_TILE
    return pl.pallas_call(
        functools.partial(_scatter_rows_kernel, n_tok=t, max_idle=max_idle),
        grid_spec=pltpu.PrefetchScalarGridSpec(
            num_scalar_prefetch=3, grid=(t // tt,),
            in_specs=[pl.BlockSpec((tt, d), lambda i, p, lt, nu: (i, 0))],
            out_specs=pl.BlockSpec(memory_space=pl.ANY),
            scratch_shapes=[pltpu.VMEM((EXPERT_TILE, d), F32), pltpu.SemaphoreType.DMA(()),
                            pltpu.SemaphoreType.DMA(())]),
        out_shape=jax.ShapeDtypeStruct((n_tiles * EXPERT_TILE, d), F32),
        compiler_params=_params("arbitrary"),
        name="scatter_rows",
    )(pos, last_tile, n_used, x)


def _expert_kernel(tile_e_ref, n_used_ref, xs_ref, wgu_ref, wd_ref, o_ref, wgu_b, wd_b, *, dh):
    i = pl.program_id(0)
    n_used = n_used_ref[0]
    e = tile_e_ref[i]
    e_prev = tile_e_ref[jnp.maximum(i - 1, 0)]

    @pl.when(jnp.logical_and(i < n_used, jnp.logical_or(i == 0, e != e_prev)))
    def _():
        wgu_b[...] = wgu_ref[0, 0].astype(BF16)
        wd_b[...] = wd_ref[0, 0].astype(BF16)

    @pl.when(i < n_used)
    def _():
        gu = jnp.dot(xs_ref[...].astype(BF16), wgu_b[...], preferred_element_type=F32)
        a = _silu(gu[:, :dh]) * gu[:, dh:]
        o_ref[...] = jnp.dot(a.astype(BF16), wd_b[...], preferred_element_type=F32)

    @pl.when(i >= n_used)
    def _():
        o_ref[...] = jnp.zeros_like(o_ref)


def _routed_experts(xs, tile_e, n_used, w_gu, w_down, layer):
    d = xs.shape[1]
    dh = w_down.shape[2]
    n_tiles = tile_e.shape[0]
    grid_spec = pltpu.PrefetchScalarGridSpec(
        num_scalar_prefetch=2,
        grid=(n_tiles,),
        in_specs=[pl.BlockSpec((EXPERT_TILE, d), lambda i, te, nu: (jnp.minimum(i, nu[0] - 1), 0)),
                  pl.BlockSpec((1, 1, d, 2 * dh), lambda i, te, nu: (layer, te[i], 0, 0)),
                  pl.BlockSpec((1, 1, dh, d), lambda i, te, nu: (layer, te[i], 0, 0))],
        out_specs=pl.BlockSpec((EXPERT_TILE, d), lambda i, te, nu: (i, 0)),
        scratch_shapes=[pltpu.VMEM((d, 2 * dh), BF16), pltpu.VMEM((dh, d), BF16)],
    )
    return pl.pallas_call(
        functools.partial(_expert_kernel, dh=dh),
        grid_spec=grid_spec,
        out_shape=jax.ShapeDtypeStruct((n_tiles * EXPERT_TILE, d), F32),
        compiler_params=_params("arbitrary"),
        name="routed_experts",
    )(tile_e, n_used, xs, w_gu, w_down)


COMBINE_TILE = 128


def _combine_kernel(pos_ref, rows_hbm, gate_ref, x_ref, sh_ref, g_ref, b_ref, o_ref, ob_ref, buf, sem):
    i = pl.program_id(0)
    n = pl.num_programs(0)
    slot = i % 2

    n_tok = n * COMBINE_TILE

    def issue(tile, s):
        base = tile * COMBINE_TILE

        def body(r, c):
            for k in range(TOP_K):
                p = pos_ref[k * n_tok + base + r]
                pltpu.make_async_copy(rows_hbm.at[pl.ds(p, 1)], buf.at[s, k, pl.ds(r, 1)], sem.at[s]).start()
            return c

        lax.fori_loop(0, COMBINE_TILE, body, 0)

    @pl.when(i == 0)
    def _():
        issue(0, 0)

    @pl.when(i + 1 < n)
    def _():
        issue(i + 1, 1 - slot)

    for k in range(TOP_K):
        pltpu.make_async_copy(rows_hbm.at[pl.ds(0, COMBINE_TILE)], buf.at[slot, k], sem.at[slot]).wait()
    gate = gate_ref[...]
    z = DN_ALPHA * x_ref[...] + sh_ref[...]
    for k in range(TOP_K):
        z = z + gate[:, k:k + 1] * buf[slot, k]
    y = _layernorm_rows(z, g_ref[...], b_ref[...])
    o_ref[...] = y
    ob_ref[...] = y.astype(BF16)


def _combine_ln(rows, pos, gate, x, shared, g, b):
    t, d = x.shape
    row = lambda w: pl.BlockSpec((COMBINE_TILE, w), lambda i, p: (i, 0))
    vec = pl.BlockSpec((1, d), lambda i, p: (0, 0))
    grid_spec = pltpu.PrefetchScalarGridSpec(
        num_scalar_prefetch=1,
        grid=(t // COMBINE_TILE,),
        in_specs=[pl.BlockSpec(memory_space=pl.ANY), row(TOP_K), row(d), row(d), vec, vec],
        out_specs=[row(d), row(d)],
        scratch_shapes=[pltpu.VMEM((2, TOP_K, COMBINE_TILE, d), F32), pltpu.SemaphoreType.DMA((2,))],
    )
    return pl.pallas_call(
        _combine_kernel,
        grid_spec=grid_spec,
        out_shape=[jax.ShapeDtypeStruct((t, d), F32), jax.ShapeDtypeStruct((t, d), BF16)],
        compiler_params=_params("arbitrary"),
        name="combine_ln",
    )(pos, rows, gate, x, shared, g.reshape(1, d), b.reshape(1, d))


def _moe_ln(x, xb, w_router, b_router, w_gu, w_down, w_gu_s, w_down_s, ln_g, ln_b, layer):
    t = x.shape[0]
    top_e_t, gate_t = _router(xb, w_router[layer].T, b_router[layer])
    n_tiles = (t * TOP_K + N_EXPERTS * (EXPERT_TILE - 1) + EXPERT_TILE - 1) // EXPERT_TILE
    pos, tile_e, n_used, last_tile = _dispatch_tables(top_e_t, n_tiles)
    xs = _sort_rows(x, pos, last_tile, n_used, n_tiles)
    rows = _routed_experts(xs, tile_e, n_used, w_gu, w_down, layer)
    shared = _dense_ffn(xb, w_gu_s, w_down_s, layer)
    return _combine_ln(rows, pos, gate_t.T, x, shared, ln_g[layer], ln_b[layer])


SUB = 16
SEQ_BLOCK = 512


def _tri(n, strict=False):
    r = lax.broadcasted_iota(jnp.int32, (n, n), 0)
    c = lax.broadcasted_iota(jnp.int32, (n, n), 1)
    return (r > c) if strict else (r >= c)


def _cumsum_rows(x):
    n = x.shape[0]
    return jnp.dot(_tri(n).astype(F32), x, preferred_element_type=F32, precision=HI)


def _bdot(a, b):
    return jnp.dot(a.astype(BF16), b.astype(BF16), preferred_element_type=F32)


def _bdot_nt(a, b):
    return lax.dot_general(a.astype(BF16), b.astype(BF16), (((1,), (1,)), ((), ())), preferred_element_type=F32)


def _bdot_tn(a, b):
    return lax.dot_general(a.astype(BF16), b.astype(BF16), (((0,), (0,)), ((), ())), preferred_element_type=F32)


def _rms(x):
    return x * lax.rsqrt(jnp.mean(x * x, axis=-1, keepdims=True) + RMS_EPS)


def _row_of(col_b):
    c = col_b.shape[0]
    return col_b.T[0:1, :c]


A_HEADS = 16
A_DK = 128
A_DV = 128
HGRN2_HEADS_PER_STEP = 4


def _hgrn2_chunk_kernel(q_ref, f_ref, v_ref, g_ref, lb_ref, gn_ref, o_ref, s_out_ref, s_ref, *, n_chunks):
    l = pl.program_id(2)

    @pl.when(l == 0)
    def _():
        s_ref[...] = jnp.zeros_like(s_ref)

    lb = lb_ref[...]
    gn = gn_ref[...]
    heads = range(HGRN2_HEADS_PER_STEP)

    def head(x, i):
        return x[:, i * A_DK:(i + 1) * A_DK]

    def chunk(j, carry):
        rows = pl.ds(pl.multiple_of(j * CHUNK, CHUNK), CHUNK)
        q = _silu(q_ref[rows, :])
        f = lb + (1.0 - lb) * jax.nn.sigmoid(f_ref[rows, :])
        k = 1.0 - f
        v = v_ref[rows, :]
        bc = _cumsum_rows(jnp.log(f))
        qe = q * jnp.exp(bc)
        s_old = [s_ref[i] for i in heads]
        o_inter = [_bdot(head(qe, i), s_old[i]) for i in heads]
        outs = [[] for _ in heads]
        for sc in range(CHUNK // SUB):
            lo, hi = sc * SUB, (sc + 1) * SUB
            bref = bc[lo - 1:lo, :] if sc > 0 else jnp.zeros((1, bc.shape[1]), F32)
            qi = q[lo:hi] * jnp.exp(bc[lo:hi] - bref)
            ki = k[:hi] * jnp.exp(bref - bc[:hi])
            causal = (lax.broadcasted_iota(jnp.int32, (SUB, hi), 0) + lo
                      >= lax.broadcasted_iota(jnp.int32, (SUB, hi), 1))
            atts = [jnp.where(causal, _bdot_nt(head(qi, i), head(ki, i)), 0.0) for i in heads]
            for i in heads:
                outs[i].append(_bdot(atts[i], head(v, i)[:hi]))
        b_end = bc[CHUNK - 1:CHUNK, :]
        kd = k * jnp.exp(b_end - bc)
        d_cols = jnp.exp(bc.T[:, CHUNK - 1:CHUNK])
        for i in heads:
            s_ref[i] = d_cols[i * A_DK:(i + 1) * A_DK] * s_old[i] + _bdot_tn(head(kd, i), head(v, i))
        o_n = jnp.concatenate([_rms(jnp.concatenate(outs[i], axis=0) + o_inter[i]) * gn for i in heads], axis=1)
        o_ref[rows, :] = (o_n * _silu(g_ref[rows, :])).astype(o_ref.dtype)
        return carry

    lax.fori_loop(0, n_chunks, chunk, 0)

    @pl.when(l == pl.num_programs(2) - 1)
    def _():
        s_out_ref[0] = s_ref[...]


def _hgrn2_prompt(proj, b, l, lb, g_norm):
    lblk = _pick_tile(l, SEQ_BLOCK, CHUNK)
    nl = l // lblk
    hg = HGRN2_HEADS_PER_STEP
    groups = A_HEADS // hg
    part = lambda p: pl.BlockSpec((lblk, hg * A_DK), lambda bi, h, li, p=p: (bi * nl + li, p * groups + h))
    return pl.pallas_call(
        functools.partial(_hgrn2_chunk_kernel, n_chunks=lblk // CHUNK),
        grid=(b, groups, nl),
        in_specs=[part(0), part(1), part(2), part(3),
                  pl.BlockSpec((1, hg * A_DK), lambda bi, h, li: (0, h)),
                  pl.BlockSpec((1, A_DV), lambda bi, h, li: (0, 0))],
        out_specs=[pl.BlockSpec((lblk, hg * A_DV), lambda bi, h, li: (bi * nl + li, h)),
                   pl.BlockSpec((1, hg, A_DK, A_DV), lambda bi, h, li: (bi, h, 0, 0))],
        out_shape=[jax.ShapeDtypeStruct((b * l, A_HEADS * A_DV), BF16),
                   jax.ShapeDtypeStruct((b, A_HEADS, A_DK, A_DV), F32)],
        scratch_shapes=[pltpu.VMEM((hg, A_DK, A_DV), F32)],
        compiler_params=_params("parallel", "parallel", "arbitrary"),
        name="hgrn2_chunks",
    )(proj, proj, proj, proj, lb.reshape(1, -1), g_norm.reshape(1, -1))


HGRN2_STEP_BATCH = 4


def _hgrn2_step_kernel(p_ref, s_ref, lb_ref, gn_ref, o_ref, s_out_ref):
    lb = lb_ref[...]
    iota_h = lax.broadcasted_iota(jnp.int32, (A_HEADS, A_DV), 0)
    for bi in range(HGRN2_STEP_BATCH):
        p = p_ref[bi]
        q = _silu(p[0:A_HEADS])
        f = lb + (1.0 - lb) * jax.nn.sigmoid(p[A_HEADS:2 * A_HEADS])
        v = p[2 * A_HEADS:3 * A_HEADS]
        g = p[3 * A_HEADS:4 * A_HEADS]
        q_t, f_t = q.T, f.T
        o = jnp.zeros((A_HEADS, A_DV), F32)
        for h in range(A_HEADS):
            fc = f_t[:, h:h + 1]
            s_new = fc * s_ref[bi, h] + (1.0 - fc) * v[h:h + 1, :]
            s_out_ref[bi, h] = s_new
            oh = jnp.sum(q_t[:, h:h + 1] * s_new, axis=0, keepdims=True)
            o = jnp.where(iota_h == h, oh, o)
        o_ref[bi] = (_rms(o) * gn_ref[...] * _silu(g)).astype(o_ref.dtype)


def _hgrn2_step(proj, s0, lb, g_norm):
    b = proj.shape[0]
    bb = HGRN2_STEP_BATCH
    p3 = proj.reshape(b, 4 * A_HEADS, A_DK)
    o, s = pl.pallas_call(
        _hgrn2_step_kernel,
        grid=(b // bb,),
        in_specs=[pl.BlockSpec((bb, 4 * A_HEADS, A_DK), lambda i: (i, 0, 0)),
                  pl.BlockSpec((bb, A_HEADS, A_DK, A_DV), lambda i: (i, 0, 0, 0)),
                  pl.BlockSpec((A_HEADS, A_DK), lambda i: (0, 0)),
                  pl.BlockSpec((1, A_DV), lambda i: (0, 0))],
        out_specs=[pl.BlockSpec((bb, A_HEADS, A_DV), lambda i: (i, 0, 0)),
                   pl.BlockSpec((bb, A_HEADS, A_DK, A_DV), lambda i: (i, 0, 0, 0))],
        out_shape=[jax.ShapeDtypeStruct((b, A_HEADS, A_DV), BF16),
                   jax.ShapeDtypeStruct(s0.shape, F32)],
        compiler_params=_params("parallel"),
        name="hgrn2_step",
    )(p3, s0, lb.reshape(A_HEADS, A_DK), g_norm.reshape(1, -1))
    return o.reshape(b, A_HEADS * A_DV), s


def _hgrn2_lower_bound(lb_logits, layer):
    return jnp.cumsum(jax.nn.softmax(lb_logits.astype(F32), axis=0), axis=0)[layer]


C_HEADS = 8
C_DK = 128
C_DV = 256
C_QK_DIM = C_HEADS * C_DK
C_V_DIM = C_HEADS * C_DV
C_GATE_CAP = 15.0
MLSTM_HEADS_PER_STEP = 4


def _log_sigmoid(x):
    return jnp.minimum(x, 0.0) - jnp.log(1.0 + jnp.exp(-jnp.abs(x)))


def _cap(x):
    return C_GATE_CAP * jnp.tanh(x / C_GATE_CAP)


def _mlstm_chunk_kernel(q_ref, k_ref, v_ref, op_ref, gt_ref, bg_ref, gn_ref,
                        o_ref, c_out_ref, n_out_ref, m_out_ref, c_ref, n_ref, m_ref, *, n_chunks):
    h = pl.program_id(1)
    l = pl.program_id(2)

    @pl.when(l == 0)
    def _():
        c_ref[...] = jnp.zeros_like(c_ref)
        n_ref[...] = jnp.zeros_like(n_ref)
        m_ref[...] = jnp.zeros_like(m_ref)

    hg = MLSTM_HEADS_PER_STEP
    heads = range(hg)
    lane = lax.broadcasted_iota(jnp.int32, (CHUNK, 2 * C_HEADS), 1)
    lane_o = lax.broadcasted_iota(jnp.int32, (CHUNK, LANES), 1)
    tri = _tri(CHUNK)

    def chunk(j, carry):
        rows = pl.ds(pl.multiple_of(j * CHUNK, CHUNK), CHUNK)
        gates = _cap(gt_ref[rows, :] + bg_ref[...])
        lsig = _log_sigmoid(gates)
        i_all = jnp.zeros((CHUNK, LANES), F32)
        lf_all = jnp.zeros((CHUNK, LANES), F32)
        for e in heads:
            he = h * hg + e
            i_all = jnp.where(lane_o == e, jnp.sum(jnp.where(lane == he, gates, 0.0), axis=-1, keepdims=True), i_all)
            lf_all = jnp.where(lane_o == e,
                               jnp.sum(jnp.where(lane == he + C_HEADS, lsig, 0.0), axis=-1, keepdims=True), lf_all)
        f_b = _cumsum_rows(lf_all)
        a_b = i_all - f_b
        a_t = a_b.T
        f_cs = [f_b[:, e:e + 1] for e in heads]
        log_ws = [jnp.where(tri, f_cs[e] + a_t[e:e + 1, :CHUNK], -jnp.inf) for e in heads]
        m_prevs = [m_ref[e, 0:1, 0:1] for e in heads]
        log_ss = [f_cs[e] + m_prevs[e] for e in heads]
        m_ts = [jnp.maximum(jnp.max(log_ws[e], axis=-1, keepdims=True), log_ss[e]) for e in heads]
        qss = [q_ref[rows, e * C_DK:(e + 1) * C_DK] * (C_DK ** -0.5) for e in heads]
        ks = [k_ref[rows, e * C_DK:(e + 1) * C_DK] for e in heads]
        vs = [v_ref[rows, e * C_DV:(e + 1) * C_DV] for e in heads]
        qks = [_bdot_nt(qss[e], ks[e]) * jnp.exp(log_ws[e] - m_ts[e]) for e in heads]
        w_ss = [jnp.exp(log_ss[e] - m_ts[e]) for e in heads]
        c_old = [c_ref[e] for e in heads]
        n_old = [n_ref[e] for e in heads]
        inter = [_bdot(qss[e], c_old[e]) for e in heads]
        intra = [_bdot(qks[e], vs[e]) for e in heads]
        hids = []
        for e in heads:
            den = (jnp.sum(qks[e], axis=-1, keepdims=True)
                   + w_ss[e] * jnp.sum(qss[e] * n_old[e], axis=-1, keepdims=True))
            num = intra[e] + w_ss[e] * inter[e]
            hids.append(_rms(num / jnp.maximum(jnp.abs(den), jnp.exp(-m_ts[e]))) * gn_ref[e])
        for e in heads:
            m_end = m_ts[e][CHUNK - 1:CHUNK, :]
            f_end = f_cs[e][CHUNK - 1:CHUNK, :]
            w_end = jnp.exp(f_end + a_b[:, e:e + 1] - m_end)
            s_end = jnp.exp(f_end + m_prevs[e] - m_end)
            kw = ks[e] * w_end
            c_ref[e] = s_end * c_old[e] + _bdot_tn(kw, vs[e])
            n_ref[e] = s_end * n_old[e] + jnp.sum(kw, axis=0, keepdims=True)
            m_ref[e] = jnp.broadcast_to(m_end, (1, LANES))
        o_ref[rows, :] = (jnp.concatenate(hids, axis=1) * jax.nn.sigmoid(op_ref[rows, :])).astype(o_ref.dtype)
        return carry

    lax.fori_loop(0, n_chunks, chunk, 0)

    @pl.when(l == pl.num_programs(2) - 1)
    def _():
        c_out_ref[0] = c_ref[...]
        n_out_ref[0] = n_ref[...]
        m_out_ref[0] = m_ref[...]


def _mlstm_prompt(proj, gates, b, l, b_gate, g_norm):
    lblk = _pick_tile(l, SEQ_BLOCK, CHUNK)
    nl = l // lblk
    hg = MLSTM_HEADS_PER_STEP
    groups = C_HEADS // hg
    qw, vw = hg * C_DK, hg * C_DV
    v_off = 2 * C_QK_DIM // vw
    o, c, n, m = pl.pallas_call(
        functools.partial(_mlstm_chunk_kernel, n_chunks=lblk // CHUNK),
        grid=(b, groups, nl),
        in_specs=[pl.BlockSpec((lblk, qw), lambda bi, h, li: (bi * nl + li, h)),
                  pl.BlockSpec((lblk, qw), lambda bi, h, li: (bi * nl + li, groups + h)),
                  pl.BlockSpec((lblk, vw), lambda bi, h, li: (bi * nl + li, v_off + h)),
                  pl.BlockSpec((lblk, vw), lambda bi, h, li: (bi * nl + li, v_off + groups + h)),
                  pl.BlockSpec((lblk, 2 * C_HEADS), lambda bi, h, li: (bi * nl + li, 0)),
                  pl.BlockSpec((1, 2 * C_HEADS), lambda bi, h, li: (0, 0)),
                  pl.BlockSpec((hg, 1, C_DV), lambda bi, h, li: (h, 0, 0))],
        out_specs=[pl.BlockSpec((lblk, vw), lambda bi, h, li: (bi * nl + li, h)),
                   pl.BlockSpec((1, hg, C_DK, C_DV), lambda bi, h, li: (bi, h, 0, 0)),
                   pl.BlockSpec((1, hg, 1, C_DK), lambda bi, h, li: (bi, h, 0, 0)),
                   pl.BlockSpec((1, hg, 1, LANES), lambda bi, h, li: (bi, h, 0, 0))],
        out_shape=[jax.ShapeDtypeStruct((b * l, C_V_DIM), BF16),
                   jax.ShapeDtypeStruct((b, C_HEADS, C_DK, C_DV), F32),
                   jax.ShapeDtypeStruct((b, C_HEADS, 1, C_DK), F32),
                   jax.ShapeDtypeStruct((b, C_HEADS, 1, LANES), F32)],
        scratch_shapes=[pltpu.VMEM((hg, C_DK, C_DV), F32), pltpu.VMEM((hg, 1, C_DK), F32),
                        pltpu.VMEM((hg, 1, LANES), F32)],
        compiler_params=_params("parallel", "parallel", "arbitrary"),
        name="mlstm_chunks",
    )(proj, proj, proj, proj, gates, b_gate.reshape(1, -1), g_norm.reshape(C_HEADS, 1, C_DV))
    return o, c, n[:, :, 0, :], m[:, :, 0, 0]


MLSTM_STEP_BATCH = 8


def _mlstm_step_kernel(qk_ref, vo_ref, gt_ref, bg_ref, gn_ref, c_ref, n_ref, m_ref,
                       o_ref, c_out_ref, n_out_ref, m_out_ref):
    iota_h = lax.broadcasted_iota(jnp.int32, (C_HEADS, C_DV), 0)
    iota_hk = lax.broadcasted_iota(jnp.int32, (C_HEADS, C_DK), 0)
    iota_m = lax.broadcasted_iota(jnp.int32, (MLSTM_STEP_BATCH, C_HEADS), 1)
    iota_mb = lax.broadcasted_iota(jnp.int32, (MLSTM_STEP_BATCH, C_HEADS), 0)
    gates = _cap(gt_ref[...] + bg_ref[...])
    m_all = m_ref[...]
    m_new_all = jnp.zeros((MLSTM_STEP_BATCH, C_HEADS), F32)
    for bi in range(MLSTM_STEP_BATCH):
        qs = qk_ref[bi, 0:C_HEADS, :] * (C_DK ** -0.5)
        k = qk_ref[bi, C_HEADS:2 * C_HEADS, :]
        v = vo_ref[bi, C_HEADS:2 * C_HEADS, :]
        op = vo_ref[bi, 2 * C_HEADS:3 * C_HEADS, :]
        q_t, k_t = qs.T, k.T
        n_all = n_ref[bi]
        hid = jnp.zeros((C_HEADS, C_DV), F32)
        n_new = jnp.zeros((C_HEADS, C_DK), F32)
        for h in range(C_HEADS):
            i_g = gates[bi:bi + 1, h:h + 1]
            lf = _log_sigmoid(gates[bi:bi + 1, C_HEADS + h:C_HEADS + h + 1])
            log_s = lf + m_all[bi:bi + 1, h:h + 1]
            m_t = jnp.maximum(i_g, log_s)
            w_i = jnp.exp(i_g - m_t)
            w_s = jnp.exp(log_s - m_t)
            qk = jnp.sum(qs[h:h + 1, :] * k[h:h + 1, :], axis=-1, keepdims=True) * w_i
            c_old = c_ref[bi, h]
            num = qk * v[h:h + 1, :] + w_s * jnp.sum(q_t[:, h:h + 1] * c_old, axis=0, keepdims=True)
            den = qk + w_s * jnp.sum(qs[h:h + 1, :] * n_all[h:h + 1, :], axis=-1, keepdims=True)
            hid = jnp.where(iota_h == h, num / jnp.maximum(jnp.abs(den), jnp.exp(-m_t)), hid)
            c_out_ref[bi, h] = w_s * c_old + (w_i * k_t[:, h:h + 1]) * v[h:h + 1, :]
            n_new = jnp.where(iota_hk == h, w_s * n_all[h:h + 1, :] + w_i * k[h:h + 1, :], n_new)
            m_new_all = jnp.where(jnp.logical_and(iota_m == h, iota_mb == bi), m_t, m_new_all)
        n_out_ref[bi] = n_new
        o_ref[bi] = (_rms(hid) * gn_ref[...] * jax.nn.sigmoid(op)).astype(o_ref.dtype)
    m_out_ref[...] = m_new_all


def _mlstm_step(proj, gates, b_gate, g_norm, c0, n0, m0):
    b = proj.shape[0]
    bb = MLSTM_STEP_BATCH
    qk_view = proj.reshape(b, -1, C_DK)
    vo_view = proj.reshape(b, -1, C_DV)
    o, c, n, m = pl.pallas_call(
        _mlstm_step_kernel,
        grid=(b // bb,),
        in_specs=[pl.BlockSpec((bb, 2 * C_HEADS, C_DK), lambda i: (i, 0, 0)),
                  pl.BlockSpec((bb, 3 * C_HEADS, C_DV), lambda i: (i, 0, 0)),
                  pl.BlockSpec((bb, 2 * C_HEADS), lambda i: (i, 0)),
                  pl.BlockSpec((1, 2 * C_HEADS), lambda i: (0, 0)),
                  pl.BlockSpec((C_HEADS, C_DV), lambda i: (0, 0)),
                  pl.BlockSpec((bb, C_HEADS, C_DK, C_DV), lambda i: (i, 0, 0, 0)),
                  pl.BlockSpec((bb, C_HEADS, C_DK), lambda i: (i, 0, 0)),
                  pl.BlockSpec((bb, C_HEADS), lambda i: (i, 0))],
        out_specs=[pl.BlockSpec((bb, C_HEADS, C_DV), lambda i: (i, 0, 0)),
                   pl.BlockSpec((bb, C_HEADS, C_DK, C_DV), lambda i: (i, 0, 0, 0)),
                   pl.BlockSpec((bb, C_HEADS, C_DK), lambda i: (i, 0, 0)),
                   pl.BlockSpec((bb, C_HEADS), lambda i: (i, 0))],
        out_shape=[jax.ShapeDtypeStruct((b, C_HEADS, C_DV), BF16),
                   jax.ShapeDtypeStruct(c0.shape, F32),
                   jax.ShapeDtypeStruct(n0.shape, F32),
                   jax.ShapeDtypeStruct(m0.shape, F32)],
        compiler_params=_params("parallel"),
        name="mlstm_step",
    )(qk_view, vo_view, gates, b_gate.reshape(1, -1), g_norm, c0, n0, m0)
    return o.reshape(b, C_V_DIM), c, n, m


B_QK_HEADS = 16
B_V_HEADS = 32
B_DK = 128
B_DV = 128
B_CONV = 4
B_QK_DIM = B_QK_HEADS * B_DK
B_V_DIM = B_V_HEADS * B_DV
B_CONV_DIM = 2 * B_QK_DIM + B_V_DIM
B_REP = B_V_HEADS // B_QK_HEADS
CONV_PAD = 8


def _softplus(x):
    return jnp.maximum(x, 0.0) + jnp.log(1.0 + jnp.exp(-jnp.abs(x)))


def _l2norm(x):
    return x * lax.rsqrt(jnp.sum(x * x, axis=-1, keepdims=True) + RMS_EPS)


def _split2(x):
    hi = x.astype(BF16)
    return hi, (x - hi.astype(F32)).astype(BF16)


def _dot3_parts(ah, al, bh, bl):
    n = bh.shape[1]
    r = jnp.dot(ah, jnp.concatenate([bh, bl], axis=1), preferred_element_type=F32)
    return r[:, :n] + r[:, n:] + jnp.dot(al, bh, preferred_element_type=F32)


def _unit_lower_inverses(mats, order):
    n = mats[0].shape[0]
    eye = (lax.broadcasted_iota(jnp.int32, (n, n), 0) == lax.broadcasted_iota(jnp.int32, (n, n), 1)).astype(F32)
    ps = [-a for a in mats]
    ts = [eye + p for p in ps]
    parts = [_split2(p) for p in ps]
    span = 1
    while 2 * span < order:
        ps = [_dot3_parts(ph, pl_, ph, pl_) for ph, pl_ in parts]
        parts = [_split2(p) for p in ps]
        t_parts = [_split2(t) for t in ts]
        ts = [t + _dot3_parts(th, tl, ph, pl_) for t, (th, tl), (ph, pl_) in zip(ts, t_parts, parts)]
        span *= 2
    return ts


GDN_QK_PER_STEP = 4
GDN_V_PER_STEP = GDN_QK_PER_STEP * B_REP


def _gdn_chunk_kernel(q_ref, k_ref, v_ref, z_ref, ba_ref, wq_ref, wk_ref, wv_ref, alog_ref, dtb_ref, gn_ref,
                      o_ref, s_out_ref, s_ref, qbuf, kbuf, vbuf, gcol, bcol, grow, *, lblk):
    h = pl.program_id(1)
    l = pl.program_id(2)
    n_chunks = lblk // CHUNK

    @pl.when(l == 0)
    def _():
        s_ref[...] = jnp.zeros_like(s_ref)
        qbuf[0:CONV_PAD, :] = jnp.zeros((CONV_PAD, qbuf.shape[1]), F32)
        kbuf[0:CONV_PAD, :] = jnp.zeros((CONV_PAD, kbuf.shape[1]), F32)
        vbuf[0:CONV_PAD, :] = jnp.zeros((CONV_PAD, vbuf.shape[1]), F32)

    def conv(x_ref, w_ref, buf, post):
        buf[CONV_PAD:CONV_PAD + lblk, :] = x_ref[...]
        tail = buf[lblk:lblk + CONV_PAD, :]
        taps = B_CONV - 1
        for c in reversed(range(n_chunks)):
            lo = CONV_PAD + c * CHUNK
            y = w_ref[0:1, :] * buf[lo - taps:lo - taps + CHUNK, :]
            for j in range(1, B_CONV):
                y = y + w_ref[j:j + 1, :] * buf[lo - taps + j:lo - taps + j + CHUNK, :]
            buf[lo:lo + CHUNK, :] = post(_silu(y))
        return tail

    def per_head_l2norm(scale):
        def post(y):
            return jnp.concatenate([_l2norm(y[:, i * B_DK:(i + 1) * B_DK]) * scale
                                    for i in range(GDN_QK_PER_STEP)], axis=1)
        return post

    q_tail = conv(q_ref, wq_ref, qbuf, per_head_l2norm(B_DK ** -0.5))
    k_tail = conv(k_ref, wk_ref, kbuf, per_head_l2norm(1.0))
    v_tail = conv(v_ref, wv_ref, vbuf, lambda y: y)

    lane = lax.broadcasted_iota(jnp.int32, (lblk, 2 * B_V_HEADS), 1)
    lane_h = lax.broadcasted_iota(jnp.int32, (1, B_V_HEADS), 1)
    lane_o = lax.broadcasted_iota(jnp.int32, (lblk, LANES), 1)
    ba = ba_ref[...]
    beta_all = jnp.zeros((lblk, LANES), F32)
    g_all = jnp.zeros((lblk, LANES), F32)
    for e in range(GDN_V_PER_STEP):
        hv = h * GDN_V_PER_STEP + e
        beta = jax.nn.sigmoid(jnp.sum(jnp.where(lane == hv, ba, 0.0), axis=-1, keepdims=True))
        a_pre = jnp.sum(jnp.where(lane == hv + B_V_HEADS, ba, 0.0), axis=-1, keepdims=True)
        a_scale = -jnp.exp(jnp.sum(jnp.where(lane_h == hv, alog_ref[...], 0.0), axis=-1, keepdims=True))
        dt_b = jnp.sum(jnp.where(lane_h == hv, dtb_ref[...], 0.0), axis=-1, keepdims=True)
        g = a_scale * _softplus(a_pre + dt_b)
        beta_all = jnp.where(lane_o == e, beta, beta_all)
        g_all = jnp.where(lane_o == e, g, g_all)
    bcol[...] = beta_all
    for j in range(n_chunks):
        g_b = _cumsum_rows(g_all[j * CHUNK:(j + 1) * CHUNK])
        gcol[j * CHUNK:(j + 1) * CHUNK, :] = g_b
        grow[j] = g_b.T[0:8, 0:CHUNK]

    tri = _tri(CHUNK)
    strict = _tri(CHUNK, strict=True)
    zero_blk = jnp.zeros((CHUNK, CHUNK), F32)

    def chunk(j, carry):
        rows = pl.ds(pl.multiple_of(j * CHUNK, CHUNK), CHUNK)
        brow = pl.ds(pl.multiple_of(j * CHUNK + CONV_PAD, 8), CHUNK)
        g_rows = grow[j]
        heads_q = range(GDN_QK_PER_STEP)
        heads_v = range(GDN_V_PER_STEP)
        qs = [qbuf[brow, i * B_DK:(i + 1) * B_DK] for i in heads_q]
        ks = [kbuf[brow, i * B_DK:(i + 1) * B_DK] for i in heads_q]
        kks = [_bdot_nt(k, k) for k in ks]
        qks = [_bdot_nt(q, k) for q, k in zip(qs, ks)]
        g_cs = [gcol[rows, e:e + 1] for e in heads_v]
        betas = [bcol[rows, e:e + 1] for e in heads_v]
        decs = [jnp.exp(jnp.where(tri, g_cs[e] - g_rows[e:e + 1, :], -jnp.inf)) for e in heads_v]
        a_blks = [jnp.where(strict, betas[e] * decs[e] * kks[e // B_REP], 0.0) for e in heads_v]
        rhs = [jnp.concatenate([betas[e] * vbuf[brow, e * B_DV:(e + 1) * B_DV],
                                (betas[e] * jnp.exp(g_cs[e])) * ks[e // B_REP]], axis=-1) for e in heads_v]
        a_bds = [jnp.concatenate(
            [jnp.concatenate([a_blks[i * B_REP + r] if r == c else zero_blk for c in range(B_REP)], axis=1)
             for r in range(B_REP)], axis=0) for i in heads_q]
        t_invs = _unit_lower_inverses(a_bds, CHUNK)
        t_parts = [_split2(t) for t in t_invs]
        r_parts = [_split2(jnp.concatenate(rhs[i * B_REP:(i + 1) * B_REP], axis=0)) for i in heads_q]
        sols = [_dot3_parts(th, tl, rh, rl) for (th, tl), (rh, rl) in zip(t_parts, r_parts)]
        sol_e = [sols[e // B_REP][(e % B_REP) * CHUNK:(e % B_REP + 1) * CHUNK] for e in heads_v]
        s_old = [s_ref[e] for e in heads_v]
        ws = [sol_e[e][:, :B_DV] - _bdot(sol_e[e][:, B_DV:], s_old[e]) for e in heads_v]
        o_inter = [_bdot(qs[e // B_REP] * jnp.exp(g_cs[e]), s_old[e]) for e in heads_v]
        o_intra = [_bdot(decs[e] * qks[e // B_REP], ws[e]) for e in heads_v]
        for e in heads_v:
            g_end = g_cs[e][CHUNK - 1:CHUNK, :]
            s_ref[e] = jnp.exp(g_end) * s_old[e] + _bdot_tn(ks[e // B_REP] * jnp.exp(g_end - g_cs[e]), ws[e])
        for e in heads_v:
            zc = z_ref[rows, e * B_DV:(e + 1) * B_DV]
            o = o_intra[e] + o_inter[e]
            o_ref[rows, e * B_DV:(e + 1) * B_DV] = (_rms(o) * gn_ref[...] * _silu(zc)).astype(o_ref.dtype)
        return carry

    lax.fori_loop(0, n_chunks, chunk, 0)
    qbuf[0:CONV_PAD, :] = q_tail
    kbuf[0:CONV_PAD, :] = k_tail
    vbuf[0:CONV_PAD, :] = v_tail

    @pl.when(l == pl.num_programs(2) - 1)
    def _():
        s_out_ref[0] = s_ref[...]


def _gdn_prompt(proj, ba, b, l, conv_w, a_log, dt_bias, g_norm):
    lblk = _pick_tile(l, SEQ_BLOCK, CHUNK)
    nl = l // lblk
    qw = GDN_QK_PER_STEP * B_DK
    vw = GDN_V_PER_STEP * B_DV
    k_off = B_QK_DIM // qw
    v_off = 2 * B_QK_DIM // vw
    z_off = B_CONV_DIM // vw
    return pl.pallas_call(
        functools.partial(_gdn_chunk_kernel, lblk=lblk),
        grid=(b, B_QK_HEADS // GDN_QK_PER_STEP, nl),
        in_specs=[pl.BlockSpec((lblk, qw), lambda bi, h, li: (bi * nl + li, h)),
                  pl.BlockSpec((lblk, qw), lambda bi, h, li: (bi * nl + li, k_off + h)),
                  pl.BlockSpec((lblk, vw), lambda bi, h, li: (bi * nl + li, v_off + h)),
                  pl.BlockSpec((lblk, vw), lambda bi, h, li: (bi * nl + li, z_off + h)),
                  pl.BlockSpec((lblk, 2 * B_V_HEADS), lambda bi, h, li: (bi * nl + li, 0)),
                  pl.BlockSpec((B_CONV, qw), lambda bi, h, li: (0, h)),
                  pl.BlockSpec((B_CONV, qw), lambda bi, h, li: (0, k_off + h)),
                  pl.BlockSpec((B_CONV, vw), lambda bi, h, li: (0, v_off + h)),
                  pl.BlockSpec((1, B_V_HEADS), lambda bi, h, li: (0, 0)),
                  pl.BlockSpec((1, B_V_HEADS), lambda bi, h, li: (0, 0)),
                  pl.BlockSpec((1, B_DV), lambda bi, h, li: (0, 0))],
        out_specs=[pl.BlockSpec((lblk, vw), lambda bi, h, li: (bi * nl + li, h)),
                   pl.BlockSpec((1, GDN_V_PER_STEP, B_DK, B_DV), lambda bi, h, li: (bi, h, 0, 0))],
        out_shape=[jax.ShapeDtypeStruct((b * l, B_V_DIM), BF16),
                   jax.ShapeDtypeStruct((b, B_V_HEADS, B_DK, B_DV), F32)],
        scratch_shapes=[pltpu.VMEM((GDN_V_PER_STEP, B_DK, B_DV), F32),
                        pltpu.VMEM((CONV_PAD + lblk, qw), F32),
                        pltpu.VMEM((CONV_PAD + lblk, qw), F32),
                        pltpu.VMEM((CONV_PAD + lblk, vw), F32),
                        pltpu.VMEM((lblk, LANES), F32),
                        pltpu.VMEM((lblk, LANES), F32),
                        pltpu.VMEM((lblk // CHUNK, 8, CHUNK), F32)],
        compiler_params=_params("parallel", "parallel", "arbitrary"),
        name="gdn_chunks",
    )(proj, proj, proj, proj, ba, conv_w, conv_w, conv_w,
      a_log.reshape(1, -1), dt_bias.reshape(1, -1), g_norm.reshape(1, -1))


GDN_STEP_BATCH = 2
B_ROWS = B_CONV_DIM // 128


def _gdn_step_kernel(p_ref, cv_ref, cw_ref, ba_ref, alog_ref, dtb_ref, gn_ref, s_ref, o_ref, s_out_ref):
    ba = ba_ref[0]
    beta_all = jax.nn.sigmoid(ba[:, :B_V_HEADS])
    g_all = -jnp.exp(alog_ref[...]) * _softplus(ba[:, B_V_HEADS:] + dtb_ref[...])
    decay_all = jnp.exp(g_all)
    iota_h = lax.broadcasted_iota(jnp.int32, (B_V_HEADS, B_DV), 0)
    for bi in range(GDN_STEP_BATCH):
        y = cw_ref[B_CONV - 1] * p_ref[bi, 0:B_ROWS, :]
        for j in range(B_CONV - 1):
            y = y + cw_ref[j] * cv_ref[bi, j]
        y = _silu(y)
        q = _l2norm(y[0:B_QK_HEADS]) * (B_DK ** -0.5)
        k = _l2norm(y[B_QK_HEADS:2 * B_QK_HEADS])
        v = y[2 * B_QK_HEADS:]
        z = p_ref[bi, B_ROWS:B_ROWS + B_V_HEADS, :]
        q_t, k_t = q.T, k.T
        qk = jnp.sum(q * k, axis=-1, keepdims=True)
        o = jnp.zeros((B_V_HEADS, B_DV), F32)
        for hv in range(B_V_HEADS):
            hq = hv // B_REP
            beta = beta_all[bi:bi + 1, hv:hv + 1]
            decay = decay_all[bi:bi + 1, hv:hv + 1]
            s = s_ref[bi, hv]
            k_col = k_t[:, hq:hq + 1]
            w = beta * v[hv:hv + 1, :] - (beta * decay) * jnp.sum(k_col * s, axis=0, keepdims=True)
            oh = qk[hq:hq + 1, :] * w + decay * jnp.sum(q_t[:, hq:hq + 1] * s, axis=0, keepdims=True)
            s_out_ref[bi, hv] = decay * s + k_col * w
            o = jnp.where(iota_h == hv, oh, o)
        o_ref[bi] = (_rms(o) * gn_ref[...] * _silu(z)).astype(o_ref.dtype)


def _gdn_step(proj, ba, conv0, conv_w, a_log, dt_bias, g_norm, s0):
    b = proj.shape[0]
    bb = GDN_STEP_BATCH
    p3 = proj.reshape(b, -1, 128)
    o, s = pl.pallas_call(
        _gdn_step_kernel,
        grid=(b // bb,),
        in_specs=[pl.BlockSpec((bb, p3.shape[1], 128), lambda i: (i, 0, 0)),
                  pl.BlockSpec((bb, B_CONV - 1, B_ROWS, 128), lambda i: (i, 0, 0, 0)),
                  pl.BlockSpec((B_CONV, B_ROWS, 128), lambda i: (0, 0, 0)),
                  pl.BlockSpec((1, bb, 2 * B_V_HEADS), lambda i: (i, 0, 0)),
                  pl.BlockSpec((1, B_V_HEADS), lambda i: (0, 0)),
                  pl.BlockSpec((1, B_V_HEADS), lambda i: (0, 0)),
                  pl.BlockSpec((1, B_DV), lambda i: (0, 0)),
                  pl.BlockSpec((bb, B_V_HEADS, B_DK, B_DV), lambda i: (i, 0, 0, 0))],
        out_specs=[pl.BlockSpec((bb, B_V_HEADS, B_DV), lambda i: (i, 0, 0)),
                   pl.BlockSpec((bb, B_V_HEADS, B_DK, B_DV), lambda i: (i, 0, 0, 0))],
        out_shape=[jax.ShapeDtypeStruct((b, B_V_HEADS, B_DV), BF16), jax.ShapeDtypeStruct(s0.shape, F32)],
        compiler_params=_params("parallel"),
        name="gdn_step",
    )(p3, conv0.reshape(b, B_CONV - 1, B_ROWS, 128), conv_w.reshape(B_CONV, B_ROWS, 128),
      ba.reshape(b // bb, bb, 2 * B_V_HEADS), a_log.reshape(1, -1), dt_bias.reshape(1, -1), g_norm.reshape(1, -1), s0)
    return o.reshape(b, B_V_DIM), s


D_GROUP = 16
D_GROUPS = D_MODEL // D_GROUP
D_STATE = 64
S5_LANES = 128
S5_GROUPS = S5_LANES // D_GROUP
S5_HALF = S5_GROUPS * D_STATE
S5_BLOCKS = D_MODEL // S5_LANES
S5_SEGMENTS = 8


def _cmul(ar, ai, br, bi):
    return ar * br - ai * bi, ar * bi + ai * br


def _gelu_tanh(x):
    return 0.5 * x * (1.0 + jnp.tanh(math.sqrt(2.0 / math.pi) * (x + 0.044715 * (x * x * x))))


def _s5_tables(a_re, a_im, b_re, b_im, c_re, c_im, log_dt, seg_len):
    lam_re, lam_im = a_re.astype(F32), a_im.astype(F32)
    dt = jnp.exp(log_dt.astype(F32))[:, None]
    mag = jnp.exp(dt * lam_re)
    ab_re, ab_im = mag * jnp.cos(dt * lam_im), mag * jnp.sin(dt * lam_im)
    inv = 1.0 / (lam_re * lam_re + lam_im * lam_im)
    e_re, e_im = _cmul(ab_re - 1.0, ab_im, lam_re * inv, -lam_im * inv)
    bb_re, bb_im = _cmul(e_re[..., None], e_im[..., None], b_re.astype(F32), b_im.astype(F32))
    j = S5_BLOCKS
    eye = jnp.eye(S5_GROUPS, dtype=F32)

    def pack_rows(t):
        return t.reshape(j, 1, S5_HALF)

    def bdiag_in(t):
        t = t.reshape(j, S5_GROUPS, D_STATE, D_GROUP)
        return jnp.einsum('jgpc,gh->jgchp', t, eye).reshape(j, S5_LANES, S5_HALF)

    def bdiag_out(t):
        t = t.reshape(j, S5_GROUPS, D_GROUP, D_STATE)
        return jnp.einsum('jgcp,gh->jgphc', t, eye).reshape(j, S5_HALF, S5_LANES)

    ab = jnp.concatenate([pack_rows(ab_re), pack_rows(ab_im)], axis=-1)
    w_bu = jnp.concatenate([bdiag_in(bb_re), bdiag_in(bb_im)], axis=-1)
    w_c = jnp.concatenate([bdiag_out(c_re.astype(F32)), -bdiag_out(c_im.astype(F32))], axis=1)
    pr, pi = pack_rows(ab_re), pack_rows(ab_im)
    n = 1
    while n < seg_len:
        tr, ti = pr[:, n - 1:n], pi[:, n - 1:n]
        nr, ni = _cmul(pr, pi, tr, ti)
        pr, pi = jnp.concatenate([pr, nr], axis=1), jnp.concatenate([pi, ni], axis=1)
        n *= 2
    pw = jnp.concatenate([pr[:, :seg_len], pi[:, :seg_len]], axis=-1)
    return ab, w_bu, w_c, pw


def _s5_scan_kernel(x_ref, ab_ref, wbu_ref, wc_ref, pw_ref, dsk_ref, y_ref, s_out_ref, xp, buf, yp, *, seq, seg):
    nc = S5_HALF // 128
    ns = S5_SEGMENTS

    def tile(tau):
        return pl.ds(pl.multiple_of(tau * ns, ns), ns)

    def interleave(tau, c):
        xp[tile(tau), :] = x_ref[pl.ds(tau, ns, stride=seg), :]
        return c

    lax.fori_loop(0, seg, interleave, 0, unroll=8)
    x = xp[...]
    bu = _bdot(x, wbu_ref[0])
    for c in range(2 * nc):
        buf[c] = bu[:, c * 128:(c + 1) * 128]

    a_res = [jnp.broadcast_to(ab_ref[0, :, c * 128:(c + 1) * 128], (ns, 128)) for c in range(nc)]
    a_ims = [jnp.broadcast_to(ab_ref[0, :, S5_HALF + c * 128:S5_HALF + (c + 1) * 128], (ns, 128))
             for c in range(nc)]

    def step(tau, carry):
        rows = tile(tau)
        out = []
        for c in range(nc):
            s_re, s_im = carry[c]
            n_re = a_res[c] * s_re - a_ims[c] * s_im + buf[c, rows, :]
            n_im = a_res[c] * s_im + a_ims[c] * s_re + buf[nc + c, rows, :]
            buf[c, rows, :] = n_re
            buf[nc + c, rows, :] = n_im
            out.append((n_re, n_im))
        return tuple(out)

    zeros = jnp.zeros((ns, 128), F32)
    local_end = lax.fori_loop(0, seg, step, tuple((zeros, zeros) for _ in range(nc)), unroll=2)

    iota_r = lax.broadcasted_iota(jnp.int32, (ns, 128), 0)
    enter = []
    for c in range(nc):
        l_re, l_im = local_end[c]
        pe_re = pw_ref[0, seg - 1:seg, c * 128:(c + 1) * 128]
        pe_im = pw_ref[0, seg - 1:seg, S5_HALF + c * 128:S5_HALF + (c + 1) * 128]
        e_re, e_im = l_re[0:1], l_im[0:1]
        in_re, in_im = zeros, zeros
        for r in range(1, ns):
            in_re = jnp.where(iota_r == r, e_re, in_re)
            in_im = jnp.where(iota_r == r, e_im, in_im)
            t_re, t_im = _cmul(pe_re, pe_im, e_re, e_im)
            e_re, e_im = l_re[r:r + 1] + t_re, l_im[r:r + 1] + t_im
        enter.append((in_re, in_im))

    def fix(group, carry):
        taus = pl.ds(pl.multiple_of(group * 8, 8), 8)
        for c in range(nc):
            p_re8 = pw_ref[0, taus, c * 128:(c + 1) * 128]
            p_im8 = pw_ref[0, taus, S5_HALF + c * 128:S5_HALF + (c + 1) * 128]
            for u in range(8):
                rows = tile(group * 8 + u)
                f_re, f_im = _cmul(p_re8[u:u + 1], p_im8[u:u + 1], enter[c][0], enter[c][1])
                buf[c, rows, :] = buf[c, rows, :] + f_re
                buf[nc + c, rows, :] = buf[nc + c, rows, :] + f_im
        return carry

    lax.fori_loop(0, seg // 8, fix, 0)

    y = dsk_ref[...] * x
    for c in range(2 * nc):
        s_out_ref[0, 0, :, c * 128:(c + 1) * 128] = buf[c, seq - 1:seq, :]
        y = y + _bdot(buf[c], wc_ref[0, c * 128:(c + 1) * 128, :])
    yp[...] = _gelu_tanh(y)
    for r in range(ns):
        y_ref[r * seg:(r + 1) * seg, :] = yp[pl.ds(r, seg, stride=ns), :].astype(y_ref.dtype)


def _unpack_state(s):
    b = s.shape[0]
    return (s[..., :S5_HALF].reshape(b, D_GROUPS, D_STATE), s[..., S5_HALF:].reshape(b, D_GROUPS, D_STATE))


def _s5_prompt(x, b, l, tables, d_skip):
    d = x.shape[1]
    ab, w_bu, w_c, pw = tables
    seg = l // S5_SEGMENTS
    y, s = pl.pallas_call(
        functools.partial(_s5_scan_kernel, seq=l, seg=seg),
        grid=(b, S5_BLOCKS),
        in_specs=[pl.BlockSpec((l, S5_LANES), lambda bi, j: (bi, j)),
                  pl.BlockSpec((1, 1, 2 * S5_HALF), lambda bi, j: (j, 0, 0)),
                  pl.BlockSpec((1, S5_LANES, 2 * S5_HALF), lambda bi, j: (j, 0, 0)),
                  pl.BlockSpec((1, 2 * S5_HALF, S5_LANES), lambda bi, j: (j, 0, 0)),
                  pl.BlockSpec((1, seg, 2 * S5_HALF), lambda bi, j: (j, 0, 0)),
                  pl.BlockSpec((1, S5_LANES), lambda bi, j: (0, j))],
        out_specs=[pl.BlockSpec((l, S5_LANES), lambda bi, j: (bi, j)),
                   pl.BlockSpec((1, 1, 1, 2 * S5_HALF), lambda bi, j: (bi, j, 0, 0))],
        out_shape=[jax.ShapeDtypeStruct((b * l, d), BF16),
                   jax.ShapeDtypeStruct((b, S5_BLOCKS, 1, 2 * S5_HALF), F32)],
        scratch_shapes=[pltpu.VMEM((l, S5_LANES), F32),
                        pltpu.VMEM((2 * S5_HALF // 128, l, 128), F32),
                        pltpu.VMEM((l, S5_LANES), F32)],
        compiler_params=_params("parallel", "parallel"),
        name="s5_scan",
    )(x, ab, w_bu, w_c, pw, d_skip.reshape(1, d))
    return y, _unpack_state(s[:, :, 0, :])


def _s5_step_kernel(x_ref, s_ref, ab_ref, wbu_ref, wc_ref, dsk_ref, y_ref, s_out_ref):
    hf = S5_HALF
    x = x_ref[...]
    bu = _bdot(x, wbu_ref[0])
    s0 = s_ref[0]
    i_re, i_im = _cmul(ab_ref[0, :, :hf], ab_ref[0, :, hf:], s0[:, :hf], s0[:, hf:])
    s_new = jnp.concatenate([i_re, i_im], axis=-1) + bu
    s_out_ref[0] = s_new
    y = _bdot(s_new, wc_ref[0]) + dsk_ref[...] * x
    y_ref[...] = _gelu_tanh(y).astype(y_ref.dtype)


def _s5_step(x, tables, d_skip, s0_re, s0_im):
    b, d = x.shape
    ab, w_bu, w_c, _ = tables
    s0 = jnp.concatenate([s0_re.reshape(b, S5_BLOCKS, S5_HALF), s0_im.reshape(b, S5_BLOCKS, S5_HALF)], axis=-1)
    s0 = jnp.swapaxes(s0, 0, 1)
    y, s = pl.pallas_call(
        _s5_step_kernel,
        grid=(S5_BLOCKS,),
        in_specs=[pl.BlockSpec((b, S5_LANES), lambda j: (0, j)),
                  pl.BlockSpec((1, b, 2 * S5_HALF), lambda j: (j, 0, 0)),
                  pl.BlockSpec((1, 1, 2 * S5_HALF), lambda j: (j, 0, 0)),
                  pl.BlockSpec((1, S5_LANES, 2 * S5_HALF), lambda j: (j, 0, 0)),
                  pl.BlockSpec((1, 2 * S5_HALF, S5_LANES), lambda j: (j, 0, 0)),
                  pl.BlockSpec((1, S5_LANES), lambda j: (0, j))],
        out_specs=[pl.BlockSpec((b, S5_LANES), lambda j: (0, j)),
                   pl.BlockSpec((1, b, 2 * S5_HALF), lambda j: (j, 0, 0))],
        out_shape=[jax.ShapeDtypeStruct((b, d), BF16), jax.ShapeDtypeStruct((S5_BLOCKS, b, 2 * S5_HALF), F32)],
        compiler_params=_params("parallel"),
        name="s5_step",
    )(x, s0, ab, w_bu, w_c, d_skip.reshape(1, d))
    return y, _unpack_state(jnp.swapaxes(s, 0, 1))


def _glu_kernel(x_ref, w1_ref, w2_ref, o_ref):
    x = x_ref[...]
    y1 = jnp.dot(x, w1_ref[...].astype(BF16), preferred_element_type=F32)
    y2 = jnp.dot(x, w2_ref[...].astype(BF16), preferred_element_type=F32)
    o_ref[...] = y1 * jax.nn.sigmoid(y2)


def _glu_matmul(x, w, tm_target=1664, tn_target=256):
    m, k = x.shape
    n = w.shape[1] // 2
    tm = _pick_tile(m, tm_target, 16)
    tn = _pick_tile(n, tn_target, 128)
    nb = n // tn
    return pl.pallas_call(
        _glu_kernel,
        grid=(m // tm, nb),
        in_specs=[pl.BlockSpec((tm, k), lambda i, j: (i, 0)),
                  pl.BlockSpec((k, tn), lambda i, j: (0, j)),
                  pl.BlockSpec((k, tn), lambda i, j: (0, j + nb))],
        out_specs=pl.BlockSpec((tm, tn), lambda i, j: (i, j)),
        out_shape=jax.ShapeDtypeStruct((m, n), F32),
        compiler_params=_params("parallel", "parallel"),
        name="glu_matmul",
    )(x, w, w)


def kernel(x_prompt, x_sample, state_a_S, state_b_S, state_b_conv, state_c_C, state_c_n, state_c_m, state_d_re, state_d_im, a_w_in, a_lb_logits, a_g_norm, a_w_out, b_w_in, b_conv_w, b_a_log, b_dt_bias, b_g_norm, b_w_out, c_w_in, c_b_gate, c_g_norm, c_w_out, d_a_re, d_a_im, d_b_re, d_b_im, d_c_re, d_c_im, d_skip, d_log_dt, d_w_glu, moe_w_router, moe_b_router, moe_w_gu, moe_w_down, moe_w_gu_s, moe_w_down_s, ln_mix_g, ln_mix_b, ln_ffn_g, ln_ffn_b):
    bp, l, d = x_prompt.shape
    bs = x_sample.shape[0]
    tp = bp * l
    x = jnp.concatenate([x_prompt.reshape(tp, d), x_sample.reshape(bs, d)], axis=0)
    xb = x.astype(BF16)

    def finish_layer(x, h, layer):
        x, xb = _add_ln(x, h, ln_mix_g[layer], ln_mix_b[layer])
        return _moe_ln(x, xb, moe_w_router, moe_b_router, moe_w_gu, moe_w_down, moe_w_gu_s, moe_w_down_s,
                       ln_ffn_g, ln_ffn_b, layer)

    proj = _matmul(xb, a_w_in)
    lb = _hgrn2_lower_bound(a_lb_logits, 0)
    o_p, a_s_p = _hgrn2_prompt(proj, bp, l, lb, a_g_norm)
    o_s, a_s_s = _hgrn2_step(proj[tp:], state_a_S, lb, a_g_norm)
    h = _matmul(jnp.concatenate([o_p, o_s], axis=0), a_w_out)
    x, xb = finish_layer(x, h, 0)

    n_main = B_CONV_DIM + B_V_DIM
    proj = _matmul(xb, b_w_in, n_cols=n_main)
    ba = _matmul(xb, b_w_in[:, n_main:])
    o_p, b_s_p = _gdn_prompt(proj, ba, bp, l, b_conv_w, b_a_log, b_dt_bias, b_g_norm)
    o_s, b_s_s = _gdn_step(proj[tp:], ba[tp:], state_b_conv, b_conv_w, b_a_log, b_dt_bias, b_g_norm, state_b_S)
    qkv_s = proj[tp:, :B_CONV_DIM]
    b_conv_p = jnp.stack([proj[(i + 1) * l - (B_CONV - 1):(i + 1) * l, :B_CONV_DIM] for i in range(bp)], axis=0)
    b_conv_s = jnp.concatenate([state_b_conv[:, 1:, :], qkv_s[:, None, :]], axis=1)
    h = _matmul(jnp.concatenate([o_p, o_s], axis=0), b_w_out)
    x, xb = finish_layer(x, h, 1)

    n_main = 2 * C_QK_DIM + 2 * C_V_DIM
    proj = _matmul(xb, c_w_in, n_cols=n_main)
    gates = _matmul(xb, c_w_in[:, n_main:])
    o_p, c_c_p, c_n_p, c_m_p = _mlstm_prompt(proj, gates, bp, l, c_b_gate, c_g_norm)
    o_s, c_c_s, c_n_s, c_m_s = _mlstm_step(proj[tp:], gates[tp:], c_b_gate, c_g_norm, state_c_C, state_c_n, state_c_m)
    h = _matmul(jnp.concatenate([o_p, o_s], axis=0), c_w_out)
    x, xb = finish_layer(x, h, 2)

    tables = _s5_tables(d_a_re, d_a_im, d_b_re, d_b_im, d_c_re, d_c_im, d_log_dt, l // S5_SEGMENTS)
    y_p, (d_re_p, d_im_p) = _s5_prompt(x, bp, l, tables, d_skip)
    y_s, (d_re_s, d_im_s) = _s5_step(x[tp:], tables, d_skip, state_d_re, state_d_im)
    h = _glu_matmul(jnp.concatenate([y_p, y_s], axis=0), d_w_glu)
    x, xb = finish_layer(x, h, 3)

    y_prompt = x[:tp].reshape(bp, l, d)
    y_sample = x[tp:].reshape(bs, 1, d)
    return (y_prompt, y_sample, a_s_p, a_s_s, b_s_p, b_s_s, b_conv_p, b_conv_s,
            c_c_p, c_c_s, c_n_p, c_n_s, c_m_p, c_m_s, d_re_p, d_re_s, d_im_p, d_im_s)
```

```python
import functools
import math

import jax
import jax.numpy as jnp
from jax import lax
from jax.experimental import pallas as pl
from jax.experimental.pallas import tpu as pltpu

F32 = jnp.float32
BF16 = jnp.bfloat16
HI = lax.Precision.HIGHEST

D_MODEL = 2048
DEPTH = 4
CHUNK = 64
DN_ALPHA = (2 * DEPTH) ** 0.25
LN_EPS = 1e-5
RMS_EPS = 1e-6

VMEM_LIMIT_BYTES = 56 * 1024 * 1024


def _params(*sem):
    return pltpu.CompilerParams(dimension_semantics=sem, vmem_limit_bytes=VMEM_LIMIT_BYTES)


def _pick_tile(n, target, mult):
    best = None
    for t in range(mult, min(n, target) + 1, mult):
        if n % t == 0:
            best = t
    return best if best is not None else n


def _mm_kernel(x_ref, w_ref, o_ref):
    o_ref[...] = jnp.dot(x_ref[...], w_ref[...].astype(BF16), preferred_element_type=F32).astype(o_ref.dtype)


def _matmul(x, w, n_cols=None, out_dtype=F32, tm_target=1664, tn_target=512):
    m, k = x.shape
    n = w.shape[1] if n_cols is None else n_cols
    tm = _pick_tile(m, tm_target, 16)
    tn = _pick_tile(n, tn_target, 128)
    return pl.pallas_call(
        _mm_kernel,
        grid=(m // tm, n // tn),
        in_specs=[pl.BlockSpec((tm, k), lambda i, j: (i, 0)), pl.BlockSpec((k, tn), lambda i, j: (0, j))],
        out_specs=pl.BlockSpec((tm, tn), lambda i, j: (i, j)),
        out_shape=jax.ShapeDtypeStruct((m, n), out_dtype),
        compiler_params=_params("parallel", "parallel"),
        name="matmul",
    )(x, w)


def _layernorm_rows(z, g, b):
    mu = jnp.mean(z, axis=-1, keepdims=True)
    zc = z - mu
    var = jnp.mean(zc * zc, axis=-1, keepdims=True)
    return zc * lax.rsqrt(var + LN_EPS) * g + b


U32 = jnp.uint32
BF16_HIGH_MASK = 0xFFFF0000


def _pack_bf16_halves(yb):
    half = yb.shape[1] // 2
    bits = lax.bitcast_convert_type(yb.astype(F32), U32)
    return (bits[:, :half] >> 16) | (bits[:, half:] & jnp.uint32(BF16_HIGH_MASK))


def _unpack_bf16_halves(w):
    lo = lax.bitcast_convert_type(w << 16, F32).astype(BF16)
    hi = lax.bitcast_convert_type(w & jnp.uint32(BF16_HIGH_MASK), F32).astype(BF16)
    return lo, hi


def _add_ln_kernel(x_ref, h_ref, g_ref, b_ref, o_ref, ob_ref, op_ref):
    y = _layernorm_rows(DN_ALPHA * x_ref[...] + h_ref[...], g_ref[...], b_ref[...])
    yb = y.astype(BF16)
    o_ref[...] = y
    ob_ref[...] = yb
    op_ref[...] = _pack_bf16_halves(yb)


def _add_ln(x, h, g, b, tm_target=640):
    m, d = x.shape
    tm = _pick_tile(m, tm_target, 16)
    row = pl.BlockSpec((tm, d), lambda i: (i, 0))
    vec = pl.BlockSpec((1, d), lambda i: (0, 0))
    return pl.pallas_call(
        _add_ln_kernel,
        grid=(m // tm,),
        in_specs=[row, row, vec, vec],
        out_specs=[row, row, pl.BlockSpec((tm, d // 2), lambda i: (i, 0))],
        out_shape=[jax.ShapeDtypeStruct((m, d), F32), jax.ShapeDtypeStruct((m, d), BF16),
                   jax.ShapeDtypeStruct((m, d // 2), U32)],
        compiler_params=_params("parallel"),
        name="add_ln",
    )(x, h, g.reshape(1, d), b.reshape(1, d))


def _silu(x):
    return x * jax.nn.sigmoid(x)


def _ffn_kernel(x_ref, wgu_ref, wd_ref, o_ref, *, dh):
    gu = jnp.dot(x_ref[...], wgu_ref[0].astype(BF16), preferred_element_type=F32)
    a = _silu(gu[:, :dh]) * gu[:, dh:]
    o_ref[...] = jnp.dot(a.astype(BF16), wd_ref[0].astype(BF16), preferred_element_type=F32)


def _dense_ffn(xb, w_gu, w_down, layer, tm_target=640):
    m, d = xb.shape
    dh = w_down.shape[1]
    tm = _pick_tile(m, tm_target, 16)
    return pl.pallas_call(
        functools.partial(_ffn_kernel, dh=dh),
        grid=(m // tm,),
        in_specs=[pl.BlockSpec((tm, d), lambda i: (i, 0)),
                  pl.BlockSpec((1, d, 2 * dh), lambda i: (layer, 0, 0)),
                  pl.BlockSpec((1, dh, d), lambda i: (layer, 0, 0))],
        out_specs=pl.BlockSpec((tm, d), lambda i: (i, 0)),
        out_shape=jax.ShapeDtypeStruct((m, d), F32),
        compiler_params=_params("parallel"),
        name="shared_ffn",
    )(xb, w_gu, w_down)


N_EXPERTS = 64
TOP_K = 8
N_ROUTE_GROUPS = 8
TOPK_ROUTE_GROUPS = 4
GROUP_SIZE = N_EXPERTS // N_ROUTE_GROUPS
ROUTED_SCALE = 2.5
EXPERT_TILE = 256


def _router_kernel(x_ref, wt_ref, b_ref, e_ref, g_ref):
    logits = lax.dot_general(wt_ref[...].astype(BF16), x_ref[...], (((1,), (1,)), ((), ())),
                             preferred_element_type=F32)
    scores = jax.nn.sigmoid(logits)
    biased = scores + b_ref[...]
    tm = scores.shape[1]
    neg = jnp.float32(-jnp.inf)
    iota_g = lax.broadcasted_iota(jnp.int32, (GROUP_SIZE, tm), 0)
    gscore = jnp.zeros((N_ROUTE_GROUPS, tm), F32)
    iota_r = lax.broadcasted_iota(jnp.int32, (N_ROUTE_GROUPS, tm), 0)
    for g in range(N_ROUTE_GROUPS):
        v = biased[g * GROUP_SIZE:(g + 1) * GROUP_SIZE, :]
        m1 = jnp.max(v, axis=0, keepdims=True)
        i1 = jnp.min(jnp.where(v == m1, iota_g, GROUP_SIZE), axis=0, keepdims=True)
        m2 = jnp.max(jnp.where(iota_g == i1, neg, v), axis=0, keepdims=True)
        gscore = jnp.where(iota_r == g, m1 + m2, gscore)
    ok = jnp.zeros((N_ROUTE_GROUPS, tm), jnp.int32)
    for _ in range(TOPK_ROUTE_GROUPS):
        m = jnp.max(gscore, axis=0, keepdims=True)
        gi = jnp.min(jnp.where(gscore == m, iota_r, N_ROUTE_GROUPS), axis=0, keepdims=True)
        hit = iota_r == gi
        ok = jnp.where(hit, 1, ok)
        gscore = jnp.where(hit, neg, gscore)
    masked = jnp.concatenate(
        [jnp.where(ok[g:g + 1, :] > 0, biased[g * GROUP_SIZE:(g + 1) * GROUP_SIZE, :], neg)
         for g in range(N_ROUTE_GROUPS)], axis=0)
    iota_e = lax.broadcasted_iota(jnp.int32, (N_EXPERTS, tm), 0)
    iota_k = lax.broadcasted_iota(jnp.int32, (TOP_K, tm), 0)
    top_e = jnp.zeros((TOP_K, tm), jnp.int32)
    gate = jnp.zeros((TOP_K, tm), F32)
    for k in range(TOP_K):
        m = jnp.max(masked, axis=0, keepdims=True)
        ei = jnp.min(jnp.where(masked == m, iota_e, N_EXPERTS), axis=0, keepdims=True)
        hit = iota_e == ei
        gk = jnp.sum(jnp.where(hit, scores, 0.0), axis=0, keepdims=True)
        top_e = jnp.where(iota_k == k, ei, top_e)
        gate = jnp.where(iota_k == k, gk, gate)
        masked = jnp.where(hit, neg, masked)
    e_ref[...] = top_e
    g_ref[...] = ROUTED_SCALE * gate / jnp.sum(gate, axis=0, keepdims=True)


def _router(x, w_router_t, b_router, tm_target=640):
    t, d = x.shape
    tm = _pick_tile(t, tm_target, 128)
    return pl.pallas_call(
        _router_kernel,
        grid=(t // tm,),
        in_specs=[pl.BlockSpec((tm, d), lambda i: (i, 0)),
                  pl.BlockSpec((N_EXPERTS, d), lambda i: (0, 0)),
                  pl.BlockSpec((N_EXPERTS, 1), lambda i: (0, 0))],
        out_specs=[pl.BlockSpec((TOP_K, tm), lambda i: (0, i)), pl.BlockSpec((TOP_K, tm), lambda i: (0, i))],
        out_shape=[jax.ShapeDtypeStruct((TOP_K, t), jnp.int32), jax.ShapeDtypeStruct((TOP_K, t), F32)],
        compiler_params=_params("parallel"),
        name="router",
    )(x, w_router_t, b_router.reshape(N_EXPERTS, 1))


LANES = 128


def _dispatch_kernel(e_ref, pos_ref, cnt_ref, rank_ref):
    n_blocks = e_ref.shape[1] // LANES
    iota_e = lax.broadcasted_iota(jnp.int32, (N_EXPERTS, LANES), 0)
    incl = (lax.broadcasted_iota(jnp.int32, (LANES, LANES), 0)
            <= lax.broadcasted_iota(jnp.int32, (LANES, LANES), 1)).astype(BF16)
    iota_k = lax.broadcasted_iota(jnp.int32, (TOP_K, LANES), 0)

    def pick(e_blk, table):
        out = jnp.zeros((TOP_K, LANES), F32)
        for k in range(TOP_K):
            v = jnp.sum(jnp.where(iota_e == e_blk[k:k + 1, :], table, 0.0), axis=0, keepdims=True)
            out = jnp.where(iota_k == k, v, out)
        return out

    def count_block(b, carry):
        cols = pl.ds(pl.multiple_of(b * LANES, LANES), LANES)
        e_blk = e_ref[:, cols]
        member = jnp.zeros((N_EXPERTS, LANES), F32)
        for k in range(TOP_K):
            member = member + (iota_e == e_blk[k:k + 1, :]).astype(F32)
        seen = jnp.dot(member.astype(BF16), incl, preferred_element_type=F32) + carry
        rank_ref[:, cols] = pick(e_blk, seen - member)
        return jnp.broadcast_to(seen[:, LANES - 1:LANES], (N_EXPERTS, LANES))

    counts = lax.fori_loop(0, n_blocks, count_block, jnp.zeros((N_EXPERTS, LANES), F32))
    cnt_ref[...] = counts.astype(jnp.int32)
    padded = jnp.ceil(counts * (1.0 / EXPERT_TILE)) * EXPERT_TILE
    start = jnp.dot(_tri(N_EXPERTS, strict=True).astype(F32), padded, preferred_element_type=F32, precision=HI)

    def place_block(b, carry):
        cols = pl.ds(pl.multiple_of(b * LANES, LANES), LANES)
        pos_ref[:, cols] = (pick(e_ref[:, cols], start) + rank_ref[:, cols]).astype(jnp.int32)
        return carry

    lax.fori_loop(0, n_blocks, place_block, 0)


def _dispatch_tables(top_e_t, n_tiles):
    k, t = top_e_t.shape
    pos_t, cnt = pl.pallas_call(
        _dispatch_kernel,
        out_shape=[jax.ShapeDtypeStruct((k, t), jnp.int32), jax.ShapeDtypeStruct((N_EXPERTS, LANES), jnp.int32)],
        scratch_shapes=[pltpu.VMEM((k, t), F32)],
        compiler_params=pltpu.CompilerParams(vmem_limit_bytes=VMEM_LIMIT_BYTES),
        name="dispatch",
    )(top_e_t)
    counts = cnt[:, 0]
    pends = jnp.cumsum((counts + EXPERT_TILE - 1) // EXPERT_TILE * EXPERT_TILE)
    n_used = pends[-1] // EXPERT_TILE
    tile_idx = jnp.arange(n_tiles, dtype=jnp.int32)
    tile_e = jnp.minimum(jnp.sum((pends[None, :] <= (tile_idx * EXPERT_TILE)[:, None]).astype(jnp.int32), axis=1),
                         N_EXPERTS - 1)
    tile_e = jnp.where(tile_idx < n_used, tile_e, tile_e[jnp.maximum(n_used - 1, 0)])
    last_tile = jnp.where(counts > 0, pends // EXPERT_TILE - 1, -1).astype(jnp.int32)
    return pos_t.reshape(-1), tile_e.astype(jnp.int32), n_used.astype(jnp.int32).reshape(1), last_tile


SCATTER_TILE = 640


def _scatter_rows_kernel(pos_ref, last_tile_ref, n_used_ref, x_ref, xs_ref, zbuf, zsem, sem, *, n_tok, max_idle):
    tt = x_ref.shape[0]
    n_tiles = xs_ref.shape[0] // EXPERT_TILE
    n_used = n_used_ref[0]

    @pl.when(pl.program_id(0) == 0)
    def _():
        zbuf[...] = jnp.zeros_like(zbuf)

        def fill(tile):
            return pltpu.make_async_copy(zbuf, xs_ref.at[pl.ds(tile * EXPERT_TILE, EXPERT_TILE)], zsem)

        def for_each_padded_tile(action):
            for e in range(N_EXPERTS):
                tile = last_tile_ref[e]

                @pl.when(tile >= 0)
                def _(tile=tile):
                    action(fill(tile))
            for j in range(max_idle):
                @pl.when(n_used + j < n_tiles)
                def _(j=j):
                    action(fill(n_used + j))

        for_each_padded_tile(lambda copy: copy.start())
        for_each_padded_tile(lambda copy: copy.wait())

    base = pl.program_id(0) * tt

    def body(r, c):
        for k in range(TOP_K):
            p = pos_ref[k * n_tok + base + r]
            pltpu.make_async_copy(x_ref.at[pl.ds(r, 1)], xs_ref.at[pl.ds(p, 1)], sem).start()
        return c

    lax.fori_loop(0, tt, body, 0)
    for k in range(TOP_K):
        pltpu.make_async_copy(x_ref, xs_ref.at[pl.ds(0, tt)], sem).wait()


def _sort_rows(x, pos, last_tile, n_used, n_tiles):
    t, d = x.shape
    tt = _pick_tile(t, SCATTER_TILE, 8)
    max_idle = n_tiles - (t * TOP_K) // EXPERT_TILE
    return pl.pallas_call(
        functools.partial(_scatter_rows_kernel, n_tok=t, max_idle=max_idle),
        grid_spec=pltpu.PrefetchScalarGridSpec(
            num_scalar_prefetch=3, grid=(t // tt,),
            in_specs=[pl.BlockSpec((tt, d), lambda i, p, lt, nu: (i, 0))],
            out_specs=pl.BlockSpec(memory_space=pl.ANY),
            scratch_shapes=[pltpu.VMEM((EXPERT_TILE, d), x.dtype), pltpu.SemaphoreType.DMA(()),
                            pltpu.SemaphoreType.DMA(())]),
        out_shape=jax.ShapeDtypeStruct((n_tiles * EXPERT_TILE, d), x.dtype),
        compiler_params=_params("arbitrary"),
        name="scatter_rows",
    )(pos, last_tile, n_used, x)


def _expert_kernel(tile_e_ref, n_used_ref, xs_ref, wgu_ref, wd_ref, o_ref, wgu_b, wd_b, *, dh):
    i = pl.program_id(0)
    n_used = n_used_ref[0]
    e = tile_e_ref[i]
    e_prev = tile_e_ref[jnp.maximum(i - 1, 0)]

    @pl.when(jnp.logical_and(i < n_used, jnp.logical_or(i == 0, e != e_prev)))
    def _():
        wgu_b[...] = wgu_ref[0, 0].astype(BF16)
        wd_b[...] = wd_ref[0, 0].astype(BF16)

    @pl.when(i < n_used)
    def _():
        x_lo, x_hi = _unpack_bf16_halves(xs_ref[...])
        half = x_lo.shape[1]
        gu = (jnp.dot(x_lo, wgu_b[:half, :], preferred_element_type=F32)
              + jnp.dot(x_hi, wgu_b[half:, :], preferred_element_type=F32))
        a = _silu(gu[:, :dh]) * gu[:, dh:]
        o_ref[...] = jnp.dot(a.astype(BF16), wd_b[...], preferred_element_type=F32)

    @pl.when(i >= n_used)
    def _():
        o_ref[...] = jnp.zeros_like(o_ref)


def _routed_experts(xs, tile_e, n_used, w_gu, w_down, layer):
    d = w_gu.shape[2]
    dh = w_down.shape[2]
    n_tiles = tile_e.shape[0]
    grid_spec = pltpu.PrefetchScalarGridSpec(
        num_scalar_prefetch=2,
        grid=(n_tiles,),
        in_specs=[pl.BlockSpec((EXPERT_TILE, d // 2), lambda i, te, nu: (jnp.minimum(i, nu[0] - 1), 0)),
                  pl.BlockSpec((1, 1, d, 2 * dh), lambda i, te, nu: (layer, te[i], 0, 0)),
                  pl.BlockSpec((1, 1, dh, d), lambda i, te, nu: (layer, te[i], 0, 0))],
        out_specs=pl.BlockSpec((EXPERT_TILE, d), lambda i, te, nu: (i, 0)),
        scratch_shapes=[pltpu.VMEM((d, 2 * dh), BF16), pltpu.VMEM((dh, d), BF16)],
    )
    return pl.pallas_call(
        functools.partial(_expert_kernel, dh=dh),
        grid_spec=grid_spec,
        out_shape=jax.ShapeDtypeStruct((n_tiles * EXPERT_TILE, d), F32),
        compiler_params=_params("arbitrary"),
        name="routed_experts",
    )(tile_e, n_used, xs, w_gu, w_down)


COMBINE_TILE = 128


def _combine_kernel(pos_ref, rows_hbm, gate_ref, x_ref, sh_ref, g_ref, b_ref, o_ref, ob_ref, buf, sem):
    i = pl.program_id(0)
    n = pl.num_programs(0)
    slot = i % 2

    n_tok = n * COMBINE_TILE

    def issue(tile, s):
        base = tile * COMBINE_TILE

        def body(r, c):
            for k in range(TOP_K):
                p = pos_ref[k * n_tok + base + r]
                pltpu.make_async_copy(rows_hbm.at[pl.ds(p, 1)], buf.at[s, k, pl.ds(r, 1)], sem.at[s]).start()
            return c

        lax.fori_loop(0, COMBINE_TILE, body, 0)

    @pl.when(i == 0)
    def _():
        issue(0, 0)

    @pl.when(i + 1 < n)
    def _():
        issue(i + 1, 1 - slot)

    for k in range(TOP_K):
        pltpu.make_async_copy(rows_hbm.at[pl.ds(0, COMBINE_TILE)], buf.at[slot, k], sem.at[slot]).wait()
    gate = gate_ref[...]
    z = DN_ALPHA * x_ref[...] + sh_ref[...]
    for k in range(TOP_K):
        z = z + gate[:, k:k + 1] * buf[slot, k]
    y = _layernorm_rows(z, g_ref[...], b_ref[...])
    o_ref[...] = y
    ob_ref[...] = y.astype(BF16)


def _combine_ln(rows, pos, gate, x, shared, g, b):
    t, d = x.shape
    row = lambda w: pl.BlockSpec((COMBINE_TILE, w), lambda i, p: (i, 0))
    vec = pl.BlockSpec((1, d), lambda i, p: (0, 0))
    grid_spec = pltpu.PrefetchScalarGridSpec(
        num_scalar_prefetch=1,
        grid=(t // COMBINE_TILE,),
        in_specs=[pl.BlockSpec(memory_space=pl.ANY), row(TOP_K), row(d), row(d), vec, vec],
        out_specs=[row(d), row(d)],
        scratch_shapes=[pltpu.VMEM((2, TOP_K, COMBINE_TILE, d), F32), pltpu.SemaphoreType.DMA((2,))],
    )
    return pl.pallas_call(
        _combine_kernel,
        grid_spec=grid_spec,
        out_shape=[jax.ShapeDtypeStruct((t, d), F32), jax.ShapeDtypeStruct((t, d), BF16)],
        compiler_params=_params("arbitrary"),
        name="combine_ln",
    )(pos, rows, gate, x, shared, g.reshape(1, d), b.reshape(1, d))


def _moe_ln(x, xb, xpk, w_router, b_router, w_gu, w_down, w_gu_s, w_down_s, ln_g, ln_b, layer):
    t = x.shape[0]
    top_e_t, gate_t = _router(xb, w_router[layer].T, b_router[layer])
    n_tiles = (t * TOP_K + N_EXPERTS * (EXPERT_TILE - 1) + EXPERT_TILE - 1) // EXPERT_TILE
    pos, tile_e, n_used, last_tile = _dispatch_tables(top_e_t, n_tiles)
    xs = _sort_rows(xpk, pos, last_tile, n_used, n_tiles)
    rows = _routed_experts(xs, tile_e, n_used, w_gu, w_down, layer)
    shared = _dense_ffn(xb, w_gu_s, w_down_s, layer)
    return _combine_ln(rows, pos, gate_t.T, x, shared, ln_g[layer], ln_b[layer])


SUB = 16
SEQ_BLOCK = 512


def _tri(n, strict=False):
    r = lax.broadcasted_iota(jnp.int32, (n, n), 0)
    c = lax.broadcasted_iota(jnp.int32, (n, n), 1)
    return (r > c) if strict else (r >= c)


def _cumsum_rows(x):
    n = x.shape[0]
    return jnp.dot(_tri(n).astype(F32), x, preferred_element_type=F32, precision=HI)


def _bdot(a, b):
    return jnp.dot(a.astype(BF16), b.astype(BF16), preferred_element_type=F32)


def _bdot_nt(a, b):
    return lax.dot_general(a.astype(BF16), b.astype(BF16), (((1,), (1,)), ((), ())), preferred_element_type=F32)


def _bdot_tn(a, b):
    return lax.dot_general(a.astype(BF16), b.astype(BF16), (((0,), (0,)), ((), ())), preferred_element_type=F32)


def _rms(x):
    return x * lax.rsqrt(jnp.mean(x * x, axis=-1, keepdims=True) + RMS_EPS)


def _row_of(col_b):
    c = col_b.shape[0]
    return col_b.T[0:1, :c]


A_HEADS = 16
A_DK = 128
A_DV = 128
HGRN2_HEADS_PER_STEP = 8


def _hgrn2_chunk_kernel(q_ref, f_ref, v_ref, g_ref, lb_ref, gn_ref, o_ref, s_out_ref, s_ref, *, n_chunks):
    l = pl.program_id(2)

    @pl.when(l == 0)
    def _():
        s_ref[...] = jnp.zeros_like(s_ref)

    lb = lb_ref[...]
    gn = gn_ref[...]
    heads = range(HGRN2_HEADS_PER_STEP)

    def head(x, i):
        return x[:, i * A_DK:(i + 1) * A_DK]

    def chunk(j, carry):
        rows = pl.ds(pl.multiple_of(j * CHUNK, CHUNK), CHUNK)
        q = _silu(q_ref[rows, :])
        f = lb + (1.0 - lb) * jax.nn.sigmoid(f_ref[rows, :])
        k = 1.0 - f
        v = v_ref[rows, :]
        bc = _cumsum_rows(jnp.log(f))
        qe = q * jnp.exp(bc)
        s_old = [s_ref[i] for i in heads]
        o_inter = [_bdot(head(qe, i), s_old[i]) for i in heads]
        outs = [[] for _ in heads]
        for sc in range(CHUNK // SUB):
            lo, hi = sc * SUB, (sc + 1) * SUB
            bref = bc[lo - 1:lo, :] if sc > 0 else jnp.zeros((1, bc.shape[1]), F32)
            qi = q[lo:hi] * jnp.exp(bc[lo:hi] - bref)
            ki = k[:hi] * jnp.exp(bref - bc[:hi])
            causal = (lax.broadcasted_iota(jnp.int32, (SUB, hi), 0) + lo
                      >= lax.broadcasted_iota(jnp.int32, (SUB, hi), 1))
            atts = [jnp.where(causal, _bdot_nt(head(qi, i), head(ki, i)), 0.0) for i in heads]
            for i in heads:
                outs[i].append(_bdot(atts[i], head(v, i)[:hi]))
        b_end = bc[CHUNK - 1:CHUNK, :]
        kd = k * jnp.exp(b_end - bc)
        d_cols = jnp.exp(bc.T[:, CHUNK - 1:CHUNK])
        for i in heads:
            s_ref[i] = d_cols[i * A_DK:(i + 1) * A_DK] * s_old[i] + _bdot_tn(head(kd, i), head(v, i))
        o_n = jnp.concatenate([_rms(jnp.concatenate(outs[i], axis=0) + o_inter[i]) * gn for i in heads], axis=1)
        o_ref[rows, :] = (o_n * _silu(g_ref[rows, :])).astype(o_ref.dtype)
        return carry

    lax.fori_loop(0, n_chunks, chunk, 0)

    @pl.when(l == pl.num_programs(2) - 1)
    def _():
        s_out_ref[0] = s_ref[...]


def _hgrn2_prompt(proj, b, l, lb, g_norm):
    lblk = _pick_tile(l, SEQ_BLOCK, CHUNK)
    nl = l // lblk
    hg = HGRN2_HEADS_PER_STEP
    groups = A_HEADS // hg
    part = lambda p: pl.BlockSpec((lblk, hg * A_DK), lambda bi, h, li, p=p: (bi * nl + li, p * groups + h))
    return pl.pallas_call(
        functools.partial(_hgrn2_chunk_kernel, n_chunks=lblk // CHUNK),
        grid=(b, groups, nl),
        in_specs=[part(0), part(1), part(2), part(3),
                  pl.BlockSpec((1, hg * A_DK), lambda bi, h, li: (0, h)),
                  pl.BlockSpec((1, A_DV), lambda bi, h, li: (0, 0))],
        out_specs=[pl.BlockSpec((lblk, hg * A_DV), lambda bi, h, li: (bi * nl + li, h)),
                   pl.BlockSpec((1, hg, A_DK, A_DV), lambda bi, h, li: (bi, h, 0, 0))],
        out_shape=[jax.ShapeDtypeStruct((b * l, A_HEADS * A_DV), BF16),
                   jax.ShapeDtypeStruct((b, A_HEADS, A_DK, A_DV), F32)],
        scratch_shapes=[pltpu.VMEM((hg, A_DK, A_DV), F32)],
        compiler_params=_params("parallel", "parallel", "arbitrary"),
        name="hgrn2_chunks",
    )(proj, proj, proj, proj, lb.reshape(1, -1), g_norm.reshape(1, -1))


HGRN2_STEP_BATCH = 4


def _hgrn2_step_kernel(p_ref, s_ref, lb_ref, gn_ref, o_ref, s_out_ref):
    lb = lb_ref[...]
    iota_h = lax.broadcasted_iota(jnp.int32, (A_HEADS, A_DV), 0)
    for bi in range(HGRN2_STEP_BATCH):
        p = p_ref[bi]
        q = _silu(p[0:A_HEADS])
        f = lb + (1.0 - lb) * jax.nn.sigmoid(p[A_HEADS:2 * A_HEADS])
        v = p[2 * A_HEADS:3 * A_HEADS]
        g = p[3 * A_HEADS:4 * A_HEADS]
        q_t, f_t = q.T, f.T
        o = jnp.zeros((A_HEADS, A_DV), F32)
        for h in range(A_HEADS):
            fc = f_t[:, h:h + 1]
            s_new = fc * s_ref[bi, h] + (1.0 - fc) * v[h:h + 1, :]
            s_out_ref[bi, h] = s_new
            oh = jnp.sum(q_t[:, h:h + 1] * s_new, axis=0, keepdims=True)
            o = jnp.where(iota_h == h, oh, o)
        o_ref[bi] = (_rms(o) * gn_ref[...] * _silu(g)).astype(o_ref.dtype)


def _hgrn2_step(proj, s0, lb, g_norm):
    b = proj.shape[0]
    bb = HGRN2_STEP_BATCH
    p3 = proj.reshape(b, 4 * A_HEADS, A_DK)
    o, s = pl.pallas_call(
        _hgrn2_step_kernel,
        grid=(b // bb,),
        in_specs=[pl.BlockSpec((bb, 4 * A_HEADS, A_DK), lambda i: (i, 0, 0)),
                  pl.BlockSpec((bb, A_HEADS, A_DK, A_DV), lambda i: (i, 0, 0, 0)),
                  pl.BlockSpec((A_HEADS, A_DK), lambda i: (0, 0)),
                  pl.BlockSpec((1, A_DV), lambda i: (0, 0))],
        out_specs=[pl.BlockSpec((bb, A_HEADS, A_DV), lambda i: (i, 0, 0)),
                   pl.BlockSpec((bb, A_HEADS, A_DK, A_DV), lambda i: (i, 0, 0, 0))],
        out_shape=[jax.ShapeDtypeStruct((b, A_HEADS, A_DV), BF16),
                   jax.ShapeDtypeStruct(s0.shape, F32)],
        compiler_params=_params("parallel"),
        name="hgrn2_step",
    )(p3, s0, lb.reshape(A_HEADS, A_DK), g_norm.reshape(1, -1))
    return o.reshape(b, A_HEADS * A_DV), s


def _hgrn2_lower_bound(lb_logits, layer):
    return jnp.cumsum(jax.nn.softmax(lb_logits.astype(F32), axis=0), axis=0)[layer]


C_HEADS = 8
C_DK = 128
C_DV = 256
C_QK_DIM = C_HEADS * C_DK
C_V_DIM = C_HEADS * C_DV
C_GATE_CAP = 15.0
MLSTM_HEADS_PER_STEP = 4


def _log_sigmoid(x):
    return jnp.minimum(x, 0.0) - jnp.log(1.0 + jnp.exp(-jnp.abs(x)))


def _cap(x):
    return C_GATE_CAP * jnp.tanh(x / C_GATE_CAP)


def _mlstm_chunk_kernel(q_ref, k_ref, v_ref, op_ref, gt_ref, bg_ref, gn_ref,
                        o_ref, c_out_ref, n_out_ref, m_out_ref, c_ref, n_ref, m_ref, *, n_chunks):
    h = pl.program_id(1)
    l = pl.program_id(2)

    @pl.when(l == 0)
    def _():
        c_ref[...] = jnp.zeros_like(c_ref)
        n_ref[...] = jnp.zeros_like(n_ref)
        m_ref[...] = jnp.zeros_like(m_ref)

    hg = MLSTM_HEADS_PER_STEP
    heads = range(hg)
    lane = lax.broadcasted_iota(jnp.int32, (CHUNK, 2 * C_HEADS), 1)
    lane_o = lax.broadcasted_iota(jnp.int32, (CHUNK, LANES), 1)
    tri = _tri(CHUNK)

    def chunk(j, carry):
        rows = pl.ds(pl.multiple_of(j * CHUNK, CHUNK), CHUNK)
        gates = _cap(gt_ref[rows, :] + bg_ref[...])
        lsig = _log_sigmoid(gates)
        i_all = jnp.zeros((CHUNK, LANES), F32)
        lf_all = jnp.zeros((CHUNK, LANES), F32)
        for e in heads:
            he = h * hg + e
            i_all = jnp.where(lane_o == e, jnp.sum(jnp.where(lane == he, gates, 0.0), axis=-1, keepdims=True), i_all)
            lf_all = jnp.where(lane_o == e,
                               jnp.sum(jnp.where(lane == he + C_HEADS, lsig, 0.0), axis=-1, keepdims=True), lf_all)
        f_b = _cumsum_rows(lf_all)
        a_b = i_all - f_b
        a_t = a_b.T
        f_cs = [f_b[:, e:e + 1] for e in heads]
        log_ws = [jnp.where(tri, f_cs[e] + a_t[e:e + 1, :CHUNK], -jnp.inf) for e in heads]
        m_prevs = [m_ref[e, 0:1, 0:1] for e in heads]
        log_ss = [f_cs[e] + m_prevs[e] for e in heads]
        m_ts = [jnp.maximum(jnp.max(log_ws[e], axis=-1, keepdims=True), log_ss[e]) for e in heads]
        qss = [q_ref[rows, e * C_DK:(e + 1) * C_DK] * (C_DK ** -0.5) for e in heads]
        ks = [k_ref[rows, e * C_DK:(e + 1) * C_DK] for e in heads]
        vs = [v_ref[rows, e * C_DV:(e + 1) * C_DV] for e in heads]
        qks = [_bdot_nt(qss[e], ks[e]) * jnp.exp(log_ws[e] - m_ts[e]) for e in heads]
        w_ss = [jnp.exp(log_ss[e] - m_ts[e]) for e in heads]
        c_old = [c_ref[e] for e in heads]
        n_old = [n_ref[e] for e in heads]
        inter = [_bdot(qss[e], c_old[e]) for e in heads]
        intra = [_bdot(qks[e], vs[e]) for e in heads]
        hids = []
        for e in heads:
            den = (jnp.sum(qks[e], axis=-1, keepdims=True)
                   + w_ss[e] * jnp.sum(qss[e] * n_old[e], axis=-1, keepdims=True))
            num = intra[e] + w_ss[e] * inter[e]
            hids.append(_rms(num / jnp.maximum(jnp.abs(den), jnp.exp(-m_ts[e]))) * gn_ref[e])
        for e in heads:
            m_end = m_ts[e][CHUNK - 1:CHUNK, :]
            f_end = f_cs[e][CHUNK - 1:CHUNK, :]
            w_end = jnp.exp(f_end + a_b[:, e:e + 1] - m_end)
            s_end = jnp.exp(f_end + m_prevs[e] - m_end)
            kw = ks[e] * w_end
            c_ref[e] = s_end * c_old[e] + _bdot_tn(kw, vs[e])
            n_ref[e] = s_end * n_old[e] + jnp.sum(kw, axis=0, keepdims=True)
            m_ref[e] = jnp.broadcast_to(m_end, (1, LANES))
        o_ref[rows, :] = (jnp.concatenate(hids, axis=1) * jax.nn.sigmoid(op_ref[rows, :])).astype(o_ref.dtype)
        return carry

    lax.fori_loop(0, n_chunks, chunk, 0)

    @pl.when(l == pl.num_programs(2) - 1)
    def _():
        c_out_ref[0] = c_ref[...]
        n_out_ref[0] = n_ref[...]
        m_out_ref[0] = m_ref[...]


def _mlstm_prompt(proj, gates, b, l, b_gate, g_norm):
    lblk = _pick_tile(l, SEQ_BLOCK, CHUNK)
    nl = l // lblk
    hg = MLSTM_HEADS_PER_STEP
    groups = C_HEADS // hg
    qw, vw = hg * C_DK, hg * C_DV
    v_off = 2 * C_QK_DIM // vw
    o, c, n, m = pl.pallas_call(
        functools.partial(_mlstm_chunk_kernel, n_chunks=lblk // CHUNK),
        grid=(b, groups, nl),
        in_specs=[pl.BlockSpec((lblk, qw), lambda bi, h, li: (bi * nl + li, h)),
                  pl.BlockSpec((lblk, qw), lambda bi, h, li: (bi * nl + li, groups + h)),
                  pl.BlockSpec((lblk, vw), lambda bi, h, li: (bi * nl + li, v_off + h)),
                  pl.BlockSpec((lblk, vw), lambda bi, h, li: (bi * nl + li, v_off + groups + h)),
                  pl.BlockSpec((lblk, 2 * C_HEADS), lambda bi, h, li: (bi * nl + li, 0)),
                  pl.BlockSpec((1, 2 * C_HEADS), lambda bi, h, li: (0, 0)),
                  pl.BlockSpec((hg, 1, C_DV), lambda bi, h, li: (h, 0, 0))],
        out_specs=[pl.BlockSpec((lblk, vw), lambda bi, h, li: (bi * nl + li, h)),
                   pl.BlockSpec((1, hg, C_DK, C_DV), lambda bi, h, li: (bi, h, 0, 0)),
                   pl.BlockSpec((1, hg, 1, C_DK), lambda bi, h, li: (bi, h, 0, 0)),
                   pl.BlockSpec((1, hg, 1, LANES), lambda bi, h, li: (bi, h, 0, 0))],
        out_shape=[jax.ShapeDtypeStruct((b * l, C_V_DIM), BF16),
                   jax.ShapeDtypeStruct((b, C_HEADS, C_DK, C_DV), F32),
                   jax.ShapeDtypeStruct((b, C_HEADS, 1, C_DK), F32),
                   jax.ShapeDtypeStruct((b, C_HEADS, 1, LANES), F32)],
        scratch_shapes=[pltpu.VMEM((hg, C_DK, C_DV), F32), pltpu.VMEM((hg, 1, C_DK), F32),
                        pltpu.VMEM((hg, 1, LANES), F32)],
        compiler_params=_params("parallel", "parallel", "arbitrary"),
        name="mlstm_chunks",
    )(proj, proj, proj, proj, gates, b_gate.reshape(1, -1), g_norm.reshape(C_HEADS, 1, C_DV))
    return o, c, n[:, :, 0, :], m[:, :, 0, 0]


MLSTM_STEP_BATCH = 8


def _mlstm_step_kernel(qk_ref, vo_ref, gt_ref, bg_ref, gn_ref, c_ref, n_ref, m_ref,
                       o_ref, c_out_ref, n_out_ref, m_out_ref):
    iota_h = lax.broadcasted_iota(jnp.int32, (C_HEADS, C_DV), 0)
    iota_hk = lax.broadcasted_iota(jnp.int32, (C_HEADS, C_DK), 0)
    iota_m = lax.broadcasted_iota(jnp.int32, (MLSTM_STEP_BATCH, C_HEADS), 1)
    iota_mb = lax.broadcasted_iota(jnp.int32, (MLSTM_STEP_BATCH, C_HEADS), 0)
    gates = _cap(gt_ref[...] + bg_ref[...])
    m_all = m_ref[...]
    m_new_all = jnp.zeros((MLSTM_STEP_BATCH, C_HEADS), F32)
    for bi in range(MLSTM_STEP_BATCH):
        qs = qk_ref[bi, 0:C_HEADS, :] * (C_DK ** -0.5)
        k = qk_ref[bi, C_HEADS:2 * C_HEADS, :]
        v = vo_ref[bi, C_HEADS:2 * C_HEADS, :]
        op = vo_ref[bi, 2 * C_HEADS:3 * C_HEADS, :]
        q_t, k_t = qs.T, k.T
        n_all = n_ref[bi]
        hid = jnp.zeros((C_HEADS, C_DV), F32)
        n_new = jnp.zeros((C_HEADS, C_DK), F32)
        for h in range(C_HEADS):
            i_g = gates[bi:bi + 1, h:h + 1]
            lf = _log_sigmoid(gates[bi:bi + 1, C_HEADS + h:C_HEADS + h + 1])
            log_s = lf + m_all[bi:bi + 1, h:h + 1]
            m_t = jnp.maximum(i_g, log_s)
            w_i = jnp.exp(i_g - m_t)
            w_s = jnp.exp(log_s - m_t)
            qk = jnp.sum(qs[h:h + 1, :] * k[h:h + 1, :], axis=-1, keepdims=True) * w_i
            c_old = c_ref[bi, h]
            num = qk * v[h:h + 1, :] + w_s * jnp.sum(q_t[:, h:h + 1] * c_old, axis=0, keepdims=True)
            den = qk + w_s * jnp.sum(qs[h:h + 1, :] * n_all[h:h + 1, :], axis=-1, keepdims=True)
            hid = jnp.where(iota_h == h, num / jnp.maximum(jnp.abs(den), jnp.exp(-m_t)), hid)
            c_out_ref[bi, h] = w_s * c_old + (w_i * k_t[:, h:h + 1]) * v[h:h + 1, :]
            n_new = jnp.where(iota_hk == h, w_s * n_all[h:h + 1, :] + w_i * k[h:h + 1, :], n_new)
            m_new_all = jnp.where(jnp.logical_and(iota_m == h, iota_mb == bi), m_t, m_new_all)
        n_out_ref[bi] = n_new
        o_ref[bi] = (_rms(hid) * gn_ref[...] * jax.nn.sigmoid(op)).astype(o_ref.dtype)
    m_out_ref[...] = m_new_all


def _mlstm_step(proj, gates, b_gate, g_norm, c0, n0, m0):
    b = proj.shape[0]
    bb = MLSTM_STEP_BATCH
    qk_view = proj.reshape(b, -1, C_DK)
    vo_view = proj.reshape(b, -1, C_DV)
    o, c, n, m = pl.pallas_call(
        _mlstm_step_kernel,
        grid=(b // bb,),
        in_specs=[pl.BlockSpec((bb, 2 * C_HEADS, C_DK), lambda i: (i, 0, 0)),
                  pl.BlockSpec((bb, 3 * C_HEADS, C_DV), lambda i: (i, 0, 0)),
                  pl.BlockSpec((bb, 2 * C_HEADS), lambda i: (i, 0)),
                  pl.BlockSpec((1, 2 * C_HEADS), lambda i: (0, 0)),
                  pl.BlockSpec((C_HEADS, C_DV), lambda i: (0, 0)),
                  pl.BlockSpec((bb, C_HEADS, C_DK, C_DV), lambda i: (i, 0, 0, 0)),
                  pl.BlockSpec((bb, C_HEADS, C_DK), lambda i: (i, 0, 0)),
                  pl.BlockSpec((bb, C_HEADS), lambda i: (i, 0))],
        out_specs=[pl.BlockSpec((bb, C_HEADS, C_DV), lambda i: (i, 0, 0)),
                   pl.BlockSpec((bb, C_HEADS, C_DK, C_DV), lambda i: (i, 0, 0, 0)),
                   pl.BlockSpec((bb, C_HEADS, C_DK), lambda i: (i, 0, 0)),
                   pl.BlockSpec((bb, C_HEADS), lambda i: (i, 0))],
        out_shape=[jax.ShapeDtypeStruct((b, C_HEADS, C_DV), BF16),
                   jax.ShapeDtypeStruct(c0.shape, F32),
                   jax.ShapeDtypeStruct(n0.shape, F32),
                   jax.ShapeDtypeStruct(m0.shape, F32)],
        compiler_params=_params("parallel"),
        name="mlstm_step",
    )(qk_view, vo_view, gates, b_gate.reshape(1, -1), g_norm, c0, n0, m0)
    return o.reshape(b, C_V_DIM), c, n, m


B_QK_HEADS = 16
B_V_HEADS = 32
B_DK = 128
B_DV = 128
B_CONV = 4
B_QK_DIM = B_QK_HEADS * B_DK
B_V_DIM = B_V_HEADS * B_DV
B_CONV_DIM = 2 * B_QK_DIM + B_V_DIM
B_REP = B_V_HEADS // B_QK_HEADS
CONV_PAD = 8


def _softplus(x):
    return jnp.maximum(x, 0.0) + jnp.log(1.0 + jnp.exp(-jnp.abs(x)))


def _l2norm(x):
    return x * lax.rsqrt(jnp.sum(x * x, axis=-1, keepdims=True) + RMS_EPS)


def _split2(x):
    hi = x.astype(BF16)
    return hi, (x - hi.astype(F32)).astype(BF16)


def _dot3_parts(ah, al, bh, bl):
    n = bh.shape[1]
    r = jnp.dot(ah, jnp.concatenate([bh, bl], axis=1), preferred_element_type=F32)
    return r[:, :n] + r[:, n:] + jnp.dot(al, bh, preferred_element_type=F32)


def _unit_lower_inverses(mats, order):
    n = mats[0].shape[0]
    eye = (lax.broadcasted_iota(jnp.int32, (n, n), 0) == lax.broadcasted_iota(jnp.int32, (n, n), 1)).astype(F32)
    ps = [-a for a in mats]
    ts = [eye + p for p in ps]
    parts = [_split2(p) for p in ps]
    span = 1
    while 2 * span < order:
        ps = [_dot3_parts(ph, pl_, ph, pl_) for ph, pl_ in parts]
        parts = [_split2(p) for p in ps]
        t_parts = [_split2(t) for t in ts]
        ts = [t + _dot3_parts(th, tl, ph, pl_) for t, (th, tl), (ph, pl_) in zip(ts, t_parts, parts)]
        span *= 2
    return ts


GDN_QK_PER_STEP = 4
GDN_V_PER_STEP = GDN_QK_PER_STEP * B_REP


def _gdn_chunk_kernel(q_ref, k_ref, v_ref, z_ref, ba_ref, wq_ref, wk_ref, wv_ref, alog_ref, dtb_ref, gn_ref,
                      o_ref, s_out_ref, s_ref, qbuf, kbuf, vbuf, gcol, bcol, grow, *, lblk):
    h = pl.program_id(1)
    l = pl.program_id(2)
    n_chunks = lblk // CHUNK

    @pl.when(l == 0)
    def _():
        s_ref[...] = jnp.zeros_like(s_ref)
        qbuf[0:CONV_PAD, :] = jnp.zeros((CONV_PAD, qbuf.shape[1]), F32)
        kbuf[0:CONV_PAD, :] = jnp.zeros((CONV_PAD, kbuf.shape[1]), F32)
        vbuf[0:CONV_PAD, :] = jnp.zeros((CONV_PAD, vbuf.shape[1]), F32)

    def conv(x_ref, w_ref, buf, post):
        buf[CONV_PAD:CONV_PAD + lblk, :] = x_ref[...]
        tail = buf[lblk:lblk + CONV_PAD, :]
        taps = B_CONV - 1
        for c in reversed(range(n_chunks)):
            lo = CONV_PAD + c * CHUNK
            y = w_ref[0:1, :] * buf[lo - taps:lo - taps + CHUNK, :]
            for j in range(1, B_CONV):
                y = y + w_ref[j:j + 1, :] * buf[lo - taps + j:lo - taps + j + CHUNK, :]
            buf[lo:lo + CHUNK, :] = post(_silu(y))
        return tail

    def per_head_l2norm(scale):
        def post(y):
            return jnp.concatenate([_l2norm(y[:, i * B_DK:(i + 1) * B_DK]) * scale
                                    for i in range(GDN_QK_PER_STEP)], axis=1)
        return post

    q_tail = conv(q_ref, wq_ref, qbuf, per_head_l2norm(B_DK ** -0.5))
    k_tail = conv(k_ref, wk_ref, kbuf, per_head_l2norm(1.0))
    v_tail = conv(v_ref, wv_ref, vbuf, lambda y: y)

    lane = lax.broadcasted_iota(jnp.int32, (lblk, 2 * B_V_HEADS), 1)
    lane_h = lax.broadcasted_iota(jnp.int32, (1, B_V_HEADS), 1)
    lane_o = lax.broadcasted_iota(jnp.int32, (lblk, LANES), 1)
    ba = ba_ref[...]
    beta_all = jnp.zeros((lblk, LANES), F32)
    g_all = jnp.zeros((lblk, LANES), F32)
    for e in range(GDN_V_PER_STEP):
        hv = h * GDN_V_PER_STEP + e
        beta = jax.nn.sigmoid(jnp.sum(jnp.where(lane == hv, ba, 0.0), axis=-1, keepdims=True))
        a_pre = jnp.sum(jnp.where(lane == hv + B_V_HEADS, ba, 0.0), axis=-1, keepdims=True)
        a_scale = -jnp.exp(jnp.sum(jnp.where(lane_h == hv, alog_ref[...], 0.0), axis=-1, keepdims=True))
        dt_b = jnp.sum(jnp.where(lane_h == hv, dtb_ref[...], 0.0), axis=-1, keepdims=True)
        g = a_scale * _softplus(a_pre + dt_b)
        beta_all = jnp.where(lane_o == e, beta, beta_all)
        g_all = jnp.where(lane_o == e, g, g_all)
    bcol[...] = beta_all
    for j in range(n_chunks):
        g_b = _cumsum_rows(g_all[j * CHUNK:(j + 1) * CHUNK])
        gcol[j * CHUNK:(j + 1) * CHUNK, :] = g_b
        grow[j] = g_b.T[0:8, 0:CHUNK]

    tri = _tri(CHUNK)
    strict = _tri(CHUNK, strict=True)
    zero_blk = jnp.zeros((CHUNK, CHUNK), F32)

    def chunk(j, carry):
        rows = pl.ds(pl.multiple_of(j * CHUNK, CHUNK), CHUNK)
        brow = pl.ds(pl.multiple_of(j * CHUNK + CONV_PAD, 8), CHUNK)
        g_rows = grow[j]
        heads_q = range(GDN_QK_PER_STEP)
        heads_v = range(GDN_V_PER_STEP)
        qs = [qbuf[brow, i * B_DK:(i + 1) * B_DK] for i in heads_q]
        ks = [kbuf[brow, i * B_DK:(i + 1) * B_DK] for i in heads_q]
        kks = [_bdot_nt(k, k) for k in ks]
        qks = [_bdot_nt(q, k) for q, k in zip(qs, ks)]
        g_cs = [gcol[rows, e:e + 1] for e in heads_v]
        betas = [bcol[rows, e:e + 1] for e in heads_v]
        decs = [jnp.exp(jnp.where(tri, g_cs[e] - g_rows[e:e + 1, :], -jnp.inf)) for e in heads_v]
        a_blks = [jnp.where(strict, betas[e] * decs[e] * kks[e // B_REP], 0.0) for e in heads_v]
        rhs = [jnp.concatenate([betas[e] * vbuf[brow, e * B_DV:(e + 1) * B_DV],
                                (betas[e] * jnp.exp(g_cs[e])) * ks[e // B_REP]], axis=-1) for e in heads_v]
        a_bds = [jnp.concatenate(
            [jnp.concatenate([a_blks[i * B_REP + r] if r == c else zero_blk for c in range(B_REP)], axis=1)
             for r in range(B_REP)], axis=0) for i in heads_q]
        t_invs = _unit_lower_inverses(a_bds, CHUNK)
        t_parts = [_split2(t) for t in t_invs]
        r_parts = [_split2(jnp.concatenate(rhs[i * B_REP:(i + 1) * B_REP], axis=0)) for i in heads_q]
        sols = [_dot3_parts(th, tl, rh, rl) for (th, tl), (rh, rl) in zip(t_parts, r_parts)]
        sol_e = [sols[e // B_REP][(e % B_REP) * CHUNK:(e % B_REP + 1) * CHUNK] for e in heads_v]
        s_old = [s_ref[e] for e in heads_v]
        ws = [sol_e[e][:, :B_DV] - _bdot(sol_e[e][:, B_DV:], s_old[e]) for e in heads_v]
        o_inter = [_bdot(qs[e // B_REP] * jnp.exp(g_cs[e]), s_old[e]) for e in heads_v]
        o_intra = [_bdot(decs[e] * qks[e // B_REP], ws[e]) for e in heads_v]
        for e in heads_v:
            g_end = g_cs[e][CHUNK - 1:CHUNK, :]
            s_ref[e] = jnp.exp(g_end) * s_old[e] + _bdot_tn(ks[e // B_REP] * jnp.exp(g_end - g_cs[e]), ws[e])
        for e in heads_v:
            zc = z_ref[rows, e * B_DV:(e + 1) * B_DV]
            o = o_intra[e] + o_inter[e]
            o_ref[rows, e * B_DV:(e + 1) * B_DV] = (_rms(o) * gn_ref[...] * _silu(zc)).astype(o_ref.dtype)
        return carry

    lax.fori_loop(0, n_chunks, chunk, 0)
    qbuf[0:CONV_PAD, :] = q_tail
    kbuf[0:CONV_PAD, :] = k_tail
    vbuf[0:CONV_PAD, :] = v_tail

    @pl.when(l == pl.num_programs(2) - 1)
    def _():
        s_out_ref[0] = s_ref[...]


def _gdn_prompt(proj, ba, b, l, conv_w, a_log, dt_bias, g_norm):
    lblk = _pick_tile(l, SEQ_BLOCK, CHUNK)
    nl = l // lblk
    qw = GDN_QK_PER_STEP * B_DK
    vw = GDN_V_PER_STEP * B_DV
    k_off = B_QK_DIM // qw
    v_off = 2 * B_QK_DIM // vw
    z_off = B_CONV_DIM // vw
    return pl.pallas_call(
        functools.partial(_gdn_chunk_kernel, lblk=lblk),
        grid=(b, B_QK_HEADS // GDN_QK_PER_STEP, nl),
        in_specs=[pl.BlockSpec((lblk, qw), lambda bi, h, li: (bi * nl + li, h)),
                  pl.BlockSpec((lblk, qw), lambda bi, h, li: (bi * nl + li, k_off + h)),
                  pl.BlockSpec((lblk, vw), lambda bi, h, li: (bi * nl + li, v_off + h)),
                  pl.BlockSpec((lblk, vw), lambda bi, h, li: (bi * nl + li, z_off + h)),
                  pl.BlockSpec((lblk, 2 * B_V_HEADS), lambda bi, h, li: (bi * nl + li, 0)),
                  pl.BlockSpec((B_CONV, qw), lambda bi, h, li: (0, h)),
                  pl.BlockSpec((B_CONV, qw), lambda bi, h, li: (0, k_off + h)),
                  pl.BlockSpec((B_CONV, vw), lambda bi, h, li: (0, v_off + h)),
                  pl.BlockSpec((1, B_V_HEADS), lambda bi, h, li: (0, 0)),
                  pl.BlockSpec((1, B_V_HEADS), lambda bi, h, li: (0, 0)),
                  pl.BlockSpec((1, B_DV), lambda bi, h, li: (0, 0))],
        out_specs=[pl.BlockSpec((lblk, vw), lambda bi, h, li: (bi * nl + li, h)),
                   pl.BlockSpec((1, GDN_V_PER_STEP, B_DK, B_DV), lambda bi, h, li: (bi, h, 0, 0))],
        out_shape=[jax.ShapeDtypeStruct((b * l, B_V_DIM), BF16),
                   jax.ShapeDtypeStruct((b, B_V_HEADS, B_DK, B_DV), F32)],
        scratch_shapes=[pltpu.VMEM((GDN_V_PER_STEP, B_DK, B_DV), F32),
                        pltpu.VMEM((CONV_PAD + lblk, qw), F32),
                        pltpu.VMEM((CONV_PAD + lblk, qw), F32),
                        pltpu.VMEM((CONV_PAD + lblk, vw), F32),
                        pltpu.VMEM((lblk, LANES), F32),
                        pltpu.VMEM((lblk, LANES), F32),
                        pltpu.VMEM((lblk // CHUNK, 8, CHUNK), F32)],
        compiler_params=_params("parallel", "parallel", "arbitrary"),
        name="gdn_chunks",
    )(proj, proj, proj, proj, ba, conv_w, conv_w, conv_w,
      a_log.reshape(1, -1), dt_bias.reshape(1, -1), g_norm.reshape(1, -1))


GDN_STEP_BATCH = 2
B_ROWS = B_CONV_DIM // 128


def _gdn_step_kernel(p_ref, cv_ref, cw_ref, ba_ref, alog_ref, dtb_ref, gn_ref, s_ref, o_ref, s_out_ref):
    ba = ba_ref[0]
    beta_all = jax.nn.sigmoid(ba[:, :B_V_HEADS])
    g_all = -jnp.exp(alog_ref[...]) * _softplus(ba[:, B_V_HEADS:] + dtb_ref[...])
    decay_all = jnp.exp(g_all)
    iota_h = lax.broadcasted_iota(jnp.int32, (B_V_HEADS, B_DV), 0)
    for bi in range(GDN_STEP_BATCH):
        y = cw_ref[B_CONV - 1] * p_ref[bi, 0:B_ROWS, :]
        for j in range(B_CONV - 1):
            y = y + cw_ref[j] * cv_ref[bi, j]
        y = _silu(y)
        q = _l2norm(y[0:B_QK_HEADS]) * (B_DK ** -0.5)
        k = _l2norm(y[B_QK_HEADS:2 * B_QK_HEADS])
        v = y[2 * B_QK_HEADS:]
        z = p_ref[bi, B_ROWS:B_ROWS + B_V_HEADS, :]
        q_t, k_t = q.T, k.T
        qk = jnp.sum(q * k, axis=-1, keepdims=True)
        o = jnp.zeros((B_V_HEADS, B_DV), F32)
        for hv in range(B_V_HEADS):
            hq = hv // B_REP
            beta = beta_all[bi:bi + 1, hv:hv + 1]
            decay = decay_all[bi:bi + 1, hv:hv + 1]
            s = s_ref[bi, hv]
            k_col = k_t[:, hq:hq + 1]
            w = beta * v[hv:hv + 1, :] - (beta * decay) * jnp.sum(k_col * s, axis=0, keepdims=True)
            oh = qk[hq:hq + 1, :] * w + decay * jnp.sum(q_t[:, hq:hq + 1] * s, axis=0, keepdims=True)
            s_out_ref[bi, hv] = decay * s + k_col * w
            o = jnp.where(iota_h == hv, oh, o)
        o_ref[bi] = (_rms(o) * gn_ref[...] * _silu(z)).astype(o_ref.dtype)


def _gdn_step(proj, ba, conv0, conv_w, a_log, dt_bias, g_norm, s0):
    b = proj.shape[0]
    bb = GDN_STEP_BATCH
    p3 = proj.reshape(b, -1, 128)
    o, s = pl.pallas_call(
        _gdn_step_kernel,
        grid=(b // bb,),
        in_specs=[pl.BlockSpec((bb, p3.shape[1], 128), lambda i: (i, 0, 0)),
                  pl.BlockSpec((bb, B_CONV - 1, B_ROWS, 128), lambda i: (i, 0, 0, 0)),
                  pl.BlockSpec((B_CONV, B_ROWS, 128), lambda i: (0, 0, 0)),
                  pl.BlockSpec((1, bb, 2 * B_V_HEADS), lambda i: (i, 0, 0)),
                  pl.BlockSpec((1, B_V_HEADS), lambda i: (0, 0)),
                  pl.BlockSpec((1, B_V_HEADS), lambda i: (0, 0)),
                  pl.BlockSpec((1, B_DV), lambda i: (0, 0)),
                  pl.BlockSpec((bb, B_V_HEADS, B_DK, B_DV), lambda i: (i, 0, 0, 0))],
        out_specs=[pl.BlockSpec((bb, B_V_HEADS, B_DV), lambda i: (i, 0, 0)),
                   pl.BlockSpec((bb, B_V_HEADS, B_DK, B_DV), lambda i: (i, 0, 0, 0))],
        out_shape=[jax.ShapeDtypeStruct((b, B_V_HEADS, B_DV), BF16), jax.ShapeDtypeStruct(s0.shape, F32)],
        compiler_params=_params("parallel"),
        name="gdn_step",
    )(p3, conv0.reshape(b, B_CONV - 1, B_ROWS, 128), conv_w.reshape(B_CONV, B_ROWS, 128),
      ba.reshape(b // bb, bb, 2 * B_V_HEADS), a_log.reshape(1, -1), dt_bias.reshape(1, -1), g_norm.reshape(1, -1), s0)
    return o.reshape(b, B_V_DIM), s


D_GROUP = 16
D_GROUPS = D_MODEL // D_GROUP
D_STATE = 64
S5_LANES = 128
S5_GROUPS = S5_LANES // D_GROUP
S5_HALF = S5_GROUPS * D_STATE
S5_BLOCKS = D_MODEL // S5_LANES
S5_SEGMENTS = 8


def _cmul(ar, ai, br, bi):
    return ar * br - ai * bi, ar * bi + ai * br


def _gelu_tanh(x):
    return 0.5 * x * (1.0 + jnp.tanh(math.sqrt(2.0 / math.pi) * (x + 0.044715 * (x * x * x))))


def _s5_tables(a_re, a_im, b_re, b_im, c_re, c_im, log_dt, seg_len):
    lam_re, lam_im = a_re.astype(F32), a_im.astype(F32)
    dt = jnp.exp(log_dt.astype(F32))[:, None]
    mag = jnp.exp(dt * lam_re)
    ab_re, ab_im = mag * jnp.cos(dt * lam_im), mag * jnp.sin(dt * lam_im)
    inv = 1.0 / (lam_re * lam_re + lam_im * lam_im)
    e_re, e_im = _cmul(ab_re - 1.0, ab_im, lam_re * inv, -lam_im * inv)
    bb_re, bb_im = _cmul(e_re[..., None], e_im[..., None], b_re.astype(F32), b_im.astype(F32))
    j = S5_BLOCKS
    eye = jnp.eye(S5_GROUPS, dtype=F32)

    def pack_rows(t):
        return t.reshape(j, 1, S5_HALF)

    def bdiag_in(t):
        t = t.reshape(j, S5_GROUPS, D_STATE, D_GROUP)
        return jnp.einsum('jgpc,gh->jgchp', t, eye).reshape(j, S5_LANES, S5_HALF)

    def bdiag_out(t):
        t = t.reshape(j, S5_GROUPS, D_GROUP, D_STATE)
        return jnp.einsum('jgcp,gh->jgphc', t, eye).reshape(j, S5_HALF, S5_LANES)

    ab = jnp.concatenate([pack_rows(ab_re), pack_rows(ab_im)], axis=-1)
    w_bu = jnp.concatenate([bdiag_in(bb_re), bdiag_in(bb_im)], axis=-1)
    w_c = jnp.concatenate([bdiag_out(c_re.astype(F32)), -bdiag_out(c_im.astype(F32))], axis=1)
    pr, pi = pack_rows(ab_re), pack_rows(ab_im)
    n = 1
    while n < seg_len:
        tr, ti = pr[:, n - 1:n], pi[:, n - 1:n]
        nr, ni = _cmul(pr, pi, tr, ti)
        pr, pi = jnp.concatenate([pr, nr], axis=1), jnp.concatenate([pi, ni], axis=1)
        n *= 2
    pw = jnp.concatenate([pr[:, :seg_len], pi[:, :seg_len]], axis=-1)
    return ab, w_bu, w_c, pw


def _s5_scan_kernel(x_ref, ab_ref, wbu_ref, wc_ref, pw_ref, dsk_ref, y_ref, s_out_ref, xp, buf, yp, *, seq, seg):
    nc = S5_HALF // 128
    ns = S5_SEGMENTS

    def tile(tau):
        return pl.ds(pl.multiple_of(tau * ns, ns), ns)

    def interleave(tau, c):
        xp[tile(tau), :] = x_ref[pl.ds(tau, ns, stride=seg), :]
        return c

    lax.fori_loop(0, seg, interleave, 0, unroll=8)
    x = xp[...]
    bu = _bdot(x, wbu_ref[0])
    for c in range(2 * nc):
        buf[c] = bu[:, c * 128:(c + 1) * 128]

    a_res = [jnp.broadcast_to(ab_ref[0, :, c * 128:(c + 1) * 128], (ns, 128)) for c in range(nc)]
    a_ims = [jnp.broadcast_to(ab_ref[0, :, S5_HALF + c * 128:S5_HALF + (c + 1) * 128], (ns, 128))
             for c in range(nc)]

    def step(tau, carry):
        rows = tile(tau)
        out = []
        for c in range(nc):
            s_re, s_im = carry[c]
            n_re = a_res[c] * s_re - a_ims[c] * s_im + buf[c, rows, :]
            n_im = a_res[c] * s_im + a_ims[c] * s_re + buf[nc + c, rows, :]
            buf[c, rows, :] = n_re
            buf[nc + c, rows, :] = n_im
            out.append((n_re, n_im))
        return tuple(out)

    zeros = jnp.zeros((ns, 128), F32)
    local_end = lax.fori_loop(0, seg, step, tuple((zeros, zeros) for _ in range(nc)), unroll=2)

    iota_r = lax.broadcasted_iota(jnp.int32, (ns, 128), 0)
    enter = []
    for c in range(nc):
        l_re, l_im = local_end[c]
        pe_re = pw_ref[0, seg - 1:seg, c * 128:(c + 1) * 128]
        pe_im = pw_ref[0, seg - 1:seg, S5_HALF + c * 128:S5_HALF + (c + 1) * 128]
        e_re, e_im = l_re[0:1], l_im[0:1]
        in_re, in_im = zeros, zeros
        for r in range(1, ns):
            in_re = jnp.where(iota_r == r, e_re, in_re)
            in_im = jnp.where(iota_r == r, e_im, in_im)
            t_re, t_im = _cmul(pe_re, pe_im, e_re, e_im)
            e_re, e_im = l_re[r:r + 1] + t_re, l_im[r:r + 1] + t_im
        enter.append((in_re, in_im))

    def fix(group, carry):
        taus = pl.ds(pl.multiple_of(group * 8, 8), 8)
        for c in range(nc):
            p_re8 = pw_ref[0, taus, c * 128:(c + 1) * 128]
            p_im8 = pw_ref[0, taus, S5_HALF + c * 128:S5_HALF + (c + 1) * 128]
            for u in range(8):
                rows = tile(group * 8 + u)
                f_re, f_im = _cmul(p_re8[u:u + 1], p_im8[u:u + 1], enter[c][0], enter[c][1])
                buf[c, rows, :] = buf[c, rows, :] + f_re
                buf[nc + c, rows, :] = buf[nc + c, rows, :] + f_im
        return carry

    lax.fori_loop(0, seg // 8, fix, 0)

    y = dsk_ref[...] * x
    for c in range(2 * nc):
        s_out_ref[0, 0, :, c * 128:(c + 1) * 128] = buf[c, seq - 1:seq, :]
        y = y + _bdot(buf[c], wc_ref[0, c * 128:(c + 1) * 128, :])
    yp[...] = _gelu_tanh(y)
    for r in range(ns):
        y_ref[r * seg:(r + 1) * seg, :] = yp[pl.ds(r, seg, stride=ns), :].astype(y_ref.dtype)


def _unpack_state(s):
    b = s.shape[0]
    return (s[..., :S5_HALF].reshape(b, D_GROUPS, D_STATE), s[..., S5_HALF:].reshape(b, D_GROUPS, D_STATE))


def _s5_prompt(x, b, l, tables, d_skip):
    d = x.shape[1]
    ab, w_bu, w_c, pw = tables
    seg = l // S5_SEGMENTS
    y, s = pl.pallas_call(
        functools.partial(_s5_scan_kernel, seq=l, seg=seg),
        grid=(b, S5_BLOCKS),
        in_specs=[pl.BlockSpec((l, S5_LANES), lambda bi, j: (bi, j)),
                  pl.BlockSpec((1, 1, 2 * S5_HALF), lambda bi, j: (j, 0, 0)),
                  pl.BlockSpec((1, S5_LANES, 2 * S5_HALF), lambda bi, j: (j, 0, 0)),
                  pl.BlockSpec((1, 2 * S5_HALF, S5_LANES), lambda bi, j: (j, 0, 0)),
                  pl.BlockSpec((1, seg, 2 * S5_HALF), lambda bi, j: (j, 0, 0)),
                  pl.BlockSpec((1, S5_LANES), lambda bi, j: (0, j))],
        out_specs=[pl.BlockSpec((l, S5_LANES), lambda bi, j: (bi, j)),
                   pl.BlockSpec((1, 1, 1, 2 * S5_HALF), lambda bi, j: (bi, j, 0, 0))],
        out_shape=[jax.ShapeDtypeStruct((b * l, d), BF16),
                   jax.ShapeDtypeStruct((b, S5_BLOCKS, 1, 2 * S5_HALF), F32)],
        scratch_shapes=[pltpu.VMEM((l, S5_LANES), F32),
                        pltpu.VMEM((2 * S5_HALF // 128, l, 128), F32),
                        pltpu.VMEM((l, S5_LANES), F32)],
        compiler_params=_params("parallel", "parallel"),
        name="s5_scan",
    )(x, ab, w_bu, w_c, pw, d_skip.reshape(1, d))
    return y, _unpack_state(s[:, :, 0, :])


def _s5_step_kernel(x_ref, s_ref, ab_ref, wbu_ref, wc_ref, dsk_ref, y_ref, s_out_ref):
    hf = S5_HALF
    x = x_ref[...]
    bu = _bdot(x, wbu_ref[0])
    s0 = s_ref[0]
    i_re, i_im = _cmul(ab_ref[0, :, :hf], ab_ref[0, :, hf:], s0[:, :hf], s0[:, hf:])
    s_new = jnp.concatenate([i_re, i_im], axis=-1) + bu
    s_out_ref[0] = s_new
    y = _bdot(s_new, wc_ref[0]) + dsk_ref[...] * x
    y_ref[...] = _gelu_tanh(y).astype(y_ref.dtype)


def _s5_step(x, tables, d_skip, s0_re, s0_im):
    b, d = x.shape
    ab, w_bu, w_c, _ = tables
    s0 = jnp.concatenate([s0_re.reshape(b, S5_BLOCKS, S5_HALF), s0_im.reshape(b, S5_BLOCKS, S5_HALF)], axis=-1)
    s0 = jnp.swapaxes(s0, 0, 1)
    y, s = pl.pallas_call(
        _s5_step_kernel,
        grid=(S5_BLOCKS,),
        in_specs=[pl.BlockSpec((b, S5_LANES), lambda j: (0, j)),
                  pl.BlockSpec((1, b, 2 * S5_HALF), lambda j: (j, 0, 0)),
                  pl.BlockSpec((1, 1, 2 * S5_HALF), lambda j: (j, 0, 0)),
                  pl.BlockSpec((1, S5_LANES, 2 * S5_HALF), lambda j: (j, 0, 0)),
                  pl.BlockSpec((1, 2 * S5_HALF, S5_LANES), lambda j: (j, 0, 0)),
                  pl.BlockSpec((1, S5_LANES), lambda j: (0, j))],
        out_specs=[pl.BlockSpec((b, S5_LANES), lambda j: (0, j)),
                   pl.BlockSpec((1, b, 2 * S5_HALF), lambda j: (j, 0, 0))],
        out_shape=[jax.ShapeDtypeStruct((b, d), BF16), jax.ShapeDtypeStruct((S5_BLOCKS, b, 2 * S5_HALF), F32)],
        compiler_params=_params("parallel"),
        name="s5_step",
    )(x, s0, ab, w_bu, w_c, d_skip.reshape(1, d))
    return y, _unpack_state(jnp.swapaxes(s, 0, 1))


def _glu_kernel(x_ref, w1_ref, w2_ref, o_ref):
    x = x_ref[...]
    y1 = jnp.dot(x, w1_ref[...].astype(BF16), preferred_element_type=F32)
    y2 = jnp.dot(x, w2_ref[...].astype(BF16), preferred_element_type=F32)
    o_ref[...] = y1 * jax.nn.sigmoid(y2)


def _glu_matmul(x, w, tm_target=1664, tn_target=256):
    m, k = x.shape
    n = w.shape[1] // 2
    tm = _pick_tile(m, tm_target, 16)
    tn = _pick_tile(n, tn_target, 128)
    nb = n // tn
    return pl.pallas_call(
        _glu_kernel,
        grid=(m // tm, nb),
        in_specs=[pl.BlockSpec((tm, k), lambda i, j: (i, 0)),
                  pl.BlockSpec((k, tn), lambda i, j: (0, j)),
                  pl.BlockSpec((k, tn), lambda i, j: (0, j + nb))],
        out_specs=pl.BlockSpec((tm, tn), lambda i, j: (i, j)),
        out_shape=jax.ShapeDtypeStruct((m, n), F32),
        compiler_params=_params("parallel", "parallel"),
        name="glu_matmul",
    )(x, w, w)


def kernel(x_prompt, x_sample, state_a_S, state_b_S, state_b_conv, state_c_C, state_c_n, state_c_m, state_d_re, state_d_im, a_w_in, a_lb_logits, a_g_norm, a_w_out, b_w_in, b_conv_w, b_a_log, b_dt_bias, b_g_norm, b_w_out, c_w_in, c_b_gate, c_g_norm, c_w_out, d_a_re, d_a_im, d_b_re, d_b_im, d_c_re, d_c_im, d_skip, d_log_dt, d_w_glu, moe_w_router, moe_b_router, moe_w_gu, moe_w_down, moe_w_gu_s, moe_w_down_s, ln_mix_g, ln_mix_b, ln_ffn_g, ln_ffn_b):
    bp, l, d = x_prompt.shape
    bs = x_sample.shape[0]
    tp = bp * l
    x = jnp.concatenate([x_prompt.reshape(tp, d), x_sample.reshape(bs, d)], axis=0)
    xb = x.astype(BF16)

    def finish_layer(x, h, layer):
        x, xb, xpk = _add_ln(x, h, ln_mix_g[layer], ln_mix_b[layer])
        return _moe_ln(x, xb, xpk, moe_w_router, moe_b_router, moe_w_gu, moe_w_down, moe_w_gu_s, moe_w_down_s,
                       ln_ffn_g, ln_ffn_b, layer)

    proj = _matmul(xb, a_w_in)
    lb = _hgrn2_lower_bound(a_lb_logits, 0)
    o_p, a_s_p = _hgrn2_prompt(proj, bp, l, lb, a_g_norm)
    o_s, a_s_s = _hgrn2_step(proj[tp:], state_a_S, lb, a_g_norm)
    h = _matmul(jnp.concatenate([o_p, o_s], axis=0), a_w_out)
    x, xb = finish_layer(x, h, 0)

    n_main = B_CONV_DIM + B_V_DIM
    proj = _matmul(xb, b_w_in, n_cols=n_main)
    ba = _matmul(xb, b_w_in[:, n_main:])
    o_p, b_s_p = _gdn_prompt(proj, ba, bp, l, b_conv_w, b_a_log, b_dt_bias, b_g_norm)
    o_s, b_s_s = _gdn_step(proj[tp:], ba[tp:], state_b_conv, b_conv_w, b_a_log, b_dt_bias, b_g_norm, state_b_S)
    qkv_s = proj[tp:, :B_CONV_DIM]
    b_conv_p = jnp.stack([proj[(i + 1) * l - (B_CONV - 1):(i + 1) * l, :B_CONV_DIM] for i in range(bp)], axis=0)
    b_conv_s = jnp.concatenate([state_b_conv[:, 1:, :], qkv_s[:, None, :]], axis=1)
    h = _matmul(jnp.concatenate([o_p, o_s], axis=0), b_w_out)
    x, xb = finish_layer(x, h, 1)

    n_main = 2 * C_QK_DIM + 2 * C_V_DIM
    proj = _matmul(xb, c_w_in, n_cols=n_main)
    gates = _matmul(xb, c_w_in[:, n_main:])
    o_p, c_c_p, c_n_p, c_m_p = _mlstm_prompt(proj, gates, bp, l, c_b_gate, c_g_norm)
    o_s, c_c_s, c_n_s, c_m_s = _mlstm_step(proj[tp:], gates[tp:], c_b_gate, c_g_norm, state_c_C, state_c_n, state_c_m)
    h = _matmul(jnp.concatenate([o_p, o_s], axis=0), c_w_out)
    x, xb = finish_layer(x, h, 2)

    tables = _s5_tables(d_a_re, d_a_im, d_b_re, d_b_im, d_c_re, d_c_im, d_log_dt, l // S5_SEGMENTS)
    y_p, (d_re_p, d_im_p) = _s5_prompt(x, bp, l, tables, d_skip)
    y_s, (d_re_s, d_im_s) = _s5_step(x[tp:], tables, d_skip, state_d_re, state_d_im)
    h = _glu_matmul(jnp.concatenate([y_p, y_s], axis=0), d_w_glu)
    x, xb = finish_layer(x, h, 3)

    y_prompt = x[:tp].reshape(bp, l, d)
    y_sample = x[tp:].reshape(bs, 1, d)
    return (y_prompt, y_sample, a_s_p, a_s_s, b_s_p, b_s_s, b_conv_p, b_conv_s,
            c_c_p, c_c_s, c_n_p, c_n_s, c_m_p, c_m_s, d_re_p, d_re_s, d_im_p, d_im_s)
```

```python
import functools
import math

import jax
import jax.numpy as jnp
from jax import lax
from jax.experimental import pallas as pl
from jax.experimental.pallas import tpu as pltpu

F32 = jnp.float32
BF16 = jnp.bfloat16
HI = lax.Precision.HIGHEST

D_MODEL = 2048
DEPTH = 4
CHUNK = 64
DN_ALPHA = (2 * DEPTH) ** 0.25
LN_EPS = 1e-5
RMS_EPS = 1e-6

VMEM_LIMIT_BYTES = 56 * 1024 * 1024


def _params(*sem):
    return pltpu.CompilerParams(dimension_semantics=sem, vmem_limit_bytes=VMEM_LIMIT_BYTES)


def _pick_tile(n, target, mult):
    best = None
    for t in range(mult, min(n, target) + 1, mult):
        if n % t == 0:
            best = t
    return best if best is not None else n


def _mm_kernel(x_ref, w_ref, o_ref):
    o_ref[...] = jnp.dot(x_ref[...], w_ref[...].astype(BF16), preferred_element_type=F32).astype(o_ref.dtype)


def _matmul(x, w, n_cols=None, out_dtype=F32, tm_target=1664, tn_target=512):
    m, k = x.shape
    n = w.shape[1] if n_cols is None else n_cols
    tm = _pick_tile(m, tm_target, 16)
    tn = _pick_tile(n, tn_target, 128)
    return pl.pallas_call(
        _mm_kernel,
        grid=(m // tm, n // tn),
        in_specs=[pl.BlockSpec((tm, k), lambda i, j: (i, 0)), pl.BlockSpec((k, tn), lambda i, j: (0, j))],
        out_specs=pl.BlockSpec((tm, tn), lambda i, j: (i, j)),
        out_shape=jax.ShapeDtypeStruct((m, n), out_dtype),
        compiler_params=_params("parallel", "parallel"),
        name="matmul",
    )(x, w)


def _layernorm_rows(z, g, b):
    mu = jnp.mean(z, axis=-1, keepdims=True)
    zc = z - mu
    var = jnp.mean(zc * zc, axis=-1, keepdims=True)
    return zc * lax.rsqrt(var + LN_EPS) * g + b


U32 = jnp.uint32
BF16_HIGH_MASK = 0xFFFF0000


def _pack_bf16_halves(yb):
    half = yb.shape[1] // 2
    bits = lax.bitcast_convert_type(yb.astype(F32), U32)
    return (bits[:, :half] >> 16) | (bits[:, half:] & jnp.uint32(BF16_HIGH_MASK))


def _unpack_bf16_halves(w):
    lo = lax.bitcast_convert_type(w << 16, F32).astype(BF16)
    hi = lax.bitcast_convert_type(w & jnp.uint32(BF16_HIGH_MASK), F32).astype(BF16)
    return lo, hi


def _add_ln_kernel(x_ref, h_ref, g_ref, b_ref, o_ref, ob_ref, op_ref):
    y = _layernorm_rows(DN_ALPHA * x_ref[...] + h_ref[...], g_ref[...], b_ref[...])
    yb = y.astype(BF16)
    o_ref[...] = y
    ob_ref[...] = yb
    op_ref[...] = _pack_bf16_halves(yb)


def _add_ln(x, h, g, b, tm_target=640):
    m, d = x.shape
    tm = _pick_tile(m, tm_target, 16)
    row = pl.BlockSpec((tm, d), lambda i: (i, 0))
    vec = pl.BlockSpec((1, d), lambda i: (0, 0))
    return pl.pallas_call(
        _add_ln_kernel,
        grid=(m // tm,),
        in_specs=[row, row, vec, vec],
        out_specs=[row, row, pl.BlockSpec((tm, d // 2), lambda i: (i, 0))],
        out_shape=[jax.ShapeDtypeStruct((m, d), F32), jax.ShapeDtypeStruct((m, d), BF16),
                   jax.ShapeDtypeStruct((m, d // 2), U32)],
        compiler_params=_params("parallel"),
        name="add_ln",
    )(x, h, g.reshape(1, d), b.reshape(1, d))


def _silu(x):
    return x * jax.nn.sigmoid(x)


def _ffn_kernel(x_ref, wgu_ref, wd_ref, o_ref, *, dh):
    gu = jnp.dot(x_ref[...], wgu_ref[0].astype(BF16), preferred_element_type=F32)
    a = _silu(gu[:, :dh]) * gu[:, dh:]
    o_ref[...] = jnp.dot(a.astype(BF16), wd_ref[0].astype(BF16), preferred_element_type=F32)


def _dense_ffn(xb, w_gu, w_down, layer, tm_target=640):
    m, d = xb.shape
    dh = w_down.shape[1]
    tm = _pick_tile(m, tm_target, 16)
    return pl.pallas_call(
        functools.partial(_ffn_kernel, dh=dh),
        grid=(m // tm,),
        in_specs=[pl.BlockSpec((tm, d), lambda i: (i, 0)),
                  pl.BlockSpec((1, d, 2 * dh), lambda i: (layer, 0, 0)),
                  pl.BlockSpec((1, dh, d), lambda i: (layer, 0, 0))],
        out_specs=pl.BlockSpec((tm, d), lambda i: (i, 0)),
        out_shape=jax.ShapeDtypeStruct((m, d), F32),
        compiler_params=_params("parallel"),
        name="shared_ffn",
    )(xb, w_gu, w_down)


N_EXPERTS = 64
TOP_K = 8
N_ROUTE_GROUPS = 8
TOPK_ROUTE_GROUPS = 4
GROUP_SIZE = N_EXPERTS // N_ROUTE_GROUPS
ROUTED_SCALE = 2.5
EXPERT_TILE = 512


def _router_kernel(x_ref, wt_ref, b_ref, e_ref, g_ref):
    logits = lax.dot_general(wt_ref[...].astype(BF16), x_ref[...], (((1,), (1,)), ((), ())),
                             preferred_element_type=F32)
    scores = jax.nn.sigmoid(logits)
    biased = scores + b_ref[...]
    tm = scores.shape[1]
    neg = jnp.float32(-jnp.inf)
    iota_g = lax.broadcasted_iota(jnp.int32, (GROUP_SIZE, tm), 0)
    gscore = jnp.zeros((N_ROUTE_GROUPS, tm), F32)
    iota_r = lax.broadcasted_iota(jnp.int32, (N_ROUTE_GROUPS, tm), 0)
    for g in range(N_ROUTE_GROUPS):
        v = biased[g * GROUP_SIZE:(g + 1) * GROUP_SIZE, :]
        m1 = jnp.max(v, axis=0, keepdims=True)
        i1 = jnp.min(jnp.where(v == m1, iota_g, GROUP_SIZE), axis=0, keepdims=True)
        m2 = jnp.max(jnp.where(iota_g == i1, neg, v), axis=0, keepdims=True)
        gscore = jnp.where(iota_r == g, m1 + m2, gscore)
    ok = jnp.zeros((N_ROUTE_GROUPS, tm), jnp.int32)
    for _ in range(TOPK_ROUTE_GROUPS):
        m = jnp.max(gscore, axis=0, keepdims=True)
        gi = jnp.min(jnp.where(gscore == m, iota_r, N_ROUTE_GROUPS), axis=0, keepdims=True)
        hit = iota_r == gi
        ok = jnp.where(hit, 1, ok)
        gscore = jnp.where(hit, neg, gscore)
    masked = jnp.concatenate(
        [jnp.where(ok[g:g + 1, :] > 0, biased[g * GROUP_SIZE:(g + 1) * GROUP_SIZE, :], neg)
         for g in range(N_ROUTE_GROUPS)], axis=0)
    iota_e = lax.broadcasted_iota(jnp.int32, (N_EXPERTS, tm), 0)
    iota_k = lax.broadcasted_iota(jnp.int32, (TOP_K, tm), 0)
    top_e = jnp.zeros((TOP_K, tm), jnp.int32)
    gate = jnp.zeros((TOP_K, tm), F32)
    for k in range(TOP_K):
        m = jnp.max(masked, axis=0, keepdims=True)
        ei = jnp.min(jnp.where(masked == m, iota_e, N_EXPERTS), axis=0, keepdims=True)
        hit = iota_e == ei
        gk = jnp.sum(jnp.where(hit, scores, 0.0), axis=0, keepdims=True)
        top_e = jnp.where(iota_k == k, ei, top_e)
        gate = jnp.where(iota_k == k, gk, gate)
        masked = jnp.where(hit, neg, masked)
    e_ref[...] = top_e
    g_ref[...] = ROUTED_SCALE * gate / jnp.sum(gate, axis=0, keepdims=True)


def _router(x, w_router_t, b_router, tm_target=640):
    t, d = x.shape
    tm = _pick_tile(t, tm_target, 128)
    return pl.pallas_call(
        _router_kernel,
        grid=(t // tm,),
        in_specs=[pl.BlockSpec((tm, d), lambda i: (i, 0)),
                  pl.BlockSpec((N_EXPERTS, d), lambda i: (0, 0)),
                  pl.BlockSpec((N_EXPERTS, 1), lambda i: (0, 0))],
        out_specs=[pl.BlockSpec((TOP_K, tm), lambda i: (0, i)), pl.BlockSpec((TOP_K, tm), lambda i: (0, i))],
        out_shape=[jax.ShapeDtypeStruct((TOP_K, t), jnp.int32), jax.ShapeDtypeStruct((TOP_K, t), F32)],
        compiler_params=_params("parallel"),
        name="router",
    )(x, w_router_t, b_router.reshape(N_EXPERTS, 1))


LANES = 128


def _dispatch_kernel(e_ref, pos_ref, cnt_ref, rank_ref):
    n_blocks = e_ref.shape[1] // LANES
    iota_e = lax.broadcasted_iota(jnp.int32, (N_EXPERTS, LANES), 0)
    incl = (lax.broadcasted_iota(jnp.int32, (LANES, LANES), 0)
            <= lax.broadcasted_iota(jnp.int32, (LANES, LANES), 1)).astype(BF16)
    iota_k = lax.broadcasted_iota(jnp.int32, (TOP_K, LANES), 0)

    def pick(e_blk, table):
        out = jnp.zeros((TOP_K, LANES), F32)
        for k in range(TOP_K):
            v = jnp.sum(jnp.where(iota_e == e_blk[k:k + 1, :], table, 0.0), axis=0, keepdims=True)
            out = jnp.where(iota_k == k, v, out)
        return out

    def count_block(b, carry):
        cols = pl.ds(pl.multiple_of(b * LANES, LANES), LANES)
        e_blk = e_ref[:, cols]
        member = jnp.zeros((N_EXPERTS, LANES), F32)
        for k in range(TOP_K):
            member = member + (iota_e == e_blk[k:k + 1, :]).astype(F32)
        seen = jnp.dot(member.astype(BF16), incl, preferred_element_type=F32) + carry
        rank_ref[:, cols] = pick(e_blk, seen - member)
        return jnp.broadcast_to(seen[:, LANES - 1:LANES], (N_EXPERTS, LANES))

    counts = lax.fori_loop(0, n_blocks, count_block, jnp.zeros((N_EXPERTS, LANES), F32))
    cnt_ref[...] = counts.astype(jnp.int32)
    padded = jnp.ceil(counts * (1.0 / EXPERT_TILE)) * EXPERT_TILE
    start = jnp.dot(_tri(N_EXPERTS, strict=True).astype(F32), padded, preferred_element_type=F32, precision=HI)

    def place_block(b, carry):
        cols = pl.ds(pl.multiple_of(b * LANES, LANES), LANES)
        pos_ref[:, cols] = (pick(e_ref[:, cols], start) + rank_ref[:, cols]).astype(jnp.int32)
        return carry

    lax.fori_loop(0, n_blocks, place_block, 0)


def _dispatch_tables(top_e_t, n_tiles):
    k, t = top_e_t.shape
    pos_t, cnt = pl.pallas_call(
        _dispatch_kernel,
        out_shape=[jax.ShapeDtypeStruct((k, t), jnp.int32), jax.ShapeDtypeStruct((N_EXPERTS, LANES), jnp.int32)],
        scratch_shapes=[pltpu.VMEM((k, t), F32)],
        compiler_params=pltpu.CompilerParams(vmem_limit_bytes=VMEM_LIMIT_BYTES),
        name="dispatch",
    )(top_e_t)
    counts = cnt[:, 0]
    pends = jnp.cumsum((counts + EXPERT_TILE - 1) // EXPERT_TILE * EXPERT_TILE)
    n_used = pends[-1] // EXPERT_TILE
    tile_idx = jnp.arange(n_tiles, dtype=jnp.int32)
    tile_e = jnp.minimum(jnp.sum((pends[None, :] <= (tile_idx * EXPERT_TILE)[:, None]).astype(jnp.int32), axis=1),
                         N_EXPERTS - 1)
    tile_e = jnp.where(tile_idx < n_used, tile_e, tile_e[jnp.maximum(n_used - 1, 0)])
    last_tile = jnp.where(counts > 0, pends // EXPERT_TILE - 1, -1).astype(jnp.int32)
    return pos_t.reshape(-1), tile_e.astype(jnp.int32), n_used.astype(jnp.int32).reshape(1), last_tile


SCATTER_TILE = 640


def _scatter_rows_kernel(pos_ref, last_tile_ref, n_used_ref, x_ref, xs_ref, zbuf, zsem, sem, *, n_tok, max_idle):
    tt = x_ref.shape[0]
    n_tiles = xs_ref.shape[0] // EXPERT_TILE
    n_used = n_used_ref[0]

    @pl.when(pl.program_id(0) == 0)
    def _():
        zbuf[...] = jnp.zeros_like(zbuf)

        def fill(tile):
            return pltpu.make_async_copy(zbuf, xs_ref.at[pl.ds(tile * EXPERT_TILE, EXPERT_TILE)], zsem)

        def for_each_padded_tile(action):
            for e in range(N_EXPERTS):
                tile = last_tile_ref[e]

                @pl.when(tile >= 0)
                def _(tile=tile):
                    action(fill(tile))
            for j in range(max_idle):
                @pl.when(n_used + j < n_tiles)
                def _(j=j):
                    action(fill(n_used + j))

        for_each_padded_tile(lambda copy: copy.start())
        for_each_padded_tile(lambda copy: copy.wait())

    base = pl.program_id(0) * tt

    def body(r, c):
        for k in range(TOP_K):
            p = pos_ref[k * n_tok + base + r]
            pltpu.make_async_copy(x_ref.at[pl.ds(r, 1)], xs_ref.at[pl.ds(p, 1)], sem).start()
        return c

    lax.fori_loop(0, tt, body, 0, unroll=2)
    for k in range(TOP_K):
        pltpu.make_async_copy(x_ref, xs_ref.at[pl.ds(0, tt)], sem).wait()


def _sort_rows(x, pos, last_tile, n_used, n_tiles):
    t, d = x.shape
    tt = _pick_tile(t, SCATTER_TILE, 8)
    max_idle = n_tiles - (t * TOP_K) // EXPERT_TILE
    return pl.pallas_call(
        functools.partial(_scatter_rows_kernel, n_tok=t, max_idle=max_idle),
        grid_spec=pltpu.PrefetchScalarGridSpec(
            num_scalar_prefetch=3, grid=(t // tt,),
            in_specs=[pl.BlockSpec((tt, d), lambda i, p, lt, nu: (i, 0))],
            out_specs=pl.BlockSpec(memory_space=pl.ANY),
            scratch_shapes=[pltpu.VMEM((EXPERT_TILE, d), x.dtype), pltpu.SemaphoreType.DMA(()),
                            pltpu.SemaphoreType.DMA(())]),
        out_shape=jax.ShapeDtypeStruct((n_tiles * EXPERT_TILE, d), x.dtype),
        compiler_params=_params("arbitrary"),
        name="scatter_rows",
    )(pos, last_tile, n_used, x)


def _expert_kernel(tile_e_ref, n_used_ref, xs_ref, wgu_ref, wd_ref, o_ref, wgu_b, wd_b, *, dh):
    i = pl.program_id(0)
    n_used = n_used_ref[0]
    e = tile_e_ref[i]
    e_prev = tile_e_ref[jnp.maximum(i - 1, 0)]

    @pl.when(jnp.logical_and(i < n_used, jnp.logical_or(i == 0, e != e_prev)))
    def _():
        wgu_b[...] = wgu_ref[0, 0].astype(BF16)
        wd_b[...] = wd_ref[0, 0].astype(BF16)

    @pl.when(i < n_used)
    def _():
        x_lo, x_hi = _unpack_bf16_halves(xs_ref[...])
        half = x_lo.shape[1]
        gu = (jnp.dot(x_lo, wgu_b[:half, :], preferred_element_type=F32)
              + jnp.dot(x_hi, wgu_b[half:, :], preferred_element_type=F32))
        a = _silu(gu[:, :dh]) * gu[:, dh:]
        o_ref[...] = jnp.dot(a.astype(BF16), wd_b[...], preferred_element_type=F32)

    @pl.when(i >= n_used)
    def _():
        o_ref[...] = jnp.zeros_like(o_ref)


def _routed_experts(xs, tile_e, n_used, w_gu, w_down, layer):
    d = w_gu.shape[2]
    dh = w_down.shape[2]
    n_tiles = tile_e.shape[0]
    grid_spec = pltpu.PrefetchScalarGridSpec(
        num_scalar_prefetch=2,
        grid=(n_tiles,),
        in_specs=[pl.BlockSpec((EXPERT_TILE, d // 2), lambda i, te, nu: (jnp.minimum(i, nu[0] - 1), 0)),
                  pl.BlockSpec((1, 1, d, 2 * dh), lambda i, te, nu: (layer, te[i], 0, 0)),
                  pl.BlockSpec((1, 1, dh, d), lambda i, te, nu: (layer, te[i], 0, 0))],
        out_specs=pl.BlockSpec((EXPERT_TILE, d), lambda i, te, nu: (i, 0)),
        scratch_shapes=[pltpu.VMEM((d, 2 * dh), BF16), pltpu.VMEM((dh, d), BF16)],
    )
    return pl.pallas_call(
        functools.partial(_expert_kernel, dh=dh),
        grid_spec=grid_spec,
        out_shape=jax.ShapeDtypeStruct((n_tiles * EXPERT_TILE, d), F32),
        compiler_params=_params("arbitrary"),
        name="routed_experts",
    )(tile_e, n_used, xs, w_gu, w_down)


COMBINE_TILE = 128


def _combine_kernel(pos_ref, rows_hbm, gate_ref, x_ref, sh_ref, g_ref, b_ref, o_ref, ob_ref, buf, sem):
    i = pl.program_id(0)
    n = pl.num_programs(0)
    slot = i % 2

    n_tok = n * COMBINE_TILE

    def issue(tile, s):
        base = tile * COMBINE_TILE

        def body(r, c):
            for k in range(TOP_K):
                p = pos_ref[k * n_tok + base + r]
                pltpu.make_async_copy(rows_hbm.at[pl.ds(p, 1)], buf.at[s, k, pl.ds(r, 1)], sem.at[s]).start()
            return c

        lax.fori_loop(0, COMBINE_TILE, body, 0, unroll=2)

    @pl.when(i == 0)
    def _():
        issue(0, 0)

    @pl.when(i + 1 < n)
    def _():
        issue(i + 1, 1 - slot)

    for k in range(TOP_K):
        pltpu.make_async_copy(rows_hbm.at[pl.ds(0, COMBINE_TILE)], buf.at[slot, k], sem.at[slot]).wait()
    gate = gate_ref[...]
    z = DN_ALPHA * x_ref[...] + sh_ref[...]
    for k in range(TOP_K):
        z = z + gate[:, k:k + 1] * buf[slot, k]
    y = _layernorm_rows(z, g_ref[...], b_ref[...])
    o_ref[...] = y
    ob_ref[...] = y.astype(BF16)


def _combine_ln(rows, pos, gate, x, shared, g, b):
    t, d = x.shape
    row = lambda w: pl.BlockSpec((COMBINE_TILE, w), lambda i, p: (i, 0))
    vec = pl.BlockSpec((1, d), lambda i, p: (0, 0))
    grid_spec = pltpu.PrefetchScalarGridSpec(
        num_scalar_prefetch=1,
        grid=(t // COMBINE_TILE,),
        in_specs=[pl.BlockSpec(memory_space=pl.ANY), row(TOP_K), row(d), row(d), vec, vec],
        out_specs=[row(d), row(d)],
        scratch_shapes=[pltpu.VMEM((2, TOP_K, COMBINE_TILE, d), F32), pltpu.SemaphoreType.DMA((2,))],
    )
    return pl.pallas_call(
        _combine_kernel,
        grid_spec=grid_spec,
        out_shape=[jax.ShapeDtypeStruct((t, d), F32), jax.ShapeDtypeStruct((t, d), BF16)],
        compiler_params=_params("arbitrary"),
        name="combine_ln",
    )(pos, rows, gate, x, shared, g.reshape(1, d), b.reshape(1, d))


def _moe_ln(x, xb, xpk, w_router, b_router, w_gu, w_down, w_gu_s, w_down_s, ln_g, ln_b, layer):
    t = x.shape[0]
    top_e_t, gate_t = _router(xb, w_router[layer].T, b_router[layer])
    n_tiles = (t * TOP_K + N_EXPERTS * (EXPERT_TILE - 1) + EXPERT_TILE - 1) // EXPERT_TILE
    pos, tile_e, n_used, last_tile = _dispatch_tables(top_e_t, n_tiles)
    xs = _sort_rows(xpk, pos, last_tile, n_used, n_tiles)
    rows = _routed_experts(xs, tile_e, n_used, w_gu, w_down, layer)
    shared = _dense_ffn(xb, w_gu_s, w_down_s, layer)
    return _combine_ln(rows, pos, gate_t.T, x, shared, ln_g[layer], ln_b[layer])


SUB = 16
SEQ_BLOCK = 512


def _tri(n, strict=False):
    r = lax.broadcasted_iota(jnp.int32, (n, n), 0)
    c = lax.broadcasted_iota(jnp.int32, (n, n), 1)
    return (r > c) if strict else (r >= c)


def _cumsum_rows(x):
    n = x.shape[0]
    return jnp.dot(_tri(n).astype(F32), x, preferred_element_type=F32, precision=HI)


def _bdot(a, b):
    return jnp.dot(a.astype(BF16), b.astype(BF16), preferred_element_type=F32)


def _bdot_nt(a, b):
    return lax.dot_general(a.astype(BF16), b.astype(BF16), (((1,), (1,)), ((), ())), preferred_element_type=F32)


def _bdot_tn(a, b):
    return lax.dot_general(a.astype(BF16), b.astype(BF16), (((0,), (0,)), ((), ())), preferred_element_type=F32)


def _rms(x):
    return x * lax.rsqrt(jnp.mean(x * x, axis=-1, keepdims=True) + RMS_EPS)


def _row_of(col_b):
    c = col_b.shape[0]
    return col_b.T[0:1, :c]


A_HEADS = 16
A_DK = 128
A_DV = 128
HGRN2_HEADS_PER_STEP = 8


def _hgrn2_chunk_kernel(q_ref, f_ref, v_ref, g_ref, lb_ref, gn_ref, o_ref, s_out_ref, s_ref, *, n_chunks):
    l = pl.program_id(2)

    @pl.when(l == 0)
    def _():
        s_ref[...] = jnp.zeros_like(s_ref)

    lb = lb_ref[...]
    gn = gn_ref[...]
    heads = range(HGRN2_HEADS_PER_STEP)

    def head(x, i):
        return x[:, i * A_DK:(i + 1) * A_DK]

    def chunk(j, carry):
        rows = pl.ds(pl.multiple_of(j * CHUNK, CHUNK), CHUNK)
        q = _silu(q_ref[rows, :])
        f = lb + (1.0 - lb) * jax.nn.sigmoid(f_ref[rows, :])
        k = 1.0 - f
        v = v_ref[rows, :]
        bc = _cumsum_rows(jnp.log(f))
        qe = q * jnp.exp(bc)
        s_old = [s_ref[i] for i in heads]
        o_inter = [_bdot(head(qe, i), s_old[i]) for i in heads]
        outs = [[] for _ in heads]
        for sc in range(CHUNK // SUB):
            lo, hi = sc * SUB, (sc + 1) * SUB
            bref = bc[lo - 1:lo, :] if sc > 0 else jnp.zeros((1, bc.shape[1]), F32)
            qi = q[lo:hi] * jnp.exp(bc[lo:hi] - bref)
            ki = k[:hi] * jnp.exp(bref - bc[:hi])
            causal = (lax.broadcasted_iota(jnp.int32, (SUB, hi), 0) + lo
                      >= lax.broadcasted_iota(jnp.int32, (SUB, hi), 1))
            atts = [jnp.where(causal, _bdot_nt(head(qi, i), head(ki, i)), 0.0) for i in heads]
            for i in heads:
                outs[i].append(_bdot(atts[i], head(v, i)[:hi]))
        b_end = bc[CHUNK - 1:CHUNK, :]
        kd = k * jnp.exp(b_end - bc)
        d_cols = jnp.exp(bc.T[:, CHUNK - 1:CHUNK])
        for i in heads:
            s_ref[i] = d_cols[i * A_DK:(i + 1) * A_DK] * s_old[i] + _bdot_tn(head(kd, i), head(v, i))
        o_n = jnp.concatenate([_rms(jnp.concatenate(outs[i], axis=0) + o_inter[i]) * gn for i in heads], axis=1)
        o_ref[rows, :] = (o_n * _silu(g_ref[rows, :])).astype(o_ref.dtype)
        return carry

    lax.fori_loop(0, n_chunks, chunk, 0)

    @pl.when(l == pl.num_programs(2) - 1)
    def _():
        s_out_ref[0] = s_ref[...]


def _hgrn2_prompt(proj, b, l, lb, g_norm):
    lblk = _pick_tile(l, SEQ_BLOCK, CHUNK)
    nl = l // lblk
    hg = HGRN2_HEADS_PER_STEP
    groups = A_HEADS // hg
    part = lambda p: pl.BlockSpec((lblk, hg * A_DK), lambda bi, h, li, p=p: (bi * nl + li, p * groups + h))
    return pl.pallas_call(
        functools.partial(_hgrn2_chunk_kernel, n_chunks=lblk // CHUNK),
        grid=(b, groups, nl),
        in_specs=[part(0), part(1), part(2), part(3),
                  pl.BlockSpec((1, hg * A_DK), lambda bi, h, li: (0, h)),
                  pl.BlockSpec((1, A_DV), lambda bi, h, li: (0, 0))],
        out_specs=[pl.BlockSpec((lblk, hg * A_DV), lambda bi, h, li: (bi * nl + li, h)),
                   pl.BlockSpec((1, hg, A_DK, A_DV), lambda bi, h, li: (bi, h, 0, 0))],
        out_shape=[jax.ShapeDtypeStruct((b * l, A_HEADS * A_DV), BF16),
                   jax.ShapeDtypeStruct((b, A_HEADS, A_DK, A_DV), F32)],
        scratch_shapes=[pltpu.VMEM((hg, A_DK, A_DV), F32)],
        compiler_params=_params("parallel", "parallel", "arbitrary"),
        name="hgrn2_chunks",
    )(proj, proj, proj, proj, lb.reshape(1, -1), g_norm.reshape(1, -1))


HGRN2_STEP_BATCH = 4


def _hgrn2_step_kernel(p_ref, s_ref, lb_ref, gn_ref, o_ref, s_out_ref):
    lb = lb_ref[...]
    iota_h = lax.broadcasted_iota(jnp.int32, (A_HEADS, A_DV), 0)
    for bi in range(HGRN2_STEP_BATCH):
        p = p_ref[bi]
        q = _silu(p[0:A_HEADS])
        f = lb + (1.0 - lb) * jax.nn.sigmoid(p[A_HEADS:2 * A_HEADS])
        v = p[2 * A_HEADS:3 * A_HEADS]
        g = p[3 * A_HEADS:4 * A_HEADS]
        q_t, f_t = q.T, f.T
        o = jnp.zeros((A_HEADS, A_DV), F32)
        for h in range(A_HEADS):
            fc = f_t[:, h:h + 1]
            s_new = fc * s_ref[bi, h] + (1.0 - fc) * v[h:h + 1, :]
            s_out_ref[bi, h] = s_new
            oh = jnp.sum(q_t[:, h:h + 1] * s_new, axis=0, keepdims=True)
            o = jnp.where(iota_h == h, oh, o)
        o_ref[bi] = (_rms(o) * gn_ref[...] * _silu(g)).astype(o_ref.dtype)


def _hgrn2_step(proj, s0, lb, g_norm):
    b = proj.shape[0]
    bb = HGRN2_STEP_BATCH
    p3 = proj.reshape(b, 4 * A_HEADS, A_DK)
    o, s = pl.pallas_call(
        _hgrn2_step_kernel,
        grid=(b // bb,),
        in_specs=[pl.BlockSpec((bb, 4 * A_HEADS, A_DK), lambda i: (i, 0, 0)),
                  pl.BlockSpec((bb, A_HEADS, A_DK, A_DV), lambda i: (i, 0, 0, 0)),
                  pl.BlockSpec((A_HEADS, A_DK), lambda i: (0, 0)),
                  pl.BlockSpec((1, A_DV), lambda i: (0, 0))],
        out_specs=[pl.BlockSpec((bb, A_HEADS, A_DV), lambda i: (i, 0, 0)),
                   pl.BlockSpec((bb, A_HEADS, A_DK, A_DV), lambda i: (i, 0, 0, 0))],
        out_shape=[jax.ShapeDtypeStruct((b, A_HEADS, A_DV), BF16),
                   jax.ShapeDtypeStruct(s0.shape, F32)],
        compiler_params=_params("parallel"),
        name="hgrn2_step",
    )(p3, s0, lb.reshape(A_HEADS, A_DK), g_norm.reshape(1, -1))
    return o.reshape(b, A_HEADS * A_DV), s


def _hgrn2_lower_bound(lb_logits, layer):
    return jnp.cumsum(jax.nn.softmax(lb_logits.astype(F32), axis=0), axis=0)[layer]


C_HEADS = 8
C_DK = 128
C_DV = 256
C_QK_DIM = C_HEADS * C_DK
C_V_DIM = C_HEADS * C_DV
C_GATE_CAP = 15.0
MLSTM_HEADS_PER_STEP = 4


def _log_sigmoid(x):
    return jnp.minimum(x, 0.0) - jnp.log(1.0 + jnp.exp(-jnp.abs(x)))


def _cap(x):
    return C_GATE_CAP * jnp.tanh(x / C_GATE_CAP)


def _mlstm_chunk_kernel(q_ref, k_ref, v_ref, op_ref, gt_ref, bg_ref, gn_ref,
                        o_ref, c_out_ref, n_out_ref, m_out_ref, c_ref, n_ref, m_ref, *, n_chunks):
    h = pl.program_id(1)
    l = pl.program_id(2)

    @pl.when(l == 0)
    def _():
        c_ref[...] = jnp.zeros_like(c_ref)
        n_ref[...] = jnp.zeros_like(n_ref)
        m_ref[...] = jnp.zeros_like(m_ref)

    hg = MLSTM_HEADS_PER_STEP
    heads = range(hg)
    lane = lax.broadcasted_iota(jnp.int32, (CHUNK, 2 * C_HEADS), 1)
    lane_o = lax.broadcasted_iota(jnp.int32, (CHUNK, LANES), 1)
    tri = _tri(CHUNK)

    def chunk(j, carry):
        rows = pl.ds(pl.multiple_of(j * CHUNK, CHUNK), CHUNK)
        gates = _cap(gt_ref[rows, :] + bg_ref[...])
        lsig = _log_sigmoid(gates)
        i_all = jnp.zeros((CHUNK, LANES), F32)
        lf_all = jnp.zeros((CHUNK, LANES), F32)
        for e in heads:
            he = h * hg + e
            i_all = jnp.where(lane_o == e, jnp.sum(jnp.where(lane == he, gates, 0.0), axis=-1, keepdims=True), i_all)
            lf_all = jnp.where(lane_o == e,
                               jnp.sum(jnp.where(lane == he + C_HEADS, lsig, 0.0), axis=-1, keepdims=True), lf_all)
        f_b = _cumsum_rows(lf_all)
        a_b = i_all - f_b
        a_t = a_b.T
        f_cs = [f_b[:, e:e + 1] for e in heads]
        log_ws = [jnp.where(tri, f_cs[e] + a_t[e:e + 1, :CHUNK], -jnp.inf) for e in heads]
        m_prevs = [m_ref[e, 0:1, 0:1] for e in heads]
        log_ss = [f_cs[e] + m_prevs[e] for e in heads]
        m_ts = [jnp.maximum(jnp.max(log_ws[e], axis=-1, keepdims=True), log_ss[e]) for e in heads]
        qss = [q_ref[rows, e * C_DK:(e + 1) * C_DK] * (C_DK ** -0.5) for e in heads]
        ks = [k_ref[rows, e * C_DK:(e + 1) * C_DK] for e in heads]
        vs = [v_ref[rows, e * C_DV:(e + 1) * C_DV] for e in heads]
        qks = [_bdot_nt(qss[e], ks[e]) * jnp.exp(log_ws[e] - m_ts[e]) for e in heads]
        w_ss = [jnp.exp(log_ss[e] - m_ts[e]) for e in heads]
        c_old = [c_ref[e] for e in heads]
        n_old = [n_ref[e] for e in heads]
        inter = [_bdot(qss[e], c_old[e]) for e in heads]
        intra = [_bdot(qks[e], vs[e]) for e in heads]
        hids = []
        for e in heads:
            den = (jnp.sum(qks[e], axis=-1, keepdims=True)
                   + w_ss[e] * jnp.sum(qss[e] * n_old[e], axis=-1, keepdims=True))
            num = intra[e] + w_ss[e] * inter[e]
            hids.append(_rms(num / jnp.maximum(jnp.abs(den), jnp.exp(-m_ts[e]))) * gn_ref[e])
        for e in heads:
            m_end = m_ts[e][CHUNK - 1:CHUNK, :]
            f_end = f_cs[e][CHUNK - 1:CHUNK, :]
            w_end = jnp.exp(f_end + a_b[:, e:e + 1] - m_end)
            s_end = jnp.exp(f_end + m_prevs[e] - m_end)
            kw = ks[e] * w_end
            c_ref[e] = s_end * c_old[e] + _bdot_tn(kw, vs[e])
            n_ref[e] = s_end * n_old[e] + jnp.sum(kw, axis=0, keepdims=True)
            m_ref[e] = jnp.broadcast_to(m_end, (1, LANES))
        o_ref[rows, :] = (jnp.concatenate(hids, axis=1) * jax.nn.sigmoid(op_ref[rows, :])).astype(o_ref.dtype)
        return carry

    lax.fori_loop(0, n_chunks, chunk, 0)

    @pl.when(l == pl.num_programs(2) - 1)
    def _():
        c_out_ref[0] = c_ref[...]
        n_out_ref[0] = n_ref[...]
        m_out_ref[0] = m_ref[...]


def _mlstm_prompt(proj, gates, b, l, b_gate, g_norm):
    lblk = _pick_tile(l, SEQ_BLOCK, CHUNK)
    nl = l // lblk
    hg = MLSTM_HEADS_PER_STEP
    groups = C_HEADS // hg
    qw, vw = hg * C_DK, hg * C_DV
    v_off = 2 * C_QK_DIM // vw
    o, c, n, m = pl.pallas_call(
        functools.partial(_mlstm_chunk_kernel, n_chunks=lblk // CHUNK),
        grid=(b, groups, nl),
        in_specs=[pl.BlockSpec((lblk, qw), lambda bi, h, li: (bi * nl + li, h)),
                  pl.BlockSpec((lblk, qw), lambda bi, h, li: (bi * nl + li, groups + h)),
                  pl.BlockSpec((lblk, vw), lambda bi, h, li: (bi * nl + li, v_off + h)),
                  pl.BlockSpec((lblk, vw), lambda bi, h, li: (bi * nl + li, v_off + groups + h)),
                  pl.BlockSpec((lblk, 2 * C_HEADS), lambda bi, h, li: (bi * nl + li, 0)),
                  pl.BlockSpec((1, 2 * C_HEADS), lambda bi, h, li: (0, 0)),
                  pl.BlockSpec((hg, 1, C_DV), lambda bi, h, li: (h, 0, 0))],
        out_specs=[pl.BlockSpec((lblk, vw), lambda bi, h, li: (bi * nl + li, h)),
                   pl.BlockSpec((1, hg, C_DK, C_DV), lambda bi, h, li: (bi, h, 0, 0)),
                   pl.BlockSpec((1, hg, 1, C_DK), lambda bi, h, li: (bi, h, 0, 0)),
                   pl.BlockSpec((1, hg, 1, LANES), lambda bi, h, li: (bi, h, 0, 0))],
        out_shape=[jax.ShapeDtypeStruct((b * l, C_V_DIM), BF16),
                   jax.ShapeDtypeStruct((b, C_HEADS, C_DK, C_DV), F32),
                   jax.ShapeDtypeStruct((b, C_HEADS, 1, C_DK), F32),
                   jax.ShapeDtypeStruct((b, C_HEADS, 1, LANES), F32)],
        scratch_shapes=[pltpu.VMEM((hg, C_DK, C_DV), F32), pltpu.VMEM((hg, 1, C_DK), F32),
                        pltpu.VMEM((hg, 1, LANES), F32)],
        compiler_params=_params("parallel", "parallel", "arbitrary"),
        name="mlstm_chunks",
    )(proj, proj, proj, proj, gates, b_gate.reshape(1, -1), g_norm.reshape(C_HEADS, 1, C_DV))
    return o, c, n[:, :, 0, :], m[:, :, 0, 0]


MLSTM_STEP_BATCH = 8


def _mlstm_step_kernel(qk_ref, vo_ref, gt_ref, bg_ref, gn_ref, c_ref, n_ref, m_ref,
                       o_ref, c_out_ref, n_out_ref, m_out_ref):
    iota_h = lax.broadcasted_iota(jnp.int32, (C_HEADS, C_DV), 0)
    iota_hk = lax.broadcasted_iota(jnp.int32, (C_HEADS, C_DK), 0)
    iota_m = lax.broadcasted_iota(jnp.int32, (MLSTM_STEP_BATCH, C_HEADS), 1)
    iota_mb = lax.broadcasted_iota(jnp.int32, (MLSTM_STEP_BATCH, C_HEADS), 0)
    gates = _cap(gt_ref[...] + bg_ref[...])
    m_all = m_ref[...]
    m_new_all = jnp.zeros((MLSTM_STEP_BATCH, C_HEADS), F32)
    for bi in range(MLSTM_STEP_BATCH):
        qs = qk_ref[bi, 0:C_HEADS, :] * (C_DK ** -0.5)
        k = qk_ref[bi, C_HEADS:2 * C_HEADS, :]
        v = vo_ref[bi, C_HEADS:2 * C_HEADS, :]
        op = vo_ref[bi, 2 * C_HEADS:3 * C_HEADS, :]
        q_t, k_t = qs.T, k.T
        n_all = n_ref[bi]
        hid = jnp.zeros((C_HEADS, C_DV), F32)
        n_new = jnp.zeros((C_HEADS, C_DK), F32)
        for h in range(C_HEADS):
            i_g = gates[bi:bi + 1, h:h + 1]
            lf = _log_sigmoid(gates[bi:bi + 1, C_HEADS + h:C_HEADS + h + 1])
            log_s = lf + m_all[bi:bi + 1, h:h + 1]
            m_t = jnp.maximum(i_g, log_s)
            w_i = jnp.exp(i_g - m_t)
            w_s = jnp.exp(log_s - m_t)
            qk = jnp.sum(qs[h:h + 1, :] * k[h:h + 1, :], axis=-1, keepdims=True) * w_i
            c_old = c_ref[bi, h]
            num = qk * v[h:h + 1, :] + w_s * jnp.sum(q_t[:, h:h + 1] * c_old, axis=0, keepdims=True)
            den = qk + w_s * jnp.sum(qs[h:h + 1, :] * n_all[h:h + 1, :], axis=-1, keepdims=True)
            hid = jnp.where(iota_h == h, num / jnp.maximum(jnp.abs(den), jnp.exp(-m_t)), hid)
            c_out_ref[bi, h] = w_s * c_old + (w_i * k_t[:, h:h + 1]) * v[h:h + 1, :]
            n_new = jnp.where(iota_hk == h, w_s * n_all[h:h + 1, :] + w_i * k[h:h + 1, :], n_new)
            m_new_all = jnp.where(jnp.logical_and(iota_m == h, iota_mb == bi), m_t, m_new_all)
        n_out_ref[bi] = n_new
        o_ref[bi] = (_rms(hid) * gn_ref[...] * jax.nn.sigmoid(op)).astype(o_ref.dtype)
    m_out_ref[...] = m_new_all


def _mlstm_step(proj, gates, b_gate, g_norm, c0, n0, m0):
    b = proj.shape[0]
    bb = MLSTM_STEP_BATCH
    qk_view = proj.reshape(b, -1, C_DK)
    vo_view = proj.reshape(b, -1, C_DV)
    o, c, n, m = pl.pallas_call(
        _mlstm_step_kernel,
        grid=(b // bb,),
        in_specs=[pl.BlockSpec((bb, 2 * C_HEADS, C_DK), lambda i: (i, 0, 0)),
                  pl.BlockSpec((bb, 3 * C_HEADS, C_DV), lambda i: (i, 0, 0)),
                  pl.BlockSpec((bb, 2 * C_HEADS), lambda i: (i, 0)),
                  pl.BlockSpec((1, 2 * C_HEADS), lambda i: (0, 0)),
                  pl.BlockSpec((C_HEADS, C_DV), lambda i: (0, 0)),
                  pl.BlockSpec((bb, C_HEADS, C_DK, C_DV), lambda i: (i, 0, 0, 0)),
                  pl.BlockSpec((bb, C_HEADS, C_DK), lambda i: (i, 0, 0)),
                  pl.BlockSpec((bb, C_HEADS), lambda i: (i, 0))],
        out_specs=[pl.BlockSpec((bb, C_HEADS, C_DV), lambda i: (i, 0, 0)),
                   pl.BlockSpec((bb, C_HEADS, C_DK, C_DV), lambda i: (i, 0, 0, 0)),
                   pl.BlockSpec((bb, C_HEADS, C_DK), lambda i: (i, 0, 0)),
                   pl.BlockSpec((bb, C_HEADS), lambda i: (i, 0))],
        out_shape=[jax.ShapeDtypeStruct((b, C_HEADS, C_DV), BF16),
                   jax.ShapeDtypeStruct(c0.shape, F32),
                   jax.ShapeDtypeStruct(n0.shape, F32),
                   jax.ShapeDtypeStruct(m0.shape, F32)],
        compiler_params=_params("parallel"),
        name="mlstm_step",
    )(qk_view, vo_view, gates, b_gate.reshape(1, -1), g_norm, c0, n0, m0)
    return o.reshape(b, C_V_DIM), c, n, m


B_QK_HEADS = 16
B_V_HEADS = 32
B_DK = 128
B_DV = 128
B_CONV = 4
B_QK_DIM = B_QK_HEADS * B_DK
B_V_DIM = B_V_HEADS * B_DV
B_CONV_DIM = 2 * B_QK_DIM + B_V_DIM
B_REP = B_V_HEADS // B_QK_HEADS
CONV_PAD = 8


def _softplus(x):
    return jnp.maximum(x, 0.0) + jnp.log(1.0 + jnp.exp(-jnp.abs(x)))


def _l2norm(x):
    return x * lax.rsqrt(jnp.sum(x * x, axis=-1, keepdims=True) + RMS_EPS)


def _split2(x):
    hi = x.astype(BF16)
    return hi, (x - hi.astype(F32)).astype(BF16)


def _dot3_parts(ah, al, bh, bl):
    n = bh.shape[1]
    r = jnp.dot(ah, jnp.concatenate([bh, bl], axis=1), preferred_element_type=F32)
    return r[:, :n] + r[:, n:] + jnp.dot(al, bh, preferred_element_type=F32)


def _unit_lower_inverses(mats, order):
    n = mats[0].shape[0]
    eye = (lax.broadcasted_iota(jnp.int32, (n, n), 0) == lax.broadcasted_iota(jnp.int32, (n, n), 1)).astype(F32)
    ps = [-a for a in mats]
    ts = [eye + p for p in ps]
    parts = [_split2(p) for p in ps]
    span = 1
    while 2 * span < order:
        ps = [_dot3_parts(ph, pl_, ph, pl_) for ph, pl_ in parts]
        parts = [_split2(p) for p in ps]
        t_parts = [_split2(t) for t in ts]
        ts = [t + _dot3_parts(th, tl, ph, pl_) for t, (th, tl), (ph, pl_) in zip(ts, t_parts, parts)]
        span *= 2
    return ts


GDN_QK_PER_STEP = 4
GDN_V_PER_STEP = GDN_QK_PER_STEP * B_REP


def _gdn_chunk_kernel(q_ref, k_ref, v_ref, z_ref, ba_ref, wq_ref, wk_ref, wv_ref, alog_ref, dtb_ref, gn_ref,
                      o_ref, s_out_ref, s_ref, qbuf, kbuf, vbuf, gcol, bcol, grow, *, lblk):
    h = pl.program_id(1)
    l = pl.program_id(2)
    n_chunks = lblk // CHUNK

    @pl.when(l == 0)
    def _():
        s_ref[...] = jnp.zeros_like(s_ref)
        qbuf[0:CONV_PAD, :] = jnp.zeros((CONV_PAD, qbuf.shape[1]), F32)
        kbuf[0:CONV_PAD, :] = jnp.zeros((CONV_PAD, kbuf.shape[1]), F32)
        vbuf[0:CONV_PAD, :] = jnp.zeros((CONV_PAD, vbuf.shape[1]), F32)

    def conv(x_ref, w_ref, buf, post):
        buf[CONV_PAD:CONV_PAD + lblk, :] = x_ref[...]
        tail = buf[lblk:lblk + CONV_PAD, :]
        taps = B_CONV - 1
        for c in reversed(range(n_chunks)):
            lo = CONV_PAD + c * CHUNK
            y = w_ref[0:1, :] * buf[lo - taps:lo - taps + CHUNK, :]
            for j in range(1, B_CONV):
                y = y + w_ref[j:j + 1, :] * buf[lo - taps + j:lo - taps + j + CHUNK, :]
            buf[lo:lo + CHUNK, :] = post(_silu(y))
        return tail

    def per_head_l2norm(scale):
        def post(y):
            return jnp.concatenate([_l2norm(y[:, i * B_DK:(i + 1) * B_DK]) * scale
                                    for i in range(GDN_QK_PER_STEP)], axis=1)
        return post

    q_tail = conv(q_ref, wq_ref, qbuf, per_head_l2norm(B_DK ** -0.5))
    k_tail = conv(k_ref, wk_ref, kbuf, per_head_l2norm(1.0))
    v_tail = conv(v_ref, wv_ref, vbuf, lambda y: y)

    lane = lax.broadcasted_iota(jnp.int32, (lblk, 2 * B_V_HEADS), 1)
    lane_h = lax.broadcasted_iota(jnp.int32, (1, B_V_HEADS), 1)
    lane_o = lax.broadcasted_iota(jnp.int32, (lblk, LANES), 1)
    ba = ba_ref[...]
    beta_all = jnp.zeros((lblk, LANES), F32)
    g_all = jnp.zeros((lblk, LANES), F32)
    for e in range(GDN_V_PER_STEP):
        hv = h * GDN_V_PER_STEP + e
        beta = jax.nn.sigmoid(jnp.sum(jnp.where(lane == hv, ba, 0.0), axis=-1, keepdims=True))
        a_pre = jnp.sum(jnp.where(lane == hv + B_V_HEADS, ba, 0.0), axis=-1, keepdims=True)
        a_scale = -jnp.exp(jnp.sum(jnp.where(lane_h == hv, alog_ref[...], 0.0), axis=-1, keepdims=True))
        dt_b = jnp.sum(jnp.where(lane_h == hv, dtb_ref[...], 0.0), axis=-1, keepdims=True)
        g = a_scale * _softplus(a_pre + dt_b)
        beta_all = jnp.where(lane_o == e, beta, beta_all)
        g_all = jnp.where(lane_o == e, g, g_all)
    bcol[...] = beta_all
    for j in range(n_chunks):
        g_b = _cumsum_rows(g_all[j * CHUNK:(j + 1) * CHUNK])
        gcol[j * CHUNK:(j + 1) * CHUNK, :] = g_b
        grow[j] = g_b.T[0:8, 0:CHUNK]

    tri = _tri(CHUNK)
    strict = _tri(CHUNK, strict=True)
    zero_blk = jnp.zeros((CHUNK, CHUNK), F32)

    def chunk(j, carry):
        rows = pl.ds(pl.multiple_of(j * CHUNK, CHUNK), CHUNK)
        brow = pl.ds(pl.multiple_of(j * CHUNK + CONV_PAD, 8), CHUNK)
        g_rows = grow[j]
        heads_q = range(GDN_QK_PER_STEP)
        heads_v = range(GDN_V_PER_STEP)
        qs = [qbuf[brow, i * B_DK:(i + 1) * B_DK] for i in heads_q]
        ks = [kbuf[brow, i * B_DK:(i + 1) * B_DK] for i in heads_q]
        kks = [_bdot_nt(k, k) for k in ks]
        qks = [_bdot_nt(q, k) for q, k in zip(qs, ks)]
        g_cs = [gcol[rows, e:e + 1] for e in heads_v]
        betas = [bcol[rows, e:e + 1] for e in heads_v]
        decs = [jnp.exp(jnp.where(tri, g_cs[e] - g_rows[e:e + 1, :], -jnp.inf)) for e in heads_v]
        a_blks = [jnp.where(strict, betas[e] * decs[e] * kks[e // B_REP], 0.0) for e in heads_v]
        rhs = [jnp.concatenate([betas[e] * vbuf[brow, e * B_DV:(e + 1) * B_DV],
                                (betas[e] * jnp.exp(g_cs[e])) * ks[e // B_REP]], axis=-1) for e in heads_v]
        a_bds = [jnp.concatenate(
            [jnp.concatenate([a_blks[i * B_REP + r] if r == c else zero_blk for c in range(B_REP)], axis=1)
             for r in range(B_REP)], axis=0) for i in heads_q]
        t_invs = _unit_lower_inverses(a_bds, CHUNK)
        t_parts = [_split2(t) for t in t_invs]
        r_parts = [_split2(jnp.concatenate(rhs[i * B_REP:(i + 1) * B_REP], axis=0)) for i in heads_q]
        sols = [_dot3_parts(th, tl, rh, rl) for (th, tl), (rh, rl) in zip(t_parts, r_parts)]
        sol_e = [sols[e // B_REP][(e % B_REP) * CHUNK:(e % B_REP + 1) * CHUNK] for e in heads_v]
        s_old = [s_ref[e] for e in heads_v]
        ws = [sol_e[e][:, :B_DV] - _bdot(sol_e[e][:, B_DV:], s_old[e]) for e in heads_v]
        o_inter = [_bdot(qs[e // B_REP] * jnp.exp(g_cs[e]), s_old[e]) for e in heads_v]
        o_intra = [_bdot(decs[e] * qks[e // B_REP], ws[e]) for e in heads_v]
        for e in heads_v:
            g_end = g_cs[e][CHUNK - 1:CHUNK, :]
            s_ref[e] = jnp.exp(g_end) * s_old[e] + _bdot_tn(ks[e // B_REP] * jnp.exp(g_end - g_cs[e]), ws[e])
        for e in heads_v:
            zc = z_ref[rows, e * B_DV:(e + 1) * B_DV]
            o = o_intra[e] + o_inter[e]
            o_ref[rows, e * B_DV:(e + 1) * B_DV] = (_rms(o) * gn_ref[...] * _silu(zc)).astype(o_ref.dtype)
        return carry

    lax.fori_loop(0, n_chunks, chunk, 0)
    qbuf[0:CONV_PAD, :] = q_tail
    kbuf[0:CONV_PAD, :] = k_tail
    vbuf[0:CONV_PAD, :] = v_tail

    @pl.when(l == pl.num_programs(2) - 1)
    def _():
        s_out_ref[0] = s_ref[...]


def _gdn_prompt(proj, ba, b, l, conv_w, a_log, dt_bias, g_norm):
    lblk = _pick_tile(l, SEQ_BLOCK, CHUNK)
    nl = l // lblk
    qw = GDN_QK_PER_STEP * B_DK
    vw = GDN_V_PER_STEP * B_DV
    k_off = B_QK_DIM // qw
    v_off = 2 * B_QK_DIM // vw
    z_off = B_CONV_DIM // vw
    return pl.pallas_call(
        functools.partial(_gdn_chunk_kernel, lblk=lblk),
        grid=(b, B_QK_HEADS // GDN_QK_PER_STEP, nl),
        in_specs=[pl.BlockSpec((lblk, qw), lambda bi, h, li: (bi * nl + li, h)),
                  pl.BlockSpec((lblk, qw), lambda bi, h, li: (bi * nl + li, k_off + h)),
                  pl.BlockSpec((lblk, vw), lambda bi, h, li: (bi * nl + li, v_off + h)),
                  pl.BlockSpec((lblk, vw), lambda bi, h, li: (bi * nl + li, z_off + h)),
                  pl.BlockSpec((lblk, 2 * B_V_HEADS), lambda bi, h, li: (bi * nl + li, 0)),
                  pl.BlockSpec((B_CONV, qw), lambda bi, h, li: (0, h)),
                  pl.BlockSpec((B_CONV, qw), lambda bi, h, li: (0, k_off + h)),
                  pl.BlockSpec((B_CONV, vw), lambda bi, h, li: (0, v_off + h)),
                  pl.BlockSpec((1, B_V_HEADS), lambda bi, h, li: (0, 0)),
                  pl.BlockSpec((1, B_V_HEADS), lambda bi, h, li: (0, 0)),
                  pl.BlockSpec((1, B_DV), lambda bi, h, li: (0, 0))],
        out_specs=[pl.BlockSpec((lblk, vw), lambda bi, h, li: (bi * nl + li, h)),
                   pl.BlockSpec((1, GDN_V_PER_STEP, B_DK, B_DV), lambda bi, h, li: (bi, h, 0, 0))],
        out_shape=[jax.ShapeDtypeStruct((b * l, B_V_DIM), BF16),
                   jax.ShapeDtypeStruct((b, B_V_HEADS, B_DK, B_DV), F32)],
        scratch_shapes=[pltpu.VMEM((GDN_V_PER_STEP, B_DK, B_DV), F32),
                        pltpu.VMEM((CONV_PAD + lblk, qw), F32),
                        pltpu.VMEM((CONV_PAD + lblk, qw), F32),
                        pltpu.VMEM((CONV_PAD + lblk, vw), F32),
                        pltpu.VMEM((lblk, LANES), F32),
                        pltpu.VMEM((lblk, LANES), F32),
                        pltpu.VMEM((lblk // CHUNK, 8, CHUNK), F32)],
        compiler_params=_params("parallel", "parallel", "arbitrary"),
        name="gdn_chunks",
    )(proj, proj, proj, proj, ba, conv_w, conv_w, conv_w,
      a_log.reshape(1, -1), dt_bias.reshape(1, -1), g_norm.reshape(1, -1))


GDN_STEP_BATCH = 2
B_ROWS = B_CONV_DIM // 128


def _gdn_step_kernel(p_ref, cv_ref, cw_ref, ba_ref, alog_ref, dtb_ref, gn_ref, s_ref, o_ref, s_out_ref):
    ba = ba_ref[0]
    beta_all = jax.nn.sigmoid(ba[:, :B_V_HEADS])
    g_all = -jnp.exp(alog_ref[...]) * _softplus(ba[:, B_V_HEADS:] + dtb_ref[...])
    decay_all = jnp.exp(g_all)
    iota_h = lax.broadcasted_iota(jnp.int32, (B_V_HEADS, B_DV), 0)
    for bi in range(GDN_STEP_BATCH):
        y = cw_ref[B_CONV - 1] * p_ref[bi, 0:B_ROWS, :]
        for j in range(B_CONV - 1):
            y = y + cw_ref[j] * cv_ref[bi, j]
        y = _silu(y)
        q = _l2norm(y[0:B_QK_HEADS]) * (B_DK ** -0.5)
        k = _l2norm(y[B_QK_HEADS:2 * B_QK_HEADS])
        v = y[2 * B_QK_HEADS:]
        z = p_ref[bi, B_ROWS:B_ROWS + B_V_HEADS, :]
        q_t, k_t = q.T, k.T
        qk = jnp.sum(q * k, axis=-1, keepdims=True)
        o = jnp.zeros((B_V_HEADS, B_DV), F32)
        for hv in range(B_V_HEADS):
            hq = hv // B_REP
            beta = beta_all[bi:bi + 1, hv:hv + 1]
            decay = decay_all[bi:bi + 1, hv:hv + 1]
            s = s_ref[bi, hv]
            k_col = k_t[:, hq:hq + 1]
            w = beta * v[hv:hv + 1, :] - (beta * decay) * jnp.sum(k_col * s, axis=0, keepdims=True)
            oh = qk[hq:hq + 1, :] * w + decay * jnp.sum(q_t[:, hq:hq + 1] * s, axis=0, keepdims=True)
            s_out_ref[bi, hv] = decay * s + k_col * w
            o = jnp.where(iota_h == hv, oh, o)
        o_ref[bi] = (_rms(o) * gn_ref[...] * _silu(z)).astype(o_ref.dtype)


def _gdn_step(proj, ba, conv0, conv_w, a_log, dt_bias, g_norm, s0):
    b = proj.shape[0]
    bb = GDN_STEP_BATCH
    p3 = proj.reshape(b, -1, 128)
    o, s = pl.pallas_call(
        _gdn_step_kernel,
        grid=(b // bb,),
        in_specs=[pl.BlockSpec((bb, p3.shape[1], 128), lambda i: (i, 0, 0)),
                  pl.BlockSpec((bb, B_CONV - 1, B_ROWS, 128), lambda i: (i, 0, 0, 0)),
                  pl.BlockSpec((B_CONV, B_ROWS, 128), lambda i: (0, 0, 0)),
                  pl.BlockSpec((1, bb, 2 * B_V_HEADS), lambda i: (i, 0, 0)),
                  pl.BlockSpec((1, B_V_HEADS), lambda i: (0, 0)),
                  pl.BlockSpec((1, B_V_HEADS), lambda i: (0, 0)),
                  pl.BlockSpec((1, B_DV), lambda i: (0, 0)),
                  pl.BlockSpec((bb, B_V_HEADS, B_DK, B_DV), lambda i: (i, 0, 0, 0))],
        out_specs=[pl.BlockSpec((bb, B_V_HEADS, B_DV), lambda i: (i, 0, 0)),
                   pl.BlockSpec((bb, B_V_HEADS, B_DK, B_DV), lambda i: (i, 0, 0, 0))],
        out_shape=[jax.ShapeDtypeStruct((b, B_V_HEADS, B_DV), BF16), jax.ShapeDtypeStruct(s0.shape, F32)],
        compiler_params=_params("parallel"),
        name="gdn_step",
    )(p3, conv0.reshape(b, B_CONV - 1, B_ROWS, 128), conv_w.reshape(B_CONV, B_ROWS, 128),
      ba.reshape(b // bb, bb, 2 * B_V_HEADS), a_log.reshape(1, -1), dt_bias.reshape(1, -1), g_norm.reshape(1, -1), s0)
    return o.reshape(b, B_V_DIM), s


D_GROUP = 16
D_GROUPS = D_MODEL // D_GROUP
D_STATE = 64
S5_LANES = 128
S5_GROUPS = S5_LANES // D_GROUP
S5_HALF = S5_GROUPS * D_STATE
S5_BLOCKS = D_MODEL // S5_LANES
S5_SEGMENTS = 8


def _cmul(ar, ai, br, bi):
    return ar * br - ai * bi, ar * bi + ai * br


def _gelu_tanh(x):
    return 0.5 * x * (1.0 + jnp.tanh(math.sqrt(2.0 / math.pi) * (x + 0.044715 * (x * x * x))))


def _s5_tables(a_re, a_im, b_re, b_im, c_re, c_im, log_dt, seg_len):
    lam_re, lam_im = a_re.astype(F32), a_im.astype(F32)
    dt = jnp.exp(log_dt.astype(F32))[:, None]
    mag = jnp.exp(dt * lam_re)
    ab_re, ab_im = mag * jnp.cos(dt * lam_im), mag * jnp.sin(dt * lam_im)
    inv = 1.0 / (lam_re * lam_re + lam_im * lam_im)
    e_re, e_im = _cmul(ab_re - 1.0, ab_im, lam_re * inv, -lam_im * inv)
    bb_re, bb_im = _cmul(e_re[..., None], e_im[..., None], b_re.astype(F32), b_im.astype(F32))
    j = S5_BLOCKS
    eye = jnp.eye(S5_GROUPS, dtype=F32)

    def pack_rows(t):
        return t.reshape(j, 1, S5_HALF)

    def bdiag_in(t):
        t = t.reshape(j, S5_GROUPS, D_STATE, D_GROUP)
        return jnp.einsum('jgpc,gh->jgchp', t, eye).reshape(j, S5_LANES, S5_HALF)

    def bdiag_out(t):
        t = t.reshape(j, S5_GROUPS, D_GROUP, D_STATE)
        return jnp.einsum('jgcp,gh->jgphc', t, eye).reshape(j, S5_HALF, S5_LANES)

    ab = jnp.concatenate([pack_rows(ab_re), pack_rows(ab_im)], axis=-1)
    w_bu = jnp.concatenate([bdiag_in(bb_re), bdiag_in(bb_im)], axis=-1)
    w_c = jnp.concatenate([bdiag_out(c_re.astype(F32)), -bdiag_out(c_im.astype(F32))], axis=1)
    pr, pi = pack_rows(ab_re), pack_rows(ab_im)
    n = 1
    while n < seg_len:
        tr, ti = pr[:, n - 1:n], pi[:, n - 1:n]
        nr, ni = _cmul(pr, pi, tr, ti)
        pr, pi = jnp.concatenate([pr, nr], axis=1), jnp.concatenate([pi, ni], axis=1)
        n *= 2
    pw = jnp.concatenate([pr[:, :seg_len], pi[:, :seg_len]], axis=-1)
    return ab, w_bu, w_c, pw


def _s5_scan_kernel(x_ref, ab_ref, wbu_ref, wc_ref, pw_ref, dsk_ref, y_ref, s_out_ref, xp, buf, yp, *, seq, seg):
    nc = S5_HALF // 128
    ns = S5_SEGMENTS

    def tile(tau):
        return pl.ds(pl.multiple_of(tau * ns, ns), ns)

    def interleave(tau, c):
        xp[tile(tau), :] = x_ref[pl.ds(tau, ns, stride=seg), :]
        return c

    lax.fori_loop(0, seg, interleave, 0, unroll=8)
    x = xp[...]
    bu = _bdot(x, wbu_ref[0])
    for c in range(2 * nc):
        buf[c] = bu[:, c * 128:(c + 1) * 128]

    a_res = [jnp.broadcast_to(ab_ref[0, :, c * 128:(c + 1) * 128], (ns, 128)) for c in range(nc)]
    a_ims = [jnp.broadcast_to(ab_ref[0, :, S5_HALF + c * 128:S5_HALF + (c + 1) * 128], (ns, 128))
             for c in range(nc)]

    def step(tau, carry):
        rows = tile(tau)
        out = []
        for c in range(nc):
            s_re, s_im = carry[c]
            n_re = a_res[c] * s_re - a_ims[c] * s_im + buf[c, rows, :]
            n_im = a_res[c] * s_im + a_ims[c] * s_re + buf[nc + c, rows, :]
            buf[c, rows, :] = n_re
            buf[nc + c, rows, :] = n_im
            out.append((n_re, n_im))
        return tuple(out)

    zeros = jnp.zeros((ns, 128), F32)
    local_end = lax.fori_loop(0, seg, step, tuple((zeros, zeros) for _ in range(nc)), unroll=2)

    iota_r = lax.broadcasted_iota(jnp.int32, (ns, 128), 0)
    enter = []
    for c in range(nc):
        l_re, l_im = local_end[c]
        pe_re = pw_ref[0, seg - 1:seg, c * 128:(c + 1) * 128]
        pe_im = pw_ref[0, seg - 1:seg, S5_HALF + c * 128:S5_HALF + (c + 1) * 128]
        e_re, e_im = l_re[0:1], l_im[0:1]
        in_re, in_im = zeros, zeros
        for r in range(1, ns):
            in_re = jnp.where(iota_r == r, e_re, in_re)
            in_im = jnp.where(iota_r == r, e_im, in_im)
            t_re, t_im = _cmul(pe_re, pe_im, e_re, e_im)
            e_re, e_im = l_re[r:r + 1] + t_re, l_im[r:r + 1] + t_im
        enter.append((in_re, in_im))

    def fix(group, carry):
        taus = pl.ds(pl.multiple_of(group * 8, 8), 8)
        for c in range(nc):
            p_re8 = pw_ref[0, taus, c * 128:(c + 1) * 128]
            p_im8 = pw_ref[0, taus, S5_HALF + c * 128:S5_HALF + (c + 1) * 128]
            for u in range(8):
                rows = tile(group * 8 + u)
                f_re, f_im = _cmul(p_re8[u:u + 1], p_im8[u:u + 1], enter[c][0], enter[c][1])
                buf[c, rows, :] = buf[c, rows, :] + f_re
                buf[nc + c, rows, :] = buf[nc + c, rows, :] + f_im
        return carry

    lax.fori_loop(0, seg // 8, fix, 0)

    y = dsk_ref[...] * x
    for c in range(2 * nc):
        s_out_ref[0, 0, :, c * 128:(c + 1) * 128] = buf[c, seq - 1:seq, :]
        y = y + _bdot(buf[c], wc_ref[0, c * 128:(c + 1) * 128, :])
    yp[...] = _gelu_tanh(y)
    for r in range(ns):
        y_ref[r * seg:(r + 1) * seg, :] = yp[pl.ds(r, seg, stride=ns), :].astype(y_ref.dtype)


def _unpack_state(s):
    b = s.shape[0]
    return (s[..., :S5_HALF].reshape(b, D_GROUPS, D_STATE), s[..., S5_HALF:].reshape(b, D_GROUPS, D_STATE))


def _s5_prompt(x, b, l, tables, d_skip):
    d = x.shape[1]
    ab, w_bu, w_c, pw = tables
    seg = l // S5_SEGMENTS
    y, s = pl.pallas_call(
        functools.partial(_s5_scan_kernel, seq=l, seg=seg),
        grid=(b, S5_BLOCKS),
        in_specs=[pl.BlockSpec((l, S5_LANES), lambda bi, j: (bi, j)),
                  pl.BlockSpec((1, 1, 2 * S5_HALF), lambda bi, j: (j, 0, 0)),
                  pl.BlockSpec((1, S5_LANES, 2 * S5_HALF), lambda bi, j: (j, 0, 0)),
                  pl.BlockSpec((1, 2 * S5_HALF, S5_LANES), lambda bi, j: (j, 0, 0)),
                  pl.BlockSpec((1, seg, 2 * S5_HALF), lambda bi, j: (j, 0, 0)),
                  pl.BlockSpec((1, S5_LANES), lambda bi, j: (0, j))],
        out_specs=[pl.BlockSpec((l, S5_LANES), lambda bi, j: (bi, j)),
                   pl.BlockSpec((1, 1, 1, 2 * S5_HALF), lambda bi, j: (bi, j, 0, 0))],
        out_shape=[jax.ShapeDtypeStruct((b * l, d), BF16),
                   jax.ShapeDtypeStruct((b, S5_BLOCKS, 1, 2 * S5_HALF), F32)],
        scratch_shapes=[pltpu.VMEM((l, S5_LANES), F32),
                        pltpu.VMEM((2 * S5_HALF // 128, l, 128), F32),
                        pltpu.VMEM((l, S5_LANES), F32)],
        compiler_params=_params("parallel", "parallel"),
        name="s5_scan",
    )(x, ab, w_bu, w_c, pw, d_skip.reshape(1, d))
    return y, _unpack_state(s[:, :, 0, :])


def _s5_step_kernel(x_ref, s_ref, ab_ref, wbu_ref, wc_ref, dsk_ref, y_ref, s_out_ref):
    hf = S5_HALF
    x = x_ref[...]
    bu = _bdot(x, wbu_ref[0])
    s0 = s_ref[0]
    i_re, i_im = _cmul(ab_ref[0, :, :hf], ab_ref[0, :, hf:], s0[:, :hf], s0[:, hf:])
    s_new = jnp.concatenate([i_re, i_im], axis=-1) + bu
    s_out_ref[0] = s_new
    y = _bdot(s_new, wc_ref[0]) + dsk_ref[...] * x
    y_ref[...] = _gelu_tanh(y).astype(y_ref.dtype)


def _s5_step(x, tables, d_skip, s0_re, s0_im):
    b, d = x.shape
    ab, w_bu, w_c, _ = tables
    s0 = jnp.concatenate([s0_re.reshape(b, S5_BLOCKS, S5_HALF), s0_im.reshape(b, S5_BLOCKS, S5_HALF)], axis=-1)
    s0 = jnp.swapaxes(s0, 0, 1)
    y, s = pl.pallas_call(
        _s5_step_kernel,
        grid=(S5_BLOCKS,),
        in_specs=[pl.BlockSpec((b, S5_LANES), lambda j: (0, j)),
                  pl.BlockSpec((1, b, 2 * S5_HALF), lambda j: (j, 0, 0)),
                  pl.BlockSpec((1, 1, 2 * S5_HALF), lambda j: (j, 0, 0)),
                  pl.BlockSpec((1, S5_LANES, 2 * S5_HALF), lambda j: (j, 0, 0)),
                  pl.BlockSpec((1, 2 * S5_HALF, S5_LANES), lambda j: (j, 0, 0)),
                  pl.BlockSpec((1, S5_LANES), lambda j: (0, j))],
        out_specs=[pl.BlockSpec((b, S5_LANES), lambda j: (0, j)),
                   pl.BlockSpec((1, b, 2 * S5_HALF), lambda j: (j, 0, 0))],
        out_shape=[jax.ShapeDtypeStruct((b, d), BF16), jax.ShapeDtypeStruct((S5_BLOCKS, b, 2 * S5_HALF), F32)],
        compiler_params=_params("parallel"),
        name="s5_step",
    )(x, s0, ab, w_bu, w_c, d_skip.reshape(1, d))
    return y, _unpack_state(jnp.swapaxes(s, 0, 1))


def _glu_kernel(x_ref, w1_ref, w2_ref, o_ref):
    x = x_ref[...]
    y1 = jnp.dot(x, w1_ref[...].astype(BF16), preferred_element_type=F32)
    y2 = jnp.dot(x, w2_ref[...].astype(BF16), preferred_element_type=F32)
    o_ref[...] = y1 * jax.nn.sigmoid(y2)


def _glu_matmul(x, w, tm_target=1664, tn_target=256):
    m, k = x.shape
    n = w.shape[1] // 2
    tm = _pick_tile(m, tm_target, 16)
    tn = _pick_tile(n, tn_target, 128)
    nb = n // tn
    return pl.pallas_call(
        _glu_kernel,
        grid=(m // tm, nb),
        in_specs=[pl.BlockSpec((tm, k), lambda i, j: (i, 0)),
                  pl.BlockSpec((k, tn), lambda i, j: (0, j)),
                  pl.BlockSpec((k, tn), lambda i, j: (0, j + nb))],
        out_specs=pl.BlockSpec((tm, tn), lambda i, j: (i, j)),
        out_shape=jax.ShapeDtypeStruct((m, n), F32),
        compiler_params=_params("parallel", "parallel"),
        name="glu_matmul",
    )(x, w, w)


def kernel(x_prompt, x_sample, state_a_S, state_b_S, state_b_conv, state_c_C, state_c_n, state_c_m, state_d_re, state_d_im, a_w_in, a_lb_logits, a_g_norm, a_w_out, b_w_in, b_conv_w, b_a_log, b_dt_bias, b_g_norm, b_w_out, c_w_in, c_b_gate, c_g_norm, c_w_out, d_a_re, d_a_im, d_b_re, d_b_im, d_c_re, d_c_im, d_skip, d_log_dt, d_w_glu, moe_w_router, moe_b_router, moe_w_gu, moe_w_down, moe_w_gu_s, moe_w_down_s, ln_mix_g, ln_mix_b, ln_ffn_g, ln_ffn_b):
    bp, l, d = x_prompt.shape
    bs = x_sample.shape[0]
    tp = bp * l
    x = jnp.concatenate([x_prompt.reshape(tp, d), x_sample.reshape(bs, d)], axis=0)
    xb = x.astype(BF16)

    def finish_layer(x, h, layer):
        x, xb, xpk = _add_ln(x, h, ln_mix_g[layer], ln_mix_b[layer])
        return _moe_ln(x, xb, xpk, moe_w_router, moe_b_router, moe_w_gu, moe_w_down, moe_w_gu_s, moe_w_down_s,
                       ln_ffn_g, ln_ffn_b, layer)

    proj = _matmul(xb, a_w_in)
    lb = _hgrn2_lower_bound(a_lb_logits, 0)
    o_p, a_s_p = _hgrn2_prompt(proj, bp, l, lb, a_g_norm)
    o_s, a_s_s = _hgrn2_step(proj[tp:], state_a_S, lb, a_g_norm)
    h = _matmul(jnp.concatenate([o_p, o_s], axis=0), a_w_out)
    x, xb = finish_layer(x, h, 0)

    n_main = B_CONV_DIM + B_V_DIM
    proj = _matmul(xb, b_w_in, n_cols=n_main)
    ba = _matmul(xb, b_w_in[:, n_main:])
    o_p, b_s_p = _gdn_prompt(proj, ba, bp, l, b_conv_w, b_a_log, b_dt_bias, b_g_norm)
    o_s, b_s_s = _gdn_step(proj[tp:], ba[tp:], state_b_conv, b_conv_w, b_a_log, b_dt_bias, b_g_norm, state_b_S)
    qkv_s = proj[tp:, :B_CONV_DIM]
    b_conv_p = jnp.stack([proj[(i + 1) * l - (B_CONV - 1):(i + 1) * l, :B_CONV_DIM] for i in range(bp)], axis=0)
    b_conv_s = jnp.concatenate([state_b_conv[:, 1:, :], qkv_s[:, None, :]], axis=1)
    h = _matmul(jnp.concatenate([o_p, o_s], axis=0), b_w_out)
    x, xb = finish_layer(x, h, 1)

    n_main = 2 * C_QK_DIM + 2 * C_V_DIM
    proj = _matmul(xb, c_w_in, n_cols=n_main)
    gates = _matmul(xb, c_w_in[:, n_main:])
    o_p, c_c_p, c_n_p, c_m_p = _mlstm_prompt(proj, gates, bp, l, c_b_gate, c_g_norm)
    o_s, c_c_s, c_n_s, c_m_s = _mlstm_step(proj[tp:], gates[tp:], c_b_gate, c_g_norm, state_c_C, state_c_n, state_c_m)
    h = _matmul(jnp.concatenate([o_p, o_s], axis=0), c_w_out)
    x, xb = finish_layer(x, h, 2)

    tables = _s5_tables(d_a_re, d_a_im, d_b_re, d_b_im, d_c_re, d_c_im, d_log_dt, l // S5_SEGMENTS)
    y_p, (d_re_p, d_im_p) = _s5_prompt(x, bp, l, tables, d_skip)
    y_s, (d_re_s, d_im_s) = _s5_step(x[tp:], tables, d_skip, state_d_re, state_d_im)
    h = _glu_matmul(jnp.concatenate([y_p, y_s], axis=0), d_w_glu)
    x, xb = finish_layer(x, h, 3)

    y_prompt = x[:tp].reshape(bp, l, d)
    y_sample = x[tp:].reshape(bs, 1, d)
    return (y_prompt, y_sample, a_s_p, a_s_s, b_s_p, b_s_s, b_conv_p, b_conv_s,
            c_c_p, c_c_s, c_n_p, c_n_s, c_m_p, c_m_s, d_re_p, d_re_s, d_im_p, d_im_s)
```

```python
import functools
import math

import jax
import jax.numpy as jnp
from jax import lax
from jax.experimental import pallas as pl
from jax.experimental.pallas import tpu as pltpu

F32 = jnp.float32
BF16 = jnp.bfloat16
HI = lax.Precision.HIGHEST

D_MODEL = 2048
DEPTH = 4
CHUNK = 64
DN_ALPHA = (2 * DEPTH) ** 0.25
LN_EPS = 1e-5
RMS_EPS = 1e-6

VMEM_LIMIT_BYTES = 56 * 1024 * 1024


def _params(*sem):
    return pltpu.CompilerParams(dimension_semantics=sem, vmem_limit_bytes=VMEM_LIMIT_BYTES)


def _pick_tile(n, target, mult):
    best = None
    for t in range(mult, min(n, target) + 1, mult):
        if n % t == 0:
            best = t
    return best if best is not None else n


def _mm_kernel(x_ref, w_ref, o_ref):
    o_ref[...] = jnp.dot(x_ref[...], w_ref[...].astype(BF16), preferred_element_type=F32).astype(o_ref.dtype)


def _matmul(x, w, n_cols=None, out_dtype=F32, tm_target=1664, tn_target=512):
    m, k = x.shape
    n = w.shape[1] if n_cols is None else n_cols
    tm = _pick_tile(m, tm_target, 16)
    tn = _pick_tile(n, tn_target, 128)
    return pl.pallas_call(
        _mm_kernel,
        grid=(m // tm, n // tn),
        in_specs=[pl.BlockSpec((tm, k), lambda i, j: (i, 0)), pl.BlockSpec((k, tn), lambda i, j: (0, j))],
        out_specs=pl.BlockSpec((tm, tn), lambda i, j: (i, j)),
        out_shape=jax.ShapeDtypeStruct((m, n), out_dtype),
        compiler_params=_params("parallel", "parallel"),
        name="matmul",
    )(x, w)


def _layernorm_rows(z, g, b):
    mu = jnp.mean(z, axis=-1, keepdims=True)
    zc = z - mu
    var = jnp.mean(zc * zc, axis=-1, keepdims=True)
    return zc * lax.rsqrt(var + LN_EPS) * g + b


U32 = jnp.uint32
BF16_HIGH_MASK = 0xFFFF0000


def _pack_bf16_halves(yb):
    half = yb.shape[1] // 2
    bits = lax.bitcast_convert_type(yb.astype(F32), U32)
    return (bits[:, :half] >> 16) | (bits[:, half:] & jnp.uint32(BF16_HIGH_MASK))


def _unpack_bf16_halves(w):
    lo = lax.bitcast_convert_type(w << 16, F32).astype(BF16)
    hi = lax.bitcast_convert_type(w & jnp.uint32(BF16_HIGH_MASK), F32).astype(BF16)
    return lo, hi


def _add_ln_kernel(x_ref, h_ref, g_ref, b_ref, o_ref, ob_ref, op_ref):
    y = _layernorm_rows(DN_ALPHA * x_ref[...] + h_ref[...], g_ref[...], b_ref[...])
    yb = y.astype(BF16)
    o_ref[...] = y
    ob_ref[...] = yb
    op_ref[...] = _pack_bf16_halves(yb)


def _add_ln(x, h, g, b, tm_target=640):
    m, d = x.shape
    tm = _pick_tile(m, tm_target, 16)
    row = pl.BlockSpec((tm, d), lambda i: (i, 0))
    vec = pl.BlockSpec((1, d), lambda i: (0, 0))
    return pl.pallas_call(
        _add_ln_kernel,
        grid=(m // tm,),
        in_specs=[row, row, vec, vec],
        out_specs=[row, row, pl.BlockSpec((tm, d // 2), lambda i: (i, 0))],
        out_shape=[jax.ShapeDtypeStruct((m, d), F32), jax.ShapeDtypeStruct((m, d), BF16),
                   jax.ShapeDtypeStruct((m, d // 2), U32)],
        compiler_params=_params("parallel"),
        name="add_ln",
    )(x, h, g.reshape(1, d), b.reshape(1, d))


def _silu(x):
    return x * jax.nn.sigmoid(x)


def _ffn_kernel(x_ref, wgu_ref, wd_ref, o_ref, *, dh):
    gu = jnp.dot(x_ref[...], wgu_ref[0].astype(BF16), preferred_element_type=F32)
    a = _silu(gu[:, :dh]) * gu[:, dh:]
    o_ref[...] = jnp.dot(a.astype(BF16), wd_ref[0].astype(BF16), preferred_element_type=F32)


def _dense_ffn(xb, w_gu, w_down, layer, tm_target=640):
    m, d = xb.shape
    dh = w_down.shape[1]
    tm = _pick_tile(m, tm_target, 16)
    return pl.pallas_call(
        functools.partial(_ffn_kernel, dh=dh),
        grid=(m // tm,),
        in_specs=[pl.BlockSpec((tm, d), lambda i: (i, 0)),
                  pl.BlockSpec((1, d, 2 * dh), lambda i: (layer, 0, 0)),
                  pl.BlockSpec((1, dh, d), lambda i: (layer, 0, 0))],
        out_specs=pl.BlockSpec((tm, d), lambda i: (i, 0)),
        out_shape=jax.ShapeDtypeStruct((m, d), F32),
        compiler_params=_params("parallel"),
        name="shared_ffn",
    )(xb, w_gu, w_down)


N_EXPERTS = 64
TOP_K = 8
N_ROUTE_GROUPS = 8
TOPK_ROUTE_GROUPS = 4
GROUP_SIZE = N_EXPERTS // N_ROUTE_GROUPS
ROUTED_SCALE = 2.5
EXPERT_TILE = 512


def _router_kernel(x_ref, wt_ref, b_ref, e_ref, g_ref):
    logits = lax.dot_general(wt_ref[...].astype(BF16), x_ref[...], (((1,), (1,)), ((), ())),
                             preferred_element_type=F32)
    scores = jax.nn.sigmoid(logits)
    biased = scores + b_ref[...]
    tm = scores.shape[1]
    neg = jnp.float32(-jnp.inf)
    iota_g = lax.broadcasted_iota(jnp.int32, (GROUP_SIZE, tm), 0)
    gscore = jnp.zeros((N_ROUTE_GROUPS, tm), F32)
    iota_r = lax.broadcasted_iota(jnp.int32, (N_ROUTE_GROUPS, tm), 0)
    for g in range(N_ROUTE_GROUPS):
        v = biased[g * GROUP_SIZE:(g + 1) * GROUP_SIZE, :]
        m1 = jnp.max(v, axis=0, keepdims=True)
        i1 = jnp.min(jnp.where(v == m1, iota_g, GROUP_SIZE), axis=0, keepdims=True)
        m2 = jnp.max(jnp.where(iota_g == i1, neg, v), axis=0, keepdims=True)
        gscore = jnp.where(iota_r == g, m1 + m2, gscore)
    ok = jnp.zeros((N_ROUTE_GROUPS, tm), jnp.int32)
    for _ in range(TOPK_ROUTE_GROUPS):
        m = jnp.max(gscore, axis=0, keepdims=True)
        gi = jnp.min(jnp.where(gscore == m, iota_r, N_ROUTE_GROUPS), axis=0, keepdims=True)
        hit = iota_r == gi
        ok = jnp.where(hit, 1, ok)
        gscore = jnp.where(hit, neg, gscore)
    masked = jnp.concatenate(
        [jnp.where(ok[g:g + 1, :] > 0, biased[g * GROUP_SIZE:(g + 1) * GROUP_SIZE, :], neg)
         for g in range(N_ROUTE_GROUPS)], axis=0)
    iota_e = lax.broadcasted_iota(jnp.int32, (N_EXPERTS, tm), 0)
    iota_k = lax.broadcasted_iota(jnp.int32, (TOP_K, tm), 0)
    top_e = jnp.zeros((TOP_K, tm), jnp.int32)
    gate = jnp.zeros((TOP_K, tm), F32)
    for k in range(TOP_K):
        m = jnp.max(masked, axis=0, keepdims=True)
        ei = jnp.min(jnp.where(masked == m, iota_e, N_EXPERTS), axis=0, keepdims=True)
        hit = iota_e == ei
        gk = jnp.sum(jnp.where(hit, scores, 0.0), axis=0, keepdims=True)
        top_e = jnp.where(iota_k == k, ei, top_e)
        gate = jnp.where(iota_k == k, gk, gate)
        masked = jnp.where(hit, neg, masked)
    e_ref[...] = top_e
    g_ref[...] = ROUTED_SCALE * gate / jnp.sum(gate, axis=0, keepdims=True)


def _router(x, w_router_t, b_router, tm_target=640):
    t, d = x.shape
    tm = _pick_tile(t, tm_target, 128)
    return pl.pallas_call(
        _router_kernel,
        grid=(t // tm,),
        in_specs=[pl.BlockSpec((tm, d), lambda i: (i, 0)),
                  pl.BlockSpec((N_EXPERTS, d), lambda i: (0, 0)),
                  pl.BlockSpec((N_EXPERTS, 1), lambda i: (0, 0))],
        out_specs=[pl.BlockSpec((TOP_K, tm), lambda i: (0, i)), pl.BlockSpec((TOP_K, tm), lambda i: (0, i))],
        out_shape=[jax.ShapeDtypeStruct((TOP_K, t), jnp.int32), jax.ShapeDtypeStruct((TOP_K, t), F32)],
        compiler_params=_params("parallel"),
        name="router",
    )(x, w_router_t, b_router.reshape(N_EXPERTS, 1))


LANES = 128


def _dispatch_kernel(e_ref, pos_ref, cnt_ref, rank_ref):
    n_blocks = e_ref.shape[1] // LANES
    iota_e = lax.broadcasted_iota(jnp.int32, (N_EXPERTS, LANES), 0)
    incl = (lax.broadcasted_iota(jnp.int32, (LANES, LANES), 0)
            <= lax.broadcasted_iota(jnp.int32, (LANES, LANES), 1)).astype(BF16)
    iota_k = lax.broadcasted_iota(jnp.int32, (TOP_K, LANES), 0)

    def pick(e_blk, table):
        out = jnp.zeros((TOP_K, LANES), F32)
        for k in range(TOP_K):
            v = jnp.sum(jnp.where(iota_e == e_blk[k:k + 1, :], table, 0.0), axis=0, keepdims=True)
            out = jnp.where(iota_k == k, v, out)
        return out

    def count_block(b, carry):
        cols = pl.ds(pl.multiple_of(b * LANES, LANES), LANES)
        e_blk = e_ref[:, cols]
        member = jnp.zeros((N_EXPERTS, LANES), F32)
        for k in range(TOP_K):
            member = member + (iota_e == e_blk[k:k + 1, :]).astype(F32)
        seen = jnp.dot(member.astype(BF16), incl, preferred_element_type=F32) + carry
        rank_ref[:, cols] = pick(e_blk, seen - member)
        return jnp.broadcast_to(seen[:, LANES - 1:LANES], (N_EXPERTS, LANES))

    counts = lax.fori_loop(0, n_blocks, count_block, jnp.zeros((N_EXPERTS, LANES), F32))
    cnt_ref[...] = counts.astype(jnp.int32)
    padded = jnp.ceil(counts * (1.0 / EXPERT_TILE)) * EXPERT_TILE
    start = jnp.dot(_tri(N_EXPERTS, strict=True).astype(F32), padded, preferred_element_type=F32, precision=HI)

    def place_block(b, carry):
        cols = pl.ds(pl.multiple_of(b * LANES, LANES), LANES)
        pos_ref[:, cols] = (pick(e_ref[:, cols], start) + rank_ref[:, cols]).astype(jnp.int32)
        return carry

    lax.fori_loop(0, n_blocks, place_block, 0)


def _dispatch_tables(top_e_t, n_tiles):
    k, t = top_e_t.shape
    pos_t, cnt = pl.pallas_call(
        _dispatch_kernel,
        out_shape=[jax.ShapeDtypeStruct((k, t), jnp.int32), jax.ShapeDtypeStruct((N_EXPERTS, LANES), jnp.int32)],
        scratch_shapes=[pltpu.VMEM((k, t), F32)],
        compiler_params=pltpu.CompilerParams(vmem_limit_bytes=VMEM_LIMIT_BYTES),
        name="dispatch",
    )(top_e_t)
    counts = cnt[:, 0]
    pends = jnp.cumsum((counts + EXPERT_TILE - 1) // EXPERT_TILE * EXPERT_TILE)
    n_used = pends[-1] // EXPERT_TILE
    tile_idx = jnp.arange(n_tiles, dtype=jnp.int32)
    tile_e = jnp.minimum(jnp.sum((pends[None, :] <= (tile_idx * EXPERT_TILE)[:, None]).astype(jnp.int32), axis=1),
                         N_EXPERTS - 1)
    tile_e = jnp.where(tile_idx < n_used, tile_e, tile_e[jnp.maximum(n_used - 1, 0)])
    last_tile = jnp.where(counts > 0, pends // EXPERT_TILE - 1, -1).astype(jnp.int32)
    return pos_t.reshape(-1), tile_e.astype(jnp.int32), n_used.astype(jnp.int32).reshape(1), last_tile


SCATTER_TILE = 640


def _scatter_rows_kernel(pos_ref, last_tile_ref, n_used_ref, x_ref, xs_ref, zbuf, zsem, sem, *, n_tok, max_idle):
    tt = x_ref.shape[0]
    n_tiles = xs_ref.shape[0] // EXPERT_TILE
    n_used = n_used_ref[0]

    @pl.when(pl.program_id(0) == 0)
    def _():
        zbuf[...] = jnp.zeros_like(zbuf)

        def fill(tile):
            return pltpu.make_async_copy(zbuf, xs_ref.at[pl.ds(tile * EXPERT_TILE, EXPERT_TILE)], zsem)

        def for_each_padded_tile(action):
            for e in range(N_EXPERTS):
                tile = last_tile_ref[e]

                @pl.when(tile >= 0)
                def _(tile=tile):
                    action(fill(tile))
            for j in range(max_idle):
                @pl.when(n_used + j < n_tiles)
                def _(j=j):
                    action(fill(n_used + j))

        for_each_padded_tile(lambda copy: copy.start())
        for_each_padded_tile(lambda copy: copy.wait())

    base = pl.program_id(0) * tt

    def body(r, c):
        for k in range(TOP_K):
            p = pos_ref[k * n_tok + base + r]
            pltpu.make_async_copy(x_ref.at[pl.ds(r, 1)], xs_ref.at[pl.ds(p, 1)], sem).start()
        return c

    lax.fori_loop(0, tt, body, 0, unroll=2)
    for k in range(TOP_K):
        pltpu.make_async_copy(x_ref, xs_ref.at[pl.ds(0, tt)], sem).wait()


def _sort_rows(x, pos, last_tile, n_used, n_tiles):
    t, d = x.shape
    tt = _pick_tile(t, SCATTER_TILE, 8)
    max_idle = n_tiles - (t * TOP_K) // EXPERT_TILE
    return pl.pallas_call(
        functools.partial(_scatter_rows_kernel, n_tok=t, max_idle=max_idle),
        grid_spec=pltpu.PrefetchScalarGridSpec(
            num_scalar_prefetch=3, grid=(t // tt,),
            in_specs=[pl.BlockSpec((tt, d), lambda i, p, lt, nu: (i, 0))],
            out_specs=pl.BlockSpec(memory_space=pl.ANY),
            scratch_shapes=[pltpu.VMEM((EXPERT_TILE, d), x.dtype), pltpu.SemaphoreType.DMA(()),
                            pltpu.SemaphoreType.DMA(())]),
        out_shape=jax.ShapeDtypeStruct((n_tiles * EXPERT_TILE, d), x.dtype),
        compiler_params=_params("arbitrary"),
        name="scatter_rows",
    )(pos, last_tile, n_used, x)


def _expert_kernel(tile_e_ref, n_used_ref, xs_ref, wgu_ref, wd_ref, o_ref, wgu_b, wd_b, *, dh):
    i = pl.program_id(0)
    n_used = n_used_ref[0]
    e = tile_e_ref[i]
    e_prev = tile_e_ref[jnp.maximum(i - 1, 0)]

    @pl.when(jnp.logical_and(i < n_used, jnp.logical_or(i == 0, e != e_prev)))
    def _():
        wgu_b[...] = wgu_ref[0, 0].astype(BF16)
        wd_b[...] = wd_ref[0, 0].astype(BF16)

    @pl.when(i < n_used)
    def _():
        x_lo, x_hi = _unpack_bf16_halves(xs_ref[...])
        half = x_lo.shape[1]
        gu = (jnp.dot(x_lo, wgu_b[:half, :], preferred_element_type=F32)
              + jnp.dot(x_hi, wgu_b[half:, :], preferred_element_type=F32))
        a = _silu(gu[:, :dh]) * gu[:, dh:]
        o_ref[...] = jnp.dot(a.astype(BF16), wd_b[...], preferred_element_type=F32)

    @pl.when(i >= n_used)
    def _():
        o_ref[...] = jnp.zeros_like(o_ref)


def _routed_experts(xs, tile_e, n_used, w_gu, w_down, layer):
    d = w_gu.shape[2]
    dh = w_down.shape[2]
    n_tiles = tile_e.shape[0]
    grid_spec = pltpu.PrefetchScalarGridSpec(
        num_scalar_prefetch=2,
        grid=(n_tiles,),
        in_specs=[pl.BlockSpec((EXPERT_TILE, d // 2), lambda i, te, nu: (jnp.minimum(i, nu[0] - 1), 0)),
                  pl.BlockSpec((1, 1, d, 2 * dh), lambda i, te, nu: (layer, te[i], 0, 0)),
                  pl.BlockSpec((1, 1, dh, d), lambda i, te, nu: (layer, te[i], 0, 0))],
        out_specs=pl.BlockSpec((EXPERT_TILE, d), lambda i, te, nu: (i, 0)),
        scratch_shapes=[pltpu.VMEM((d, 2 * dh), BF16), pltpu.VMEM((dh, d), BF16)],
    )
    return pl.pallas_call(
        functools.partial(_expert_kernel, dh=dh),
        grid_spec=grid_spec,
        out_shape=jax.ShapeDtypeStruct((n_tiles * EXPERT_TILE, d), F32),
        compiler_params=_params("arbitrary"),
        name="routed_experts",
    )(tile_e, n_used, xs, w_gu, w_down)


COMBINE_TILE = 128


def _combine_kernel(pos_ref, rows_hbm, gate_ref, x_ref, sh_ref, g_ref, b_ref, o_ref, ob_ref, buf, sem):
    i = pl.program_id(0)
    n = pl.num_programs(0)
    slot = i % 2

    n_tok = n * COMBINE_TILE

    def issue(tile, s):
        base = tile * COMBINE_TILE

        def body(r, c):
            for k in range(TOP_K):
                p = pos_ref[k * n_tok + base + r]
                pltpu.make_async_copy(rows_hbm.at[pl.ds(p, 1)], buf.at[s, k, pl.ds(r, 1)], sem.at[s]).start()
            return c

        lax.fori_loop(0, COMBINE_TILE, body, 0, unroll=2)

    @pl.when(i == 0)
    def _():
        issue(0, 0)

    @pl.when(i + 1 < n)
    def _():
        issue(i + 1, 1 - slot)

    for k in range(TOP_K):
        pltpu.make_async_copy(rows_hbm.at[pl.ds(0, COMBINE_TILE)], buf.at[slot, k], sem.at[slot]).wait()
    gate = gate_ref[...]
    z = DN_ALPHA * x_ref[...] + sh_ref[...]
    for k in range(TOP_K):
        z = z + gate[:, k:k + 1] * buf[slot, k]
    y = _layernorm_rows(z, g_ref[...], b_ref[...])
    o_ref[...] = y
    ob_ref[...] = y.astype(BF16)


def _combine_ln(rows, pos, gate, x, shared, g, b):
    t, d = x.shape
    row = lambda w: pl.BlockSpec((COMBINE_TILE, w), lambda i, p: (i, 0))
    vec = pl.BlockSpec((1, d), lambda i, p: (0, 0))
    grid_spec = pltpu.PrefetchScalarGridSpec(
        num_scalar_prefetch=1,
        grid=(t // COMBINE_TILE,),
        in_specs=[pl.BlockSpec(memory_space=pl.ANY), row(TOP_K), row(d), row(d), vec, vec],
        out_specs=[row(d), row(d)],
        scratch_shapes=[pltpu.VMEM((2, TOP_K, COMBINE_TILE, d), F32), pltpu.SemaphoreType.DMA((2,))],
    )
    return pl.pallas_call(
        _combine_kernel,
        grid_spec=grid_spec,
        out_shape=[jax.ShapeDtypeStruct((t, d), F32), jax.ShapeDtypeStruct((t, d), BF16)],
        compiler_params=_params("arbitrary"),
        name="combine_ln",
    )(pos, rows, gate, x, shared, g.reshape(1, d), b.reshape(1, d))


def _moe_ln(x, xb, xpk, w_router, b_router, w_gu, w_down, w_gu_s, w_down_s, ln_g, ln_b, layer):
    t = x.shape[0]
    top_e_t, gate_t = _router(xb, w_router[layer].T, b_router[layer])
    n_tiles = (t * TOP_K + N_EXPERTS * (EXPERT_TILE - 1) + EXPERT_TILE - 1) // EXPERT_TILE
    pos, tile_e, n_used, last_tile = _dispatch_tables(top_e_t, n_tiles)
    xs = _sort_rows(xpk, pos, last_tile, n_used, n_tiles)
    rows = _routed_experts(xs, tile_e, n_used, w_gu, w_down, layer)
    shared = _dense_ffn(xb, w_gu_s, w_down_s, layer)
    return _combine_ln(rows, pos, gate_t.T, x, shared, ln_g[layer], ln_b[layer])


SUB = 16
SEQ_BLOCK = 512


def _tri(n, strict=False):
    r = lax.broadcasted_iota(jnp.int32, (n, n), 0)
    c = lax.broadcasted_iota(jnp.int32, (n, n), 1)
    return (r > c) if strict else (r >= c)


def _cumsum_rows(x):
    n = x.shape[0]
    return jnp.dot(_tri(n).astype(F32), x, preferred_element_type=F32, precision=HI)


def _bdot(a, b):
    return jnp.dot(a.astype(BF16), b.astype(BF16), preferred_element_type=F32)


def _bdot_nt(a, b):
    return lax.dot_general(a.astype(BF16), b.astype(BF16), (((1,), (1,)), ((), ())), preferred_element_type=F32)


def _bdot_tn(a, b):
    return lax.dot_general(a.astype(BF16), b.astype(BF16), (((0,), (0,)), ((), ())), preferred_element_type=F32)


def _rms(x):
    return x * lax.rsqrt(jnp.mean(x * x, axis=-1, keepdims=True) + RMS_EPS)


def _row_of(col_b):
    c = col_b.shape[0]
    return col_b.T[0:1, :c]


A_HEADS = 16
A_DK = 128
A_DV = 128
HGRN2_HEADS_PER_STEP = 8


def _hgrn2_chunk_kernel(q_ref, f_ref, v_ref, g_ref, lb_ref, gn_ref, o_ref, s_out_ref, s_ref, *, n_chunks):
    l = pl.program_id(2)

    @pl.when(l == 0)
    def _():
        s_ref[...] = jnp.zeros_like(s_ref)

    lb = lb_ref[...]
    gn = gn_ref[...]
    heads = range(HGRN2_HEADS_PER_STEP)

    def head(x, i):
        return x[:, i * A_DK:(i + 1) * A_DK]

    def chunk(j, carry):
        rows = pl.ds(pl.multiple_of(j * CHUNK, CHUNK), CHUNK)
        q = _silu(q_ref[rows, :])
        f = lb + (1.0 - lb) * jax.nn.sigmoid(f_ref[rows, :])
        k = 1.0 - f
        v = v_ref[rows, :]
        bc = _cumsum_rows(jnp.log(f))
        qe = q * jnp.exp(bc)
        s_old = [s_ref[i] for i in heads]
        o_inter = [_bdot(head(qe, i), s_old[i]) for i in heads]
        outs = [[] for _ in heads]
        for sc in range(CHUNK // SUB):
            lo, hi = sc * SUB, (sc + 1) * SUB
            bref = bc[lo - 1:lo, :] if sc > 0 else jnp.zeros((1, bc.shape[1]), F32)
            qi = q[lo:hi] * jnp.exp(bc[lo:hi] - bref)
            ki = k[:hi] * jnp.exp(bref - bc[:hi])
            causal = (lax.broadcasted_iota(jnp.int32, (SUB, hi), 0) + lo
                      >= lax.broadcasted_iota(jnp.int32, (SUB, hi), 1))
            atts = [jnp.where(causal, _bdot_nt(head(qi, i), head(ki, i)), 0.0) for i in heads]
            for i in heads:
                outs[i].append(_bdot(atts[i], head(v, i)[:hi]))
        b_end = bc[CHUNK - 1:CHUNK, :]
        kd = k * jnp.exp(b_end - bc)
        d_cols = jnp.exp(bc.T[:, CHUNK - 1:CHUNK])
        for i in heads:
            s_ref[i] = d_cols[i * A_DK:(i + 1) * A_DK] * s_old[i] + _bdot_tn(head(kd, i), head(v, i))
        o_n = jnp.concatenate([_rms(jnp.concatenate(outs[i], axis=0) + o_inter[i]) * gn for i in heads], axis=1)
        o_ref[rows, :] = (o_n * _silu(g_ref[rows, :])).astype(o_ref.dtype)
        return carry

    lax.fori_loop(0, n_chunks, chunk, 0)

    @pl.when(l == pl.num_programs(2) - 1)
    def _():
        s_out_ref[0] = s_ref[...]


def _hgrn2_prompt(proj, b, l, lb, g_norm):
    lblk = _pick_tile(l, SEQ_BLOCK, CHUNK)
    nl = l // lblk
    hg = HGRN2_HEADS_PER_STEP
    groups = A_HEADS // hg
    part = lambda p: pl.BlockSpec((lblk, hg * A_DK), lambda bi, h, li, p=p: (bi * nl + li, p * groups + h))
    return pl.pallas_call(
        functools.partial(_hgrn2_chunk_kernel, n_chunks=lblk // CHUNK),
        grid=(b, groups, nl),
        in_specs=[part(0), part(1), part(2), part(3),
                  pl.BlockSpec((1, hg * A_DK), lambda bi, h, li: (0, h)),
                  pl.BlockSpec((1, A_DV), lambda bi, h, li: (0, 0))],
        out_specs=[pl.BlockSpec((lblk, hg * A_DV), lambda bi, h, li: (bi * nl + li, h)),
                   pl.BlockSpec((1, hg, A_DK, A_DV), lambda bi, h, li: (bi, h, 0, 0))],
        out_shape=[jax.ShapeDtypeStruct((b * l, A_HEADS * A_DV), BF16),
                   jax.ShapeDtypeStruct((b, A_HEADS, A_DK, A_DV), F32)],
        scratch_shapes=[pltpu.VMEM((hg, A_DK, A_DV), F32)],
        compiler_params=_params("parallel", "parallel", "arbitrary"),
        name="hgrn2_chunks",
    )(proj, proj, proj, proj, lb.reshape(1, -1), g_norm.reshape(1, -1))


HGRN2_STEP_BATCH = 4


def _hgrn2_step_kernel(p_ref, s_ref, lb_ref, gn_ref, o_ref, s_out_ref):
    lb = lb_ref[...]
    iota_h = lax.broadcasted_iota(jnp.int32, (A_HEADS, A_DV), 0)
    for bi in range(HGRN2_STEP_BATCH):
        p = p_ref[bi]
        q = _silu(p[0:A_HEADS])
        f = lb + (1.0 - lb) * jax.nn.sigmoid(p[A_HEADS:2 * A_HEADS])
        v = p[2 * A_HEADS:3 * A_HEADS]
        g = p[3 * A_HEADS:4 * A_HEADS]
        q_t, f_t = q.T, f.T
        o = jnp.zeros((A_HEADS, A_DV), F32)
        for h in range(A_HEADS):
            fc = f_t[:, h:h + 1]
            s_new = fc * s_ref[bi, h] + (1.0 - fc) * v[h:h + 1, :]
            s_out_ref[bi, h] = s_new
            oh = jnp.sum(q_t[:, h:h + 1] * s_new, axis=0, keepdims=True)
            o = jnp.where(iota_h == h, oh, o)
        o_ref[bi] = (_rms(o) * gn_ref[...] * _silu(g)).astype(o_ref.dtype)


def _hgrn2_step(proj, s0, lb, g_norm):
    b = proj.shape[0]
    bb = HGRN2_STEP_BATCH
    p3 = proj.reshape(b, 4 * A_HEADS, A_DK)
    o, s = pl.pallas_call(
        _hgrn2_step_kernel,
        grid=(b // bb,),
        in_specs=[pl.BlockSpec((bb, 4 * A_HEADS, A_DK), lambda i: (i, 0, 0)),
                  pl.BlockSpec((bb, A_HEADS, A_DK, A_DV), lambda i: (i, 0, 0, 0)),
                  pl.BlockSpec((A_HEADS, A_DK), lambda i: (0, 0)),
                  pl.BlockSpec((1, A_DV), lambda i: (0, 0))],
        out_specs=[pl.BlockSpec((bb, A_HEADS, A_DV), lambda i: (i, 0, 0)),
                   pl.BlockSpec((bb, A_HEADS, A_DK, A_DV), lambda i: (i, 0, 0, 0))],
        out_shape=[jax.ShapeDtypeStruct((b, A_HEADS, A_DV), BF16),
                   jax.ShapeDtypeStruct(s0.shape, F32)],
        compiler_params=_params("parallel"),
        name="hgrn2_step",
    )(p3, s0, lb.reshape(A_HEADS, A_DK), g_norm.reshape(1, -1))
    return o.reshape(b, A_HEADS * A_DV), s


def _hgrn2_lower_bound(lb_logits, layer):
    return jnp.cumsum(jax.nn.softmax(lb_logits.astype(F32), axis=0), axis=0)[layer]


C_HEADS = 8
C_DK = 128
C_DV = 256
C_QK_DIM = C_HEADS * C_DK
C_V_DIM = C_HEADS * C_DV
C_GATE_CAP = 15.0
MLSTM_HEADS_PER_STEP = 4


def _log_sigmoid(x):
    return jnp.minimum(x, 0.0) - jnp.log(1.0 + jnp.exp(-jnp.abs(x)))


def _cap(x):
    return C_GATE_CAP * jnp.tanh(x / C_GATE_CAP)


def _mlstm_chunk_kernel(q_ref, k_ref, v_ref, op_ref, gt_ref, bg_ref, gn_ref,
                        o_ref, c_out_ref, n_out_ref, m_out_ref, c_ref, n_ref, m_ref, *, n_chunks):
    h = pl.program_id(1)
    l = pl.program_id(2)

    @pl.when(l == 0)
    def _():
        c_ref[...] = jnp.zeros_like(c_ref)
        n_ref[...] = jnp.zeros_like(n_ref)
        m_ref[...] = jnp.zeros_like(m_ref)

    hg = MLSTM_HEADS_PER_STEP
    heads = range(hg)
    lane = lax.broadcasted_iota(jnp.int32, (CHUNK, 2 * C_HEADS), 1)
    lane_o = lax.broadcasted_iota(jnp.int32, (CHUNK, LANES), 1)
    tri = _tri(CHUNK)

    def chunk(j, carry):
        rows = pl.ds(pl.multiple_of(j * CHUNK, CHUNK), CHUNK)
        gates = _cap(gt_ref[rows, :] + bg_ref[...])
        lsig = _log_sigmoid(gates)
        i_all = jnp.zeros((CHUNK, LANES), F32)
        lf_all = jnp.zeros((CHUNK, LANES), F32)
        for e in heads:
            he = h * hg + e
            i_all = jnp.where(lane_o == e, jnp.sum(jnp.where(lane == he, gates, 0.0), axis=-1, keepdims=True), i_all)
            lf_all = jnp.where(lane_o == e,
                               jnp.sum(jnp.where(lane == he + C_HEADS, lsig, 0.0), axis=-1, keepdims=True), lf_all)
        f_b = _cumsum_rows(lf_all)
        a_b = i_all - f_b
        a_t = a_b.T
        f_cs = [f_b[:, e:e + 1] for e in heads]
        log_ws = [jnp.where(tri, f_cs[e] + a_t[e:e + 1, :CHUNK], -jnp.inf) for e in heads]
        m_prevs = [m_ref[e, 0:1, 0:1] for e in heads]
        log_ss = [f_cs[e] + m_prevs[e] for e in heads]
        m_ts = [jnp.maximum(jnp.max(log_ws[e], axis=-1, keepdims=True), log_ss[e]) for e in heads]
        qss = [q_ref[rows, e * C_DK:(e + 1) * C_DK] * (C_DK ** -0.5) for e in heads]
        ks = [k_ref[rows, e * C_DK:(e + 1) * C_DK] for e in heads]
        vs = [v_ref[rows, e * C_DV:(e + 1) * C_DV] for e in heads]
        qks = [_bdot_nt(qss[e], ks[e]) * jnp.exp(log_ws[e] - m_ts[e]) for e in heads]
        w_ss = [jnp.exp(log_ss[e] - m_ts[e]) for e in heads]
        c_old = [c_ref[e] for e in heads]
        n_old = [n_ref[e] for e in heads]
        inter = [_bdot(qss[e], c_old[e]) for e in heads]
        intra = [_bdot(qks[e], vs[e]) for e in heads]
        hids = []
        for e in heads:
            den = (jnp.sum(qks[e], axis=-1, keepdims=True)
                   + w_ss[e] * jnp.sum(qss[e] * n_old[e], axis=-1, keepdims=True))
            num = intra[e] + w_ss[e] * inter[e]
            hids.append(_rms(num / jnp.maximum(jnp.abs(den), jnp.exp(-m_ts[e]))) * gn_ref[e])
        for e in heads:
            m_end = m_ts[e][CHUNK - 1:CHUNK, :]
            f_end = f_cs[e][CHUNK - 1:CHUNK, :]
            w_end = jnp.exp(f_end + a_b[:, e:e + 1] - m_end)
            s_end = jnp.exp(f_end + m_prevs[e] - m_end)
            kw = ks[e] * w_end
            c_ref[e] = s_end * c_old[e] + _bdot_tn(kw, vs[e])
            n_ref[e] = s_end * n_old[e] + jnp.sum(kw, axis=0, keepdims=True)
            m_ref[e] = jnp.broadcast_to(m_end, (1, LANES))
        o_ref[rows, :] = (jnp.concatenate(hids, axis=1) * jax.nn.sigmoid(op_ref[rows, :])).astype(o_ref.dtype)
        return carry

    lax.fori_loop(0, n_chunks, chunk, 0)

    @pl.when(l == pl.num_programs(2) - 1)
    def _():
        c_out_ref[0] = c_ref[...]
        n_out_ref[0] = n_ref[...]
        m_out_ref[0] = m_ref[...]


def _mlstm_prompt(proj, gates, b, l, b_gate, g_norm):
    lblk = _pick_tile(l, SEQ_BLOCK, CHUNK)
    nl = l // lblk
    hg = MLSTM_HEADS_PER_STEP
    groups = C_HEADS // hg
    qw, vw = hg * C_DK, hg * C_DV
    v_off = 2 * C_QK_DIM // vw
    o, c, n, m = pl.pallas_call(
        functools.partial(_mlstm_chunk_kernel, n_chunks=lblk // CHUNK),
        grid=(b, groups, nl),
        in_specs=[pl.BlockSpec((lblk, qw), lambda bi, h, li: (bi * nl + li, h)),
                  pl.BlockSpec((lblk, qw), lambda bi, h, li: (bi * nl + li, groups + h)),
                  pl.BlockSpec((lblk, vw), lambda bi, h, li: (bi * nl + li, v_off + h)),
                  pl.BlockSpec((lblk, vw), lambda bi, h, li: (bi * nl + li, v_off + groups + h)),
                  pl.BlockSpec((lblk, 2 * C_HEADS), lambda bi, h, li: (bi * nl + li, 0)),
                  pl.BlockSpec((1, 2 * C_HEADS), lambda bi, h, li: (0, 0)),
                  pl.BlockSpec((hg, 1, C_DV), lambda bi, h, li: (h, 0, 0))],
        out_specs=[pl.BlockSpec((lblk, vw), lambda bi, h, li: (bi * nl + li, h)),
                   pl.BlockSpec((1, hg, C_DK, C_DV), lambda bi, h, li: (bi, h, 0, 0)),
                   pl.BlockSpec((1, hg, 1, C_DK), lambda bi, h, li: (bi, h, 0, 0)),
                   pl.BlockSpec((1, hg, 1, LANES), lambda bi, h, li: (bi, h, 0, 0))],
        out_shape=[jax.ShapeDtypeStruct((b * l, C_V_DIM), BF16),
                   jax.ShapeDtypeStruct((b, C_HEADS, C_DK, C_DV), F32),
                   jax.ShapeDtypeStruct((b, C_HEADS, 1, C_DK), F32),
                   jax.ShapeDtypeStruct((b, C_HEADS, 1, LANES), F32)],
        scratch_shapes=[pltpu.VMEM((hg, C_DK, C_DV), F32), pltpu.VMEM((hg, 1, C_DK), F32),
                        pltpu.VMEM((hg, 1, LANES), F32)],
        compiler_params=_params("parallel", "parallel", "arbitrary"),
        name="mlstm_chunks",
    )(proj, proj, proj, proj, gates, b_gate.reshape(1, -1), g_norm.reshape(C_HEADS, 1, C_DV))
    return o, c, n[:, :, 0, :], m[:, :, 0, 0]


MLSTM_STEP_BATCH = 8


def _mlstm_step_kernel(qk_ref, vo_ref, gt_ref, bg_ref, gn_ref, c_ref, n_ref, m_ref,
                       o_ref, c_out_ref, n_out_ref, m_out_ref):
    iota_h = lax.broadcasted_iota(jnp.int32, (C_HEADS, C_DV), 0)
    iota_hk = lax.broadcasted_iota(jnp.int32, (C_HEADS, C_DK), 0)
    iota_m = lax.broadcasted_iota(jnp.int32, (MLSTM_STEP_BATCH, C_HEADS), 1)
    iota_mb = lax.broadcasted_iota(jnp.int32, (MLSTM_STEP_BATCH, C_HEADS), 0)
    gates = _cap(gt_ref[...] + bg_ref[...])
    m_all = m_ref[...]
    m_new_all = jnp.zeros((MLSTM_STEP_BATCH, C_HEADS), F32)
    for bi in range(MLSTM_STEP_BATCH):
        qs = qk_ref[bi, 0:C_HEADS, :] * (C_DK ** -0.5)
        k = qk_ref[bi, C_HEADS:2 * C_HEADS, :]
        v = vo_ref[bi, C_HEADS:2 * C_HEADS, :]
        op = vo_ref[bi, 2 * C_HEADS:3 * C_HEADS, :]
        q_t, k_t = qs.T, k.T
        n_all = n_ref[bi]
        hid = jnp.zeros((C_HEADS, C_DV), F32)
        n_new = jnp.zeros((C_HEADS, C_DK), F32)
        for h in range(C_HEADS):
            i_g = gates[bi:bi + 1, h:h + 1]
            lf = _log_sigmoid(gates[bi:bi + 1, C_HEADS + h:C_HEADS + h + 1])
            log_s = lf + m_all[bi:bi + 1, h:h + 1]
            m_t = jnp.maximum(i_g, log_s)
            w_i = jnp.exp(i_g - m_t)
            w_s = jnp.exp(log_s - m_t)
            qk = jnp.sum(qs[h:h + 1, :] * k[h:h + 1, :], axis=-1, keepdims=True) * w_i
            c_old = c_ref[bi, h]
            num = qk * v[h:h + 1, :] + w_s * jnp.sum(q_t[:, h:h + 1] * c_old, axis=0, keepdims=True)
            den = qk + w_s * jnp.sum(qs[h:h + 1, :] * n_all[h:h + 1, :], axis=-1, keepdims=True)
            hid = jnp.where(iota_h == h, num / jnp.maximum(jnp.abs(den), jnp.exp(-m_t)), hid)
            c_out_ref[bi, h] = w_s * c_old + (w_i * k_t[:, h:h + 1]) * v[h:h + 1, :]
            n_new = jnp.where(iota_hk == h, w_s * n_all[h:h + 1, :] + w_i * k[h:h + 1, :], n_new)
            m_new_all = jnp.where(jnp.logical_and(iota_m == h, iota_mb == bi), m_t, m_new_all)
        n_out_ref[bi] = n_new
        o_ref[bi] = (_rms(hid) * gn_ref[...] * jax.nn.sigmoid(op)).astype(o_ref.dtype)
    m_out_ref[...] = m_new_all


def _mlstm_step(proj, gates, b_gate, g_norm, c0, n0, m0):
    b = proj.shape[0]
    bb = MLSTM_STEP_BATCH
    qk_view = proj.reshape(b, -1, C_DK)
    vo_view = proj.reshape(b, -1, C_DV)
    o, c, n, m = pl.pallas_call(
        _mlstm_step_kernel,
        grid=(b // bb,),
        in_specs=[pl.BlockSpec((bb, 2 * C_HEADS, C_DK), lambda i: (i, 0, 0)),
                  pl.BlockSpec((bb, 3 * C_HEADS, C_DV), lambda i: (i, 0, 0)),
                  pl.BlockSpec((bb, 2 * C_HEADS), lambda i: (i, 0)),
                  pl.BlockSpec((1, 2 * C_HEADS), lambda i: (0, 0)),
                  pl.BlockSpec((C_HEADS, C_DV), lambda i: (0, 0)),
                  pl.BlockSpec((bb, C_HEADS, C_DK, C_DV), lambda i: (i, 0, 0, 0)),
                  pl.BlockSpec((bb, C_HEADS, C_DK), lambda i: (i, 0, 0)),
                  pl.BlockSpec((bb, C_HEADS), lambda i: (i, 0))],
        out_specs=[pl.BlockSpec((bb, C_HEADS, C_DV), lambda i: (i, 0, 0)),
                   pl.BlockSpec((bb, C_HEADS, C_DK, C_DV), lambda i: (i, 0, 0, 0)),
                   pl.BlockSpec((bb, C_HEADS, C_DK), lambda i: (i, 0, 0)),
                   pl.BlockSpec((bb, C_HEADS), lambda i: (i, 0))],
        out_shape=[jax.ShapeDtypeStruct((b, C_HEADS, C_DV), BF16),
                   jax.ShapeDtypeStruct(c0.shape, F32),
                   jax.ShapeDtypeStruct(n0.shape, F32),
                   jax.ShapeDtypeStruct(m0.shape, F32)],
        compiler_params=_params("parallel"),
        name="mlstm_step",
    )(qk_view, vo_view, gates, b_gate.reshape(1, -1), g_norm, c0, n0, m0)
    return o.reshape(b, C_V_DIM), c, n, m


B_QK_HEADS = 16
B_V_HEADS = 32
B_DK = 128
B_DV = 128
B_CONV = 4
B_QK_DIM = B_QK_HEADS * B_DK
B_V_DIM = B_V_HEADS * B_DV
B_CONV_DIM = 2 * B_QK_DIM + B_V_DIM
B_REP = B_V_HEADS // B_QK_HEADS
CONV_PAD = 8


def _softplus(x):
    return jnp.maximum(x, 0.0) + jnp.log(1.0 + jnp.exp(-jnp.abs(x)))


def _l2norm(x):
    return x * lax.rsqrt(jnp.sum(x * x, axis=-1, keepdims=True) + RMS_EPS)


def _split2(x):
    hi = x.astype(BF16)
    return hi, (x - hi.astype(F32)).astype(BF16)


def _dot3_parts(ah, al, bh, bl):
    n = bh.shape[1]
    r = jnp.dot(ah, jnp.concatenate([bh, bl], axis=1), preferred_element_type=F32)
    return r[:, :n] + r[:, n:] + jnp.dot(al, bh, preferred_element_type=F32)


def _unit_lower_inverses(mats, order):
    n = mats[0].shape[0]
    eye = (lax.broadcasted_iota(jnp.int32, (n, n), 0) == lax.broadcasted_iota(jnp.int32, (n, n), 1)).astype(F32)
    ps = [-a for a in mats]
    ts = [eye + p for p in ps]
    parts = [_split2(p) for p in ps]
    span = 1
    while 2 * span < order:
        ps = [_dot3_parts(ph, pl_, ph, pl_) for ph, pl_ in parts]
        parts = [_split2(p) for p in ps]
        t_parts = [_split2(t) for t in ts]
        ts = [t + _dot3_parts(th, tl, ph, pl_) for t, (th, tl), (ph, pl_) in zip(ts, t_parts, parts)]
        span *= 2
    return ts


GDN_QK_PER_STEP = 8
GDN_V_PER_STEP = GDN_QK_PER_STEP * B_REP


def _gdn_chunk_kernel(q_ref, k_ref, v_ref, z_ref, ba_ref, wq_ref, wk_ref, wv_ref, alog_ref, dtb_ref, gn_ref,
                      o_ref, s_out_ref, s_ref, qbuf, kbuf, vbuf, gcol, bcol, grow, *, lblk):
    h = pl.program_id(1)
    l = pl.program_id(2)
    n_chunks = lblk // CHUNK

    @pl.when(l == 0)
    def _():
        s_ref[...] = jnp.zeros_like(s_ref)
        qbuf[0:CONV_PAD, :] = jnp.zeros((CONV_PAD, qbuf.shape[1]), F32)
        kbuf[0:CONV_PAD, :] = jnp.zeros((CONV_PAD, kbuf.shape[1]), F32)
        vbuf[0:CONV_PAD, :] = jnp.zeros((CONV_PAD, vbuf.shape[1]), F32)

    def conv(x_ref, w_ref, buf, post):
        buf[CONV_PAD:CONV_PAD + lblk, :] = x_ref[...]
        tail = buf[lblk:lblk + CONV_PAD, :]
        taps = B_CONV - 1
        width = buf.shape[1]
        slab = min(width, 4 * LANES)
        for c in reversed(range(n_chunks)):
            lo = CONV_PAD + c * CHUNK
            for s in range(0, width, slab):
                y = w_ref[0:1, s:s + slab] * buf[lo - taps:lo - taps + CHUNK, s:s + slab]
                for j in range(1, B_CONV):
                    y = y + w_ref[j:j + 1, s:s + slab] * buf[lo - taps + j:lo - taps + j + CHUNK, s:s + slab]
                buf[lo:lo + CHUNK, s:s + slab] = post(_silu(y))
        return tail

    def per_head_l2norm(scale):
        def post(y):
            return jnp.concatenate([_l2norm(y[:, i * B_DK:(i + 1) * B_DK]) * scale
                                    for i in range(y.shape[1] // B_DK)], axis=1)
        return post

    q_tail = conv(q_ref, wq_ref, qbuf, per_head_l2norm(B_DK ** -0.5))
    k_tail = conv(k_ref, wk_ref, kbuf, per_head_l2norm(1.0))
    v_tail = conv(v_ref, wv_ref, vbuf, lambda y: y)

    lane = lax.broadcasted_iota(jnp.int32, (lblk, 2 * B_V_HEADS), 1)
    lane_o = lax.broadcasted_iota(jnp.int32, (lblk, LANES), 1)
    ba = ba_ref[...]
    pad = jnp.zeros((1, B_V_HEADS), F32)
    a_scale = -jnp.exp(jnp.concatenate([pad, alog_ref[...]], axis=1))
    dt_b = jnp.concatenate([pad, dtb_ref[...]], axis=1)
    gate_vals = jnp.where(lane < B_V_HEADS, jax.nn.sigmoid(ba), a_scale * _softplus(ba + dt_b))
    beta_all = jnp.zeros((lblk, LANES), F32)
    g_all = jnp.zeros((lblk, LANES), F32)
    for e in range(GDN_V_PER_STEP):
        hv = h * GDN_V_PER_STEP + e
        beta = jnp.sum(jnp.where(lane == hv, gate_vals, 0.0), axis=-1, keepdims=True)
        g = jnp.sum(jnp.where(lane == hv + B_V_HEADS, gate_vals, 0.0), axis=-1, keepdims=True)
        beta_all = jnp.where(lane_o == e, beta, beta_all)
        g_all = jnp.where(lane_o == e, g, g_all)
    bcol[...] = beta_all
    for j in range(n_chunks):
        g_b = _cumsum_rows(g_all[j * CHUNK:(j + 1) * CHUNK])
        gcol[j * CHUNK:(j + 1) * CHUNK, :] = g_b
        grow[j] = g_b.T[0:grow.shape[1], 0:CHUNK]

    tri = _tri(CHUNK)
    strict = _tri(CHUNK, strict=True)
    zero_blk = jnp.zeros((CHUNK, CHUNK), F32)

    def chunk(j, carry):
        rows = pl.ds(pl.multiple_of(j * CHUNK, CHUNK), CHUNK)
        brow = pl.ds(pl.multiple_of(j * CHUNK + CONV_PAD, 8), CHUNK)
        g_rows = grow[j]
        heads_q = range(GDN_QK_PER_STEP)
        heads_v = range(GDN_V_PER_STEP)
        qs = [qbuf[brow, i * B_DK:(i + 1) * B_DK] for i in heads_q]
        ks = [kbuf[brow, i * B_DK:(i + 1) * B_DK] for i in heads_q]
        kks = [_bdot_nt(k, k) for k in ks]
        qks = [_bdot_nt(q, k) for q, k in zip(qs, ks)]
        g_cs = [gcol[rows, e:e + 1] for e in heads_v]
        betas = [bcol[rows, e:e + 1] for e in heads_v]
        decs = [jnp.exp(jnp.where(tri, g_cs[e] - g_rows[e:e + 1, :], -jnp.inf)) for e in heads_v]
        a_blks = [jnp.where(strict, betas[e] * decs[e] * kks[e // B_REP], 0.0) for e in heads_v]
        rhs = [jnp.concatenate([betas[e] * vbuf[brow, e * B_DV:(e + 1) * B_DV],
                                (betas[e] * jnp.exp(g_cs[e])) * ks[e // B_REP]], axis=-1) for e in heads_v]
        a_bds = [jnp.concatenate(
            [jnp.concatenate([a_blks[i * B_REP + r] if r == c else zero_blk for c in range(B_REP)], axis=1)
             for r in range(B_REP)], axis=0) for i in heads_q]
        t_invs = _unit_lower_inverses(a_bds, CHUNK)
        t_parts = [_split2(t) for t in t_invs]
        r_parts = [_split2(jnp.concatenate(rhs[i * B_REP:(i + 1) * B_REP], axis=0)) for i in heads_q]
        sols = [_dot3_parts(th, tl, rh, rl) for (th, tl), (rh, rl) in zip(t_parts, r_parts)]
        sol_e = [sols[e // B_REP][(e % B_REP) * CHUNK:(e % B_REP + 1) * CHUNK] for e in heads_v]
        s_old = [s_ref[e] for e in heads_v]
        ws = [sol_e[e][:, :B_DV] - _bdot(sol_e[e][:, B_DV:], s_old[e]) for e in heads_v]
        o_inter = [_bdot(qs[e // B_REP] * jnp.exp(g_cs[e]), s_old[e]) for e in heads_v]
        o_intra = [_bdot(decs[e] * qks[e // B_REP], ws[e]) for e in heads_v]
        for e in heads_v:
            g_end = g_cs[e][CHUNK - 1:CHUNK, :]
            s_ref[e] = jnp.exp(g_end) * s_old[e] + _bdot_tn(ks[e // B_REP] * jnp.exp(g_end - g_cs[e]), ws[e])
        for e in heads_v:
            zc = z_ref[rows, e * B_DV:(e + 1) * B_DV]
            o = o_intra[e] + o_inter[e]
            o_ref[rows, e * B_DV:(e + 1) * B_DV] = (_rms(o) * gn_ref[...] * _silu(zc)).astype(o_ref.dtype)
        return carry

    lax.fori_loop(0, n_chunks, chunk, 0)
    qbuf[0:CONV_PAD, :] = q_tail
    kbuf[0:CONV_PAD, :] = k_tail
    vbuf[0:CONV_PAD, :] = v_tail

    @pl.when(l == pl.num_programs(2) - 1)
    def _():
        s_out_ref[0] = s_ref[...]


def _gdn_prompt(proj, ba, b, l, conv_w, a_log, dt_bias, g_norm):
    lblk = _pick_tile(l, SEQ_BLOCK, CHUNK)
    nl = l // lblk
    qw = GDN_QK_PER_STEP * B_DK
    vw = GDN_V_PER_STEP * B_DV
    k_off = B_QK_DIM // qw
    v_off = 2 * B_QK_DIM // vw
    z_off = B_CONV_DIM // vw
    return pl.pallas_call(
        functools.partial(_gdn_chunk_kernel, lblk=lblk),
        grid=(b, B_QK_HEADS // GDN_QK_PER_STEP, nl),
        in_specs=[pl.BlockSpec((lblk, qw), lambda bi, h, li: (bi * nl + li, h)),
                  pl.BlockSpec((lblk, qw), lambda bi, h, li: (bi * nl + li, k_off + h)),
                  pl.BlockSpec((lblk, vw), lambda bi, h, li: (bi * nl + li, v_off + h)),
                  pl.BlockSpec((lblk, vw), lambda bi, h, li: (bi * nl + li, z_off + h)),
                  pl.BlockSpec((lblk, 2 * B_V_HEADS), lambda bi, h, li: (bi * nl + li, 0)),
                  pl.BlockSpec((B_CONV, qw), lambda bi, h, li: (0, h)),
                  pl.BlockSpec((B_CONV, qw), lambda bi, h, li: (0, k_off + h)),
                  pl.BlockSpec((B_CONV, vw), lambda bi, h, li: (0, v_off + h)),
                  pl.BlockSpec((1, B_V_HEADS), lambda bi, h, li: (0, 0)),
                  pl.BlockSpec((1, B_V_HEADS), lambda bi, h, li: (0, 0)),
                  pl.BlockSpec((1, B_DV), lambda bi, h, li: (0, 0))],
        out_specs=[pl.BlockSpec((lblk, vw), lambda bi, h, li: (bi * nl + li, h)),
                   pl.BlockSpec((1, GDN_V_PER_STEP, B_DK, B_DV), lambda bi, h, li: (bi, h, 0, 0))],
        out_shape=[jax.ShapeDtypeStruct((b * l, B_V_DIM), BF16),
                   jax.ShapeDtypeStruct((b, B_V_HEADS, B_DK, B_DV), F32)],
        scratch_shapes=[pltpu.VMEM((GDN_V_PER_STEP, B_DK, B_DV), F32),
                        pltpu.VMEM((CONV_PAD + lblk, qw), F32),
                        pltpu.VMEM((CONV_PAD + lblk, qw), F32),
                        pltpu.VMEM((CONV_PAD + lblk, vw), F32),
                        pltpu.VMEM((lblk, LANES), F32),
                        pltpu.VMEM((lblk, LANES), F32),
                        pltpu.VMEM((lblk // CHUNK, max(8, GDN_V_PER_STEP), CHUNK), F32)],
        compiler_params=_params("parallel", "parallel", "arbitrary"),
        name="gdn_chunks",
    )(proj, proj, proj, proj, ba, conv_w, conv_w, conv_w,
      a_log.reshape(1, -1), dt_bias.reshape(1, -1), g_norm.reshape(1, -1))


GDN_STEP_BATCH = 2
B_ROWS = B_CONV_DIM // 128


def _gdn_step_kernel(p_ref, cv_ref, cw_ref, ba_ref, alog_ref, dtb_ref, gn_ref, s_ref, o_ref, s_out_ref):
    ba = ba_ref[0]
    beta_all = jax.nn.sigmoid(ba[:, :B_V_HEADS])
    g_all = -jnp.exp(alog_ref[...]) * _softplus(ba[:, B_V_HEADS:] + dtb_ref[...])
    decay_all = jnp.exp(g_all)
    iota_h = lax.broadcasted_iota(jnp.int32, (B_V_HEADS, B_DV), 0)
    for bi in range(GDN_STEP_BATCH):
        y = cw_ref[B_CONV - 1] * p_ref[bi, 0:B_ROWS, :]
        for j in range(B_CONV - 1):
            y = y + cw_ref[j] * cv_ref[bi, j]
        y = _silu(y)
        q = _l2norm(y[0:B_QK_HEADS]) * (B_DK ** -0.5)
        k = _l2norm(y[B_QK_HEADS:2 * B_QK_HEADS])
        v = y[2 * B_QK_HEADS:]
        z = p_ref[bi, B_ROWS:B_ROWS + B_V_HEADS, :]
        q_t, k_t = q.T, k.T
        qk = jnp.sum(q * k, axis=-1, keepdims=True)
        o = jnp.zeros((B_V_HEADS, B_DV), F32)
        for hv in range(B_V_HEADS):
            hq = hv // B_REP
            beta = beta_all[bi:bi + 1, hv:hv + 1]
            decay = decay_all[bi:bi + 1, hv:hv + 1]
            s = s_ref[bi, hv]
            k_col = k_t[:, hq:hq + 1]
            w = beta * v[hv:hv + 1, :] - (beta * decay) * jnp.sum(k_col * s, axis=0, keepdims=True)
            oh = qk[hq:hq + 1, :] * w + decay * jnp.sum(q_t[:, hq:hq + 1] * s, axis=0, keepdims=True)
            s_out_ref[bi, hv] = decay * s + k_col * w
            o = jnp.where(iota_h == hv, oh, o)
        o_ref[bi] = (_rms(o) * gn_ref[...] * _silu(z)).astype(o_ref.dtype)


def _gdn_step(proj, ba, conv0, conv_w, a_log, dt_bias, g_norm, s0):
    b = proj.shape[0]
    bb = GDN_STEP_BATCH
    p3 = proj.reshape(b, -1, 128)
    o, s = pl.pallas_call(
        _gdn_step_kernel,
        grid=(b // bb,),
        in_specs=[pl.BlockSpec((bb, p3.shape[1], 128), lambda i: (i, 0, 0)),
                  pl.BlockSpec((bb, B_CONV - 1, B_ROWS, 128), lambda i: (i, 0, 0, 0)),
                  pl.BlockSpec((B_CONV, B_ROWS, 128), lambda i: (0, 0, 0)),
                  pl.BlockSpec((1, bb, 2 * B_V_HEADS), lambda i: (i, 0, 0)),
                  pl.BlockSpec((1, B_V_HEADS), lambda i: (0, 0)),
                  pl.BlockSpec((1, B_V_HEADS), lambda i: (0, 0)),
                  pl.BlockSpec((1, B_DV), lambda i: (0, 0)),
                  pl.BlockSpec((bb, B_V_HEADS, B_DK, B_DV), lambda i: (i, 0, 0, 0))],
        out_specs=[pl.BlockSpec((bb, B_V_HEADS, B_DV), lambda i: (i, 0, 0)),
                   pl.BlockSpec((bb, B_V_HEADS, B_DK, B_DV), lambda i: (i, 0, 0, 0))],
        out_shape=[jax.ShapeDtypeStruct((b, B_V_HEADS, B_DV), BF16), jax.ShapeDtypeStruct(s0.shape, F32)],
        compiler_params=_params("parallel"),
        name="gdn_step",
    )(p3, conv0.reshape(b, B_CONV - 1, B_ROWS, 128), conv_w.reshape(B_CONV, B_ROWS, 128),
      ba.reshape(b // bb, bb, 2 * B_V_HEADS), a_log.reshape(1, -1), dt_bias.reshape(1, -1), g_norm.reshape(1, -1), s0)
    return o.reshape(b, B_V_DIM), s


D_GROUP = 16
D_GROUPS = D_MODEL // D_GROUP
D_STATE = 64
S5_LANES = 128
S5_GROUPS = S5_LANES // D_GROUP
S5_HALF = S5_GROUPS * D_STATE
S5_BLOCKS = D_MODEL // S5_LANES
S5_SEGMENTS = 8


def _cmul(ar, ai, br, bi):
    return ar * br - ai * bi, ar * bi + ai * br


def _gelu_tanh(x):
    return 0.5 * x * (1.0 + jnp.tanh(math.sqrt(2.0 / math.pi) * (x + 0.044715 * (x * x * x))))


def _s5_tables(a_re, a_im, b_re, b_im, c_re, c_im, log_dt, seg_len):
    lam_re, lam_im = a_re.astype(F32), a_im.astype(F32)
    dt = jnp.exp(log_dt.astype(F32))[:, None]
    mag = jnp.exp(dt * lam_re)
    ab_re, ab_im = mag * jnp.cos(dt * lam_im), mag * jnp.sin(dt * lam_im)
    inv = 1.0 / (lam_re * lam_re + lam_im * lam_im)
    e_re, e_im = _cmul(ab_re - 1.0, ab_im, lam_re * inv, -lam_im * inv)
    bb_re, bb_im = _cmul(e_re[..., None], e_im[..., None], b_re.astype(F32), b_im.astype(F32))
    j = S5_BLOCKS
    eye = jnp.eye(S5_GROUPS, dtype=F32)

    def pack_rows(t):
        return t.reshape(j, 1, S5_HALF)

    def bdiag_in(t):
        t = t.reshape(j, S5_GROUPS, D_STATE, D_GROUP)
        return jnp.einsum('jgpc,gh->jgchp', t, eye).reshape(j, S5_LANES, S5_HALF)

    def bdiag_out(t):
        t = t.reshape(j, S5_GROUPS, D_GROUP, D_STATE)
        return jnp.einsum('jgcp,gh->jgphc', t, eye).reshape(j, S5_HALF, S5_LANES)

    ab = jnp.concatenate([pack_rows(ab_re), pack_rows(ab_im)], axis=-1)
    w_bu = jnp.concatenate([bdiag_in(bb_re), bdiag_in(bb_im)], axis=-1)
    w_c = jnp.concatenate([bdiag_out(c_re.astype(F32)), -bdiag_out(c_im.astype(F32))], axis=1)
    pr, pi = pack_rows(ab_re), pack_rows(ab_im)
    n = 1
    while n < seg_len:
        tr, ti = pr[:, n - 1:n], pi[:, n - 1:n]
        nr, ni = _cmul(pr, pi, tr, ti)
        pr, pi = jnp.concatenate([pr, nr], axis=1), jnp.concatenate([pi, ni], axis=1)
        n *= 2
    pw = jnp.concatenate([pr[:, :seg_len], pi[:, :seg_len]], axis=-1)
    return ab, w_bu, w_c, pw


def _s5_scan_kernel(x_ref, ab_ref, wbu_ref, wc_ref, pw_ref, dsk_ref, y_ref, s_out_ref, xp, buf, yp, *, seq, seg):
    nc = S5_HALF // 128
    ns = S5_SEGMENTS

    def tile(tau):
        return pl.ds(pl.multiple_of(tau * ns, ns), ns)

    def interleave(tau, c):
        xp[tile(tau), :] = x_ref[pl.ds(tau, ns, stride=seg), :]
        return c

    lax.fori_loop(0, seg, interleave, 0, unroll=8)
    x = xp[...]
    bu = _bdot(x, wbu_ref[0])
    for c in range(2 * nc):
        buf[c] = bu[:, c * 128:(c + 1) * 128]

    a_res = [jnp.broadcast_to(ab_ref[0, :, c * 128:(c + 1) * 128], (ns, 128)) for c in range(nc)]
    a_ims = [jnp.broadcast_to(ab_ref[0, :, S5_HALF + c * 128:S5_HALF + (c + 1) * 128], (ns, 128))
             for c in range(nc)]

    def step(tau, carry):
        rows = tile(tau)
        out = []
        for c in range(nc):
            s_re, s_im = carry[c]
            n_re = a_res[c] * s_re - a_ims[c] * s_im + buf[c, rows, :]
            n_im = a_res[c] * s_im + a_ims[c] * s_re + buf[nc + c, rows, :]
            buf[c, rows, :] = n_re
            buf[nc + c, rows, :] = n_im
            out.append((n_re, n_im))
        return tuple(out)

    zeros = jnp.zeros((ns, 128), F32)
    local_end = lax.fori_loop(0, seg, step, tuple((zeros, zeros) for _ in range(nc)), unroll=2)

    iota_r = lax.broadcasted_iota(jnp.int32, (ns, 128), 0)
    enter = []
    for c in range(nc):
        l_re, l_im = local_end[c]
        pe_re = pw_ref[0, seg - 1:seg, c * 128:(c + 1) * 128]
        pe_im = pw_ref[0, seg - 1:seg, S5_HALF + c * 128:S5_HALF + (c + 1) * 128]
        e_re, e_im = l_re[0:1], l_im[0:1]
        in_re, in_im = zeros, zeros
        for r in range(1, ns):
            in_re = jnp.where(iota_r == r, e_re, in_re)
            in_im = jnp.where(iota_r == r, e_im, in_im)
            t_re, t_im = _cmul(pe_re, pe_im, e_re, e_im)
            e_re, e_im = l_re[r:r + 1] + t_re, l_im[r:r + 1] + t_im
        enter.append((in_re, in_im))

    def fix(group, carry):
        taus = pl.ds(pl.multiple_of(group * 8, 8), 8)
        for c in range(nc):
            p_re8 = pw_ref[0, taus, c * 128:(c + 1) * 128]
            p_im8 = pw_ref[0, taus, S5_HALF + c * 128:S5_HALF + (c + 1) * 128]
            for u in range(8):
                rows = tile(group * 8 + u)
                f_re, f_im = _cmul(p_re8[u:u + 1], p_im8[u:u + 1], enter[c][0], enter[c][1])
                buf[c, rows, :] = buf[c, rows, :] + f_re
                buf[nc + c, rows, :] = buf[nc + c, rows, :] + f_im
        return carry

    lax.fori_loop(0, seg // 8, fix, 0)

    y = dsk_ref[...] * x
    for c in range(2 * nc):
        s_out_ref[0, 0, :, c * 128:(c + 1) * 128] = buf[c, seq - 1:seq, :]
        y = y + _bdot(buf[c], wc_ref[0, c * 128:(c + 1) * 128, :])
    yp[...] = _gelu_tanh(y)
    for r in range(ns):
        y_ref[r * seg:(r + 1) * seg, :] = yp[pl.ds(r, seg, stride=ns), :].astype(y_ref.dtype)


def _unpack_state(s):
    b = s.shape[0]
    return (s[..., :S5_HALF].reshape(b, D_GROUPS, D_STATE), s[..., S5_HALF:].reshape(b, D_GROUPS, D_STATE))


def _s5_prompt(x, b, l, tables, d_skip):
    d = x.shape[1]
    ab, w_bu, w_c, pw = tables
    seg = l // S5_SEGMENTS
    y, s = pl.pallas_call(
        functools.partial(_s5_scan_kernel, seq=l, seg=seg),
        grid=(b, S5_BLOCKS),
        in_specs=[pl.BlockSpec((l, S5_LANES), lambda bi, j: (bi, j)),
                  pl.BlockSpec((1, 1, 2 * S5_HALF), lambda bi, j: (j, 0, 0)),
                  pl.BlockSpec((1, S5_LANES, 2 * S5_HALF), lambda bi, j: (j, 0, 0)),
                  pl.BlockSpec((1, 2 * S5_HALF, S5_LANES), lambda bi, j: (j, 0, 0)),
                  pl.BlockSpec((1, seg, 2 * S5_HALF), lambda bi, j: (j, 0, 0)),
                  pl.BlockSpec((1, S5_LANES), lambda bi, j: (0, j))],
        out_specs=[pl.BlockSpec((l, S5_LANES), lambda bi, j: (bi, j)),
                   pl.BlockSpec((1, 1, 1, 2 * S5_HALF), lambda bi, j: (bi, j, 0, 0))],
        out_shape=[jax.ShapeDtypeStruct((b * l, d), BF16),
                   jax.ShapeDtypeStruct((b, S5_BLOCKS, 1, 2 * S5_HALF), F32)],
        scratch_shapes=[pltpu.VMEM((l, S5_LANES), F32),
                        pltpu.VMEM((2 * S5_HALF // 128, l, 128), F32),
                        pltpu.VMEM((l, S5_LANES), F32)],
        compiler_params=_params("parallel", "parallel"),
        name="s5_scan",
    )(x, ab, w_bu, w_c, pw, d_skip.reshape(1, d))
    return y, _unpack_state(s[:, :, 0, :])


def _s5_step_kernel(x_ref, s_ref, ab_ref, wbu_ref, wc_ref, dsk_ref, y_ref, s_out_ref):
    hf = S5_HALF
    x = x_ref[...]
    bu = _bdot(x, wbu_ref[0])
    s0 = s_ref[0]
    i_re, i_im = _cmul(ab_ref[0, :, :hf], ab_ref[0, :, hf:], s0[:, :hf], s0[:, hf:])
    s_new = jnp.concatenate([i_re, i_im], axis=-1) + bu
    s_out_ref[0] = s_new
    y = _bdot(s_new, wc_ref[0]) + dsk_ref[...] * x
    y_ref[...] = _gelu_tanh(y).astype(y_ref.dtype)


def _s5_step(x, tables, d_skip, s0_re, s0_im):
    b, d = x.shape
    ab, w_bu, w_c, _ = tables
    s0 = jnp.concatenate([s0_re.reshape(b, S5_BLOCKS, S5_HALF), s0_im.reshape(b, S5_BLOCKS, S5_HALF)], axis=-1)
    s0 = jnp.swapaxes(s0, 0, 1)
    y, s = pl.pallas_call(
        _s5_step_kernel,
        grid=(S5_BLOCKS,),
        in_specs=[pl.BlockSpec((b, S5_LANES), lambda j: (0, j)),
                  pl.BlockSpec((1, b, 2 * S5_HALF), lambda j: (j, 0, 0)),
                  pl.BlockSpec((1, 1, 2 * S5_HALF), lambda j: (j, 0, 0)),
                  pl.BlockSpec((1, S5_LANES, 2 * S5_HALF), lambda j: (j, 0, 0)),
                  pl.BlockSpec((1, 2 * S5_HALF, S5_LANES), lambda j: (j, 0, 0)),
                  pl.BlockSpec((1, S5_LANES), lambda j: (0, j))],
        out_specs=[pl.BlockSpec((b, S5_LANES), lambda j: (0, j)),
                   pl.BlockSpec((1, b, 2 * S5_HALF), lambda j: (j, 0, 0))],
        out_shape=[jax.ShapeDtypeStruct((b, d), BF16), jax.ShapeDtypeStruct((S5_BLOCKS, b, 2 * S5_HALF), F32)],
        compiler_params=_params("parallel"),
        name="s5_step",
    )(x, s0, ab, w_bu, w_c, d_skip.reshape(1, d))
    return y, _unpack_state(jnp.swapaxes(s, 0, 1))


def _glu_kernel(x_ref, w1_ref, w2_ref, o_ref):
    x = x_ref[...]
    y1 = jnp.dot(x, w1_ref[...].astype(BF16), preferred_element_type=F32)
    y2 = jnp.dot(x, w2_ref[...].astype(BF16), preferred_element_type=F32)
    o_ref[...] = y1 * jax.nn.sigmoid(y2)


def _glu_matmul(x, w, tm_target=1664, tn_target=256):
    m, k = x.shape
    n = w.shape[1] // 2
    tm = _pick_tile(m, tm_target, 16)
    tn = _pick_tile(n, tn_target, 128)
    nb = n // tn
    return pl.pallas_call(
        _glu_kernel,
        grid=(m // tm, nb),
        in_specs=[pl.BlockSpec((tm, k), lambda i, j: (i, 0)),
                  pl.BlockSpec((k, tn), lambda i, j: (0, j)),
                  pl.BlockSpec((k, tn), lambda i, j: (0, j + nb))],
        out_specs=pl.BlockSpec((tm, tn), lambda i, j: (i, j)),
        out_shape=jax.ShapeDtypeStruct((m, n), F32),
        compiler_params=_params("parallel", "parallel"),
        name="glu_matmul",
    )(x, w, w)


def kernel(x_prompt, x_sample, state_a_S, state_b_S, state_b_conv, state_c_C, state_c_n, state_c_m, state_d_re, state_d_im, a_w_in, a_lb_logits, a_g_norm, a_w_out, b_w_in, b_conv_w, b_a_log, b_dt_bias, b_g_norm, b_w_out, c_w_in, c_b_gate, c_g_norm, c_w_out, d_a_re, d_a_im, d_b_re, d_b_im, d_c_re, d_c_im, d_skip, d_log_dt, d_w_glu, moe_w_router, moe_b_router, moe_w_gu, moe_w_down, moe_w_gu_s, moe_w_down_s, ln_mix_g, ln_mix_b, ln_ffn_g, ln_ffn_b):
    bp, l, d = x_prompt.shape
    bs = x_sample.shape[0]
    tp = bp * l
    x = jnp.concatenate([x_prompt.reshape(tp, d), x_sample.reshape(bs, d)], axis=0)
    xb = x.astype(BF16)

    def finish_layer(x, h, layer):
        x, xb, xpk = _add_ln(x, h, ln_mix_g[layer], ln_mix_b[layer])
        return _moe_ln(x, xb, xpk, moe_w_router, moe_b_router, moe_w_gu, moe_w_down, moe_w_gu_s, moe_w_down_s,
                       ln_ffn_g, ln_ffn_b, layer)

    proj = _matmul(xb, a_w_in)
    lb = _hgrn2_lower_bound(a_lb_logits, 0)
    o_p, a_s_p = _hgrn2_prompt(proj, bp, l, lb, a_g_norm)
    o_s, a_s_s = _hgrn2_step(proj[tp:], state_a_S, lb, a_g_norm)
    h = _matmul(jnp.concatenate([o_p, o_s], axis=0), a_w_out)
    x, xb = finish_layer(x, h, 0)

    n_main = B_CONV_DIM + B_V_DIM
    proj = _matmul(xb, b_w_in, n_cols=n_main)
    ba = _matmul(xb, b_w_in[:, n_main:])
    o_p, b_s_p = _gdn_prompt(proj, ba, bp, l, b_conv_w, b_a_log, b_dt_bias, b_g_norm)
    o_s, b_s_s = _gdn_step(proj[tp:], ba[tp:], state_b_conv, b_conv_w, b_a_log, b_dt_bias, b_g_norm, state_b_S)
    qkv_s = proj[tp:, :B_CONV_DIM]
    b_conv_p = jnp.stack([proj[(i + 1) * l - (B_CONV - 1):(i + 1) * l, :B_CONV_DIM] for i in range(bp)], axis=0)
    b_conv_s = jnp.concatenate([state_b_conv[:, 1:, :], qkv_s[:, None, :]], axis=1)
    h = _matmul(jnp.concatenate([o_p, o_s], axis=0), b_w_out)
    x, xb = finish_layer(x, h, 1)

    n_main = 2 * C_QK_DIM + 2 * C_V_DIM
    proj = _matmul(xb, c_w_in, n_cols=n_main)
    gates = _matmul(xb, c_w_in[:, n_main:])
    o_p, c_c_p, c_n_p, c_m_p = _mlstm_prompt(proj, gates, bp, l, c_b_gate, c_g_norm)
    o_s, c_c_s, c_n_s, c_m_s = _mlstm_step(proj[tp:], gates[tp:], c_b_gate, c_g_norm, state_c_C, state_c_n, state_c_m)
    h = _matmul(jnp.concatenate([o_p, o_s], axis=0), c_w_out)
    x, xb = finish_layer(x, h, 2)

    tables = _s5_tables(d_a_re, d_a_im, d_b_re, d_b_im, d_c_re, d_c_im, d_log_dt, l // S5_SEGMENTS)
    y_p, (d_re_p, d_im_p) = _s5_prompt(x, bp, l, tables, d_skip)
    y_s, (d_re_s, d_im_s) = _s5_step(x[tp:], tables, d_skip, state_d_re, state_d_im)
    h = _glu_matmul(jnp.concatenate([y_p, y_s], axis=0), d_w_glu)
    x, xb = finish_layer(x, h, 3)

    y_prompt = x[:tp].reshape(bp, l, d)
    y_sample = x[tp:].reshape(bs, 1, d)
    return (y_prompt, y_sample, a_s_p, a_s_s, b_s_p, b_s_s, b_conv_p, b_conv_s,
            c_c_p, c_c_s, c_n_p, c_n_s, c_m_p, c_m_s, d_re_p, d_re_s, d_im_p, d_im_s)
```

```python
import functools
import math

import jax
import jax.numpy as jnp
from jax import lax
from jax.experimental import pallas as pl
from jax.experimental.pallas import tpu as pltpu

F32 = jnp.float32
BF16 = jnp.bfloat16
HI = lax.Precision.HIGHEST

D_MODEL = 2048
DEPTH = 4
CHUNK = 64
DN_ALPHA = (2 * DEPTH) ** 0.25
LN_EPS = 1e-5
RMS_EPS = 1e-6

VMEM_LIMIT_BYTES = 56 * 1024 * 1024


def _params(*sem):
    return pltpu.CompilerParams(dimension_semantics=sem, vmem_limit_bytes=VMEM_LIMIT_BYTES)


def _pick_tile(n, target, mult):
    best = None
    for t in range(mult, min(n, target) + 1, mult):
        if n % t == 0:
            best = t
    return best if best is not None else n


def _mm_kernel(x_ref, w_ref, o_ref):
    o_ref[...] = jnp.dot(x_ref[...], w_ref[...].astype(BF16), preferred_element_type=F32).astype(o_ref.dtype)


def _matmul(x, w, n_cols=None, out_dtype=F32, tm_target=1664, tn_target=512):
    m, k = x.shape
    n = w.shape[1] if n_cols is None else n_cols
    tm = _pick_tile(m, tm_target, 16)
    tn = _pick_tile(n, tn_target, 128)
    return pl.pallas_call(
        _mm_kernel,
        grid=(m // tm, n // tn),
        in_specs=[pl.BlockSpec((tm, k), lambda i, j: (i, 0)), pl.BlockSpec((k, tn), lambda i, j: (0, j))],
        out_specs=pl.BlockSpec((tm, tn), lambda i, j: (i, j)),
        out_shape=jax.ShapeDtypeStruct((m, n), out_dtype),
        compiler_params=_params("parallel", "parallel"),
        name="matmul",
    )(x, w)


def _layernorm_rows(z, g, b):
    mu = jnp.mean(z, axis=-1, keepdims=True)
    zc = z - mu
    var = jnp.mean(zc * zc, axis=-1, keepdims=True)
    return zc * lax.rsqrt(var + LN_EPS) * g + b


U32 = jnp.uint32
BF16_HIGH_MASK = 0xFFFF0000


def _pack_bf16_halves(yb):
    half = yb.shape[1] // 2
    bits = lax.bitcast_convert_type(yb.astype(F32), U32)
    return (bits[:, :half] >> 16) | (bits[:, half:] & jnp.uint32(BF16_HIGH_MASK))


def _unpack_bf16_halves(w):
    lo = lax.bitcast_convert_type(w << 16, F32).astype(BF16)
    hi = lax.bitcast_convert_type(w & jnp.uint32(BF16_HIGH_MASK), F32).astype(BF16)
    return lo, hi


def _add_ln_kernel(x_ref, h_ref, g_ref, b_ref, o_ref, ob_ref, op_ref):
    y = _layernorm_rows(DN_ALPHA * x_ref[...] + h_ref[...], g_ref[...], b_ref[...])
    yb = y.astype(BF16)
    o_ref[...] = y
    ob_ref[...] = yb
    op_ref[...] = _pack_bf16_halves(yb)


def _add_ln(x, h, g, b, tm_target=640):
    m, d = x.shape
    tm = _pick_tile(m, tm_target, 16)
    row = pl.BlockSpec((tm, d), lambda i: (i, 0))
    vec = pl.BlockSpec((1, d), lambda i: (0, 0))
    return pl.pallas_call(
        _add_ln_kernel,
        grid=(m // tm,),
        in_specs=[row, row, vec, vec],
        out_specs=[row, row, pl.BlockSpec((tm, d // 2), lambda i: (i, 0))],
        out_shape=[jax.ShapeDtypeStruct((m, d), F32), jax.ShapeDtypeStruct((m, d), BF16),
                   jax.ShapeDtypeStruct((m, d // 2), U32)],
        compiler_params=_params("parallel"),
        name="add_ln",
    )(x, h, g.reshape(1, d), b.reshape(1, d))


def _silu(x):
    return x * jax.nn.sigmoid(x)


def _ffn_kernel(x_ref, wgu_ref, wd_ref, o_ref, *, dh):
    gu = jnp.dot(x_ref[...], wgu_ref[0].astype(BF16), preferred_element_type=F32)
    a = _silu(gu[:, :dh]) * gu[:, dh:]
    o_ref[...] = jnp.dot(a.astype(BF16), wd_ref[0].astype(BF16), preferred_element_type=F32)


def _dense_ffn(xb, w_gu, w_down, layer, tm_target=640):
    m, d = xb.shape
    dh = w_down.shape[1]
    tm = _pick_tile(m, tm_target, 16)
    return pl.pallas_call(
        functools.partial(_ffn_kernel, dh=dh),
        grid=(m // tm,),
        in_specs=[pl.BlockSpec((tm, d), lambda i: (i, 0)),
                  pl.BlockSpec((1, d, 2 * dh), lambda i: (layer, 0, 0)),
                  pl.BlockSpec((1, dh, d), lambda i: (layer, 0, 0))],
        out_specs=pl.BlockSpec((tm, d), lambda i: (i, 0)),
        out_shape=jax.ShapeDtypeStruct((m, d), F32),
        compiler_params=_params("parallel"),
        name="shared_ffn",
    )(xb, w_gu, w_down)


N_EXPERTS = 64
TOP_K = 8
N_ROUTE_GROUPS = 8
TOPK_ROUTE_GROUPS = 4
GROUP_SIZE = N_EXPERTS // N_ROUTE_GROUPS
ROUTED_SCALE = 2.5
EXPERT_TILE = 512


def _router_kernel(x_ref, wt_ref, b_ref, e_ref, g_ref):
    logits = lax.dot_general(wt_ref[...].astype(BF16), x_ref[...], (((1,), (1,)), ((), ())),
                             preferred_element_type=F32)
    scores = jax.nn.sigmoid(logits)
    biased = scores + b_ref[...]
    tm = scores.shape[1]
    neg = jnp.float32(-jnp.inf)
    iota_g = lax.broadcasted_iota(jnp.int32, (GROUP_SIZE, tm), 0)
    gscore = jnp.zeros((N_ROUTE_GROUPS, tm), F32)
    iota_r = lax.broadcasted_iota(jnp.int32, (N_ROUTE_GROUPS, tm), 0)
    for g in range(N_ROUTE_GROUPS):
        v = biased[g * GROUP_SIZE:(g + 1) * GROUP_SIZE, :]
        m1 = jnp.max(v, axis=0, keepdims=True)
        i1 = jnp.min(jnp.where(v == m1, iota_g, GROUP_SIZE), axis=0, keepdims=True)
        m2 = jnp.max(jnp.where(iota_g == i1, neg, v), axis=0, keepdims=True)
        gscore = jnp.where(iota_r == g, m1 + m2, gscore)
    ok = jnp.zeros((N_ROUTE_GROUPS, tm), jnp.int32)
    for _ in range(TOPK_ROUTE_GROUPS):
        m = jnp.max(gscore, axis=0, keepdims=True)
        gi = jnp.min(jnp.where(gscore == m, iota_r, N_ROUTE_GROUPS), axis=0, keepdims=True)
        hit = iota_r == gi
        ok = jnp.where(hit, 1, ok)
        gscore = jnp.where(hit, neg, gscore)
    masked = jnp.concatenate(
        [jnp.where(ok[g:g + 1, :] > 0, biased[g * GROUP_SIZE:(g + 1) * GROUP_SIZE, :], neg)
         for g in range(N_ROUTE_GROUPS)], axis=0)
    iota_e = lax.broadcasted_iota(jnp.int32, (N_EXPERTS, tm), 0)
    iota_k = lax.broadcasted_iota(jnp.int32, (TOP_K, tm), 0)
    top_e = jnp.zeros((TOP_K, tm), jnp.int32)
    gate = jnp.zeros((TOP_K, tm), F32)
    for k in range(TOP_K):
        m = jnp.max(masked, axis=0, keepdims=True)
        ei = jnp.min(jnp.where(masked == m, iota_e, N_EXPERTS), axis=0, keepdims=True)
        hit = iota_e == ei
        gk = jnp.sum(jnp.where(hit, scores, 0.0), axis=0, keepdims=True)
        top_e = jnp.where(iota_k == k, ei, top_e)
        gate = jnp.where(iota_k == k, gk, gate)
        masked = jnp.where(hit, neg, masked)
    e_ref[...] = top_e
    g_ref[...] = ROUTED_SCALE * gate / jnp.sum(gate, axis=0, keepdims=True)


def _router(x, w_router_t, b_router, tm_target=640):
    t, d = x.shape
    tm = _pick_tile(t, tm_target, 128)
    return pl.pallas_call(
        _router_kernel,
        grid=(t // tm,),
        in_specs=[pl.BlockSpec((tm, d), lambda i: (i, 0)),
                  pl.BlockSpec((N_EXPERTS, d), lambda i: (0, 0)),
                  pl.BlockSpec((N_EXPERTS, 1), lambda i: (0, 0))],
        out_specs=[pl.BlockSpec((TOP_K, tm), lambda i: (0, i)), pl.BlockSpec((TOP_K, tm), lambda i: (0, i))],
        out_shape=[jax.ShapeDtypeStruct((TOP_K, t), jnp.int32), jax.ShapeDtypeStruct((TOP_K, t), F32)],
        compiler_params=_params("parallel"),
        name="router",
    )(x, w_router_t, b_router.reshape(N_EXPERTS, 1))


LANES = 128


def _dispatch_kernel(e_ref, pos_ref, cnt_ref, rank_ref):
    n_blocks = e_ref.shape[1] // LANES
    iota_e = lax.broadcasted_iota(jnp.int32, (N_EXPERTS, LANES), 0)
    incl = (lax.broadcasted_iota(jnp.int32, (LANES, LANES), 0)
            <= lax.broadcasted_iota(jnp.int32, (LANES, LANES), 1)).astype(BF16)
    iota_k = lax.broadcasted_iota(jnp.int32, (TOP_K, LANES), 0)

    def pick(e_blk, table):
        out = jnp.zeros((TOP_K, LANES), F32)
        for k in range(TOP_K):
            v = jnp.sum(jnp.where(iota_e == e_blk[k:k + 1, :], table, 0.0), axis=0, keepdims=True)
            out = jnp.where(iota_k == k, v, out)
        return out

    def count_block(b, carry):
        cols = pl.ds(pl.multiple_of(b * LANES, LANES), LANES)
        e_blk = e_ref[:, cols]
        member = jnp.zeros((N_EXPERTS, LANES), F32)
        for k in range(TOP_K):
            member = member + (iota_e == e_blk[k:k + 1, :]).astype(F32)
        seen = jnp.dot(member.astype(BF16), incl, preferred_element_type=F32) + carry
        rank_ref[:, cols] = pick(e_blk, seen - member)
        return jnp.broadcast_to(seen[:, LANES - 1:LANES], (N_EXPERTS, LANES))

    counts = lax.fori_loop(0, n_blocks, count_block, jnp.zeros((N_EXPERTS, LANES), F32))
    cnt_ref[...] = counts.astype(jnp.int32)
    padded = jnp.ceil(counts * (1.0 / EXPERT_TILE)) * EXPERT_TILE
    start = jnp.dot(_tri(N_EXPERTS, strict=True).astype(F32), padded, preferred_element_type=F32, precision=HI)

    def place_block(b, carry):
        cols = pl.ds(pl.multiple_of(b * LANES, LANES), LANES)
        pos_ref[:, cols] = (pick(e_ref[:, cols], start) + rank_ref[:, cols]).astype(jnp.int32)
        return carry

    lax.fori_loop(0, n_blocks, place_block, 0)


def _dispatch_tables(top_e_t, n_tiles):
    k, t = top_e_t.shape
    pos_t, cnt = pl.pallas_call(
        _dispatch_kernel,
        out_shape=[jax.ShapeDtypeStruct((k, t), jnp.int32), jax.ShapeDtypeStruct((N_EXPERTS, LANES), jnp.int32)],
        scratch_shapes=[pltpu.VMEM((k, t), F32)],
        compiler_params=pltpu.CompilerParams(vmem_limit_bytes=VMEM_LIMIT_BYTES),
        name="dispatch",
    )(top_e_t)
    counts = cnt[:, 0]
    pends = jnp.cumsum((counts + EXPERT_TILE - 1) // EXPERT_TILE * EXPERT_TILE)
    n_used = pends[-1] // EXPERT_TILE
    tile_idx = jnp.arange(n_tiles, dtype=jnp.int32)
    tile_e = jnp.minimum(jnp.sum((pends[None, :] <= (tile_idx * EXPERT_TILE)[:, None]).astype(jnp.int32), axis=1),
                         N_EXPERTS - 1)
    tile_e = jnp.where(tile_idx < n_used, tile_e, tile_e[jnp.maximum(n_used - 1, 0)])
    last_tile = jnp.where(counts > 0, pends // EXPERT_TILE - 1, -1).astype(jnp.int32)
    return pos_t.reshape(-1), tile_e.astype(jnp.int32), n_used.astype(jnp.int32).reshape(1), last_tile


SCATTER_TILE = 640


def _scatter_rows_kernel(pos_ref, last_tile_ref, n_used_ref, x_ref, xs_ref, zbuf, zsem, sem, *, n_tok, max_idle):
    tt = x_ref.shape[0]
    n_tiles = xs_ref.shape[0] // EXPERT_TILE
    n_used = n_used_ref[0]

    @pl.when(pl.program_id(0) == 0)
    def _():
        zbuf[...] = jnp.zeros_like(zbuf)

        def fill(tile):
            return pltpu.make_async_copy(zbuf, xs_ref.at[pl.ds(tile * EXPERT_TILE, EXPERT_TILE)], zsem)

        def for_each_padded_tile(action):
            for e in range(N_EXPERTS):
                tile = last_tile_ref[e]

                @pl.when(tile >= 0)
                def _(tile=tile):
                    action(fill(tile))
            for j in range(max_idle):
                @pl.when(n_used + j < n_tiles)
                def _(j=j):
                    action(fill(n_used + j))

        for_each_padded_tile(lambda copy: copy.start())
        for_each_padded_tile(lambda copy: copy.wait())

    base = pl.program_id(0) * tt

    def body(r, c):
        for k in range(TOP_K):
            p = pos_ref[k * n_tok + base + r]
            pltpu.make_async_copy(x_ref.at[pl.ds(r, 1)], xs_ref.at[pl.ds(p, 1)], sem).start()
        return c

    lax.fori_loop(0, tt, body, 0, unroll=2)
    for k in range(TOP_K):
        pltpu.make_async_copy(x_ref, xs_ref.at[pl.ds(0, tt)], sem).wait()


def _sort_rows(x, pos, last_tile, n_used, n_tiles):
    t, d = x.shape
    tt = _pick_tile(t, SCATTER_TILE, 8)
    max_idle = n_tiles - (t * TOP_K) // EXPERT_TILE
    return pl.pallas_call(
        functools.partial(_scatter_rows_kernel, n_tok=t, max_idle=max_idle),
        grid_spec=pltpu.PrefetchScalarGridSpec(
            num_scalar_prefetch=3, grid=(t // tt,),
            in_specs=[pl.BlockSpec((tt, d), lambda i, p, lt, nu: (i, 0))],
            out_specs=pl.BlockSpec(memory_space=pl.ANY),
            scratch_shapes=[pltpu.VMEM((EXPERT_TILE, d), x.dtype), pltpu.SemaphoreType.DMA(()),
                            pltpu.SemaphoreType.DMA(())]),
        out_shape=jax.ShapeDtypeStruct((n_tiles * EXPERT_TILE, d), x.dtype),
        compiler_params=_params("arbitrary"),
        name="scatter_rows",
    )(pos, last_tile, n_used, x)


def _expert_kernel(tile_e_ref, n_used_ref, xs_ref, wgu_ref, wd_ref, o_ref, wgu_b, wd_b, *, dh):
    i = pl.program_id(0)
    n_used = n_used_ref[0]
    e = tile_e_ref[i]
    e_prev = tile_e_ref[jnp.maximum(i - 1, 0)]

    @pl.when(jnp.logical_and(i < n_used, jnp.logical_or(i == 0, e != e_prev)))
    def _():
        wgu_b[...] = wgu_ref[0, 0].astype(BF16)
        wd_b[...] = wd_ref[0, 0].astype(BF16)

    @pl.when(i < n_used)
    def _():
        x_lo, x_hi = _unpack_bf16_halves(xs_ref[...])
        half = x_lo.shape[1]
        gu = (jnp.dot(x_lo, wgu_b[:half, :], preferred_element_type=F32)
              + jnp.dot(x_hi, wgu_b[half:, :], preferred_element_type=F32))
        a = _silu(gu[:, :dh]) * gu[:, dh:]
        o_ref[...] = jnp.dot(a.astype(BF16), wd_b[...], preferred_element_type=F32)

    @pl.when(i >= n_used)
    def _():
        o_ref[...] = jnp.zeros_like(o_ref)


def _routed_experts(xs, tile_e, n_used, w_gu, w_down, layer):
    d = w_gu.shape[2]
    dh = w_down.shape[2]
    n_tiles = tile_e.shape[0]
    grid_spec = pltpu.PrefetchScalarGridSpec(
        num_scalar_prefetch=2,
        grid=(n_tiles,),
        in_specs=[pl.BlockSpec((EXPERT_TILE, d // 2), lambda i, te, nu: (jnp.minimum(i, nu[0] - 1), 0)),
                  pl.BlockSpec((1, 1, d, 2 * dh), lambda i, te, nu: (layer, te[i], 0, 0)),
                  pl.BlockSpec((1, 1, dh, d), lambda i, te, nu: (layer, te[i], 0, 0))],
        out_specs=pl.BlockSpec((EXPERT_TILE, d), lambda i, te, nu: (i, 0)),
        scratch_shapes=[pltpu.VMEM((d, 2 * dh), BF16), pltpu.VMEM((dh, d), BF16)],
    )
    return pl.pallas_call(
        functools.partial(_expert_kernel, dh=dh),
        grid_spec=grid_spec,
        out_shape=jax.ShapeDtypeStruct((n_tiles * EXPERT_TILE, d), F32),
        compiler_params=_params("arbitrary"),
        name="routed_experts",
    )(tile_e, n_used, xs, w_gu, w_down)


COMBINE_TILE = 128


def _combine_kernel(pos_ref, rows_hbm, gate_ref, x_ref, sh_ref, g_ref, b_ref, o_ref, ob_ref, buf, sem):
    i = pl.program_id(0)
    n = pl.num_programs(0)
    slot = i % 2

    n_tok = n * COMBINE_TILE

    def issue(tile, s):
        base = tile * COMBINE_TILE

        def body(r, c):
            for k in range(TOP_K):
                p = pos_ref[k * n_tok + base + r]
                pltpu.make_async_copy(rows_hbm.at[pl.ds(p, 1)], buf.at[s, k, pl.ds(r, 1)], sem.at[s]).start()
            return c

        lax.fori_loop(0, COMBINE_TILE, body, 0, unroll=2)

    @pl.when(i == 0)
    def _():
        issue(0, 0)

    @pl.when(i + 1 < n)
    def _():
        issue(i + 1, 1 - slot)

    for k in range(TOP_K):
        pltpu.make_async_copy(rows_hbm.at[pl.ds(0, COMBINE_TILE)], buf.at[slot, k], sem.at[slot]).wait()
    gate = gate_ref[...]
    z = DN_ALPHA * x_ref[...] + sh_ref[...]
    for k in range(TOP_K):
        z = z + gate[:, k:k + 1] * buf[slot, k]
    y = _layernorm_rows(z, g_ref[...], b_ref[...])
    o_ref[...] = y
    ob_ref[...] = y.astype(BF16)


def _combine_ln(rows, pos, gate, x, shared, g, b):
    t, d = x.shape
    row = lambda w: pl.BlockSpec((COMBINE_TILE, w), lambda i, p: (i, 0))
    vec = pl.BlockSpec((1, d), lambda i, p: (0, 0))
    grid_spec = pltpu.PrefetchScalarGridSpec(
        num_scalar_prefetch=1,
        grid=(t // COMBINE_TILE,),
        in_specs=[pl.BlockSpec(memory_space=pl.ANY), row(TOP_K), row(d), row(d), vec, vec],
        out_specs=[row(d), row(d)],
        scratch_shapes=[pltpu.VMEM((2, TOP_K, COMBINE_TILE, d), F32), pltpu.SemaphoreType.DMA((2,))],
    )
    return pl.pallas_call(
        _combine_kernel,
        grid_spec=grid_spec,
        out_shape=[jax.ShapeDtypeStruct((t, d), F32), jax.ShapeDtypeStruct((t, d), BF16)],
        compiler_params=_params("arbitrary"),
        name="combine_ln",
    )(pos, rows, gate, x, shared, g.reshape(1, d), b.reshape(1, d))


def _moe_ln(x, xb, xpk, w_router, b_router, w_gu, w_down, w_gu_s, w_down_s, ln_g, ln_b, layer):
    t = x.shape[0]
    top_e_t, gate_t = _router(xb, w_router[layer].T, b_router[layer])
    n_tiles = (t * TOP_K + N_EXPERTS * (EXPERT_TILE - 1) + EXPERT_TILE - 1) // EXPERT_TILE
    pos, tile_e, n_used, last_tile = _dispatch_tables(top_e_t, n_tiles)
    xs = _sort_rows(xpk, pos, last_tile, n_used, n_tiles)
    rows = _routed_experts(xs, tile_e, n_used, w_gu, w_down, layer)
    shared = _dense_ffn(xb, w_gu_s, w_down_s, layer)
    return _combine_ln(rows, pos, gate_t.T, x, shared, ln_g[layer], ln_b[layer])


SUB = 16
SEQ_BLOCK = 512


def _tri(n, strict=False):
    r = lax.broadcasted_iota(jnp.int32, (n, n), 0)
    c = lax.broadcasted_iota(jnp.int32, (n, n), 1)
    return (r > c) if strict else (r >= c)


def _cumsum_rows(x):
    n = x.shape[0]
    return jnp.dot(_tri(n).astype(F32), x, preferred_element_type=F32, precision=HI)


def _bdot(a, b):
    return jnp.dot(a.astype(BF16), b.astype(BF16), preferred_element_type=F32)


def _bdot_nt(a, b):
    return lax.dot_general(a.astype(BF16), b.astype(BF16), (((1,), (1,)), ((), ())), preferred_element_type=F32)


def _bdot_tn(a, b):
    return lax.dot_general(a.astype(BF16), b.astype(BF16), (((0,), (0,)), ((), ())), preferred_element_type=F32)


def _rms(x):
    return x * lax.rsqrt(jnp.mean(x * x, axis=-1, keepdims=True) + RMS_EPS)


A_HEADS = 16
A_DK = 128
A_DV = 128
HGRN2_HEADS_PER_STEP = 8


def _hgrn2_chunk_kernel(q_ref, f_ref, v_ref, g_ref, lb_ref, gn_ref, o_ref, s_out_ref, s_ref, *, n_chunks):
    l = pl.program_id(2)

    @pl.when(l == 0)
    def _():
        s_ref[...] = jnp.zeros_like(s_ref)

    lb = lb_ref[...]
    gn = gn_ref[...]
    heads = range(HGRN2_HEADS_PER_STEP)

    def head(x, i):
        return x[:, i * A_DK:(i + 1) * A_DK]

    def chunk(j, carry):
        rows = pl.ds(pl.multiple_of(j * CHUNK, CHUNK), CHUNK)
        q = _silu(q_ref[rows, :])
        f = lb + (1.0 - lb) * jax.nn.sigmoid(f_ref[rows, :])
        k = 1.0 - f
        v = v_ref[rows, :]
        bc = _cumsum_rows(jnp.log(f))
        qe = q * jnp.exp(bc)
        s_old = [s_ref[i] for i in heads]
        o_inter = [_bdot(head(qe, i), s_old[i]) for i in heads]
        outs = [[] for _ in heads]
        for sc in range(CHUNK // SUB):
            lo, hi = sc * SUB, (sc + 1) * SUB
            bref = bc[lo - 1:lo, :] if sc > 0 else jnp.zeros((1, bc.shape[1]), F32)
            qi = q[lo:hi] * jnp.exp(bc[lo:hi] - bref)
            ki = k[:hi] * jnp.exp(bref - bc[:hi])
            causal = (lax.broadcasted_iota(jnp.int32, (SUB, hi), 0) + lo
                      >= lax.broadcasted_iota(jnp.int32, (SUB, hi), 1))
            atts = [jnp.where(causal, _bdot_nt(head(qi, i), head(ki, i)), 0.0) for i in heads]
            for i in heads:
                outs[i].append(_bdot(atts[i], head(v, i)[:hi]))
        b_end = bc[CHUNK - 1:CHUNK, :]
        kd = k * jnp.exp(b_end - bc)
        d_cols = jnp.exp(bc.T[:, CHUNK - 1:CHUNK])
        for i in heads:
            s_ref[i] = d_cols[i * A_DK:(i + 1) * A_DK] * s_old[i] + _bdot_tn(head(kd, i), head(v, i))
        o_n = jnp.concatenate([_rms(jnp.concatenate(outs[i], axis=0) + o_inter[i]) * gn for i in heads], axis=1)
        o_ref[rows, :] = (o_n * _silu(g_ref[rows, :])).astype(o_ref.dtype)
        return carry

    lax.fori_loop(0, n_chunks, chunk, 0)

    @pl.when(l == pl.num_programs(2) - 1)
    def _():
        s_out_ref[0] = s_ref[...]


def _hgrn2_prompt(proj, b, l, lb, g_norm):
    lblk = _pick_tile(l, SEQ_BLOCK, CHUNK)
    nl = l // lblk
    hg = HGRN2_HEADS_PER_STEP
    groups = A_HEADS // hg
    part = lambda p: pl.BlockSpec((lblk, hg * A_DK), lambda bi, h, li, p=p: (bi * nl + li, p * groups + h))
    return pl.pallas_call(
        functools.partial(_hgrn2_chunk_kernel, n_chunks=lblk // CHUNK),
        grid=(b, groups, nl),
        in_specs=[part(0), part(1), part(2), part(3),
                  pl.BlockSpec((1, hg * A_DK), lambda bi, h, li: (0, h)),
                  pl.BlockSpec((1, A_DV), lambda bi, h, li: (0, 0))],
        out_specs=[pl.BlockSpec((lblk, hg * A_DV), lambda bi, h, li: (bi * nl + li, h)),
                   pl.BlockSpec((1, hg, A_DK, A_DV), lambda bi, h, li: (bi, h, 0, 0))],
        out_shape=[jax.ShapeDtypeStruct((b * l, A_HEADS * A_DV), BF16),
                   jax.ShapeDtypeStruct((b, A_HEADS, A_DK, A_DV), F32)],
        scratch_shapes=[pltpu.VMEM((hg, A_DK, A_DV), F32)],
        compiler_params=_params("parallel", "parallel", "arbitrary"),
        name="hgrn2_chunks",
    )(proj, proj, proj, proj, lb.reshape(1, -1), g_norm.reshape(1, -1))


HGRN2_STEP_BATCH = 2


def _hgrn2_step_kernel(p_ref, s_ref, lb_ref, gn_ref, o_ref, s_out_ref):
    lb = lb_ref[...]
    iota_h = lax.broadcasted_iota(jnp.int32, (A_HEADS, A_DV), 0)
    for bi in range(HGRN2_STEP_BATCH):
        p = p_ref[bi]
        q = _silu(p[0:A_HEADS])
        f = lb + (1.0 - lb) * jax.nn.sigmoid(p[A_HEADS:2 * A_HEADS])
        v = p[2 * A_HEADS:3 * A_HEADS]
        g = p[3 * A_HEADS:4 * A_HEADS]
        q_t, f_t = q.T, f.T
        o = jnp.zeros((A_HEADS, A_DV), F32)
        for h in range(A_HEADS):
            fc = f_t[:, h:h + 1]
            s_new = fc * s_ref[bi, h] + (1.0 - fc) * v[h:h + 1, :]
            s_out_ref[bi, h] = s_new
            oh = jnp.sum(q_t[:, h:h + 1] * s_new, axis=0, keepdims=True)
            o = jnp.where(iota_h == h, oh, o)
        o_ref[bi] = (_rms(o) * gn_ref[...] * _silu(g)).astype(o_ref.dtype)


def _hgrn2_step(proj, s0, lb, g_norm):
    b = proj.shape[0]
    bb = HGRN2_STEP_BATCH
    p3 = proj.reshape(b, 4 * A_HEADS, A_DK)
    o, s = pl.pallas_call(
        _hgrn2_step_kernel,
        grid=(b // bb,),
        in_specs=[pl.BlockSpec((bb, 4 * A_HEADS, A_DK), lambda i: (i, 0, 0)),
                  pl.BlockSpec((bb, A_HEADS, A_DK, A_DV), lambda i: (i, 0, 0, 0)),
                  pl.BlockSpec((A_HEADS, A_DK), lambda i: (0, 0)),
                  pl.BlockSpec((1, A_DV), lambda i: (0, 0))],
        out_specs=[pl.BlockSpec((bb, A_HEADS, A_DV), lambda i: (i, 0, 0)),
                   pl.BlockSpec((bb, A_HEADS, A_DK, A_DV), lambda i: (i, 0, 0, 0))],
        out_shape=[jax.ShapeDtypeStruct((b, A_HEADS, A_DV), BF16),
                   jax.ShapeDtypeStruct(s0.shape, F32)],
        compiler_params=_params("parallel"),
        name="hgrn2_step",
    )(p3, s0, lb.reshape(A_HEADS, A_DK), g_norm.reshape(1, -1))
    return o.reshape(b, A_HEADS * A_DV), s


def _hgrn2_lower_bound(lb_logits, layer):
    return jnp.cumsum(jax.nn.softmax(lb_logits.astype(F32), axis=0), axis=0)[layer]


C_HEADS = 8
C_DK = 128
C_DV = 256
C_QK_DIM = C_HEADS * C_DK
C_V_DIM = C_HEADS * C_DV
C_GATE_CAP = 15.0
MLSTM_HEADS_PER_STEP = 4


def _log_sigmoid(x):
    return jnp.minimum(x, 0.0) - jnp.log(1.0 + jnp.exp(-jnp.abs(x)))


def _cap(x):
    return C_GATE_CAP * jnp.tanh(x / C_GATE_CAP)


def _mlstm_chunk_kernel(q_ref, k_ref, v_ref, op_ref, gt_ref, bg_ref, gn_ref,
                        o_ref, c_out_ref, n_out_ref, m_out_ref, c_ref, n_ref, m_ref, *, n_chunks):
    h = pl.program_id(1)
    l = pl.program_id(2)

    @pl.when(l == 0)
    def _():
        c_ref[...] = jnp.zeros_like(c_ref)
        n_ref[...] = jnp.zeros_like(n_ref)
        m_ref[...] = jnp.zeros_like(m_ref)

    hg = MLSTM_HEADS_PER_STEP
    heads = range(hg)
    lane = lax.broadcasted_iota(jnp.int32, (CHUNK, 2 * C_HEADS), 1)
    lane_o = lax.broadcasted_iota(jnp.int32, (CHUNK, LANES), 1)
    tri = _tri(CHUNK)

    def chunk(j, carry):
        rows = pl.ds(pl.multiple_of(j * CHUNK, CHUNK), CHUNK)
        gates = _cap(gt_ref[rows, :] + bg_ref[...])
        lsig = _log_sigmoid(gates)
        i_all = jnp.zeros((CHUNK, LANES), F32)
        lf_all = jnp.zeros((CHUNK, LANES), F32)
        for e in heads:
            he = h * hg + e
            i_all = jnp.where(lane_o == e, jnp.sum(jnp.where(lane == he, gates, 0.0), axis=-1, keepdims=True), i_all)
            lf_all = jnp.where(lane_o == e,
                               jnp.sum(jnp.where(lane == he + C_HEADS, lsig, 0.0), axis=-1, keepdims=True), lf_all)
        f_b = _cumsum_rows(lf_all)
        a_b = i_all - f_b
        a_t = a_b.T
        f_cs = [f_b[:, e:e + 1] for e in heads]
        log_ws = [jnp.where(tri, f_cs[e] + a_t[e:e + 1, :CHUNK], -jnp.inf) for e in heads]
        m_prevs = [m_ref[e, 0:1, 0:1] for e in heads]
        log_ss = [f_cs[e] + m_prevs[e] for e in heads]
        m_ts = [jnp.maximum(jnp.max(log_ws[e], axis=-1, keepdims=True), log_ss[e]) for e in heads]
        qss = [q_ref[rows, e * C_DK:(e + 1) * C_DK] * (C_DK ** -0.5) for e in heads]
        ks = [k_ref[rows, e * C_DK:(e + 1) * C_DK] for e in heads]
        vs = [v_ref[rows, e * C_DV:(e + 1) * C_DV] for e in heads]
        qks = [_bdot_nt(qss[e], ks[e]) * jnp.exp(log_ws[e] - m_ts[e]) for e in heads]
        w_ss = [jnp.exp(log_ss[e] - m_ts[e]) for e in heads]
        c_old = [c_ref[e] for e in heads]
        n_old = [n_ref[e] for e in heads]
        inter = [_bdot(qss[e], c_old[e]) for e in heads]
        intra = [_bdot(qks[e], vs[e]) for e in heads]
        hids = []
        for e in heads:
            den = (jnp.sum(qks[e], axis=-1, keepdims=True)
                   + w_ss[e] * jnp.sum(qss[e] * n_old[e], axis=-1, keepdims=True))
            num = intra[e] + w_ss[e] * inter[e]
            hids.append(_rms(num / jnp.maximum(jnp.abs(den), jnp.exp(-m_ts[e]))) * gn_ref[e])
        for e in heads:
            m_end = m_ts[e][CHUNK - 1:CHUNK, :]
            f_end = f_cs[e][CHUNK - 1:CHUNK, :]
            w_end = jnp.exp(f_end + a_b[:, e:e + 1] - m_end)
            s_end = jnp.exp(f_end + m_prevs[e] - m_end)
            kw = ks[e] * w_end
            c_ref[e] = s_end * c_old[e] + _bdot_tn(kw, vs[e])
            n_ref[e] = s_end * n_old[e] + jnp.sum(kw, axis=0, keepdims=True)
            m_ref[e] = jnp.broadcast_to(m_end, (1, LANES))
        o_ref[rows, :] = (jnp.concatenate(hids, axis=1) * jax.nn.sigmoid(op_ref[rows, :])).astype(o_ref.dtype)
        return carry

    lax.fori_loop(0, n_chunks, chunk, 0)

    @pl.when(l == pl.num_programs(2) - 1)
    def _():
        c_out_ref[0] = c_ref[...]
        n_out_ref[0] = n_ref[...]
        m_out_ref[0] = m_ref[...]


def _mlstm_prompt(proj, gates, b, l, b_gate, g_norm):
    lblk = _pick_tile(l, SEQ_BLOCK, CHUNK)
    nl = l // lblk
    hg = MLSTM_HEADS_PER_STEP
    groups = C_HEADS // hg
    qw, vw = hg * C_DK, hg * C_DV
    v_off = 2 * C_QK_DIM // vw
    o, c, n, m = pl.pallas_call(
        functools.partial(_mlstm_chunk_kernel, n_chunks=lblk // CHUNK),
        grid=(b, groups, nl),
        in_specs=[pl.BlockSpec((lblk, qw), lambda bi, h, li: (bi * nl + li, h)),
                  pl.BlockSpec((lblk, qw), lambda bi, h, li: (bi * nl + li, groups + h)),
                  pl.BlockSpec((lblk, vw), lambda bi, h, li: (bi * nl + li, v_off + h)),
                  pl.BlockSpec((lblk, vw), lambda bi, h, li: (bi * nl + li, v_off + groups + h)),
                  pl.BlockSpec((lblk, 2 * C_HEADS), lambda bi, h, li: (bi * nl + li, 0)),
                  pl.BlockSpec((1, 2 * C_HEADS), lambda bi, h, li: (0, 0)),
                  pl.BlockSpec((hg, 1, C_DV), lambda bi, h, li: (h, 0, 0))],
        out_specs=[pl.BlockSpec((lblk, vw), lambda bi, h, li: (bi * nl + li, h)),
                   pl.BlockSpec((1, hg, C_DK, C_DV), lambda bi, h, li: (bi, h, 0, 0)),
                   pl.BlockSpec((1, hg, 1, C_DK), lambda bi, h, li: (bi, h, 0, 0)),
                   pl.BlockSpec((1, hg, 1, LANES), lambda bi, h, li: (bi, h, 0, 0))],
        out_shape=[jax.ShapeDtypeStruct((b * l, C_V_DIM), BF16),
                   jax.ShapeDtypeStruct((b, C_HEADS, C_DK, C_DV), F32),
                   jax.ShapeDtypeStruct((b, C_HEADS, 1, C_DK), F32),
                   jax.ShapeDtypeStruct((b, C_HEADS, 1, LANES), F32)],
        scratch_shapes=[pltpu.VMEM((hg, C_DK, C_DV), F32), pltpu.VMEM((hg, 1, C_DK), F32),
                        pltpu.VMEM((hg, 1, LANES), F32)],
        compiler_params=_params("parallel", "parallel", "arbitrary"),
        name="mlstm_chunks",
    )(proj, proj, proj, proj, gates, b_gate.reshape(1, -1), g_norm.reshape(C_HEADS, 1, C_DV))
    return o, c, n[:, :, 0, :], m[:, :, 0, 0]


MLSTM_STEP_BATCH = 8


def _mlstm_step_kernel(qk_ref, vo_ref, gt_ref, bg_ref, gn_ref, c_ref, n_ref, m_ref,
                       o_ref, c_out_ref, n_out_ref, m_out_ref):
    iota_h = lax.broadcasted_iota(jnp.int32, (C_HEADS, C_DV), 0)
    iota_hk = lax.broadcasted_iota(jnp.int32, (C_HEADS, C_DK), 0)
    iota_m = lax.broadcasted_iota(jnp.int32, (MLSTM_STEP_BATCH, C_HEADS), 1)
    iota_mb = lax.broadcasted_iota(jnp.int32, (MLSTM_STEP_BATCH, C_HEADS), 0)
    gates = _cap(gt_ref[...] + bg_ref[...])
    m_all = m_ref[...]
    m_new_all = jnp.zeros((MLSTM_STEP_BATCH, C_HEADS), F32)
    for bi in range(MLSTM_STEP_BATCH):
        qs = qk_ref[bi, 0:C_HEADS, :] * (C_DK ** -0.5)
        k = qk_ref[bi, C_HEADS:2 * C_HEADS, :]
        v = vo_ref[bi, C_HEADS:2 * C_HEADS, :]
        op = vo_ref[bi, 2 * C_HEADS:3 * C_HEADS, :]
        q_t, k_t = qs.T, k.T
        n_all = n_ref[bi]
        hid = jnp.zeros((C_HEADS, C_DV), F32)
        n_new = jnp.zeros((C_HEADS, C_DK), F32)
        for h in range(C_HEADS):
            i_g = gates[bi:bi + 1, h:h + 1]
            lf = _log_sigmoid(gates[bi:bi + 1, C_HEADS + h:C_HEADS + h + 1])
            log_s = lf + m_all[bi:bi + 1, h:h + 1]
            m_t = jnp.maximum(i_g, log_s)
            w_i = jnp.exp(i_g - m_t)
            w_s = jnp.exp(log_s - m_t)
            qk = jnp.sum(qs[h:h + 1, :] * k[h:h + 1, :], axis=-1, keepdims=True) * w_i
            c_old = c_ref[bi, h]
            num = qk * v[h:h + 1, :] + w_s * jnp.sum(q_t[:, h:h + 1] * c_old, axis=0, keepdims=True)
            den = qk + w_s * jnp.sum(qs[h:h + 1, :] * n_all[h:h + 1, :], axis=-1, keepdims=True)
            hid = jnp.where(iota_h == h, num / jnp.maximum(jnp.abs(den), jnp.exp(-m_t)), hid)
            c_out_ref[bi, h] = w_s * c_old + (w_i * k_t[:, h:h + 1]) * v[h:h + 1, :]
            n_new = jnp.where(iota_hk == h, w_s * n_all[h:h + 1, :] + w_i * k[h:h + 1, :], n_new)
            m_new_all = jnp.where(jnp.logical_and(iota_m == h, iota_mb == bi), m_t, m_new_all)
        n_out_ref[bi] = n_new
        o_ref[bi] = (_rms(hid) * gn_ref[...] * jax.nn.sigmoid(op)).astype(o_ref.dtype)
    m_out_ref[...] = m_new_all


def _mlstm_step(proj, gates, b_gate, g_norm, c0, n0, m0):
    b = proj.shape[0]
    bb = MLSTM_STEP_BATCH
    qk_view = proj.reshape(b, -1, C_DK)
    vo_view = proj.reshape(b, -1, C_DV)
    o, c, n, m = pl.pallas_call(
        _mlstm_step_kernel,
        grid=(b // bb,),
        in_specs=[pl.BlockSpec((bb, 2 * C_HEADS, C_DK), lambda i: (i, 0, 0)),
                  pl.BlockSpec((bb, 3 * C_HEADS, C_DV), lambda i: (i, 0, 0)),
                  pl.BlockSpec((bb, 2 * C_HEADS), lambda i: (i, 0)),
                  pl.BlockSpec((1, 2 * C_HEADS), lambda i: (0, 0)),
                  pl.BlockSpec((C_HEADS, C_DV), lambda i: (0, 0)),
                  pl.BlockSpec((bb, C_HEADS, C_DK, C_DV), lambda i: (i, 0, 0, 0)),
                  pl.BlockSpec((bb, C_HEADS, C_DK), lambda i: (i, 0, 0)),
                  pl.BlockSpec((bb, C_HEADS), lambda i: (i, 0))],
        out_specs=[pl.BlockSpec((bb, C_HEADS, C_DV), lambda i: (i, 0, 0)),
                   pl.BlockSpec((bb, C_HEADS, C_DK, C_DV), lambda i: (i, 0, 0, 0)),
                   pl.BlockSpec((bb, C_HEADS, C_DK), lambda i: (i, 0, 0)),
                   pl.BlockSpec((bb, C_HEADS), lambda i: (i, 0))],
        out_shape=[jax.ShapeDtypeStruct((b, C_HEADS, C_DV), BF16),
                   jax.ShapeDtypeStruct(c0.shape, F32),
                   jax.ShapeDtypeStruct(n0.shape, F32),
                   jax.ShapeDtypeStruct(m0.shape, F32)],
        compiler_params=_params("parallel"),
        name="mlstm_step",
    )(qk_view, vo_view, gates, b_gate.reshape(1, -1), g_norm, c0, n0, m0)
    return o.reshape(b, C_V_DIM), c, n, m


B_QK_HEADS = 16
B_V_HEADS = 32
B_DK = 128
B_DV = 128
B_CONV = 4
B_QK_DIM = B_QK_HEADS * B_DK
B_V_DIM = B_V_HEADS * B_DV
B_CONV_DIM = 2 * B_QK_DIM + B_V_DIM
B_REP = B_V_HEADS // B_QK_HEADS
CONV_PAD = 8


def _softplus(x):
    return jnp.maximum(x, 0.0) + jnp.log(1.0 + jnp.exp(-jnp.abs(x)))


def _l2norm(x):
    return x * lax.rsqrt(jnp.sum(x * x, axis=-1, keepdims=True) + RMS_EPS)


def _split2(x):
    hi = x.astype(BF16)
    return hi, (x - hi.astype(F32)).astype(BF16)


def _dot3_parts(ah, al, bh, bl):
    n = bh.shape[1]
    r = jnp.dot(ah, jnp.concatenate([bh, bl], axis=1), preferred_element_type=F32)
    return r[:, :n] + r[:, n:] + jnp.dot(al, bh, preferred_element_type=F32)


def _unit_lower_inverses(mats, order):
    n = mats[0].shape[0]
    eye = (lax.broadcasted_iota(jnp.int32, (n, n), 0) == lax.broadcasted_iota(jnp.int32, (n, n), 1)).astype(F32)
    ps = [-a for a in mats]
    ts = [eye + p for p in ps]
    parts = [_split2(p) for p in ps]
    span = 1
    while 2 * span < order:
        ps = [_dot3_parts(ph, pl_, ph, pl_) for ph, pl_ in parts]
        parts = [_split2(p) for p in ps]
        t_parts = [_split2(t) for t in ts]
        ts = [t + _dot3_parts(th, tl, ph, pl_) for t, (th, tl), (ph, pl_) in zip(ts, t_parts, parts)]
        span *= 2
    return ts


GDN_QK_PER_STEP = 8
GDN_V_PER_STEP = GDN_QK_PER_STEP * B_REP


def _gdn_chunk_kernel(q_ref, k_ref, v_ref, z_ref, ba_ref, wq_ref, wk_ref, wv_ref, alog_ref, dtb_ref, gn_ref,
                      o_ref, s_out_ref, s_ref, qbuf, kbuf, vbuf, gcol, bcol, grow, *, lblk):
    h = pl.program_id(1)
    l = pl.program_id(2)
    n_chunks = lblk // CHUNK

    @pl.when(l == 0)
    def _():
        s_ref[...] = jnp.zeros_like(s_ref)
        qbuf[0:CONV_PAD, :] = jnp.zeros((CONV_PAD, qbuf.shape[1]), F32)
        kbuf[0:CONV_PAD, :] = jnp.zeros((CONV_PAD, kbuf.shape[1]), F32)
        vbuf[0:CONV_PAD, :] = jnp.zeros((CONV_PAD, vbuf.shape[1]), F32)

    def conv(x_ref, w_ref, buf, post):
        buf[CONV_PAD:CONV_PAD + lblk, :] = x_ref[...]
        tail = buf[lblk:lblk + CONV_PAD, :]
        taps = B_CONV - 1
        width = buf.shape[1]
        slab = min(width, 4 * LANES)
        for c in reversed(range(n_chunks)):
            lo = CONV_PAD + c * CHUNK
            for s in range(0, width, slab):
                y = w_ref[0:1, s:s + slab] * buf[lo - taps:lo - taps + CHUNK, s:s + slab]
                for j in range(1, B_CONV):
                    y = y + w_ref[j:j + 1, s:s + slab] * buf[lo - taps + j:lo - taps + j + CHUNK, s:s + slab]
                buf[lo:lo + CHUNK, s:s + slab] = post(_silu(y))
        return tail

    def per_head_l2norm(scale):
        def post(y):
            return jnp.concatenate([_l2norm(y[:, i * B_DK:(i + 1) * B_DK]) * scale
                                    for i in range(y.shape[1] // B_DK)], axis=1)
        return post

    q_tail = conv(q_ref, wq_ref, qbuf, per_head_l2norm(B_DK ** -0.5))
    k_tail = conv(k_ref, wk_ref, kbuf, per_head_l2norm(1.0))
    v_tail = conv(v_ref, wv_ref, vbuf, lambda y: y)

    lane = lax.broadcasted_iota(jnp.int32, (lblk, 2 * B_V_HEADS), 1)
    lane_o = lax.broadcasted_iota(jnp.int32, (lblk, LANES), 1)
    ba = ba_ref[...]
    pad = jnp.zeros((1, B_V_HEADS), F32)
    a_scale = -jnp.exp(jnp.concatenate([pad, alog_ref[...]], axis=1))
    dt_b = jnp.concatenate([pad, dtb_ref[...]], axis=1)
    gate_vals = jnp.where(lane < B_V_HEADS, jax.nn.sigmoid(ba), a_scale * _softplus(ba + dt_b))
    beta_all = jnp.zeros((lblk, LANES), F32)
    g_all = jnp.zeros((lblk, LANES), F32)
    for e in range(GDN_V_PER_STEP):
        hv = h * GDN_V_PER_STEP + e
        beta = jnp.sum(jnp.where(lane == hv, gate_vals, 0.0), axis=-1, keepdims=True)
        g = jnp.sum(jnp.where(lane == hv + B_V_HEADS, gate_vals, 0.0), axis=-1, keepdims=True)
        beta_all = jnp.where(lane_o == e, beta, beta_all)
        g_all = jnp.where(lane_o == e, g, g_all)
    bcol[...] = beta_all
    for j in range(n_chunks):
        g_b = _cumsum_rows(g_all[j * CHUNK:(j + 1) * CHUNK])
        gcol[j * CHUNK:(j + 1) * CHUNK, :] = g_b
        grow[j] = g_b.T[0:grow.shape[1], 0:CHUNK]

    tri = _tri(CHUNK)
    strict = _tri(CHUNK, strict=True)
    zero_blk = jnp.zeros((CHUNK, CHUNK), F32)

    def chunk(j, carry):
        rows = pl.ds(pl.multiple_of(j * CHUNK, CHUNK), CHUNK)
        brow = pl.ds(pl.multiple_of(j * CHUNK + CONV_PAD, 8), CHUNK)
        g_rows = grow[j]
        heads_q = range(GDN_QK_PER_STEP)
        heads_v = range(GDN_V_PER_STEP)
        qs = [qbuf[brow, i * B_DK:(i + 1) * B_DK] for i in heads_q]
        ks = [kbuf[brow, i * B_DK:(i + 1) * B_DK] for i in heads_q]
        kks = [_bdot_nt(k, k) for k in ks]
        qks = [_bdot_nt(q, k) for q, k in zip(qs, ks)]
        g_cs = [gcol[rows, e:e + 1] for e in heads_v]
        betas = [bcol[rows, e:e + 1] for e in heads_v]
        decs = [jnp.exp(jnp.where(tri, g_cs[e] - g_rows[e:e + 1, :], -jnp.inf)) for e in heads_v]
        a_blks = [jnp.where(strict, betas[e] * decs[e] * kks[e // B_REP], 0.0) for e in heads_v]
        rhs = [jnp.concatenate([betas[e] * vbuf[brow, e * B_DV:(e + 1) * B_DV],
                                (betas[e] * jnp.exp(g_cs[e])) * ks[e // B_REP]], axis=-1) for e in heads_v]
        a_bds = [jnp.concatenate(
            [jnp.concatenate([a_blks[i * B_REP + r] if r == c else zero_blk for c in range(B_REP)], axis=1)
             for r in range(B_REP)], axis=0) for i in heads_q]
        t_invs = _unit_lower_inverses(a_bds, CHUNK)
        t_parts = [_split2(t) for t in t_invs]
        r_parts = [_split2(jnp.concatenate(rhs[i * B_REP:(i + 1) * B_REP], axis=0)) for i in heads_q]
        sols = [_dot3_parts(th, tl, rh, rl) for (th, tl), (rh, rl) in zip(t_parts, r_parts)]
        sol_e = [sols[e // B_REP][(e % B_REP) * CHUNK:(e % B_REP + 1) * CHUNK] for e in heads_v]
        s_old = [s_ref[e] for e in heads_v]
        ws = [sol_e[e][:, :B_DV] - _bdot(sol_e[e][:, B_DV:], s_old[e]) for e in heads_v]
        o_inter = [_bdot(qs[e // B_REP] * jnp.exp(g_cs[e]), s_old[e]) for e in heads_v]
        o_intra = [_bdot(decs[e] * qks[e // B_REP], ws[e]) for e in heads_v]
        for e in heads_v:
            g_end = g_cs[e][CHUNK - 1:CHUNK, :]
            s_ref[e] = jnp.exp(g_end) * s_old[e] + _bdot_tn(ks[e // B_REP] * jnp.exp(g_end - g_cs[e]), ws[e])
        for e in heads_v:
            zc = z_ref[rows, e * B_DV:(e + 1) * B_DV]
            o = o_intra[e] + o_inter[e]
            o_ref[rows, e * B_DV:(e + 1) * B_DV] = (_rms(o) * gn_ref[...] * _silu(zc)).astype(o_ref.dtype)
        return carry

    lax.fori_loop(0, n_chunks, chunk, 0)
    qbuf[0:CONV_PAD, :] = q_tail
    kbuf[0:CONV_PAD, :] = k_tail
    vbuf[0:CONV_PAD, :] = v_tail

    @pl.when(l == pl.num_programs(2) - 1)
    def _():
        s_out_ref[0] = s_ref[...]


def _gdn_prompt(proj, ba, b, l, conv_w, a_log, dt_bias, g_norm):
    lblk = _pick_tile(l, SEQ_BLOCK, CHUNK)
    nl = l // lblk
    qw = GDN_QK_PER_STEP * B_DK
    vw = GDN_V_PER_STEP * B_DV
    k_off = B_QK_DIM // qw
    v_off = 2 * B_QK_DIM // vw
    z_off = B_CONV_DIM // vw
    return pl.pallas_call(
        functools.partial(_gdn_chunk_kernel, lblk=lblk),
        grid=(b, B_QK_HEADS // GDN_QK_PER_STEP, nl),
        in_specs=[pl.BlockSpec((lblk, qw), lambda bi, h, li: (bi * nl + li, h)),
                  pl.BlockSpec((lblk, qw), lambda bi, h, li: (bi * nl + li, k_off + h)),
                  pl.BlockSpec((lblk, vw), lambda bi, h, li: (bi * nl + li, v_off + h)),
                  pl.BlockSpec((lblk, vw), lambda bi, h, li: (bi * nl + li, z_off + h)),
                  pl.BlockSpec((lblk, 2 * B_V_HEADS), lambda bi, h, li: (bi * nl + li, 0)),
                  pl.BlockSpec((B_CONV, qw), lambda bi, h, li: (0, h)),
                  pl.BlockSpec((B_CONV, qw), lambda bi, h, li: (0, k_off + h)),
                  pl.BlockSpec((B_CONV, vw), lambda bi, h, li: (0, v_off + h)),
                  pl.BlockSpec((1, B_V_HEADS), lambda bi, h, li: (0, 0)),
                  pl.BlockSpec((1, B_V_HEADS), lambda bi, h, li: (0, 0)),
                  pl.BlockSpec((1, B_DV), lambda bi, h, li: (0, 0))],
        out_specs=[pl.BlockSpec((lblk, vw), lambda bi, h, li: (bi * nl + li, h)),
                   pl.BlockSpec((1, GDN_V_PER_STEP, B_DK, B_DV), lambda bi, h, li: (bi, h, 0, 0))],
        out_shape=[jax.ShapeDtypeStruct((b * l, B_V_DIM), BF16),
                   jax.ShapeDtypeStruct((b, B_V_HEADS, B_DK, B_DV), F32)],
        scratch_shapes=[pltpu.VMEM((GDN_V_PER_STEP, B_DK, B_DV), F32),
                        pltpu.VMEM((CONV_PAD + lblk, qw), F32),
                        pltpu.VMEM((CONV_PAD + lblk, qw), F32),
                        pltpu.VMEM((CONV_PAD + lblk, vw), F32),
                        pltpu.VMEM((lblk, LANES), F32),
                        pltpu.VMEM((lblk, LANES), F32),
                        pltpu.VMEM((lblk // CHUNK, max(8, GDN_V_PER_STEP), CHUNK), F32)],
        compiler_params=_params("parallel", "parallel", "arbitrary"),
        name="gdn_chunks",
    )(proj, proj, proj, proj, ba, conv_w, conv_w, conv_w,
      a_log.reshape(1, -1), dt_bias.reshape(1, -1), g_norm.reshape(1, -1))


GDN_STEP_BATCH = 2
B_ROWS = B_CONV_DIM // 128


def _gdn_step_kernel(p_ref, cv_ref, cw_ref, ba_ref, alog_ref, dtb_ref, gn_ref, s_ref, o_ref, s_out_ref):
    ba = ba_ref[0]
    beta_all = jax.nn.sigmoid(ba[:, :B_V_HEADS])
    g_all = -jnp.exp(alog_ref[...]) * _softplus(ba[:, B_V_HEADS:] + dtb_ref[...])
    decay_all = jnp.exp(g_all)
    iota_h = lax.broadcasted_iota(jnp.int32, (B_V_HEADS, B_DV), 0)
    for bi in range(GDN_STEP_BATCH):
        y = cw_ref[B_CONV - 1] * p_ref[bi, 0:B_ROWS, :]
        for j in range(B_CONV - 1):
            y = y + cw_ref[j] * cv_ref[bi, j]
        y = _silu(y)
        q = _l2norm(y[0:B_QK_HEADS]) * (B_DK ** -0.5)
        k = _l2norm(y[B_QK_HEADS:2 * B_QK_HEADS])
        v = y[2 * B_QK_HEADS:]
        z = p_ref[bi, B_ROWS:B_ROWS + B_V_HEADS, :]
        q_t, k_t = q.T, k.T
        qk = jnp.sum(q * k, axis=-1, keepdims=True)
        o = jnp.zeros((B_V_HEADS, B_DV), F32)
        for hv in range(B_V_HEADS):
            hq = hv // B_REP
            beta = beta_all[bi:bi + 1, hv:hv + 1]
            decay = decay_all[bi:bi + 1, hv:hv + 1]
            s = s_ref[bi, hv]
            k_col = k_t[:, hq:hq + 1]
            w = beta * v[hv:hv + 1, :] - (beta * decay) * jnp.sum(k_col * s, axis=0, keepdims=True)
            oh = qk[hq:hq + 1, :] * w + decay * jnp.sum(q_t[:, hq:hq + 1] * s, axis=0, keepdims=True)
            s_out_ref[bi, hv] = decay * s + k_col * w
            o = jnp.where(iota_h == hv, oh, o)
        o_ref[bi] = (_rms(o) * gn_ref[...] * _silu(z)).astype(o_ref.dtype)


def _gdn_step(proj, ba, conv0, conv_w, a_log, dt_bias, g_norm, s0):
    b = proj.shape[0]
    bb = GDN_STEP_BATCH
    p3 = proj.reshape(b, -1, 128)
    o, s = pl.pallas_call(
        _gdn_step_kernel,
        grid=(b // bb,),
        in_specs=[pl.BlockSpec((bb, p3.shape[1], 128), lambda i: (i, 0, 0)),
                  pl.BlockSpec((bb, B_CONV - 1, B_ROWS, 128), lambda i: (i, 0, 0, 0)),
                  pl.BlockSpec((B_CONV, B_ROWS, 128), lambda i: (0, 0, 0)),
                  pl.BlockSpec((1, bb, 2 * B_V_HEADS), lambda i: (i, 0, 0)),
                  pl.BlockSpec((1, B_V_HEADS), lambda i: (0, 0)),
                  pl.BlockSpec((1, B_V_HEADS), lambda i: (0, 0)),
                  pl.BlockSpec((1, B_DV), lambda i: (0, 0)),
                  pl.BlockSpec((bb, B_V_HEADS, B_DK, B_DV), lambda i: (i, 0, 0, 0))],
        out_specs=[pl.BlockSpec((bb, B_V_HEADS, B_DV), lambda i: (i, 0, 0)),
                   pl.BlockSpec((bb, B_V_HEADS, B_DK, B_DV), lambda i: (i, 0, 0, 0))],
        out_shape=[jax.ShapeDtypeStruct((b, B_V_HEADS, B_DV), BF16), jax.ShapeDtypeStruct(s0.shape, F32)],
        compiler_params=_params("parallel"),
        name="gdn_step",
    )(p3, conv0.reshape(b, B_CONV - 1, B_ROWS, 128), conv_w.reshape(B_CONV, B_ROWS, 128),
      ba.reshape(b // bb, bb, 2 * B_V_HEADS), a_log.reshape(1, -1), dt_bias.reshape(1, -1), g_norm.reshape(1, -1), s0)
    return o.reshape(b, B_V_DIM), s


D_GROUP = 16
D_GROUPS = D_MODEL // D_GROUP
D_STATE = 64
S5_LANES = 128
S5_GROUPS = S5_LANES // D_GROUP
S5_HALF = S5_GROUPS * D_STATE
S5_BLOCKS = D_MODEL // S5_LANES
S5_SEGMENTS = 8


def _cmul(ar, ai, br, bi):
    return ar * br - ai * bi, ar * bi + ai * br


def _gelu_tanh(x):
    return 0.5 * x * (1.0 + jnp.tanh(math.sqrt(2.0 / math.pi) * (x + 0.044715 * (x * x * x))))


def _s5_tables(a_re, a_im, b_re, b_im, c_re, c_im, log_dt, seg_len):
    lam_re, lam_im = a_re.astype(F32), a_im.astype(F32)
    dt = jnp.exp(log_dt.astype(F32))[:, None]
    mag = jnp.exp(dt * lam_re)
    ab_re, ab_im = mag * jnp.cos(dt * lam_im), mag * jnp.sin(dt * lam_im)
    inv = 1.0 / (lam_re * lam_re + lam_im * lam_im)
    e_re, e_im = _cmul(ab_re - 1.0, ab_im, lam_re * inv, -lam_im * inv)
    bb_re, bb_im = _cmul(e_re[..., None], e_im[..., None], b_re.astype(F32), b_im.astype(F32))
    j = S5_BLOCKS
    eye = jnp.eye(S5_GROUPS, dtype=F32)

    def pack_rows(t):
        return t.reshape(j, 1, S5_HALF)

    def bdiag_in(t):
        t = t.reshape(j, S5_GROUPS, D_STATE, D_GROUP)
        return jnp.einsum('jgpc,gh->jgchp', t, eye).reshape(j, S5_LANES, S5_HALF)

    def bdiag_out(t):
        t = t.reshape(j, S5_GROUPS, D_GROUP, D_STATE)
        return jnp.einsum('jgcp,gh->jgphc', t, eye).reshape(j, S5_HALF, S5_LANES)

    ab = jnp.concatenate([pack_rows(ab_re), pack_rows(ab_im)], axis=-1)
    w_bu = jnp.concatenate([bdiag_in(bb_re), bdiag_in(bb_im)], axis=-1)
    w_c = jnp.concatenate([bdiag_out(c_re.astype(F32)), -bdiag_out(c_im.astype(F32))], axis=1)
    pr, pi = pack_rows(ab_re), pack_rows(ab_im)
    n = 1
    while n < seg_len:
        tr, ti = pr[:, n - 1:n], pi[:, n - 1:n]
        nr, ni = _cmul(pr, pi, tr, ti)
        pr, pi = jnp.concatenate([pr, nr], axis=1), jnp.concatenate([pi, ni], axis=1)
        n *= 2
    pw = jnp.concatenate([pr[:, :seg_len], pi[:, :seg_len]], axis=-1)
    return ab, w_bu, w_c, pw


def _s5_scan_kernel(x_ref, ab_ref, wbu_ref, wc_ref, pw_ref, dsk_ref, y_ref, s_out_ref, xp, buf, yp, *, seq, seg):
    nc = S5_HALF // 128
    ns = S5_SEGMENTS

    def tile(tau):
        return pl.ds(pl.multiple_of(tau * ns, ns), ns)

    def interleave(tau, c):
        xp[tile(tau), :] = x_ref[pl.ds(tau, ns, stride=seg), :]
        return c

    lax.fori_loop(0, seg, interleave, 0, unroll=8)
    x = xp[...]
    bu = _bdot(x, wbu_ref[0])
    for c in range(2 * nc):
        buf[c] = bu[:, c * 128:(c + 1) * 128]

    a_res = [jnp.broadcast_to(ab_ref[0, :, c * 128:(c + 1) * 128], (ns, 128)) for c in range(nc)]
    a_ims = [jnp.broadcast_to(ab_ref[0, :, S5_HALF + c * 128:S5_HALF + (c + 1) * 128], (ns, 128))
             for c in range(nc)]

    def step(tau, carry):
        rows = tile(tau)
        out = []
        for c in range(nc):
            s_re, s_im = carry[c]
            n_re = a_res[c] * s_re - a_ims[c] * s_im + buf[c, rows, :]
            n_im = a_res[c] * s_im + a_ims[c] * s_re + buf[nc + c, rows, :]
            buf[c, rows, :] = n_re
            buf[nc + c, rows, :] = n_im
            out.append((n_re, n_im))
        return tuple(out)

    zeros = jnp.zeros((ns, 128), F32)
    local_end = lax.fori_loop(0, seg, step, tuple((zeros, zeros) for _ in range(nc)), unroll=2)

    iota_r = lax.broadcasted_iota(jnp.int32, (ns, 128), 0)
    enter = []
    for c in range(nc):
        l_re, l_im = local_end[c]
        pe_re = pw_ref[0, seg - 1:seg, c * 128:(c + 1) * 128]
        pe_im = pw_ref[0, seg - 1:seg, S5_HALF + c * 128:S5_HALF + (c + 1) * 128]
        e_re, e_im = l_re[0:1], l_im[0:1]
        in_re, in_im = zeros, zeros
        for r in range(1, ns):
            in_re = jnp.where(iota_r == r, e_re, in_re)
            in_im = jnp.where(iota_r == r, e_im, in_im)
            t_re, t_im = _cmul(pe_re, pe_im, e_re, e_im)
            e_re, e_im = l_re[r:r + 1] + t_re, l_im[r:r + 1] + t_im
        enter.append((in_re, in_im))

    def fix(group, carry):
        taus = pl.ds(pl.multiple_of(group * 8, 8), 8)
        for c in range(nc):
            p_re8 = pw_ref[0, taus, c * 128:(c + 1) * 128]
            p_im8 = pw_ref[0, taus, S5_HALF + c * 128:S5_HALF + (c + 1) * 128]
            for u in range(8):
                rows = tile(group * 8 + u)
                f_re, f_im = _cmul(p_re8[u:u + 1], p_im8[u:u + 1], enter[c][0], enter[c][1])
                buf[c, rows, :] = buf[c, rows, :] + f_re
                buf[nc + c, rows, :] = buf[nc + c, rows, :] + f_im
        return carry

    lax.fori_loop(0, seg // 8, fix, 0)

    y = dsk_ref[...] * x
    for c in range(2 * nc):
        s_out_ref[0, 0, :, c * 128:(c + 1) * 128] = buf[c, seq - 1:seq, :]
        y = y + _bdot(buf[c], wc_ref[0, c * 128:(c + 1) * 128, :])
    yp[...] = _gelu_tanh(y)
    for r in range(ns):
        y_ref[r * seg:(r + 1) * seg, :] = yp[pl.ds(r, seg, stride=ns), :].astype(y_ref.dtype)


def _unpack_state(s):
    b = s.shape[0]
    return (s[..., :S5_HALF].reshape(b, D_GROUPS, D_STATE), s[..., S5_HALF:].reshape(b, D_GROUPS, D_STATE))


def _s5_prompt(x, b, l, tables, d_skip):
    d = x.shape[1]
    ab, w_bu, w_c, pw = tables
    seg = l // S5_SEGMENTS
    y, s = pl.pallas_call(
        functools.partial(_s5_scan_kernel, seq=l, seg=seg),
        grid=(b, S5_BLOCKS),
        in_specs=[pl.BlockSpec((l, S5_LANES), lambda bi, j: (bi, j)),
                  pl.BlockSpec((1, 1, 2 * S5_HALF), lambda bi, j: (j, 0, 0)),
                  pl.BlockSpec((1, S5_LANES, 2 * S5_HALF), lambda bi, j: (j, 0, 0)),
                  pl.BlockSpec((1, 2 * S5_HALF, S5_LANES), lambda bi, j: (j, 0, 0)),
                  pl.BlockSpec((1, seg, 2 * S5_HALF), lambda bi, j: (j, 0, 0)),
                  pl.BlockSpec((1, S5_LANES), lambda bi, j: (0, j))],
        out_specs=[pl.BlockSpec((l, S5_LANES), lambda bi, j: (bi, j)),
                   pl.BlockSpec((1, 1, 1, 2 * S5_HALF), lambda bi, j: (bi, j, 0, 0))],
        out_shape=[jax.ShapeDtypeStruct((b * l, d), BF16),
                   jax.ShapeDtypeStruct((b, S5_BLOCKS, 1, 2 * S5_HALF), F32)],
        scratch_shapes=[pltpu.VMEM((l, S5_LANES), F32),
                        pltpu.VMEM((2 * S5_HALF // 128, l, 128), F32),
                        pltpu.VMEM((l, S5_LANES), F32)],
        compiler_params=_params("parallel", "parallel"),
        name="s5_scan",
    )(x, ab, w_bu, w_c, pw, d_skip.reshape(1, d))
    return y, _unpack_state(s[:, :, 0, :])


def _s5_step_kernel(x_ref, s_ref, ab_ref, wbu_ref, wc_ref, dsk_ref, y_ref, s_out_ref):
    hf = S5_HALF
    x = x_ref[...]
    bu = _bdot(x, wbu_ref[0])
    s0 = s_ref[0]
    i_re, i_im = _cmul(ab_ref[0, :, :hf], ab_ref[0, :, hf:], s0[:, :hf], s0[:, hf:])
    s_new = jnp.concatenate([i_re, i_im], axis=-1) + bu
    s_out_ref[0] = s_new
    y = _bdot(s_new, wc_ref[0]) + dsk_ref[...] * x
    y_ref[...] = _gelu_tanh(y).astype(y_ref.dtype)


def _s5_step(x, tables, d_skip, s0_re, s0_im):
    b, d = x.shape
    ab, w_bu, w_c, _ = tables
    s0 = jnp.concatenate([s0_re.reshape(b, S5_BLOCKS, S5_HALF), s0_im.reshape(b, S5_BLOCKS, S5_HALF)], axis=-1)
    s0 = jnp.swapaxes(s0, 0, 1)
    y, s = pl.pallas_call(
        _s5_step_kernel,
        grid=(S5_BLOCKS,),
        in_specs=[pl.BlockSpec((b, S5_LANES), lambda j: (0, j)),
                  pl.BlockSpec((1, b, 2 * S5_HALF), lambda j: (j, 0, 0)),
                  pl.BlockSpec((1, 1, 2 * S5_HALF), lambda j: (j, 0, 0)),
                  pl.BlockSpec((1, S5_LANES, 2 * S5_HALF), lambda j: (j, 0, 0)),
                  pl.BlockSpec((1, 2 * S5_HALF, S5_LANES), lambda j: (j, 0, 0)),
                  pl.BlockSpec((1, S5_LANES), lambda j: (0, j))],
        out_specs=[pl.BlockSpec((b, S5_LANES), lambda j: (0, j)),
                   pl.BlockSpec((1, b, 2 * S5_HALF), lambda j: (j, 0, 0))],
        out_shape=[jax.ShapeDtypeStruct((b, d), BF16), jax.ShapeDtypeStruct((S5_BLOCKS, b, 2 * S5_HALF), F32)],
        compiler_params=_params("parallel"),
        name="s5_step",
    )(x, s0, ab, w_bu, w_c, d_skip.reshape(1, d))
    return y, _unpack_state(jnp.swapaxes(s, 0, 1))


def _glu_kernel(x_ref, w1_ref, w2_ref, o_ref):
    x = x_ref[...]
    y1 = jnp.dot(x, w1_ref[...].astype(BF16), preferred_element_type=F32)
    y2 = jnp.dot(x, w2_ref[...].astype(BF16), preferred_element_type=F32)
    o_ref[...] = y1 * jax.nn.sigmoid(y2)


def _glu_matmul(x, w, tm_target=1664, tn_target=256):
    m, k = x.shape
    n = w.shape[1] // 2
    tm = _pick_tile(m, tm_target, 16)
    tn = _pick_tile(n, tn_target, 128)
    nb = n // tn
    return pl.pallas_call(
        _glu_kernel,
        grid=(m // tm, nb),
        in_specs=[pl.BlockSpec((tm, k), lambda i, j: (i, 0)),
                  pl.BlockSpec((k, tn), lambda i, j: (0, j)),
                  pl.BlockSpec((k, tn), lambda i, j: (0, j + nb))],
        out_specs=pl.BlockSpec((tm, tn), lambda i, j: (i, j)),
        out_shape=jax.ShapeDtypeStruct((m, n), F32),
        compiler_params=_params("parallel", "parallel"),
        name="glu_matmul",
    )(x, w, w)


def kernel(x_prompt, x_sample, state_a_S, state_b_S, state_b_conv, state_c_C, state_c_n, state_c_m, state_d_re, state_d_im, a_w_in, a_lb_logits, a_g_norm, a_w_out, b_w_in, b_conv_w, b_a_log, b_dt_bias, b_g_norm, b_w_out, c_w_in, c_b_gate, c_g_norm, c_w_out, d_a_re, d_a_im, d_b_re, d_b_im, d_c_re, d_c_im, d_skip, d_log_dt, d_w_glu, moe_w_router, moe_b_router, moe_w_gu, moe_w_down, moe_w_gu_s, moe_w_down_s, ln_mix_g, ln_mix_b, ln_ffn_g, ln_ffn_b):
    bp, l, d = x_prompt.shape
    bs = x_sample.shape[0]
    tp = bp * l
    x = jnp.concatenate([x_prompt.reshape(tp, d), x_sample.reshape(bs, d)], axis=0)
    xb = x.astype(BF16)

    def finish_layer(x, h, layer):
        x, xb, xpk = _add_ln(x, h, ln_mix_g[layer], ln_mix_b[layer])
        return _moe_ln(x, xb, xpk, moe_w_router, moe_b_router, moe_w_gu, moe_w_down, moe_w_gu_s, moe_w_down_s,
                       ln_ffn_g, ln_ffn_b, layer)

    proj = _matmul(xb, a_w_in)
    lb = _hgrn2_lower_bound(a_lb_logits, 0)
    o_p, a_s_p = _hgrn2_prompt(proj, bp, l, lb, a_g_norm)
    o_s, a_s_s = _hgrn2_step(proj[tp:], state_a_S, lb, a_g_norm)
    h = _matmul(jnp.concatenate([o_p, o_s], axis=0), a_w_out)
    x, xb = finish_layer(x, h, 0)

    n_main = B_CONV_DIM + B_V_DIM
    proj = _matmul(xb, b_w_in, n_cols=n_main)
    ba = _matmul(xb, b_w_in[:, n_main:])
    o_p, b_s_p = _gdn_prompt(proj, ba, bp, l, b_conv_w, b_a_log, b_dt_bias, b_g_norm)
    o_s, b_s_s = _gdn_step(proj[tp:], ba[tp:], state_b_conv, b_conv_w, b_a_log, b_dt_bias, b_g_norm, state_b_S)
    qkv_s = proj[tp:, :B_CONV_DIM]
    b_conv_p = jnp.stack([proj[(i + 1) * l - (B_CONV - 1):(i + 1) * l, :B_CONV_DIM] for i in range(bp)], axis=0)
    b_conv_s = jnp.concatenate([state_b_conv[:, 1:, :], qkv_s[:, None, :]], axis=1)
    h = _matmul(jnp.concatenate([o_p, o_s], axis=0), b_w_out)
    x, xb = finish_layer(x, h, 1)

    n_main = 2 * C_QK_DIM + 2 * C_V_DIM
    proj = _matmul(xb, c_w_in, n_cols=n_main)
    gates = _matmul(xb, c_w_in[:, n_main:])
    o_p, c_c_p, c_n_p, c_m_p = _mlstm_prompt(proj, gates, bp, l, c_b_gate, c_g_norm)
    o_s, c_c_s, c_n_s, c_m_s = _mlstm_step(proj[tp:], gates[tp:], c_b_gate, c_g_norm, state_c_C, state_c_n, state_c_m)
    h = _matmul(jnp.concatenate([o_p, o_s], axis=0), c_w_out)
    x, xb = finish_layer(x, h, 2)

    tables = _s5_tables(d_a_re, d_a_im, d_b_re, d_b_im, d_c_re, d_c_im, d_log_dt, l // S5_SEGMENTS)
    y_p, (d_re_p, d_im_p) = _s5_prompt(x, bp, l, tables, d_skip)
    y_s, (d_re_s, d_im_s) = _s5_step(x[tp:], tables, d_skip, state_d_re, state_d_im)
    h = _glu_matmul(jnp.concatenate([y_p, y_s], axis=0), d_w_glu)
    x, xb = finish_layer(x, h, 3)

    y_prompt = x[:tp].reshape(bp, l, d)
    y_sample = x[tp:].reshape(bs, 1, d)
    return (y_prompt, y_sample, a_s_p, a_s_s, b_s_p, b_s_s, b_conv_p, b_conv_s,
            c_c_p, c_c_s, c_n_p, c_n_s, c_m_p, c_m_s, d_re_p, d_re_s, d_im_p, d_im_s)
```

```python
import functools
import math

import jax
import jax.numpy as jnp
from jax import lax
from jax.experimental import pallas as pl
from jax.experimental.pallas import tpu as pltpu

F32 = jnp.float32
BF16 = jnp.bfloat16
HI = lax.Precision.HIGHEST

D_MODEL = 2048
DEPTH = 4
CHUNK = 64
DN_ALPHA = (2 * DEPTH) ** 0.25
LN_EPS = 1e-5
RMS_EPS = 1e-6

VMEM_LIMIT_BYTES = 56 * 1024 * 1024


def _params(*sem):
    return pltpu.CompilerParams(dimension_semantics=sem, vmem_limit_bytes=VMEM_LIMIT_BYTES)


def _pick_tile(n, target, mult):
    best = None
    for t in range(mult, min(n, target) + 1, mult):
        if n % t == 0:
            best = t
    return best if best is not None else n


def _mm_kernel(x_ref, w_ref, o_ref):
    o_ref[...] = jnp.dot(x_ref[...], w_ref[...].astype(BF16), preferred_element_type=F32).astype(o_ref.dtype)


def _matmul(x, w, n_cols=None, out_dtype=F32, tm_target=1664, tn_target=512):
    m, k = x.shape
    n = w.shape[1] if n_cols is None else n_cols
    tm = _pick_tile(m, tm_target, 16)
    tn = _pick_tile(n, tn_target, 128)
    return pl.pallas_call(
        _mm_kernel,
        grid=(m // tm, n // tn),
        in_specs=[pl.BlockSpec((tm, k), lambda i, j: (i, 0)), pl.BlockSpec((k, tn), lambda i, j: (0, j))],
        out_specs=pl.BlockSpec((tm, tn), lambda i, j: (i, j)),
        out_shape=jax.ShapeDtypeStruct((m, n), out_dtype),
        compiler_params=_params("parallel", "parallel"),
        name="matmul",
    )(x, w)


def _layernorm_rows(z, g, b):
    mu = jnp.mean(z, axis=-1, keepdims=True)
    zc = z - mu
    var = jnp.mean(zc * zc, axis=-1, keepdims=True)
    return zc * lax.rsqrt(var + LN_EPS) * g + b


U32 = jnp.uint32
BF16_HIGH_MASK = 0xFFFF0000


def _pack_bf16_halves(yb):
    half = yb.shape[1] // 2
    bits = lax.bitcast_convert_type(yb.astype(F32), U32)
    return (bits[:, :half] >> 16) | (bits[:, half:] & jnp.uint32(BF16_HIGH_MASK))


def _unpack_bf16_halves(w):
    lo = lax.bitcast_convert_type(w << 16, F32).astype(BF16)
    hi = lax.bitcast_convert_type(w & jnp.uint32(BF16_HIGH_MASK), F32).astype(BF16)
    return lo, hi


def _add_ln_kernel(x_ref, h_ref, g_ref, b_ref, o_ref, ob_ref, op_ref):
    y = _layernorm_rows(DN_ALPHA * x_ref[...] + h_ref[...], g_ref[...], b_ref[...])
    yb = y.astype(BF16)
    o_ref[...] = y
    ob_ref[...] = yb
    op_ref[...] = _pack_bf16_halves(yb)


def _add_ln(x, h, g, b, tm_target=640):
    m, d = x.shape
    tm = _pick_tile(m, tm_target, 16)
    row = pl.BlockSpec((tm, d), lambda i: (i, 0))
    vec = pl.BlockSpec((1, d), lambda i: (0, 0))
    return pl.pallas_call(
        _add_ln_kernel,
        grid=(m // tm,),
        in_specs=[row, row, vec, vec],
        out_specs=[row, row, pl.BlockSpec((tm, d // 2), lambda i: (i, 0))],
        out_shape=[jax.ShapeDtypeStruct((m, d), F32), jax.ShapeDtypeStruct((m, d), BF16),
                   jax.ShapeDtypeStruct((m, d // 2), U32)],
        compiler_params=_params("parallel"),
        name="add_ln",
    )(x, h, g.reshape(1, d), b.reshape(1, d))


def _silu(x):
    return x * jax.nn.sigmoid(x)


def _ffn_kernel(x_ref, wgu_ref, wd_ref, o_ref, *, dh):
    gu = jnp.dot(x_ref[...], wgu_ref[0].astype(BF16), preferred_element_type=F32)
    a = _silu(gu[:, :dh]) * gu[:, dh:]
    o_ref[...] = jnp.dot(a.astype(BF16), wd_ref[0].astype(BF16), preferred_element_type=F32)


def _dense_ffn(xb, w_gu, w_down, layer, tm_target=640):
    m, d = xb.shape
    dh = w_down.shape[1]
    tm = _pick_tile(m, tm_target, 16)
    return pl.pallas_call(
        functools.partial(_ffn_kernel, dh=dh),
        grid=(m // tm,),
        in_specs=[pl.BlockSpec((tm, d), lambda i: (i, 0)),
                  pl.BlockSpec((1, d, 2 * dh), lambda i: (layer, 0, 0)),
                  pl.BlockSpec((1, dh, d), lambda i: (layer, 0, 0))],
        out_specs=pl.BlockSpec((tm, d), lambda i: (i, 0)),
        out_shape=jax.ShapeDtypeStruct((m, d), F32),
        compiler_params=_params("parallel"),
        name="shared_ffn",
    )(xb, w_gu, w_down)


N_EXPERTS = 64
TOP_K = 8
N_ROUTE_GROUPS = 8
TOPK_ROUTE_GROUPS = 4
GROUP_SIZE = N_EXPERTS // N_ROUTE_GROUPS
ROUTED_SCALE = 2.5
EXPERT_TILE = 512


def _router_kernel(x_ref, wt_ref, b_ref, e_ref, g_ref):
    logits = lax.dot_general(wt_ref[...].astype(BF16), x_ref[...], (((1,), (1,)), ((), ())),
                             preferred_element_type=F32)
    scores = jax.nn.sigmoid(logits)
    biased = scores + b_ref[...]
    tm = scores.shape[1]
    neg = jnp.float32(-jnp.inf)
    iota_g = lax.broadcasted_iota(jnp.int32, (GROUP_SIZE, tm), 0)
    gscore = jnp.zeros((N_ROUTE_GROUPS, tm), F32)
    iota_r = lax.broadcasted_iota(jnp.int32, (N_ROUTE_GROUPS, tm), 0)
    for g in range(N_ROUTE_GROUPS):
        v = biased[g * GROUP_SIZE:(g + 1) * GROUP_SIZE, :]
        m1 = jnp.max(v, axis=0, keepdims=True)
        i1 = jnp.min(jnp.where(v == m1, iota_g, GROUP_SIZE), axis=0, keepdims=True)
        m2 = jnp.max(jnp.where(iota_g == i1, neg, v), axis=0, keepdims=True)
        gscore = jnp.where(iota_r == g, m1 + m2, gscore)
    ok = jnp.zeros((N_ROUTE_GROUPS, tm), jnp.int32)
    for _ in range(TOPK_ROUTE_GROUPS):
        m = jnp.max(gscore, axis=0, keepdims=True)
        gi = jnp.min(jnp.where(gscore == m, iota_r, N_ROUTE_GROUPS), axis=0, keepdims=True)
        hit = iota_r == gi
        ok = jnp.where(hit, 1, ok)
        gscore = jnp.where(hit, neg, gscore)
    masked = jnp.concatenate(
        [jnp.where(ok[g:g + 1, :] > 0, biased[g * GROUP_SIZE:(g + 1) * GROUP_SIZE, :], neg)
         for g in range(N_ROUTE_GROUPS)], axis=0)
    iota_e = lax.broadcasted_iota(jnp.int32, (N_EXPERTS, tm), 0)
    iota_k = lax.broadcasted_iota(jnp.int32, (TOP_K, tm), 0)
    top_e = jnp.zeros((TOP_K, tm), jnp.int32)
    gate = jnp.zeros((TOP_K, tm), F32)
    for k in range(TOP_K):
        m = jnp.max(masked, axis=0, keepdims=True)
        ei = jnp.min(jnp.where(masked == m, iota_e, N_EXPERTS), axis=0, keepdims=True)
        hit = iota_e == ei
        gk = jnp.sum(jnp.where(hit, scores, 0.0), axis=0, keepdims=True)
        top_e = jnp.where(iota_k == k, ei, top_e)
        gate = jnp.where(iota_k == k, gk, gate)
        masked = jnp.where(hit, neg, masked)
    e_ref[...] = top_e
    g_ref[...] = ROUTED_SCALE * gate / jnp.sum(gate, axis=0, keepdims=True)


def _router(x, w_router_t, b_router, tm_target=640):
    t, d = x.shape
    tm = _pick_tile(t, tm_target, 128)
    return pl.pallas_call(
        _router_kernel,
        grid=(t // tm,),
        in_specs=[pl.BlockSpec((tm, d), lambda i: (i, 0)),
                  pl.BlockSpec((N_EXPERTS, d), lambda i: (0, 0)),
                  pl.BlockSpec((N_EXPERTS, 1), lambda i: (0, 0))],
        out_specs=[pl.BlockSpec((TOP_K, tm), lambda i: (0, i)), pl.BlockSpec((TOP_K, tm), lambda i: (0, i))],
        out_shape=[jax.ShapeDtypeStruct((TOP_K, t), jnp.int32), jax.ShapeDtypeStruct((TOP_K, t), F32)],
        compiler_params=_params("parallel"),
        name="router",
    )(x, w_router_t, b_router.reshape(N_EXPERTS, 1))


LANES = 128


def _dispatch_kernel(e_ref, pos_ref, cnt_ref, rank_ref):
    n_blocks = e_ref.shape[1] // LANES
    iota_e = lax.broadcasted_iota(jnp.int32, (N_EXPERTS, LANES), 0)
    incl = (lax.broadcasted_iota(jnp.int32, (LANES, LANES), 0)
            <= lax.broadcasted_iota(jnp.int32, (LANES, LANES), 1)).astype(BF16)
    iota_k = lax.broadcasted_iota(jnp.int32, (TOP_K, LANES), 0)

    def pick(e_blk, table):
        out = jnp.zeros((TOP_K, LANES), F32)
        for k in range(TOP_K):
            v = jnp.sum(jnp.where(iota_e == e_blk[k:k + 1, :], table, 0.0), axis=0, keepdims=True)
            out = jnp.where(iota_k == k, v, out)
        return out

    def count_block(b, carry):
        cols = pl.ds(pl.multiple_of(b * LANES, LANES), LANES)
        e_blk = e_ref[:, cols]
        member = jnp.zeros((N_EXPERTS, LANES), F32)
        for k in range(TOP_K):
            member = member + (iota_e == e_blk[k:k + 1, :]).astype(F32)
        seen = jnp.dot(member.astype(BF16), incl, preferred_element_type=F32) + carry
        rank_ref[:, cols] = pick(e_blk, seen - member)
        return jnp.broadcast_to(seen[:, LANES - 1:LANES], (N_EXPERTS, LANES))

    counts = lax.fori_loop(0, n_blocks, count_block, jnp.zeros((N_EXPERTS, LANES), F32))
    cnt_ref[...] = counts.astype(jnp.int32)
    padded = jnp.ceil(counts * (1.0 / EXPERT_TILE)) * EXPERT_TILE
    start = jnp.dot(_tri(N_EXPERTS, strict=True).astype(F32), padded, preferred_element_type=F32, precision=HI)

    def place_block(b, carry):
        cols = pl.ds(pl.multiple_of(b * LANES, LANES), LANES)
        pos_ref[:, cols] = (pick(e_ref[:, cols], start) + rank_ref[:, cols]).astype(jnp.int32)
        return carry

    lax.fori_loop(0, n_blocks, place_block, 0)


def _dispatch_tables(top_e_t, n_tiles):
    k, t = top_e_t.shape
    pos_t, cnt = pl.pallas_call(
        _dispatch_kernel,
        out_shape=[jax.ShapeDtypeStruct((k, t), jnp.int32), jax.ShapeDtypeStruct((N_EXPERTS, LANES), jnp.int32)],
        scratch_shapes=[pltpu.VMEM((k, t), F32)],
        compiler_params=pltpu.CompilerParams(vmem_limit_bytes=VMEM_LIMIT_BYTES),
        name="dispatch",
    )(top_e_t)
    counts = cnt[:, 0]
    pends = jnp.cumsum((counts + EXPERT_TILE - 1) // EXPERT_TILE * EXPERT_TILE)
    n_used = pends[-1] // EXPERT_TILE
    tile_idx = jnp.arange(n_tiles, dtype=jnp.int32)
    tile_e = jnp.minimum(jnp.sum((pends[None, :] <= (tile_idx * EXPERT_TILE)[:, None]).astype(jnp.int32), axis=1),
                         N_EXPERTS - 1)
    tile_e = jnp.where(tile_idx < n_used, tile_e, tile_e[jnp.maximum(n_used - 1, 0)])
    last_tile = jnp.where(counts > 0, pends // EXPERT_TILE - 1, -1).astype(jnp.int32)
    return pos_t.reshape(-1), tile_e.astype(jnp.int32), n_used.astype(jnp.int32).reshape(1), last_tile


SCATTER_TILE = 640


def _scatter_rows_kernel(pos_ref, last_tile_ref, n_used_ref, x_ref, xs_ref, zbuf, zsem, sem, *, n_tok, max_idle):
    tt = x_ref.shape[0]
    n_tiles = xs_ref.shape[0] // EXPERT_TILE
    n_used = n_used_ref[0]

    @pl.when(pl.program_id(0) == 0)
    def _():
        zbuf[...] = jnp.zeros_like(zbuf)

        def fill(tile):
            return pltpu.make_async_copy(zbuf, xs_ref.at[pl.ds(tile * EXPERT_TILE, EXPERT_TILE)], zsem)

        def for_each_padded_tile(action):
            for e in range(N_EXPERTS):
                tile = last_tile_ref[e]

                @pl.when(tile >= 0)
                def _(tile=tile):
                    action(fill(tile))
            for j in range(max_idle):
                @pl.when(n_used + j < n_tiles)
                def _(j=j):
                    action(fill(n_used + j))

        for_each_padded_tile(lambda copy: copy.start())
        for_each_padded_tile(lambda copy: copy.wait())

    base = pl.program_id(0) * tt

    def body(r, c):
        for k in range(TOP_K):
            p = pos_ref[k * n_tok + base + r]
            pltpu.make_async_copy(x_ref.at[pl.ds(r, 1)], xs_ref.at[pl.ds(p, 1)], sem).start(priority=k % 2)
        return c

    lax.fori_loop(0, tt, body, 0, unroll=2)
    for k in range(TOP_K):
        pltpu.make_async_copy(x_ref, xs_ref.at[pl.ds(0, tt)], sem).wait()


def _sort_rows(x, pos, last_tile, n_used, n_tiles):
    t, d = x.shape
    tt = _pick_tile(t, SCATTER_TILE, 8)
    max_idle = n_tiles - (t * TOP_K) // EXPERT_TILE
    return pl.pallas_call(
        functools.partial(_scatter_rows_kernel, n_tok=t, max_idle=max_idle),
        grid_spec=pltpu.PrefetchScalarGridSpec(
            num_scalar_prefetch=3, grid=(t // tt,),
            in_specs=[pl.BlockSpec((tt, d), lambda i, p, lt, nu: (i, 0))],
            out_specs=pl.BlockSpec(memory_space=pl.ANY),
            scratch_shapes=[pltpu.VMEM((EXPERT_TILE, d), x.dtype), pltpu.SemaphoreType.DMA(()),
                            pltpu.SemaphoreType.DMA(())]),
        out_shape=jax.ShapeDtypeStruct((n_tiles * EXPERT_TILE, d), x.dtype),
        compiler_params=_params("arbitrary"),
        name="scatter_rows",
    )(pos, last_tile, n_used, x)


def _expert_kernel(tile_e_ref, n_used_ref, xs_ref, wgu_ref, wd_ref, o_ref, wgu_b, wd_b, *, dh):
    i = pl.program_id(0)
    n_used = n_used_ref[0]
    e = tile_e_ref[i]
    e_prev = tile_e_ref[jnp.maximum(i - 1, 0)]

    @pl.when(jnp.logical_and(i < n_used, jnp.logical_or(i == 0, e != e_prev)))
    def _():
        wgu_b[...] = wgu_ref[0, 0].astype(BF16)
        wd_b[...] = wd_ref[0, 0].astype(BF16)

    @pl.when(i < n_used)
    def _():
        x_lo, x_hi = _unpack_bf16_halves(xs_ref[...])
        half = x_lo.shape[1]
        gu = (jnp.dot(x_lo, wgu_b[:half, :], preferred_element_type=F32)
              + jnp.dot(x_hi, wgu_b[half:, :], preferred_element_type=F32))
        a = _silu(gu[:, :dh]) * gu[:, dh:]
        o_ref[...] = jnp.dot(a.astype(BF16), wd_b[...], preferred_element_type=F32)

    @pl.when(i >= n_used)
    def _():
        o_ref[...] = jnp.zeros_like(o_ref)


def _routed_experts(xs, tile_e, n_used, w_gu, w_down, layer):
    d = w_gu.shape[2]
    dh = w_down.shape[2]
    n_tiles = tile_e.shape[0]
    grid_spec = pltpu.PrefetchScalarGridSpec(
        num_scalar_prefetch=2,
        grid=(n_tiles,),
        in_specs=[pl.BlockSpec((EXPERT_TILE, d // 2), lambda i, te, nu: (jnp.minimum(i, nu[0] - 1), 0)),
                  pl.BlockSpec((1, 1, d, 2 * dh), lambda i, te, nu: (layer, te[i], 0, 0)),
                  pl.BlockSpec((1, 1, dh, d), lambda i, te, nu: (layer, te[i], 0, 0))],
        out_specs=pl.BlockSpec((EXPERT_TILE, d), lambda i, te, nu: (i, 0)),
        scratch_shapes=[pltpu.VMEM((d, 2 * dh), BF16), pltpu.VMEM((dh, d), BF16)],
    )
    return pl.pallas_call(
        functools.partial(_expert_kernel, dh=dh),
        grid_spec=grid_spec,
        out_shape=jax.ShapeDtypeStruct((n_tiles * EXPERT_TILE, d), F32),
        compiler_params=_params("arbitrary"),
        name="routed_experts",
    )(tile_e, n_used, xs, w_gu, w_down)


COMBINE_TILE = 128


def _combine_kernel(pos_ref, rows_hbm, gate_ref, x_ref, sh_ref, g_ref, b_ref, o_ref, ob_ref, buf, sem):
    i = pl.program_id(0)
    n = pl.num_programs(0)
    slot = i % 2

    n_tok = n * COMBINE_TILE

    def issue(tile, s):
        base = tile * COMBINE_TILE

        def body(r, c):
            for k in range(TOP_K):
                p = pos_ref[k * n_tok + base + r]
                pltpu.make_async_copy(rows_hbm.at[pl.ds(p, 1)], buf.at[s, k, pl.ds(r, 1)],
                                      sem.at[s]).start(priority=k % 2)
            return c

        lax.fori_loop(0, COMBINE_TILE, body, 0, unroll=2)

    @pl.when(i == 0)
    def _():
        issue(0, 0)

    @pl.when(i + 1 < n)
    def _():
        issue(i + 1, 1 - slot)

    for k in range(TOP_K):
        pltpu.make_async_copy(rows_hbm.at[pl.ds(0, COMBINE_TILE)], buf.at[slot, k], sem.at[slot]).wait()
    gate = gate_ref[...]
    z = DN_ALPHA * x_ref[...] + sh_ref[...]
    for k in range(TOP_K):
        z = z + gate[:, k:k + 1] * buf[slot, k]
    y = _layernorm_rows(z, g_ref[...], b_ref[...])
    o_ref[...] = y
    ob_ref[...] = y.astype(BF16)


def _combine_ln(rows, pos, gate, x, shared, g, b):
    t, d = x.shape
    row = lambda w: pl.BlockSpec((COMBINE_TILE, w), lambda i, p: (i, 0))
    vec = pl.BlockSpec((1, d), lambda i, p: (0, 0))
    grid_spec = pltpu.PrefetchScalarGridSpec(
        num_scalar_prefetch=1,
        grid=(t // COMBINE_TILE,),
        in_specs=[pl.BlockSpec(memory_space=pl.ANY), row(TOP_K), row(d), row(d), vec, vec],
        out_specs=[row(d), row(d)],
        scratch_shapes=[pltpu.VMEM((2, TOP_K, COMBINE_TILE, d), F32), pltpu.SemaphoreType.DMA((2,))],
    )
    return pl.pallas_call(
        _combine_kernel,
        grid_spec=grid_spec,
        out_shape=[jax.ShapeDtypeStruct((t, d), F32), jax.ShapeDtypeStruct((t, d), BF16)],
        compiler_params=_params("arbitrary"),
        name="combine_ln",
    )(pos, rows, gate, x, shared, g.reshape(1, d), b.reshape(1, d))


def _moe_ln(x, xb, xpk, w_router, b_router, w_gu, w_down, w_gu_s, w_down_s, ln_g, ln_b, layer):
    t = x.shape[0]
    top_e_t, gate_t = _router(xb, w_router[layer].T, b_router[layer])
    n_tiles = (t * TOP_K + N_EXPERTS * (EXPERT_TILE - 1) + EXPERT_TILE - 1) // EXPERT_TILE
    pos, tile_e, n_used, last_tile = _dispatch_tables(top_e_t, n_tiles)
    xs = _sort_rows(xpk, pos, last_tile, n_used, n_tiles)
    rows = _routed_experts(xs, tile_e, n_used, w_gu, w_down, layer)
    shared = _dense_ffn(xb, w_gu_s, w_down_s, layer)
    return _combine_ln(rows, pos, gate_t.T, x, shared, ln_g[layer], ln_b[layer])


SUB = 16
SEQ_BLOCK = 512


def _tri(n, strict=False):
    r = lax.broadcasted_iota(jnp.int32, (n, n), 0)
    c = lax.broadcasted_iota(jnp.int32, (n, n), 1)
    return (r > c) if strict else (r >= c)


def _cumsum_rows(x):
    n = x.shape[0]
    return jnp.dot(_tri(n).astype(F32), x, preferred_element_type=F32, precision=HI)


def _bdot(a, b):
    return jnp.dot(a.astype(BF16), b.astype(BF16), preferred_element_type=F32)


def _bdot_nt(a, b):
    return lax.dot_general(a.astype(BF16), b.astype(BF16), (((1,), (1,)), ((), ())), preferred_element_type=F32)


def _bdot_tn(a, b):
    return lax.dot_general(a.astype(BF16), b.astype(BF16), (((0,), (0,)), ((), ())), preferred_element_type=F32)


def _rms(x):
    return x * lax.rsqrt(jnp.mean(x * x, axis=-1, keepdims=True) + RMS_EPS)


A_HEADS = 16
A_DK = 128
A_DV = 128
HGRN2_HEADS_PER_STEP = 8


def _hgrn2_chunk_kernel(q_ref, f_ref, v_ref, g_ref, lb_ref, gn_ref, o_ref, s_out_ref, s_ref, *, n_chunks):
    l = pl.program_id(2)

    @pl.when(l == 0)
    def _():
        s_ref[...] = jnp.zeros_like(s_ref)

    lb = lb_ref[...]
    gn = gn_ref[...]
    heads = range(HGRN2_HEADS_PER_STEP)

    def head(x, i):
        return x[:, i * A_DK:(i + 1) * A_DK]

    def chunk(j, carry):
        rows = pl.ds(pl.multiple_of(j * CHUNK, CHUNK), CHUNK)
        q = _silu(q_ref[rows, :])
        f = lb + (1.0 - lb) * jax.nn.sigmoid(f_ref[rows, :])
        k = 1.0 - f
        v = v_ref[rows, :]
        bc = _cumsum_rows(jnp.log(f))
        qe = q * jnp.exp(bc)
        s_old = [s_ref[i] for i in heads]
        o_inter = [_bdot(head(qe, i), s_old[i]) for i in heads]
        outs = [[] for _ in heads]
        for sc in range(CHUNK // SUB):
            lo, hi = sc * SUB, (sc + 1) * SUB
            bref = bc[lo - 1:lo, :] if sc > 0 else jnp.zeros((1, bc.shape[1]), F32)
            qi = q[lo:hi] * jnp.exp(bc[lo:hi] - bref)
            ki = k[:hi] * jnp.exp(bref - bc[:hi])
            causal = (lax.broadcasted_iota(jnp.int32, (SUB, hi), 0) + lo
                      >= lax.broadcasted_iota(jnp.int32, (SUB, hi), 1))
            atts = [jnp.where(causal, _bdot_nt(head(qi, i), head(ki, i)), 0.0) for i in heads]
            for i in heads:
                outs[i].append(_bdot(atts[i], head(v, i)[:hi]))
        b_end = bc[CHUNK - 1:CHUNK, :]
        kd = k * jnp.exp(b_end - bc)
        d_cols = jnp.exp(bc.T[:, CHUNK - 1:CHUNK])
        for i in heads:
            s_ref[i] = d_cols[i * A_DK:(i + 1) * A_DK] * s_old[i] + _bdot_tn(head(kd, i), head(v, i))
        o_n = jnp.concatenate([_rms(jnp.concatenate(outs[i], axis=0) + o_inter[i]) * gn for i in heads], axis=1)
        o_ref[rows, :] = (o_n * _silu(g_ref[rows, :])).astype(o_ref.dtype)
        return carry

    lax.fori_loop(0, n_chunks, chunk, 0)

    @pl.when(l == pl.num_programs(2) - 1)
    def _():
        s_out_ref[0] = s_ref[...]


def _hgrn2_prompt(proj, b, l, lb, g_norm):
    lblk = _pick_tile(l, SEQ_BLOCK, CHUNK)
    nl = l // lblk
    hg = HGRN2_HEADS_PER_STEP
    groups = A_HEADS // hg
    part = lambda p: pl.BlockSpec((lblk, hg * A_DK), lambda bi, h, li, p=p: (bi * nl + li, p * groups + h))
    return pl.pallas_call(
        functools.partial(_hgrn2_chunk_kernel, n_chunks=lblk // CHUNK),
        grid=(b, groups, nl),
        in_specs=[part(0), part(1), part(2), part(3),
                  pl.BlockSpec((1, hg * A_DK), lambda bi, h, li: (0, h)),
                  pl.BlockSpec((1, A_DV), lambda bi, h, li: (0, 0))],
        out_specs=[pl.BlockSpec((lblk, hg * A_DV), lambda bi, h, li: (bi * nl + li, h)),
                   pl.BlockSpec((1, hg, A_DK, A_DV), lambda bi, h, li: (bi, h, 0, 0))],
        out_shape=[jax.ShapeDtypeStruct((b * l, A_HEADS * A_DV), BF16),
                   jax.ShapeDtypeStruct((b, A_HEADS, A_DK, A_DV), F32)],
        scratch_shapes=[pltpu.VMEM((hg, A_DK, A_DV), F32)],
        compiler_params=_params("parallel", "parallel", "arbitrary"),
        name="hgrn2_chunks",
    )(proj, proj, proj, proj, lb.reshape(1, -1), g_norm.reshape(1, -1))


HGRN2_STEP_BATCH = 2


def _hgrn2_step_kernel(p_ref, s_ref, lb_ref, gn_ref, o_ref, s_out_ref):
    lb = lb_ref[...]
    iota_h = lax.broadcasted_iota(jnp.int32, (A_HEADS, A_DV), 0)
    for bi in range(HGRN2_STEP_BATCH):
        p = p_ref[bi]
        q = _silu(p[0:A_HEADS])
        f = lb + (1.0 - lb) * jax.nn.sigmoid(p[A_HEADS:2 * A_HEADS])
        v = p[2 * A_HEADS:3 * A_HEADS]
        g = p[3 * A_HEADS:4 * A_HEADS]
        q_t, f_t = q.T, f.T
        o = jnp.zeros((A_HEADS, A_DV), F32)
        for h in range(A_HEADS):
            fc = f_t[:, h:h + 1]
            s_new = fc * s_ref[bi, h] + (1.0 - fc) * v[h:h + 1, :]
            s_out_ref[bi, h] = s_new
            oh = jnp.sum(q_t[:, h:h + 1] * s_new, axis=0, keepdims=True)
            o = jnp.where(iota_h == h, oh, o)
        o_ref[bi] = (_rms(o) * gn_ref[...] * _silu(g)).astype(o_ref.dtype)


def _hgrn2_step(proj, s0, lb, g_norm):
    b = proj.shape[0]
    bb = HGRN2_STEP_BATCH
    p3 = proj.reshape(b, 4 * A_HEADS, A_DK)
    o, s = pl.pallas_call(
        _hgrn2_step_kernel,
        grid=(b // bb,),
        in_specs=[pl.BlockSpec((bb, 4 * A_HEADS, A_DK), lambda i: (i, 0, 0)),
                  pl.BlockSpec((bb, A_HEADS, A_DK, A_DV), lambda i: (i, 0, 0, 0)),
                  pl.BlockSpec((A_HEADS, A_DK), lambda i: (0, 0)),
                  pl.BlockSpec((1, A_DV), lambda i: (0, 0))],
        out_specs=[pl.BlockSpec((bb, A_HEADS, A_DV), lambda i: (i, 0, 0)),
                   pl.BlockSpec((bb, A_HEADS, A_DK, A_DV), lambda i: (i, 0, 0, 0))],
        out_shape=[jax.ShapeDtypeStruct((b, A_HEADS, A_DV), BF16),
                   jax.ShapeDtypeStruct(s0.shape, F32)],
        compiler_params=_params("parallel"),
        name="hgrn2_step",
    )(p3, s0, lb.reshape(A_HEADS, A_DK), g_norm.reshape(1, -1))
    return o.reshape(b, A_HEADS * A_DV), s


def _hgrn2_lower_bound(lb_logits, layer):
    return jnp.cumsum(jax.nn.softmax(lb_logits.astype(F32), axis=0), axis=0)[layer]


C_HEADS = 8
C_DK = 128
C_DV = 256
C_QK_DIM = C_HEADS * C_DK
C_V_DIM = C_HEADS * C_DV
C_GATE_CAP = 15.0
MLSTM_HEADS_PER_STEP = 4


def _log_sigmoid(x):
    return jnp.minimum(x, 0.0) - jnp.log(1.0 + jnp.exp(-jnp.abs(x)))


def _cap(x):
    return C_GATE_CAP * jnp.tanh(x / C_GATE_CAP)


def _mlstm_chunk_kernel(q_ref, k_ref, v_ref, op_ref, gt_ref, bg_ref, gn_ref,
                        o_ref, c_out_ref, n_out_ref, m_out_ref, c_ref, n_ref, m_ref, *, n_chunks):
    h = pl.program_id(1)
    l = pl.program_id(2)

    @pl.when(l == 0)
    def _():
        c_ref[...] = jnp.zeros_like(c_ref)
        n_ref[...] = jnp.zeros_like(n_ref)
        m_ref[...] = jnp.zeros_like(m_ref)

    hg = MLSTM_HEADS_PER_STEP
    heads = range(hg)
    lane = lax.broadcasted_iota(jnp.int32, (CHUNK, 2 * C_HEADS), 1)
    lane_o = lax.broadcasted_iota(jnp.int32, (CHUNK, LANES), 1)
    tri = _tri(CHUNK)

    def chunk(j, carry):
        rows = pl.ds(pl.multiple_of(j * CHUNK, CHUNK), CHUNK)
        gates = _cap(gt_ref[rows, :] + bg_ref[...])
        lsig = _log_sigmoid(gates)
        i_all = jnp.zeros((CHUNK, LANES), F32)
        lf_all = jnp.zeros((CHUNK, LANES), F32)
        for e in heads:
            he = h * hg + e
            i_all = jnp.where(lane_o == e, jnp.sum(jnp.where(lane == he, gates, 0.0), axis=-1, keepdims=True), i_all)
            lf_all = jnp.where(lane_o == e,
                               jnp.sum(jnp.where(lane == he + C_HEADS, lsig, 0.0), axis=-1, keepdims=True), lf_all)
        f_b = _cumsum_rows(lf_all)
        a_b = i_all - f_b
        a_t = a_b.T
        f_cs = [f_b[:, e:e + 1] for e in heads]
        log_ws = [jnp.where(tri, f_cs[e] + a_t[e:e + 1, :CHUNK], -jnp.inf) for e in heads]
        m_prevs = [m_ref[e, 0:1, 0:1] for e in heads]
        log_ss = [f_cs[e] + m_prevs[e] for e in heads]
        m_ts = [jnp.maximum(jnp.max(log_ws[e], axis=-1, keepdims=True), log_ss[e]) for e in heads]
        qss = [q_ref[rows, e * C_DK:(e + 1) * C_DK] * (C_DK ** -0.5) for e in heads]
        ks = [k_ref[rows, e * C_DK:(e + 1) * C_DK] for e in heads]
        vs = [v_ref[rows, e * C_DV:(e + 1) * C_DV] for e in heads]
        qks = [_bdot_nt(qss[e], ks[e]) * jnp.exp(log_ws[e] - m_ts[e]) for e in heads]
        w_ss = [jnp.exp(log_ss[e] - m_ts[e]) for e in heads]
        c_old = [c_ref[e] for e in heads]
        n_old = [n_ref[e] for e in heads]
        inter = [_bdot(qss[e], c_old[e]) for e in heads]
        intra = [_bdot(qks[e], vs[e]) for e in heads]
        hids = []
        for e in heads:
            den = (jnp.sum(qks[e], axis=-1, keepdims=True)
                   + w_ss[e] * jnp.sum(qss[e] * n_old[e], axis=-1, keepdims=True))
            num = intra[e] + w_ss[e] * inter[e]
            hids.append(_rms(num / jnp.maximum(jnp.abs(den), jnp.exp(-m_ts[e]))) * gn_ref[e])
        for e in heads:
            m_end = m_ts[e][CHUNK - 1:CHUNK, :]
            f_end = f_cs[e][CHUNK - 1:CHUNK, :]
            w_end = jnp.exp(f_end + a_b[:, e:e + 1] - m_end)
            s_end = jnp.exp(f_end + m_prevs[e] - m_end)
            kw = ks[e] * w_end
            c_ref[e] = s_end * c_old[e] + _bdot_tn(kw, vs[e])
            n_ref[e] = s_end * n_old[e] + jnp.sum(kw, axis=0, keepdims=True)
            m_ref[e] = jnp.broadcast_to(m_end, (1, LANES))
        o_ref[rows, :] = (jnp.concatenate(hids, axis=1) * jax.nn.sigmoid(op_ref[rows, :])).astype(o_ref.dtype)
        return carry

    lax.fori_loop(0, n_chunks, chunk, 0)

    @pl.when(l == pl.num_programs(2) - 1)
    def _():
        c_out_ref[0] = c_ref[...]
        n_out_ref[0] = n_ref[...]
        m_out_ref[0] = m_ref[...]


def _mlstm_prompt(proj, gates, b, l, b_gate, g_norm):
    lblk = _pick_tile(l, SEQ_BLOCK, CHUNK)
    nl = l // lblk
    hg = MLSTM_HEADS_PER_STEP
    groups = C_HEADS // hg
    qw, vw = hg * C_DK, hg * C_DV
    v_off = 2 * C_QK_DIM // vw
    o, c, n, m = pl.pallas_call(
        functools.partial(_mlstm_chunk_kernel, n_chunks=lblk // CHUNK),
        grid=(b, groups, nl),
        in_specs=[pl.BlockSpec((lblk, qw), lambda bi, h, li: (bi * nl + li, h)),
                  pl.BlockSpec((lblk, qw), lambda bi, h, li: (bi * nl + li, groups + h)),
                  pl.BlockSpec((lblk, vw), lambda bi, h, li: (bi * nl + li, v_off + h)),
                  pl.BlockSpec((lblk, vw), lambda bi, h, li: (bi * nl + li, v_off + groups + h)),
                  pl.BlockSpec((lblk, 2 * C_HEADS), lambda bi, h, li: (bi * nl + li, 0)),
                  pl.BlockSpec((1, 2 * C_HEADS), lambda bi, h, li: (0, 0)),
                  pl.BlockSpec((hg, 1, C_DV), lambda bi, h, li: (h, 0, 0))],
        out_specs=[pl.BlockSpec((lblk, vw), lambda bi, h, li: (bi * nl + li, h)),
                   pl.BlockSpec((1, hg, C_DK, C_DV), lambda bi, h, li: (bi, h, 0, 0)),
                   pl.BlockSpec((1, hg, 1, C_DK), lambda bi, h, li: (bi, h, 0, 0)),
                   pl.BlockSpec((1, hg, 1, LANES), lambda bi, h, li: (bi, h, 0, 0))],
        out_shape=[jax.ShapeDtypeStruct((b * l, C_V_DIM), BF16),
                   jax.ShapeDtypeStruct((b, C_HEADS, C_DK, C_DV), F32),
                   jax.ShapeDtypeStruct((b, C_HEADS, 1, C_DK), F32),
                   jax.ShapeDtypeStruct((b, C_HEADS, 1, LANES), F32)],
        scratch_shapes=[pltpu.VMEM((hg, C_DK, C_DV), F32), pltpu.VMEM((hg, 1, C_DK), F32),
                        pltpu.VMEM((hg, 1, LANES), F32)],
        compiler_params=_params("parallel", "parallel", "arbitrary"),
        name="mlstm_chunks",
    )(proj, proj, proj, proj, gates, b_gate.reshape(1, -1), g_norm.reshape(C_HEADS, 1, C_DV))
    return o, c, n[:, :, 0, :], m[:, :, 0, 0]


MLSTM_STEP_BATCH = 8


def _mlstm_step_kernel(qk_ref, vo_ref, gt_ref, bg_ref, gn_ref, c_ref, n_ref, m_ref,
                       o_ref, c_out_ref, n_out_ref, m_out_ref):
    iota_h = lax.broadcasted_iota(jnp.int32, (C_HEADS, C_DV), 0)
    iota_hk = lax.broadcasted_iota(jnp.int32, (C_HEADS, C_DK), 0)
    iota_m = lax.broadcasted_iota(jnp.int32, (MLSTM_STEP_BATCH, C_HEADS), 1)
    iota_mb = lax.broadcasted_iota(jnp.int32, (MLSTM_STEP_BATCH, C_HEADS), 0)
    gates = _cap(gt_ref[...] + bg_ref[...])
    m_all = m_ref[...]
    m_new_all = jnp.zeros((MLSTM_STEP_BATCH, C_HEADS), F32)
    for bi in range(MLSTM_STEP_BATCH):
        qs = qk_ref[bi, 0:C_HEADS, :] * (C_DK ** -0.5)
        k = qk_ref[bi, C_HEADS:2 * C_HEADS, :]
        v = vo_ref[bi, C_HEADS:2 * C_HEADS, :]
        op = vo_ref[bi, 2 * C_HEADS:3 * C_HEADS, :]
        q_t, k_t = qs.T, k.T
        n_all = n_ref[bi]
        hid = jnp.zeros((C_HEADS, C_DV), F32)
        n_new = jnp.zeros((C_HEADS, C_DK), F32)
        for h in range(C_HEADS):
            i_g = gates[bi:bi + 1, h:h + 1]
            lf = _log_sigmoid(gates[bi:bi + 1, C_HEADS + h:C_HEADS + h + 1])
            log_s = lf + m_all[bi:bi + 1, h:h + 1]
            m_t = jnp.maximum(i_g, log_s)
            w_i = jnp.exp(i_g - m_t)
            w_s = jnp.exp(log_s - m_t)
            qk = jnp.sum(qs[h:h + 1, :] * k[h:h + 1, :], axis=-1, keepdims=True) * w_i
            c_old = c_ref[bi, h]
            num = qk * v[h:h + 1, :] + w_s * jnp.sum(q_t[:, h:h + 1] * c_old, axis=0, keepdims=True)
            den = qk + w_s * jnp.sum(qs[h:h + 1, :] * n_all[h:h + 1, :], axis=-1, keepdims=True)
            hid = jnp.where(iota_h == h, num / jnp.maximum(jnp.abs(den), jnp.exp(-m_t)), hid)
            c_out_ref[bi, h] = w_s * c_old + (w_i * k_t[:, h:h + 1]) * v[h:h + 1, :]
            n_new = jnp.where(iota_hk == h, w_s * n_all[h:h + 1, :] + w_i * k[h:h + 1, :], n_new)
            m_new_all = jnp.where(jnp.logical_and(iota_m == h, iota_mb == bi), m_t, m_new_all)
        n_out_ref[bi] = n_new
        o_ref[bi] = (_rms(hid) * gn_ref[...] * jax.nn.sigmoid(op)).astype(o_ref.dtype)
    m_out_ref[...] = m_new_all


def _mlstm_step(proj, gates, b_gate, g_norm, c0, n0, m0):
    b = proj.shape[0]
    bb = MLSTM_STEP_BATCH
    qk_view = proj.reshape(b, -1, C_DK)
    vo_view = proj.reshape(b, -1, C_DV)
    o, c, n, m = pl.pallas_call(
        _mlstm_step_kernel,
        grid=(b // bb,),
        in_specs=[pl.BlockSpec((bb, 2 * C_HEADS, C_DK), lambda i: (i, 0, 0)),
                  pl.BlockSpec((bb, 3 * C_HEADS, C_DV), lambda i: (i, 0, 0)),
                  pl.BlockSpec((bb, 2 * C_HEADS), lambda i: (i, 0)),
                  pl.BlockSpec((1, 2 * C_HEADS), lambda i: (0, 0)),
                  pl.BlockSpec((C_HEADS, C_DV), lambda i: (0, 0)),
                  pl.BlockSpec((bb, C_HEADS, C_DK, C_DV), lambda i: (i, 0, 0, 0)),
                  pl.BlockSpec((bb, C_HEADS, C_DK), lambda i: (i, 0, 0)),
                  pl.BlockSpec((bb, C_HEADS), lambda i: (i, 0))],
        out_specs=[pl.BlockSpec((bb, C_HEADS, C_DV), lambda i: (i, 0, 0)),
                   pl.BlockSpec((bb, C_HEADS, C_DK, C_DV), lambda i: (i, 0, 0, 0)),
                   pl.BlockSpec((bb, C_HEADS, C_DK), lambda i: (i, 0, 0)),
                   pl.BlockSpec((bb, C_HEADS), lambda i: (i, 0))],
        out_shape=[jax.ShapeDtypeStruct((b, C_HEADS, C_DV), BF16),
                   jax.ShapeDtypeStruct(c0.shape, F32),
                   jax.ShapeDtypeStruct(n0.shape, F32),
                   jax.ShapeDtypeStruct(m0.shape, F32)],
        compiler_params=_params("parallel"),
        name="mlstm_step",
    )(qk_view, vo_view, gates, b_gate.reshape(1, -1), g_norm, c0, n0, m0)
    return o.reshape(b, C_V_DIM), c, n, m


B_QK_HEADS = 16
B_V_HEADS = 32
B_DK = 128
B_DV = 128
B_CONV = 4
B_QK_DIM = B_QK_HEADS * B_DK
B_V_DIM = B_V_HEADS * B_DV
B_CONV_DIM = 2 * B_QK_DIM + B_V_DIM
B_REP = B_V_HEADS // B_QK_HEADS
CONV_PAD = 8


def _softplus(x):
    return jnp.maximum(x, 0.0) + jnp.log(1.0 + jnp.exp(-jnp.abs(x)))


def _l2norm(x):
    return x * lax.rsqrt(jnp.sum(x * x, axis=-1, keepdims=True) + RMS_EPS)


def _split2(x):
    hi = x.astype(BF16)
    return hi, (x - hi.astype(F32)).astype(BF16)


def _dot3_parts(ah, al, bh, bl):
    n = bh.shape[1]
    r = jnp.dot(ah, jnp.concatenate([bh, bl], axis=1), preferred_element_type=F32)
    return r[:, :n] + r[:, n:] + jnp.dot(al, bh, preferred_element_type=F32)


def _unit_lower_inverses(mats, order):
    n = mats[0].shape[0]
    eye = (lax.broadcasted_iota(jnp.int32, (n, n), 0) == lax.broadcasted_iota(jnp.int32, (n, n), 1)).astype(F32)
    ps = [-a for a in mats]
    ts = [eye + p for p in ps]
    parts = [_split2(p) for p in ps]
    span = 1
    while 2 * span < order:
        ps = [_dot3_parts(ph, pl_, ph, pl_) for ph, pl_ in parts]
        parts = [_split2(p) for p in ps]
        t_parts = [_split2(t) for t in ts]
        ts = [t + _dot3_parts(th, tl, ph, pl_) for t, (th, tl), (ph, pl_) in zip(ts, t_parts, parts)]
        span *= 2
    return ts


GDN_QK_PER_STEP = 8
GDN_V_PER_STEP = GDN_QK_PER_STEP * B_REP


def _gdn_chunk_kernel(q_ref, k_ref, v_ref, z_ref, ba_ref, wq_ref, wk_ref, wv_ref, alog_ref, dtb_ref, gn_ref,
                      o_ref, s_out_ref, s_ref, qbuf, kbuf, vbuf, gcol, bcol, grow, *, lblk):
    h = pl.program_id(1)
    l = pl.program_id(2)
    n_chunks = lblk // CHUNK

    @pl.when(l == 0)
    def _():
        s_ref[...] = jnp.zeros_like(s_ref)
        qbuf[0:CONV_PAD, :] = jnp.zeros((CONV_PAD, qbuf.shape[1]), F32)
        kbuf[0:CONV_PAD, :] = jnp.zeros((CONV_PAD, kbuf.shape[1]), F32)
        vbuf[0:CONV_PAD, :] = jnp.zeros((CONV_PAD, vbuf.shape[1]), F32)

    def conv(x_ref, w_ref, buf, post):
        buf[CONV_PAD:CONV_PAD + lblk, :] = x_ref[...]
        tail = buf[lblk:lblk + CONV_PAD, :]
        taps = B_CONV - 1
        width = buf.shape[1]
        slab = min(width, 4 * LANES)
        for c in reversed(range(n_chunks)):
            lo = CONV_PAD + c * CHUNK
            for s in range(0, width, slab):
                y = w_ref[0:1, s:s + slab] * buf[lo - taps:lo - taps + CHUNK, s:s + slab]
                for j in range(1, B_CONV):
                    y = y + w_ref[j:j + 1, s:s + slab] * buf[lo - taps + j:lo - taps + j + CHUNK, s:s + slab]
                buf[lo:lo + CHUNK, s:s + slab] = post(_silu(y))
        return tail

    def per_head_l2norm(scale):
        def post(y):
            return jnp.concatenate([_l2norm(y[:, i * B_DK:(i + 1) * B_DK]) * scale
                                    for i in range(y.shape[1] // B_DK)], axis=1)
        return post

    q_tail = conv(q_ref, wq_ref, qbuf, per_head_l2norm(B_DK ** -0.5))
    k_tail = conv(k_ref, wk_ref, kbuf, per_head_l2norm(1.0))
    v_tail = conv(v_ref, wv_ref, vbuf, lambda y: y)

    lane = lax.broadcasted_iota(jnp.int32, (lblk, 2 * B_V_HEADS), 1)
    lane_o = lax.broadcasted_iota(jnp.int32, (lblk, LANES), 1)
    ba = ba_ref[...]
    pad = jnp.zeros((1, B_V_HEADS), F32)
    a_scale = -jnp.exp(jnp.concatenate([pad, alog_ref[...]], axis=1))
    dt_b = jnp.concatenate([pad, dtb_ref[...]], axis=1)
    gate_vals = jnp.where(lane < B_V_HEADS, jax.nn.sigmoid(ba), a_scale * _softplus(ba + dt_b))
    beta_all = jnp.zeros((lblk, LANES), F32)
    g_all = jnp.zeros((lblk, LANES), F32)
    for e in range(GDN_V_PER_STEP):
        hv = h * GDN_V_PER_STEP + e
        beta = jnp.sum(jnp.where(lane == hv, gate_vals, 0.0), axis=-1, keepdims=True)
        g = jnp.sum(jnp.where(lane == hv + B_V_HEADS, gate_vals, 0.0), axis=-1, keepdims=True)
        beta_all = jnp.where(lane_o == e, beta, beta_all)
        g_all = jnp.where(lane_o == e, g, g_all)
    bcol[...] = beta_all
    for j in range(n_chunks):
        g_b = _cumsum_rows(g_all[j * CHUNK:(j + 1) * CHUNK])
        gcol[j * CHUNK:(j + 1) * CHUNK, :] = g_b
        grow[j] = g_b.T[0:grow.shape[1], 0:CHUNK]

    tri = _tri(CHUNK)
    strict = _tri(CHUNK, strict=True)
    zero_blk = jnp.zeros((CHUNK, CHUNK), F32)

    def chunk(j, carry):
        rows = pl.ds(pl.multiple_of(j * CHUNK, CHUNK), CHUNK)
        brow = pl.ds(pl.multiple_of(j * CHUNK + CONV_PAD, 8), CHUNK)
        g_rows = grow[j]
        heads_q = range(GDN_QK_PER_STEP)
        heads_v = range(GDN_V_PER_STEP)
        qs = [qbuf[brow, i * B_DK:(i + 1) * B_DK] for i in heads_q]
        ks = [kbuf[brow, i * B_DK:(i + 1) * B_DK] for i in heads_q]
        kks = [_bdot_nt(k, k) for k in ks]
        qks = [_bdot_nt(q, k) for q, k in zip(qs, ks)]
        g_cs = [gcol[rows, e:e + 1] for e in heads_v]
        betas = [bcol[rows, e:e + 1] for e in heads_v]
        decs = [jnp.exp(jnp.where(tri, g_cs[e] - g_rows[e:e + 1, :], -jnp.inf)) for e in heads_v]
        a_blks = [jnp.where(strict, betas[e] * decs[e] * kks[e // B_REP], 0.0) for e in heads_v]
        rhs = [jnp.concatenate([betas[e] * vbuf[brow, e * B_DV:(e + 1) * B_DV],
                                (betas[e] * jnp.exp(g_cs[e])) * ks[e // B_REP]], axis=-1) for e in heads_v]
        a_bds = [jnp.concatenate(
            [jnp.concatenate([a_blks[i * B_REP + r] if r == c else zero_blk for c in range(B_REP)], axis=1)
             for r in range(B_REP)], axis=0) for i in heads_q]
        t_invs = _unit_lower_inverses(a_bds, CHUNK)
        t_parts = [_split2(t) for t in t_invs]
        r_parts = [_split2(jnp.concatenate(rhs[i * B_REP:(i + 1) * B_REP], axis=0)) for i in heads_q]
        sols = [_dot3_parts(th, tl, rh, rl) for (th, tl), (rh, rl) in zip(t_parts, r_parts)]
        sol_e = [sols[e // B_REP][(e % B_REP) * CHUNK:(e % B_REP + 1) * CHUNK] for e in heads_v]
        s_old = [s_ref[e] for e in heads_v]
        ws = [sol_e[e][:, :B_DV] - _bdot(sol_e[e][:, B_DV:], s_old[e]) for e in heads_v]
        o_inter = [_bdot(qs[e // B_REP] * jnp.exp(g_cs[e]), s_old[e]) for e in heads_v]
        o_intra = [_bdot(decs[e] * qks[e // B_REP], ws[e]) for e in heads_v]
        for e in heads_v:
            g_end = g_cs[e][CHUNK - 1:CHUNK, :]
            s_ref[e] = jnp.exp(g_end) * s_old[e] + _bdot_tn(ks[e // B_REP] * jnp.exp(g_end - g_cs[e]), ws[e])
        for e in heads_v:
            zc = z_ref[rows, e * B_DV:(e + 1) * B_DV]
            o = o_intra[e] + o_inter[e]
            o_ref[rows, e * B_DV:(e + 1) * B_DV] = (_rms(o) * gn_ref[...] * _silu(zc)).astype(o_ref.dtype)
        return carry

    lax.fori_loop(0, n_chunks, chunk, 0)
    qbuf[0:CONV_PAD, :] = q_tail
    kbuf[0:CONV_PAD, :] = k_tail
    vbuf[0:CONV_PAD, :] = v_tail

    @pl.when(l == pl.num_programs(2) - 1)
    def _():
        s_out_ref[0] = s_ref[...]


def _gdn_prompt(proj, ba, b, l, conv_w, a_log, dt_bias, g_norm):
    lblk = _pick_tile(l, SEQ_BLOCK, CHUNK)
    nl = l // lblk
    qw = GDN_QK_PER_STEP * B_DK
    vw = GDN_V_PER_STEP * B_DV
    k_off = B_QK_DIM // qw
    v_off = 2 * B_QK_DIM // vw
    z_off = B_CONV_DIM // vw
    return pl.pallas_call(
        functools.partial(_gdn_chunk_kernel, lblk=lblk),
        grid=(b, B_QK_HEADS // GDN_QK_PER_STEP, nl),
        in_specs=[pl.BlockSpec((lblk, qw), lambda bi, h, li: (bi * nl + li, h)),
                  pl.BlockSpec((lblk, qw), lambda bi, h, li: (bi * nl + li, k_off + h)),
                  pl.BlockSpec((lblk, vw), lambda bi, h, li: (bi * nl + li, v_off + h)),
                  pl.BlockSpec((lblk, vw), lambda bi, h, li: (bi * nl + li, z_off + h)),
                  pl.BlockSpec((lblk, 2 * B_V_HEADS), lambda bi, h, li: (bi * nl + li, 0)),
                  pl.BlockSpec((B_CONV, qw), lambda bi, h, li: (0, h)),
                  pl.BlockSpec((B_CONV, qw), lambda bi, h, li: (0, k_off + h)),
                  pl.BlockSpec((B_CONV, vw), lambda bi, h, li: (0, v_off + h)),
                  pl.BlockSpec((1, B_V_HEADS), lambda bi, h, li: (0, 0)),
                  pl.BlockSpec((1, B_V_HEADS), lambda bi, h, li: (0, 0)),
                  pl.BlockSpec((1, B_DV), lambda bi, h, li: (0, 0))],
        out_specs=[pl.BlockSpec((lblk, vw), lambda bi, h, li: (bi * nl + li, h)),
                   pl.BlockSpec((1, GDN_V_PER_STEP, B_DK, B_DV), lambda bi, h, li: (bi, h, 0, 0))],
        out_shape=[jax.ShapeDtypeStruct((b * l, B_V_DIM), BF16),
                   jax.ShapeDtypeStruct((b, B_V_HEADS, B_DK, B_DV), F32)],
        scratch_shapes=[pltpu.VMEM((GDN_V_PER_STEP, B_DK, B_DV), F32),
                        pltpu.VMEM((CONV_PAD + lblk, qw), F32),
                        pltpu.VMEM((CONV_PAD + lblk, qw), F32),
                        pltpu.VMEM((CONV_PAD + lblk, vw), F32),
                        pltpu.VMEM((lblk, LANES), F32),
                        pltpu.VMEM((lblk, LANES), F32),
                        pltpu.VMEM((lblk // CHUNK, max(8, GDN_V_PER_STEP), CHUNK), F32)],
        compiler_params=_params("parallel", "parallel", "arbitrary"),
        name="gdn_chunks",
    )(proj, proj, proj, proj, ba, conv_w, conv_w, conv_w,
      a_log.reshape(1, -1), dt_bias.reshape(1, -1), g_norm.reshape(1, -1))


GDN_STEP_BATCH = 2
B_ROWS = B_CONV_DIM // 128


def _gdn_step_kernel(p_ref, cv_ref, cw_ref, ba_ref, alog_ref, dtb_ref, gn_ref, s_ref, o_ref, s_out_ref):
    ba = ba_ref[0]
    beta_all = jax.nn.sigmoid(ba[:, :B_V_HEADS])
    g_all = -jnp.exp(alog_ref[...]) * _softplus(ba[:, B_V_HEADS:] + dtb_ref[...])
    decay_all = jnp.exp(g_all)
    iota_h = lax.broadcasted_iota(jnp.int32, (B_V_HEADS, B_DV), 0)
    for bi in range(GDN_STEP_BATCH):
        y = cw_ref[B_CONV - 1] * p_ref[bi, 0:B_ROWS, :]
        for j in range(B_CONV - 1):
            y = y + cw_ref[j] * cv_ref[bi, j]
        y = _silu(y)
        q = _l2norm(y[0:B_QK_HEADS]) * (B_DK ** -0.5)
        k = _l2norm(y[B_QK_HEADS:2 * B_QK_HEADS])
        v = y[2 * B_QK_HEADS:]
        z = p_ref[bi, B_ROWS:B_ROWS + B_V_HEADS, :]
        q_t, k_t = q.T, k.T
        qk = jnp.sum(q * k, axis=-1, keepdims=True)
        o = jnp.zeros((B_V_HEADS, B_DV), F32)
        for hv in range(B_V_HEADS):
            hq = hv // B_REP
            beta = beta_all[bi:bi + 1, hv:hv + 1]
            decay = decay_all[bi:bi + 1, hv:hv + 1]
            s = s_ref[bi, hv]
            k_col = k_t[:, hq:hq + 1]
            w = beta * v[hv:hv + 1, :] - (beta * decay) * jnp.sum(k_col * s, axis=0, keepdims=True)
            oh = qk[hq:hq + 1, :] * w + decay * jnp.sum(q_t[:, hq:hq + 1] * s, axis=0, keepdims=True)
            s_out_ref[bi, hv] = decay * s + k_col * w
            o = jnp.where(iota_h == hv, oh, o)
        o_ref[bi] = (_rms(o) * gn_ref[...] * _silu(z)).astype(o_ref.dtype)


def _gdn_step(proj, ba, conv0, conv_w, a_log, dt_bias, g_norm, s0):
    b = proj.shape[0]
    bb = GDN_STEP_BATCH
    p3 = proj.reshape(b, -1, 128)
    o, s = pl.pallas_call(
        _gdn_step_kernel,
        grid=(b // bb,),
        in_specs=[pl.BlockSpec((bb, p3.shape[1], 128), lambda i: (i, 0, 0)),
                  pl.BlockSpec((bb, B_CONV - 1, B_ROWS, 128), lambda i: (i, 0, 0, 0)),
                  pl.BlockSpec((B_CONV, B_ROWS, 128), lambda i: (0, 0, 0)),
                  pl.BlockSpec((1, bb, 2 * B_V_HEADS), lambda i: (i, 0, 0)),
                  pl.BlockSpec((1, B_V_HEADS), lambda i: (0, 0)),
                  pl.BlockSpec((1, B_V_HEADS), lambda i: (0, 0)),
                  pl.BlockSpec((1, B_DV), lambda i: (0, 0)),
                  pl.BlockSpec((bb, B_V_HEADS, B_DK, B_DV), lambda i: (i, 0, 0, 0))],
        out_specs=[pl.BlockSpec((bb, B_V_HEADS, B_DV), lambda i: (i, 0, 0)),
                   pl.BlockSpec((bb, B_V_HEADS, B_DK, B_DV), lambda i: (i, 0, 0, 0))],
        out_shape=[jax.ShapeDtypeStruct((b, B_V_HEADS, B_DV), BF16), jax.ShapeDtypeStruct(s0.shape, F32)],
        compiler_params=_params("parallel"),
        name="gdn_step",
    )(p3, conv0.reshape(b, B_CONV - 1, B_ROWS, 128), conv_w.reshape(B_CONV, B_ROWS, 128),
      ba.reshape(b // bb, bb, 2 * B_V_HEADS), a_log.reshape(1, -1), dt_bias.reshape(1, -1), g_norm.reshape(1, -1), s0)
    return o.reshape(b, B_V_DIM), s


D_GROUP = 16
D_GROUPS = D_MODEL // D_GROUP
D_STATE = 64
S5_LANES = 128
S5_GROUPS = S5_LANES // D_GROUP
S5_HALF = S5_GROUPS * D_STATE
S5_BLOCKS = D_MODEL // S5_LANES
S5_SEGMENTS = 8


def _cmul(ar, ai, br, bi):
    return ar * br - ai * bi, ar * bi + ai * br


def _gelu_tanh(x):
    return 0.5 * x * (1.0 + jnp.tanh(math.sqrt(2.0 / math.pi) * (x + 0.044715 * (x * x * x))))


def _s5_tables(a_re, a_im, b_re, b_im, c_re, c_im, log_dt, seg_len):
    lam_re, lam_im = a_re.astype(F32), a_im.astype(F32)
    dt = jnp.exp(log_dt.astype(F32))[:, None]
    mag = jnp.exp(dt * lam_re)
    ab_re, ab_im = mag * jnp.cos(dt * lam_im), mag * jnp.sin(dt * lam_im)
    inv = 1.0 / (lam_re * lam_re + lam_im * lam_im)
    e_re, e_im = _cmul(ab_re - 1.0, ab_im, lam_re * inv, -lam_im * inv)
    bb_re, bb_im = _cmul(e_re[..., None], e_im[..., None], b_re.astype(F32), b_im.astype(F32))
    j = S5_BLOCKS
    eye = jnp.eye(S5_GROUPS, dtype=F32)

    def pack_rows(t):
        return t.reshape(j, 1, S5_HALF)

    def bdiag_in(t):
        t = t.reshape(j, S5_GROUPS, D_STATE, D_GROUP)
        return jnp.einsum('jgpc,gh->jgchp', t, eye).reshape(j, S5_LANES, S5_HALF)

    def bdiag_out(t):
        t = t.reshape(j, S5_GROUPS, D_GROUP, D_STATE)
        return jnp.einsum('jgcp,gh->jgphc', t, eye).reshape(j, S5_HALF, S5_LANES)

    ab = jnp.concatenate([pack_rows(ab_re), pack_rows(ab_im)], axis=-1)
    w_bu = jnp.concatenate([bdiag_in(bb_re), bdiag_in(bb_im)], axis=-1)
    w_c = jnp.concatenate([bdiag_out(c_re.astype(F32)), -bdiag_out(c_im.astype(F32))], axis=1)
    pr, pi = pack_rows(ab_re), pack_rows(ab_im)
    n = 1
    while n < seg_len:
        tr, ti = pr[:, n - 1:n], pi[:, n - 1:n]
        nr, ni = _cmul(pr, pi, tr, ti)
        pr, pi = jnp.concatenate([pr, nr], axis=1), jnp.concatenate([pi, ni], axis=1)
        n *= 2
    pw = jnp.concatenate([pr[:, :seg_len], pi[:, :seg_len]], axis=-1)
    return ab, w_bu, w_c, pw


def _s5_scan_kernel(x_ref, ab_ref, wbu_ref, wc_ref, pw_ref, dsk_ref, y_ref, s_out_ref, xp, buf, yp, *, seq, seg):
    nc = S5_HALF // 128
    ns = S5_SEGMENTS

    def tile(tau):
        return pl.ds(pl.multiple_of(tau * ns, ns), ns)

    def interleave(tau, c):
        xp[tile(tau), :] = x_ref[pl.ds(tau, ns, stride=seg), :]
        return c

    lax.fori_loop(0, seg, interleave, 0, unroll=8)
    x = xp[...]
    bu = _bdot(x, wbu_ref[0])
    for c in range(2 * nc):
        buf[c] = bu[:, c * 128:(c + 1) * 128]

    a_res = [jnp.broadcast_to(ab_ref[0, :, c * 128:(c + 1) * 128], (ns, 128)) for c in range(nc)]
    a_ims = [jnp.broadcast_to(ab_ref[0, :, S5_HALF + c * 128:S5_HALF + (c + 1) * 128], (ns, 128))
             for c in range(nc)]

    def step(tau, carry):
        rows = tile(tau)
        out = []
        for c in range(nc):
            s_re, s_im = carry[c]
            n_re = a_res[c] * s_re - a_ims[c] * s_im + buf[c, rows, :]
            n_im = a_res[c] * s_im + a_ims[c] * s_re + buf[nc + c, rows, :]
            buf[c, rows, :] = n_re
            buf[nc + c, rows, :] = n_im
            out.append((n_re, n_im))
        return tuple(out)

    zeros = jnp.zeros((ns, 128), F32)
    local_end = lax.fori_loop(0, seg, step, tuple((zeros, zeros) for _ in range(nc)), unroll=2)

    iota_r = lax.broadcasted_iota(jnp.int32, (ns, 128), 0)
    enter = []
    for c in range(nc):
        l_re, l_im = local_end[c]
        pe_re = pw_ref[0, seg - 1:seg, c * 128:(c + 1) * 128]
        pe_im = pw_ref[0, seg - 1:seg, S5_HALF + c * 128:S5_HALF + (c + 1) * 128]
        e_re, e_im = l_re[0:1], l_im[0:1]
        in_re, in_im = zeros, zeros
        for r in range(1, ns):
            in_re = jnp.where(iota_r == r, e_re, in_re)
            in_im = jnp.where(iota_r == r, e_im, in_im)
            t_re, t_im = _cmul(pe_re, pe_im, e_re, e_im)
            e_re, e_im = l_re[r:r + 1] + t_re, l_im[r:r + 1] + t_im
        enter.append((in_re, in_im))

    def fix(group, carry):
        taus = pl.ds(pl.multiple_of(group * 8, 8), 8)
        for c in range(nc):
            p_re8 = pw_ref[0, taus, c * 128:(c + 1) * 128]
            p_im8 = pw_ref[0, taus, S5_HALF + c * 128:S5_HALF + (c + 1) * 128]
            for u in range(8):
                rows = tile(group * 8 + u)
                f_re, f_im = _cmul(p_re8[u:u + 1], p_im8[u:u + 1], enter[c][0], enter[c][1])
                buf[c, rows, :] = buf[c, rows, :] + f_re
                buf[nc + c, rows, :] = buf[nc + c, rows, :] + f_im
        return carry

    lax.fori_loop(0, seg // 8, fix, 0)

    y = dsk_ref[...] * x
    for c in range(2 * nc):
        s_out_ref[0, 0, :, c * 128:(c + 1) * 128] = buf[c, seq - 1:seq, :]
        y = y + _bdot(buf[c], wc_ref[0, c * 128:(c + 1) * 128, :])
    yp[...] = _gelu_tanh(y)
    for r in range(ns):
        y_ref[r * seg:(r + 1) * seg, :] = yp[pl.ds(r, seg, stride=ns), :].astype(y_ref.dtype)


def _unpack_state(s):
    b = s.shape[0]
    return (s[..., :S5_HALF].reshape(b, D_GROUPS, D_STATE), s[..., S5_HALF:].reshape(b, D_GROUPS, D_STATE))


def _s5_prompt(x, b, l, tables, d_skip):
    d = x.shape[1]
    ab, w_bu, w_c, pw = tables
    seg = l // S5_SEGMENTS
    y, s = pl.pallas_call(
        functools.partial(_s5_scan_kernel, seq=l, seg=seg),
        grid=(b, S5_BLOCKS),
        in_specs=[pl.BlockSpec((l, S5_LANES), lambda bi, j: (bi, j)),
                  pl.BlockSpec((1, 1, 2 * S5_HALF), lambda bi, j: (j, 0, 0)),
                  pl.BlockSpec((1, S5_LANES, 2 * S5_HALF), lambda bi, j: (j, 0, 0)),
                  pl.BlockSpec((1, 2 * S5_HALF, S5_LANES), lambda bi, j: (j, 0, 0)),
                  pl.BlockSpec((1, seg, 2 * S5_HALF), lambda bi, j: (j, 0, 0)),
                  pl.BlockSpec((1, S5_LANES), lambda bi, j: (0, j))],
        out_specs=[pl.BlockSpec((l, S5_LANES), lambda bi, j: (bi, j)),
                   pl.BlockSpec((1, 1, 1, 2 * S5_HALF), lambda bi, j: (bi, j, 0, 0))],
        out_shape=[jax.ShapeDtypeStruct((b * l, d), BF16),
                   jax.ShapeDtypeStruct((b, S5_BLOCKS, 1, 2 * S5_HALF), F32)],
        scratch_shapes=[pltpu.VMEM((l, S5_LANES), F32),
                        pltpu.VMEM((2 * S5_HALF // 128, l, 128), F32),
                        pltpu.VMEM((l, S5_LANES), F32)],
        compiler_params=_params("parallel", "parallel"),
        name="s5_scan",
    )(x, ab, w_bu, w_c, pw, d_skip.reshape(1, d))
    return y, _unpack_state(s[:, :, 0, :])


def _s5_step_kernel(x_ref, s_ref, ab_ref, wbu_ref, wc_ref, dsk_ref, y_ref, s_out_ref):
    hf = S5_HALF
    x = x_ref[...]
    bu = _bdot(x, wbu_ref[0])
    s0 = s_ref[0]
    i_re, i_im = _cmul(ab_ref[0, :, :hf], ab_ref[0, :, hf:], s0[:, :hf], s0[:, hf:])
    s_new = jnp.concatenate([i_re, i_im], axis=-1) + bu
    s_out_ref[0] = s_new
    y = _bdot(s_new, wc_ref[0]) + dsk_ref[...] * x
    y_ref[...] = _gelu_tanh(y).astype(y_ref.dtype)


def _s5_step(x, tables, d_skip, s0_re, s0_im):
    b, d = x.shape
    ab, w_bu, w_c, _ = tables
    s0 = jnp.concatenate([s0_re.reshape(b, S5_BLOCKS, S5_HALF), s0_im.reshape(b, S5_BLOCKS, S5_HALF)], axis=-1)
    s0 = jnp.swapaxes(s0, 0, 1)
    y, s = pl.pallas_call(
        _s5_step_kernel,
        grid=(S5_BLOCKS,),
        in_specs=[pl.BlockSpec((b, S5_LANES), lambda j: (0, j)),
                  pl.BlockSpec((1, b, 2 * S5_HALF), lambda j: (j, 0, 0)),
                  pl.BlockSpec((1, 1, 2 * S5_HALF), lambda j: (j, 0, 0)),
                  pl.BlockSpec((1, S5_LANES, 2 * S5_HALF), lambda j: (j, 0, 0)),
                  pl.BlockSpec((1, 2 * S5_HALF, S5_LANES), lambda j: (j, 0, 0)),
                  pl.BlockSpec((1, S5_LANES), lambda j: (0, j))],
        out_specs=[pl.BlockSpec((b, S5_LANES), lambda j: (0, j)),
                   pl.BlockSpec((1, b, 2 * S5_HALF), lambda j: (j, 0, 0))],
        out_shape=[jax.ShapeDtypeStruct((b, d), BF16), jax.ShapeDtypeStruct((S5_BLOCKS, b, 2 * S5_HALF), F32)],
        compiler_params=_params("parallel"),
        name="s5_step",
    )(x, s0, ab, w_bu, w_c, d_skip.reshape(1, d))
    return y, _unpack_state(jnp.swapaxes(s, 0, 1))


def _glu_kernel(x_ref, w1_ref, w2_ref, o_ref):
    x = x_ref[...]
    y1 = jnp.dot(x, w1_ref[...].astype(BF16), preferred_element_type=F32)
    y2 = jnp.dot(x, w2_ref[...].astype(BF16), preferred_element_type=F32)
    o_ref[...] = y1 * jax.nn.sigmoid(y2)


def _glu_matmul(x, w, tm_target=1664, tn_target=256):
    m, k = x.shape
    n = w.shape[1] // 2
    tm = _pick_tile(m, tm_target, 16)
    tn = _pick_tile(n, tn_target, 128)
    nb = n // tn
    return pl.pallas_call(
        _glu_kernel,
        grid=(m // tm, nb),
        in_specs=[pl.BlockSpec((tm, k), lambda i, j: (i, 0)),
                  pl.BlockSpec((k, tn), lambda i, j: (0, j)),
                  pl.BlockSpec((k, tn), lambda i, j: (0, j + nb))],
        out_specs=pl.BlockSpec((tm, tn), lambda i, j: (i, j)),
        out_shape=jax.ShapeDtypeStruct((m, n), F32),
        compiler_params=_params("parallel", "parallel"),
        name="glu_matmul",
    )(x, w, w)


def kernel(x_prompt, x_sample, state_a_S, state_b_S, state_b_conv, state_c_C, state_c_n, state_c_m, state_d_re, state_d_im, a_w_in, a_lb_logits, a_g_norm, a_w_out, b_w_in, b_conv_w, b_a_log, b_dt_bias, b_g_norm, b_w_out, c_w_in, c_b_gate, c_g_norm, c_w_out, d_a_re, d_a_im, d_b_re, d_b_im, d_c_re, d_c_im, d_skip, d_log_dt, d_w_glu, moe_w_router, moe_b_router, moe_w_gu, moe_w_down, moe_w_gu_s, moe_w_down_s, ln_mix_g, ln_mix_b, ln_ffn_g, ln_ffn_b):
    bp, l, d = x_prompt.shape
    bs = x_sample.shape[0]
    tp = bp * l
    x = jnp.concatenate([x_prompt.reshape(tp, d), x_sample.reshape(bs, d)], axis=0)
    xb = x.astype(BF16)

    def finish_layer(x, h, layer):
        x, xb, xpk = _add_ln(x, h, ln_mix_g[layer], ln_mix_b[layer])
        return _moe_ln(x, xb, xpk, moe_w_router, moe_b_router, moe_w_gu, moe_w_down, moe_w_gu_s, moe_w_down_s,
                       ln_ffn_g, ln_ffn_b, layer)

    proj = _matmul(xb, a_w_in)
    lb = _hgrn2_lower_bound(a_lb_logits, 0)
    o_p, a_s_p = _hgrn2_prompt(proj, bp, l, lb, a_g_norm)
    o_s, a_s_s = _hgrn2_step(proj[tp:], state_a_S, lb, a_g_norm)
    h = _matmul(jnp.concatenate([o_p, o_s], axis=0), a_w_out)
    x, xb = finish_layer(x, h, 0)

    n_main = B_CONV_DIM + B_V_DIM
    proj = _matmul(xb, b_w_in, n_cols=n_main)
    ba = _matmul(xb, b_w_in[:, n_main:])
    o_p, b_s_p = _gdn_prompt(proj, ba, bp, l, b_conv_w, b_a_log, b_dt_bias, b_g_norm)
    o_s, b_s_s = _gdn_step(proj[tp:], ba[tp:], state_b_conv, b_conv_w, b_a_log, b_dt_bias, b_g_norm, state_b_S)
    qkv_s = proj[tp:, :B_CONV_DIM]
    b_conv_p = jnp.stack([proj[(i + 1) * l - (B_CONV - 1):(i + 1) * l, :B_CONV_DIM] for i in range(bp)], axis=0)
    b_conv_s = jnp.concatenate([state_b_conv[:, 1:, :], qkv_s[:, None, :]], axis=1)
    h = _matmul(jnp.concatenate([o_p, o_s], axis=0), b_w_out)
    x, xb = finish_layer(x, h, 1)

    n_main = 2 * C_QK_DIM + 2 * C_V_DIM
    proj = _matmul(xb, c_w_in, n_cols=n_main)
    gates = _matmul(xb, c_w_in[:, n_main:])
    o_p, c_c_p, c_n_p, c_m_p = _mlstm_prompt(proj, gates, bp, l, c_b_gate, c_g_norm)
    o_s, c_c_s, c_n_s, c_m_s = _mlstm_step(proj[tp:], gates[tp:], c_b_gate, c_g_norm, state_c_C, state_c_n, state_c_m)
    h = _matmul(jnp.concatenate([o_p, o_s], axis=0), c_w_out)
    x, xb = finish_layer(x, h, 2)

    tables = _s5_tables(d_a_re, d_a_im, d_b_re, d_b_im, d_c_re, d_c_im, d_log_dt, l // S5_SEGMENTS)
    y_p, (d_re_p, d_im_p) = _s5_prompt(x, bp, l, tables, d_skip)
    y_s, (d_re_s, d_im_s) = _s5_step(x[tp:], tables, d_skip, state_d_re, state_d_im)
    h = _glu_matmul(jnp.concatenate([y_p, y_s], axis=0), d_w_glu)
    x, xb = finish_layer(x, h, 3)

    y_prompt = x[:tp].reshape(bp, l, d)
    y_sample = x[tp:].reshape(bs, 1, d)
    return (y_prompt, y_sample, a_s_p, a_s_s, b_s_p, b_s_s, b_conv_p, b_conv_s,
            c_c_p, c_c_s, c_n_p, c_n_s, c_m_p, c_m_s, d_re_p, d_re_s, d_im_p, d_im_s)
```
